```python
import jax
import jax.numpy as jnp
from jax import lax
import numpy as np

D_MODEL = 4096
BATCH = 2
SEQ = 8192
DEPTH = 2

N_META = 16
CHUNK = 64
CONV_W = 4
CONV_LEFT = 2
EPS = 1e-6
NEG = -1e30

A_HEADS = 4
A_DK = D_MODEL // 16
A_DV = D_MODEL // 8
A_QK = A_HEADS * A_DK
A_V = A_HEADS * A_DV
A_GATES = 4 * A_HEADS

B_HEADS = 4
B_DK = D_MODEL // 16
B_DV = D_MODEL // 8
B_QK = B_HEADS * B_DK
B_V = B_HEADS * B_DV
B_RANK = 16
B_TAU = 16.0

EVEN_SPLITS = (A_QK, A_QK, A_V, A_V, A_GATES, B_QK, B_QK, B_V, B_V, 2 * B_RANK)
EVEN_IN = 2 * A_QK + 2 * A_V + A_GATES + 2 * B_QK + 2 * B_V + 2 * B_RANK

D_RNN = D_MODEL * 5 // 4
RNN_BLOCKS = 16
RNN_BLOCK = D_RNN // RNN_BLOCKS
RNN_C = 8.0

N_GROUPS = 4
EXPERTS_PER_GROUP = 8
TOP_K = 2
D_EXPERT = D_MODEL // 16

kernel_name = 'hybrid_mlstm_gla_rglru_hmoe_encoder'


def rmsnorm(x, g):
    xf = x.astype(jnp.float32)
    y = xf * lax.rsqrt(jnp.mean(xf * xf, axis=-1, keepdims=True) + EPS)
    return (y * g.astype(jnp.float32)).astype(x.dtype)


def head_rmsnorm(h, g):
    b, t, nh, d = h.shape
    return rmsnorm(h, g.reshape(nh, d)).reshape(b, t, nh * d)


def centred_dwconv(u, w):
    t = u.shape[1]
    up = jnp.pad(u, ((0, 0), (CONV_LEFT, CONV_W - 1 - CONV_LEFT), (0, 0)))
    out = up[:, 0:t] * w[0]
    for j in range(1, CONV_W):
        out = out + up[:, j:j + t] * w[j]
    return out


def to_chunks(u, pad_value=0.0):
    b, t = u.shape[:2]
    pad = CHUNK - N_META
    u = jnp.pad(u, ((0, 0), (pad, 0)) + ((0, 0),) * (u.ndim - 2), constant_values=pad_value)
    n = (t + pad) // CHUNK
    u = u.reshape((b, n, CHUNK) + u.shape[2:])
    return jnp.swapaxes(jnp.moveaxis(u, 1, 0), 2, 3)


def from_chunks(u):
    u = jnp.moveaxis(jnp.swapaxes(u, 2, 3), 0, 1)
    b, n = u.shape[:2]
    u = u.reshape((b, n * CHUNK) + u.shape[3:])
    return u[:, CHUNK - N_META:]


def bidirectional(scan_fn, fwd_args, bwd_args):
    flip = lambda u: jnp.flip(u, axis=(0, 3))
    return scan_fn(*fwd_args) + flip(scan_fn(*[flip(a) for a in bwd_args]))


def mlstm_scan(q, k, v, log_i, log_f):
    f32 = jnp.float32
    q, k, v = q.astype(f32), k.astype(f32), v.astype(f32)
    _, b, nh, l, dk = q.shape
    dv = v.shape[-1]
    causal = jnp.tril(jnp.ones((l, l), dtype=bool))

    def step(carry, xs):
        c, n, m = carry
        qc, kc, vc, li, lf = xs
        cum = jnp.cumsum(lf, axis=-1)
        d_mat = jnp.where(causal, cum[..., :, None] - cum[..., None, :] + li[..., None, :], NEG)
        inter = cum + m[..., None]
        m_t = jnp.maximum(inter, jnp.max(d_mat, axis=-1))
        w_inter = jnp.exp(inter - m_t)
        s = jnp.einsum('bhtd,bhsd->bhts', qc, kc) * jnp.exp(d_mat - m_t[..., None])
        num = w_inter[..., None] * jnp.einsum('bhtd,bhde->bhte', qc, c) + jnp.einsum('bhts,bhse->bhte', s, vc)
        den = w_inter * jnp.einsum('bhtd,bhd->bht', qc, n) + jnp.sum(s, axis=-1)
        h = num / jnp.maximum(jnp.abs(den), jnp.exp(-m_t))[..., None]
        tot = cum[..., -1]
        g = tot[..., None] - cum + li
        m_new = jnp.maximum(tot + m, jnp.max(g, axis=-1))
        decay = jnp.exp(tot + m - m_new)
        ws = jnp.exp(g - m_new[..., None])
        c_new = decay[..., None, None] * c + jnp.einsum('bhs,bhsd,bhse->bhde', ws, kc, vc)
        n_new = decay[..., None] * n + jnp.einsum('bhs,bhsd->bhd', ws, kc)
        return (c_new, n_new, m_new), h

    init = (jnp.zeros((b, nh, dk, dv), f32), jnp.zeros((b, nh, dk), f32), jnp.zeros((b, nh), f32))
    _, h = lax.scan(step, init, (q, k, v, log_i, log_f))
    return h


def gla_scan(q, k, v, log_a):
    f32 = jnp.float32
    q, k, v, log_a = q.astype(f32), k.astype(f32), v.astype(f32), log_a.astype(f32)
    _, b, nh, l, dk = q.shape
    dv = v.shape[-1]
    causal = jnp.tril(jnp.ones((l, l), dtype=bool))[:, :, None]

    def step(s_state, xs):
        qc, kc, vc, la = xs
        cum = jnp.cumsum(la, axis=-2)
        rel = jnp.exp(jnp.where(causal, cum[..., :, None, :] - cum[..., None, :, :], NEG))
        att = jnp.einsum('bhtc,bhsc,bhtsc->bhts', qc, kc, rel)
        o = jnp.einsum('bhtc,bhce->bhte', qc * jnp.exp(cum), s_state) + jnp.einsum('bhts,bhse->bhte', att, vc)
        tot = cum[..., -1:, :]
        s_new = jnp.exp(tot[..., 0, :])[..., None] * s_state + jnp.einsum('bhsc,bhse->bhce', kc * jnp.exp(tot - cum), vc)
        return s_new, o

    _, o = lax.scan(step, jnp.zeros((b, nh, dk, dv), f32), (q, k, v, log_a))
    return o


def even_mixer(x, w_in, gate_bias, qk_conv, lr_up, lr_bias, norm_a, norm_b, w_out):
    f32 = jnp.float32
    b, t, _ = x.shape
    idx = np.cumsum(EVEN_SPLITS)[:-1].tolist()
    qa, ka, va, oa, ga, qb, kb, vb, gb, lrb = jnp.split(x @ w_in, idx, axis=-1)

    qk = jax.nn.silu(centred_dwconv(jnp.concatenate([qa, ka], axis=-1), qk_conv))
    qa, ka = jnp.split(qk, 2, axis=-1)
    qa = qa.reshape(b, t, A_HEADS, A_DK)
    ka = ka.reshape(b, t, A_HEADS, A_DK) * (A_DK ** -0.5)
    va = va.reshape(b, t, A_HEADS, A_DV)
    gpre = ga.astype(f32).reshape(b, t, 4, A_HEADS) + gate_bias.astype(f32)
    li_f = gpre[:, :, 0]
    lf_f = jax.nn.log_sigmoid(gpre[:, :, 1])
    li_b = gpre[:, :, 2]
    lf_b = jax.nn.log_sigmoid(gpre[:, :, 3])
    qc, kc, vc = to_chunks(qa), to_chunks(ka), to_chunks(va)
    ha = bidirectional(mlstm_scan,
                       (qc, kc, vc, to_chunks(li_f, NEG), to_chunks(lf_f)),
                       (qc, kc, vc, to_chunks(li_b, NEG), to_chunks(lf_b)))
    ya = jax.nn.sigmoid(oa) * head_rmsnorm(from_chunks(ha).astype(x.dtype), norm_a)

    qb = qb.reshape(b, t, B_HEADS, B_DK) * (B_DK ** -0.5)
    kb = kb.reshape(b, t, B_HEADS, B_DK)
    vb = vb.reshape(b, t, B_HEADS, B_DV)
    lrb = lrb.astype(f32)
    la_f = (jax.nn.log_sigmoid(lrb[..., :B_RANK] @ lr_up[0].astype(f32) + lr_bias[0].astype(f32)) / B_TAU).reshape(b, t, B_HEADS, B_DK)
    la_b = (jax.nn.log_sigmoid(lrb[..., B_RANK:] @ lr_up[1].astype(f32) + lr_bias[1].astype(f32)) / B_TAU).reshape(b, t, B_HEADS, B_DK)
    qc, kc, vc = to_chunks(qb), to_chunks(kb), to_chunks(vb)
    hb = bidirectional(gla_scan, (qc, kc, vc, to_chunks(la_f)), (qc, kc, vc, to_chunks(la_b)))
    yb = jax.nn.silu(gb) * head_rmsnorm(from_chunks(hb).astype(x.dtype), norm_b)

    return jnp.concatenate([ya, yb], axis=-1) @ w_out


def _linear_combine(c1, c2):
    a1, b1 = c1
    a2, b2 = c2
    return (a1 * a2, a2 * b1 + b2)


def odd_mixer(x, w_in, conv_w, conv_b, w_r, b_r, w_i, b_i, lam, w_out):
    f32 = jnp.float32
    b, t, _ = x.shape
    gate, u = jnp.split(x @ w_in, 2, axis=-1)
    u = centred_dwconv(u, conv_w) + conv_b
    ub = u.reshape(b, t, RNN_BLOCKS, RNN_BLOCK)
    uf = u.astype(f32)

    def direction(d, reverse):
        r = jax.nn.sigmoid(jnp.einsum('btnc,ncd->btnd', ub, w_r[d]).reshape(b, t, D_RNN).astype(f32) + b_r[d].astype(f32))
        i = jax.nn.sigmoid(jnp.einsum('btnc,ncd->btnd', ub, w_i[d]).reshape(b, t, D_RNN).astype(f32) + b_i[d].astype(f32))
        log_a = -RNN_C * r * jax.nn.softplus(-lam[d].astype(f32))
        a = jnp.exp(log_a)
        inp = jnp.sqrt(-jnp.expm1(2.0 * log_a)) * (i * uf)
        _, h = lax.associative_scan(_linear_combine, (a, inp), axis=1, reverse=reverse)
        return h

    h = direction(0, False) + direction(1, True)
    return (jax.nn.gelu(gate) * h.astype(x.dtype)) @ w_out


def hier_moe(x, wg, bg, we, be, w1, w3, w2):
    f32 = jnp.float32
    b, t, d = x.shape
    xt = x.reshape(b * t, d)
    gp = jax.nn.softmax((xt @ wg).astype(f32) + bg.astype(f32), axis=-1)
    gval, gidx = lax.top_k(gp, 1)
    gmask = jax.nn.one_hot(gidx[:, 0], N_GROUPS, dtype=f32)
    el = ((xt @ we).astype(f32) + be.astype(f32)).reshape(b * t, N_GROUPS, EXPERTS_PER_GROUP)
    el_sel = jnp.einsum('nge,ng->ne', el, gmask)
    ev, eidx = lax.top_k(jax.nn.softmax(el_sel, axis=-1), TOP_K)
    ev = ev / jnp.sum(ev, axis=-1, keepdims=True)
    ew = jnp.einsum('nke,nk->ne', jax.nn.one_hot(eidx, EXPERTS_PER_GROUP, dtype=f32), ev) * gval
    comb = (gmask[:, :, None] * ew[:, None, :]).astype(x.dtype)
    y = jnp.zeros_like(xt)
    for g in range(N_GROUPS):
        hid = jax.nn.silu(jnp.einsum('nd,edf->nef', xt, w1[g])) * jnp.einsum('nd,edf->nef', xt, w3[g])
        y = y + jnp.einsum('nef,efd->nd', hid * comb[:, g, :, None], w2[g])
    return y.reshape(b, t, d)


def setup_inputs(seed: int = 0) -> dict:
    key = jax.random.key(seed)
    ks = iter(jax.random.split(key, 32))
    f32 = jnp.float32

    def nrm(shape, scale):
        return jax.random.normal(next(ks), shape, f32) * scale

    ne = (DEPTH + 1) // 2
    no = DEPTH // 2
    x = nrm((BATCH, SEQ, D_MODEL), 1.0)
    meta_tokens = nrm((N_META, D_MODEL), 1.0)
    mix_norm = 1.0 + nrm((DEPTH, D_MODEL), 0.02)
    ffn_norm = 1.0 + nrm((DEPTH, D_MODEL), 0.02)
    final_norm = 1.0 + nrm((D_MODEL,), 0.02)

    ev_w_in = nrm((ne, D_MODEL, EVEN_IN), D_MODEL ** -0.5)
    f_bias = jnp.array([0.0, 1.0, 0.0, 1.0], f32)[:, None] * jnp.linspace(3.0, 6.0, A_HEADS, dtype=f32)[None, :]
    ev_gate_bias = nrm((ne, 4, A_HEADS), 0.1) + f_bias
    ev_qk_conv = nrm((ne, CONV_W, 2 * A_QK), CONV_W ** -0.5)
    ev_lr_up = nrm((ne, 2, B_RANK, B_QK), B_RANK ** -0.5)
    ev_lr_bias = nrm((ne, 2, B_QK), 0.1)
    ev_norm_a = 1.0 + nrm((ne, A_V), 0.02)
    ev_norm_b = 1.0 + nrm((ne, B_V), 0.02)
    ev_w_out = nrm((ne, A_V + B_V, D_MODEL), (A_V + B_V) ** -0.5)

    od_w_in = nrm((no, D_MODEL, 2 * D_RNN), D_MODEL ** -0.5)
    od_conv = nrm((no, CONV_W, D_RNN), CONV_W ** -0.5)
    od_conv_bias = nrm((no, D_RNN), 0.02)
    od_w_r = nrm((no, 2, RNN_BLOCKS, RNN_BLOCK, RNN_BLOCK), RNN_BLOCK ** -0.5)
    od_b_r = nrm((no, 2, D_RNN), 0.1)
    od_w_i = nrm((no, 2, RNN_BLOCKS, RNN_BLOCK, RNN_BLOCK), RNN_BLOCK ** -0.5)
    od_b_i = nrm((no, 2, D_RNN), 0.1)
    a_c = jax.random.uniform(next(ks), (no, 2, D_RNN), f32, 0.9, 0.999)
    a0 = a_c ** (1.0 / RNN_C)
    od_lambda = jnp.log(a0) - jnp.log1p(-a0)
    od_w_out = nrm((no, D_RNN, D_MODEL), D_RNN ** -0.5)

    n_exp = N_GROUPS * EXPERTS_PER_GROUP
    moe_wg = nrm((DEPTH, D_MODEL, N_GROUPS), D_MODEL ** -0.5)
    moe_bg = nrm((DEPTH, N_GROUPS), 0.01)
    moe_we = nrm((DEPTH, D_MODEL, n_exp), D_MODEL ** -0.5)
    moe_be = nrm((DEPTH, n_exp), 0.01)
    moe_w1 = nrm((DEPTH, N_GROUPS, EXPERTS_PER_GROUP, D_MODEL, D_EXPERT), D_MODEL ** -0.5)
    moe_w3 = nrm((DEPTH, N_GROUPS, EXPERTS_PER_GROUP, D_MODEL, D_EXPERT), D_MODEL ** -0.5)
    moe_w2 = nrm((DEPTH, N_GROUPS, EXPERTS_PER_GROUP, D_EXPERT, D_MODEL), D_EXPERT ** -0.5)

    return {'x': x, 'meta_tokens': meta_tokens, 'mix_norm': mix_norm, 'ffn_norm': ffn_norm,
            'final_norm': final_norm,
            'ev_w_in': ev_w_in, 'ev_gate_bias': ev_gate_bias, 'ev_qk_conv': ev_qk_conv,
            'ev_lr_up': ev_lr_up, 'ev_lr_bias': ev_lr_bias, 'ev_norm_a': ev_norm_a,
            'ev_norm_b': ev_norm_b, 'ev_w_out': ev_w_out,
            'od_w_in': od_w_in, 'od_conv': od_conv, 'od_conv_bias': od_conv_bias,
            'od_w_r': od_w_r, 'od_b_r': od_b_r, 'od_w_i': od_w_i, 'od_b_i': od_b_i,
            'od_lambda': od_lambda, 'od_w_out': od_w_out,
            'moe_wg': moe_wg, 'moe_bg': moe_bg, 'moe_we': moe_we, 'moe_be': moe_be,
            'moe_w1': moe_w1, 'moe_w3': moe_w3, 'moe_w2': moe_w2}


def reference(x, meta_tokens, mix_norm, ffn_norm, final_norm,
              ev_w_in, ev_gate_bias, ev_qk_conv, ev_lr_up, ev_lr_bias, ev_norm_a, ev_norm_b, ev_w_out,
              od_w_in, od_conv, od_conv_bias, od_w_r, od_b_r, od_w_i, od_b_i, od_lambda, od_w_out,
              moe_wg, moe_bg, moe_we, moe_be, moe_w1, moe_w3, moe_w2):
    b = x.shape[0]
    meta = jnp.broadcast_to(meta_tokens.astype(x.dtype)[None], (b, N_META, D_MODEL))
    h = jnp.concatenate([meta, x], axis=1)
    for layer in range(DEPTH):
        hn = rmsnorm(h, mix_norm[layer])
        if layer % 2 == 0:
            e = layer // 2
            h = h + even_mixer(hn, ev_w_in[e], ev_gate_bias[e], ev_qk_conv[e], ev_lr_up[e],
                               ev_lr_bias[e], ev_norm_a[e], ev_norm_b[e], ev_w_out[e])
        else:
            o = layer // 2
            h = h + odd_mixer(hn, od_w_in[o], od_conv[o], od_conv_bias[o], od_w_r[o], od_b_r[o],
                              od_w_i[o], od_b_i[o], od_lambda[o], od_w_out[o])
        h = h + hier_moe(rmsnorm(h, ffn_norm[layer]), moe_wg[layer], moe_bg[layer], moe_we[layer],
                         moe_be[layer], moe_w1[layer], moe_w3[layer], moe_w2[layer])
    return rmsnorm(h, final_norm)[:, N_META:]
```

```python
import functools

import jax
import jax.numpy as jnp
from jax import lax
from jax.experimental import pallas as pl
from jax.experimental.pallas import tpu as pltpu

F32 = jnp.float32
BF16 = jnp.bfloat16
HIGHEST = lax.Precision.HIGHEST

N_META = 16
CHUNK = 64
PAD = CHUNK - N_META
SUB = 16
EPS = 1e-6
NEG = -1e30
HEADS = 4
GATE_COLS = 4 * HEADS
B_RANK = 16
B_TAU = 16.0
RNN_BLOCKS = 16
RNN_C = 8.0
N_GROUPS = 4
EXPERTS_PER_GROUP = 8
N_EXPERTS = N_GROUPS * EXPERTS_PER_GROUP
LANES = 128
SUBLANES = 8
BF16_ROWS = 16
VMEM_LIMIT = 56 * 1024 * 1024


def _params(*sem):
    return pltpu.CompilerParams(dimension_semantics=sem, vmem_limit_bytes=VMEM_LIMIT)


def _tile(n, target, mult):
    best = None
    for t in range(mult, min(n, target) + 1, mult):
        if n % t == 0:
            best = t
    assert best is not None, (n, target, mult)
    return best


def _log_sigmoid(x):
    return jnp.minimum(x, 0.0) - jnp.log1p(jnp.exp(-jnp.abs(x)))


def _dot(a, b):
    return jnp.dot(a, b, preferred_element_type=F32)


def _dot_t(a, b):
    return lax.dot_general(a, b, (((1,), (1,)), ((), ())), preferred_element_type=F32)


def _tdot(a, b, precision=None):
    return lax.dot_general(a, b, (((0,), (0,)), ((), ())), preferred_element_type=F32,
                           precision=precision)


def _norm_kernel(*refs, has_delta, has_router):
    it = iter(refs)
    h_ref = next(it)
    d_ref = next(it) if has_delta else None
    g_ref = next(it)
    if has_router:
        w2_ref, wh_ref = next(it), next(it)
    hnew_ref = next(it) if has_delta else None
    hn_ref = next(it)
    lg_ref = next(it) if has_router else None

    x = h_ref[...]
    if has_delta:
        x = x + d_ref[...].astype(F32)
        hnew_ref[...] = x
    y = x * lax.rsqrt(jnp.mean(x * x, axis=-1, keepdims=True) + EPS) * g_ref[...]
    yh = y.astype(BF16)
    hn_ref[...] = yh
    if has_router:
        yl = (y - yh.astype(F32)).astype(BF16)
        r1 = _dot(yh, w2_ref[...])
        lg_ref[...] = r1[:, :LANES] + r1[:, LANES:] + _dot(yl, wh_ref[...])


def _norm(h, g, delta=None, router_w=None):
    n, d = h.shape
    tr = _tile(n, 192, BF16_ROWS)
    row = pl.BlockSpec((tr, d), lambda i: (i, 0))
    in_specs, args = [row], [h]
    if delta is not None:
        in_specs.append(row)
        args.append(delta)
    in_specs.append(pl.BlockSpec((1, d), lambda i: (0, 0)))
    args.append(g.reshape(1, d).astype(F32))
    if router_w is not None:
        wh = router_w.astype(BF16)
        wl = (router_w - wh.astype(F32)).astype(BF16)
        in_specs += [pl.BlockSpec((d, 2 * LANES), lambda i: (0, 0)), pl.BlockSpec((d, LANES), lambda i: (0, 0))]
        args += [jnp.concatenate([wh, wl], axis=1), wh]
    out_specs, out_shape = [], []
    if delta is not None:
        out_specs.append(row)
        out_shape.append(jax.ShapeDtypeStruct((n, d), F32))
    out_specs.append(row)
    out_shape.append(jax.ShapeDtypeStruct((n, d), BF16))
    if router_w is not None:
        out_specs.append(pl.BlockSpec((tr, LANES), lambda i: (i, 0)))
        out_shape.append(jax.ShapeDtypeStruct((n, LANES), F32))
    outs = pl.pallas_call(
        functools.partial(_norm_kernel, has_delta=delta is not None, has_router=router_w is not None),
        grid=(n // tr,), in_specs=in_specs, out_specs=out_specs, out_shape=out_shape,
        compiler_params=_params("parallel"), name="norm")(*args)
    outs = list(outs)
    h_new = outs.pop(0) if delta is not None else None
    hn = outs.pop(0)
    lg = outs.pop(0) if router_w is not None else None
    return h_new, hn, lg


def _final_norm_kernel(h_ref, d_ref, g_ref, o_ref):
    x = h_ref[...] + d_ref[...].astype(F32)
    o_ref[...] = x * lax.rsqrt(jnp.mean(x * x, axis=-1, keepdims=True) + EPS) * g_ref[...]


def _final_norm(h, delta, g, batch, seq):
    n, d = h.shape
    nc = n // batch // CHUNK
    src = pl.BlockSpec((CHUNK, d), lambda b, i: (b * nc + 1 + i, 0))
    return pl.pallas_call(
        _final_norm_kernel, grid=(batch, nc - 1),
        in_specs=[src, src, pl.BlockSpec((1, d), lambda b, i: (0, 0))],
        out_specs=pl.BlockSpec((CHUNK, d), lambda b, i: (b * (nc - 1) + i, 0)),
        out_shape=jax.ShapeDtypeStruct((batch * seq, d), F32),
        compiler_params=_params("parallel", "parallel"), name="final_norm")(h, delta, g.reshape(1, d).astype(F32))


def _mm_kernel(*refs, has_res):
    if has_res:
        a_ref, w_ref, r_ref, o_ref = refs
    else:
        a_ref, w_ref, o_ref = refs
    acc = _dot(a_ref[...], w_ref[...])
    if has_res:
        acc = acc + r_ref[...]
    o_ref[...] = acc.astype(o_ref.dtype)


def _matmul(a, w, out_dtype, res=None, tm_target=688, tn_target=512):
    n, k = a.shape
    m = w.shape[1]
    tm = _tile(n, tm_target, BF16_ROWS)
    tn = _tile(m, tn_target, LANES)
    in_specs = [pl.BlockSpec((tm, k), lambda i, j: (i, 0)), pl.BlockSpec((k, tn), lambda i, j: (0, j))]
    args = [a, w]
    if res is not None:
        in_specs.append(pl.BlockSpec((tm, tn), lambda i, j: (i, j)))
        args.append(res)
    return pl.pallas_call(
        functools.partial(_mm_kernel, has_res=res is not None),
        grid=(n // tm, m // tn), in_specs=in_specs,
        out_specs=pl.BlockSpec((tm, tn), lambda i, j: (i, j)),
        out_shape=jax.ShapeDtypeStruct((n, m), out_dtype),
        compiler_params=_params("parallel", "arbitrary"), name="matmul")(*args)


def _conv_taps(ext_s, cw_ref, tt):
    out = cw_ref[0:1, :] * ext_s[pl.ds(SUBLANES - 2, tt), :]
    for j in range(1, 4):
        out = out + cw_ref[j:j + 1, :] * ext_s[pl.ds(SUBLANES - 2 + j, tt), :]
    return out


def _fill_ext(ext_s, cur_ref, prev_ref, next_ref, row0, tt, has_next):
    rows = row0 + lax.broadcasted_iota(jnp.int32, (tt, 1), 0)
    ext_s[pl.ds(SUBLANES, tt), :] = jnp.where(rows >= PAD, cur_ref[...].astype(F32), 0.0)
    prow = row0 - SUBLANES + lax.broadcasted_iota(jnp.int32, (SUBLANES, 1), 0)
    ext_s[pl.ds(0, SUBLANES), :] = jnp.where(prow >= PAD, prev_ref[...].astype(F32)[SUBLANES:, :], 0.0)
    ext_s[pl.ds(SUBLANES + tt, SUBLANES), :] = jnp.where(has_next, next_ref[...].astype(F32)[:SUBLANES, :], 0.0)


def _qkconv_kernel(cur_ref, prev_ref, next_ref, cw_ref, o_ref, ext_s, *, tt, nt, kscale, half):
    t = pl.program_id(1)
    row0 = t * tt
    _fill_ext(ext_s, cur_ref, prev_ref, next_ref, row0, tt, t < nt - 1)
    y = _conv_taps(ext_s, cw_ref, tt)
    y = y * jax.nn.sigmoid(y)
    col = lax.broadcasted_iota(jnp.int32, (1, 2 * half), 1)
    y = y * jnp.where(col >= half, kscale, 1.0)
    rows = row0 + lax.broadcasted_iota(jnp.int32, (tt, 1), 0)
    o_ref[...] = jnp.where(rows >= PAD, y, 0.0).astype(o_ref.dtype)


def _halo_specs(width, col_block, tt, tp, n):
    per_b, per_t = tp // BF16_ROWS, tt // BF16_ROWS
    last = n // BF16_ROWS - 1

    def make(tmap):
        cur = pl.BlockSpec((tt, width), lambda b, t, *_: (b * (tp // tt) + tmap(t), col_block(*_)))
        prev = pl.BlockSpec((BF16_ROWS, width),
                            lambda b, t, *_: (jnp.maximum(b * per_b + tmap(t) * per_t - 1, 0), col_block(*_)))
        nxt = pl.BlockSpec((BF16_ROWS, width),
                           lambda b, t, *_: (jnp.minimum(b * per_b + (tmap(t) + 1) * per_t, last), col_block(*_)))
        return cur, prev, nxt
    return make


def _qk_conv(proj, conv_w, batch, dk):
    n = proj.shape[0]
    tp = n // batch
    width = 2 * HEADS * dk
    tt = _tile(tp, 688, BF16_ROWS)
    nt = tp // tt
    cur, prev, nxt = _halo_specs(width, lambda: 0, tt, tp, n)(lambda t: t)
    return pl.pallas_call(
        functools.partial(_qkconv_kernel, tt=tt, nt=nt, kscale=dk ** -0.5, half=HEADS * dk),
        grid=(batch, nt),
        in_specs=[cur, prev, nxt, pl.BlockSpec((4, width), lambda b, t: (0, 0))],
        out_specs=pl.BlockSpec((tt, width), lambda b, t: (b * nt + t, 0)),
        out_shape=jax.ShapeDtypeStruct((n, width), BF16),
        scratch_shapes=[pltpu.VMEM((tt + 2 * SUBLANES, width), F32)],
        compiler_params=_params("parallel", "parallel"), name="qk_conv")(proj, proj, proj, conv_w.astype(F32))


def _mlstm_kernel(q_ref, k_ref, v_ref, g_ref, gt_ref, gb_ref, gbt_ref, o_ref, c_s, m_s, *, reverse, dk, dv, nc):
    step = pl.program_id(1)

    @pl.when(step == 0)
    def _():
        c_s[...] = jnp.zeros_like(c_s)
        m_s[...] = jnp.zeros_like(m_s)

    chunk = nc - 1 - step if reverse else step
    L = CHUNK
    real = chunk > 0
    valid_c = jnp.logical_or(real, lax.broadcasted_iota(jnp.int32, (L, 1), 0) >= PAD)
    valid_r = jnp.logical_or(real, lax.broadcasted_iota(jnp.int32, (1, L), 1) >= PAD)
    off = 2 * HEADS if reverse else 0
    g = g_ref[:, :GATE_COLS] + gb_ref[...]
    gt = gt_ref[0] + gbt_ref[...]
    li_c = jnp.where(valid_c, g[:, off:off + HEADS], NEG)
    lf_c = jnp.where(valid_c, _log_sigmoid(g[:, off + HEADS:off + 2 * HEADS]), 0.0)
    li_r = jnp.where(valid_r, gt[off:off + HEADS, :], NEG)
    lf_r = jnp.where(valid_r, _log_sigmoid(gt[off + HEADS:off + 2 * HEADS, :]), 0.0)
    ri = lax.broadcasted_iota(jnp.int32, (L, L), 0)
    ci = lax.broadcasted_iota(jnp.int32, (L, L), 1)
    mask = (ci >= ri) if reverse else (ci <= ri)
    cum_c = jnp.dot(mask.astype(F32), lf_c, precision=HIGHEST, preferred_element_type=F32)
    inc = (ri >= ci) if reverse else (ri <= ci)
    cum_r = jnp.dot(lf_r, inc.astype(F32), precision=HIGHEST, preferred_element_type=F32)
    last = 0 if reverse else L - 1
    ones_col = jnp.where(lax.broadcasted_iota(jnp.int32, (L, LANES), 1) == 0, 1.0, 0.0).astype(BF16)

    for h in range(HEADS):
        cc = cum_c[:, h:h + 1]
        cr = cum_r[h:h + 1, :]
        lic = li_c[:, h:h + 1]
        lir = li_r[h:h + 1, :]
        tot = cc[last:last + 1, :]
        m = m_s[h, 0:1, 0:1]
        qh = q_ref[:, h * dk:(h + 1) * dk]
        kh = k_ref[:, h * dk:(h + 1) * dk]
        vh = jnp.where(valid_c, v_ref[:, h * dv:(h + 1) * dv], 0.0).astype(BF16)
        vaug = jnp.concatenate([vh, ones_col], axis=1)

        d_mat = jnp.where(mask, cc - cr + lir, NEG)
        inter = cc + m
        m_t = jnp.maximum(inter, jnp.max(d_mat, axis=1, keepdims=True))
        w_inter = jnp.exp(inter - m_t)
        s = _dot_t(qh, kh) * jnp.exp(d_mat - m_t)
        haug = w_inter * _dot(qh, c_s[h].astype(BF16)) + _dot(s.astype(BF16), vaug)
        den = haug[:, dv:dv + 1]
        o_ref[:, h * dv:(h + 1) * dv] = (haug[:, :dv] / jnp.maximum(jnp.abs(den), jnp.exp(-m_t))).astype(o_ref.dtype)

        gs = tot - cc + lic
        m_new = jnp.maximum(tot + m, jnp.max(gs, axis=0, keepdims=True))
        decay = jnp.exp(tot + m - m_new)
        ks = (kh.astype(F32) * jnp.exp(gs - m_new)).astype(BF16)
        c_s[h] = decay * c_s[h] + _tdot(ks, vaug)
        m_s[h] = jnp.broadcast_to(m_new, m_s.shape[1:])


def _mlstm(qk, proj, gates, gates_t, gate_bias, batch, dk, dv, v_block, reverse):
    n = qk.shape[0]
    nc = n // batch // CHUNK
    cmap = (lambda b, i: b * nc + nc - 1 - i) if reverse else (lambda b, i: b * nc + i)
    gb = gate_bias.reshape(1, GATE_COLS).astype(F32)
    return pl.pallas_call(
        functools.partial(_mlstm_kernel, reverse=reverse, dk=dk, dv=dv, nc=nc),
        grid=(batch, nc),
        in_specs=[pl.BlockSpec((CHUNK, HEADS * dk), lambda b, i: (cmap(b, i), 0)),
                  pl.BlockSpec((CHUNK, HEADS * dk), lambda b, i: (cmap(b, i), 1)),
                  pl.BlockSpec((CHUNK, HEADS * dv), lambda b, i: (cmap(b, i), v_block)),
                  pl.BlockSpec((CHUNK, LANES), lambda b, i: (cmap(b, i), 0)),
                  pl.BlockSpec((1, GATE_COLS, CHUNK), lambda b, i: (cmap(b, i), 0, 0)),
                  pl.BlockSpec((1, GATE_COLS), lambda b, i: (0, 0)),
                  pl.BlockSpec((GATE_COLS, 1), lambda b, i: (0, 0))],
        out_specs=pl.BlockSpec((CHUNK, HEADS * dv), lambda b, i: (cmap(b, i), 0)),
        out_shape=jax.ShapeDtypeStruct((n, HEADS * dv), BF16),
        scratch_shapes=[pltpu.VMEM((HEADS, dk, dv + LANES), F32), pltpu.VMEM((HEADS, SUBLANES, LANES), F32)],
        compiler_params=_params("parallel", "arbitrary"),
        name="mlstm_bwd" if reverse else "mlstm_fwd")(qk, qk, proj, gates, gates_t, gb, gb.reshape(GATE_COLS, 1))


def _gla_kernel(q_ref, k_ref, v_ref, lr_ref, up_ref, ub_ref, o_ref, s_s, *, reverse, dk, dv, nc):
    step = pl.program_id(1)

    @pl.when(step == 0)
    def _():
        s_s[...] = jnp.zeros_like(s_s)

    chunk = nc - 1 - step if reverse else step
    L = CHUNK
    valid_c = jnp.logical_or(chunk > 0, lax.broadcasted_iota(jnp.int32, (L, 1), 0) >= PAD)
    off = GATE_COLS + (B_RANK if reverse else 0)
    z = jnp.dot(lr_ref[:, off:off + B_RANK], up_ref[...], precision=HIGHEST, preferred_element_type=F32)
    la = jnp.where(valid_c, _log_sigmoid(z + ub_ref[...]) / B_TAU, 0.0)
    ri = lax.broadcasted_iota(jnp.int32, (L, L), 0)
    ci = lax.broadcasted_iota(jnp.int32, (L, L), 1)
    mask = (ci >= ri) if reverse else (ci <= ri)
    cum = jnp.dot(mask.astype(F32), la, precision=HIGHEST, preferred_element_type=F32)
    tot_col = _tdot(la, jnp.ones((L, LANES), F32), precision=HIGHEST)
    last = 0 if reverse else L - 1
    nsub = L // SUB
    sub_lane = lax.broadcasted_iota(jnp.int32, (SUB, L), 1)
    sub_row = lax.broadcasted_iota(jnp.int32, (SUB, 1), 0)

    for h in range(HEADS):
        sl = slice(h * dk, (h + 1) * dk)
        q = jnp.where(valid_c, q_ref[:, sl], 0.0).astype(F32) * dk ** -0.5
        k = jnp.where(valid_c, k_ref[:, sl], 0.0).astype(F32)
        v = jnp.where(valid_c, v_ref[:, h * dv:(h + 1) * dv], 0.0).astype(BF16)
        cumh = cum[:, sl]
        tot = cumh[last:last + 1, :]
        state = s_s[h]
        o_inter = _dot((q * jnp.exp(cumh)).astype(BF16), state.astype(BF16))

        for blk in range(nsub):
            r0 = blk * SUB
            if reverse:
                cs = cumh[r0 + SUB:r0 + SUB + 1, :] if blk < nsub - 1 else jnp.zeros((1, dk), F32)
                earlier = sub_lane >= r0 + SUB
            else:
                cs = cumh[r0 - 1:r0, :] if blk > 0 else jnp.zeros((1, dk), F32)
                earlier = sub_lane < r0
            q_b = q[r0:r0 + SUB, :]
            cum_b = cumh[r0:r0 + SUB, :]
            qd = (q_b * jnp.exp(cum_b - cs)).astype(BF16)
            kd = (k * jnp.exp(jnp.minimum(cs - cumh, 0.0))).astype(BF16)
            att = jnp.where(earlier, _dot_t(qd, kd), 0.0)
            for j in range(SUB):
                s_idx = r0 + j
                tmask = (sub_row <= j) if reverse else (sub_row >= j)
                e = jnp.where(tmask, cum_b - cumh[s_idx:s_idx + 1, :], NEG)
                col = jnp.sum(q_b * k[s_idx:s_idx + 1, :] * jnp.exp(e), axis=1, keepdims=True)
                att = jnp.where(sub_lane == s_idx, col, att)
            o_b = o_inter[r0:r0 + SUB, :] + _dot(att.astype(BF16), v)
            o_ref[r0:r0 + SUB, h * dv:(h + 1) * dv] = o_b.astype(o_ref.dtype)

        kdec = (k * jnp.exp(tot - cumh)).astype(BF16)
        s_s[h] = jnp.exp(tot_col[sl, 0:1]) * state + _tdot(kdec, v)


def _gla(proj, small, lr_up, lr_bias, batch, dk, dv, qkv_blocks, reverse):
    n = proj.shape[0]
    nc = n // batch // CHUNK
    cmap = (lambda b, i: b * nc + nc - 1 - i) if reverse else (lambda b, i: b * nc + i)
    qb, kb, vb = qkv_blocks
    return pl.pallas_call(
        functools.partial(_gla_kernel, reverse=reverse, dk=dk, dv=dv, nc=nc),
        grid=(batch, nc),
        in_specs=[pl.BlockSpec((CHUNK, HEADS * dk), lambda b, i: (cmap(b, i), qb)),
                  pl.BlockSpec((CHUNK, HEADS * dk), lambda b, i: (cmap(b, i), kb)),
                  pl.BlockSpec((CHUNK, HEADS * dv), lambda b, i: (cmap(b, i), vb)),
                  pl.BlockSpec((CHUNK, LANES), lambda b, i: (cmap(b, i), 0)),
                  pl.BlockSpec((B_RANK, HEADS * dk), lambda b, i: (0, 0)),
                  pl.BlockSpec((1, HEADS * dk), lambda b, i: (0, 0))],
        out_specs=pl.BlockSpec((CHUNK, HEADS * dv), lambda b, i: (cmap(b, i), 0)),
        out_shape=jax.ShapeDtypeStruct((n, HEADS * dv), BF16),
        scratch_shapes=[pltpu.VMEM((HEADS, dk, dv), F32)],
        compiler_params=_params("parallel", "arbitrary"),
        name="gla_bwd" if reverse else "gla_fwd")(
            proj, proj, proj, small, lr_up.astype(F32), lr_bias.reshape(1, -1).astype(F32))


def _head_norm(x, g, dv):
    parts = []
    for h in range(HEADS):
        xh = x[:, h * dv:(h + 1) * dv]
        parts.append(xh * lax.rsqrt(jnp.mean(xh * xh, axis=-1, keepdims=True) + EPS))
    return jnp.concatenate(parts, axis=1) * g


def _even_combine_kernel(af_ref, ab_ref, bf_ref, bb_ref, oa_ref, gb_ref, na_ref, nb_ref, o_ref, *, tr, tp, dv):
    rows = (pl.program_id(0) * tr) % tp + lax.broadcasted_iota(jnp.int32, (tr, 1), 0)
    valid = rows >= PAD
    w = HEADS * dv
    ha = af_ref[...].astype(F32) + ab_ref[...].astype(F32)
    ya = jax.nn.sigmoid(oa_ref[...].astype(F32)) * _head_norm(ha, na_ref[...], dv)
    o_ref[:, :w] = jnp.where(valid, ya, 0.0).astype(o_ref.dtype)
    hb = bf_ref[...].astype(F32) + bb_ref[...].astype(F32)
    gb = gb_ref[...].astype(F32)
    yb = gb * jax.nn.sigmoid(gb) * _head_norm(hb, nb_ref[...], dv)
    o_ref[:, w:] = jnp.where(valid, yb, 0.0).astype(o_ref.dtype)


def _even_combine(ha_f, ha_b, hb_f, hb_b, proj, norm_a, norm_b, batch, dv, oa_block, gb_block):
    n, w = ha_f.shape
    tp = n // batch
    tr = _tile(tp, 384, BF16_ROWS)
    row = pl.BlockSpec((tr, w), lambda i: (i, 0))
    vec = pl.BlockSpec((1, w), lambda i: (0, 0))
    return pl.pallas_call(
        functools.partial(_even_combine_kernel, tr=tr, tp=tp, dv=dv),
        grid=(n // tr,),
        in_specs=[row, row, row, row, pl.BlockSpec((tr, w), lambda i: (i, oa_block)),
                  pl.BlockSpec((tr, w), lambda i: (i, gb_block)), vec, vec],
        out_specs=pl.BlockSpec((tr, 2 * w), lambda i: (i, 0)),
        out_shape=jax.ShapeDtypeStruct((n, 2 * w), BF16),
        compiler_params=_params("parallel"), name="even_combine")(
            ha_f, ha_b, hb_f, hb_b, proj, proj, norm_a.reshape(1, w).astype(F32), norm_b.reshape(1, w).astype(F32))


def _block_scan(a, b, reverse):
    sub = lax.broadcasted_iota(jnp.int32, a.shape, 1)
    for k in (1, 2, 4):
        if reverse:
            a_sh, b_sh, m = pltpu.roll(a, SUBLANES - k, 1), pltpu.roll(b, SUBLANES - k, 1), sub < SUBLANES - k
        else:
            a_sh, b_sh, m = pltpu.roll(a, k, 1), pltpu.roll(b, k, 1), sub >= k
        b = jnp.where(m, a * b_sh + b, b)
        a = jnp.where(m, a * a_sh, a)
    return a, b


def _rglru_kernel(cur_ref, prev_ref, next_ref, cw_ref, cb_ref, wr_ref, br_ref, wi_ref, bi_ref, lam_ref,
                  o_ref, ext_s, a_s, b_s, h_s, carry_s, *, reverse, tt, nt):
    step = pl.program_id(2)

    @pl.when(step == 0)
    def _():
        carry_s[...] = jnp.zeros_like(carry_s)

    t = nt - 1 - step if reverse else step
    row0 = t * tt
    _fill_ext(ext_s, cur_ref, prev_ref, next_ref, row0, tt, t < nt - 1)
    u = _conv_taps(ext_s, cw_ref, tt) + cb_ref[...]
    ub = u.astype(BF16)
    r = jax.nn.sigmoid(_dot(ub, wr_ref[0]) + br_ref[...])
    gi = jax.nn.sigmoid(_dot(ub, wi_ref[0]) + bi_ref[...])
    lam = lam_ref[...]
    softplus = jnp.maximum(-lam, 0.0) + jnp.log1p(jnp.exp(-jnp.abs(lam)))
    log_a = -RNN_C * r * softplus
    a = jnp.exp(log_a)
    rows = row0 + lax.broadcasted_iota(jnp.int32, (tt, 1), 0)
    inp = jnp.where(rows >= PAD, jnp.sqrt(1.0 - a * a) * (gi * u), 0.0)
    c = a.shape[1]
    ng = tt // SUBLANES
    a_g, b_g = _block_scan(a.reshape(ng, SUBLANES, c), inp.reshape(ng, SUBLANES, c), reverse)
    a_s[...] = a_g.reshape(tt, c)
    b_s[...] = b_g.reshape(tt, c)
    out_row = 0 if reverse else SUBLANES - 1

    def body(i, carry):
        g = ng - 1 - i if reverse else i
        r0 = pl.multiple_of(g * SUBLANES, SUBLANES)
        hh = b_s[pl.ds(r0, SUBLANES), :] + a_s[pl.ds(r0, SUBLANES), :] * carry
        h_s[pl.ds(r0, SUBLANES), :] = hh
        return hh[out_row:out_row + 1, :]

    carry_s[...] = lax.fori_loop(0, ng, body, carry_s[...])
    o_ref[...] = h_s[...].astype(o_ref.dtype)


def _pair_blocks(w):
    nb, r, _ = w.shape
    z = jnp.zeros((nb // 2, r, r), w.dtype)
    top = jnp.concatenate([w[0::2], z], axis=2)
    bot = jnp.concatenate([z, w[1::2]], axis=2)
    return jnp.concatenate([top, bot], axis=1).astype(BF16)


def _rglru(proj, conv_w, conv_b, w_r, b_r, w_i, b_i, lam, batch, d_rnn, reverse):
    n = proj.shape[0]
    tp = n // batch
    cw = 2 * d_rnn // RNN_BLOCKS
    ncb = d_rnn // cw
    tt = _tile(tp, 688, BF16_ROWS)
    nt = tp // tt
    tmap = (lambda t: nt - 1 - t) if reverse else (lambda t: t)
    cur, prev, nxt = _halo_specs(cw, lambda j: ncb + j, tt, tp, n)(tmap)
    def swap(spec):
        f = spec.index_map
        return pl.BlockSpec(spec.block_shape, lambda b, j, t: f(b, t, j))
    vec = pl.BlockSpec((1, cw), lambda b, j, t: (0, j))
    wspec = pl.BlockSpec((1, cw, cw), lambda b, j, t: (j, 0, 0))
    row = lambda x: x.reshape(1, d_rnn).astype(F32)
    return pl.pallas_call(
        functools.partial(_rglru_kernel, reverse=reverse, tt=tt, nt=nt),
        grid=(batch, ncb, nt),
        in_specs=[swap(cur), swap(prev), swap(nxt), pl.BlockSpec((4, cw), lambda b, j, t: (0, j)), vec,
                  wspec, vec, wspec, vec, vec],
        out_specs=pl.BlockSpec((tt, cw), lambda b, j, t: (b * nt + tmap(t), j)),
        out_shape=jax.ShapeDtypeStruct((n, d_rnn), BF16),
        scratch_shapes=[pltpu.VMEM((tt + 2 * SUBLANES, cw), F32), pltpu.VMEM((tt, cw), F32),
                        pltpu.VMEM((tt, cw), F32), pltpu.VMEM((tt, cw), F32), pltpu.VMEM((1, cw), F32)],
        compiler_params=_params("parallel", "parallel", "arbitrary"),
        name="rglru_bwd" if reverse else "rglru_fwd")(
            proj, proj, proj, conv_w.astype(F32), row(conv_b), _pair_blocks(w_r), row(b_r),
            _pair_blocks(w_i), row(b_i), row(lam))


def _odd_combine_kernel(g_ref, hf_ref, hb_ref, o_ref, *, tr, tp):
    rows = (pl.program_id(0) * tr) % tp + lax.broadcasted_iota(jnp.int32, (tr, 1), 0)
    y = jax.nn.gelu(g_ref[...].astype(F32)) * (hf_ref[...].astype(F32) + hb_ref[...].astype(F32))
    o_ref[...] = jnp.where(rows >= PAD, y, 0.0).astype(o_ref.dtype)


def _odd_combine(proj, hf, hb, batch):
    n, w = hf.shape
    tp = n // batch
    tr = _tile(tp, 384, BF16_ROWS)
    row = pl.BlockSpec((tr, w), lambda i: (i, 0))
    return pl.pallas_call(
        functools.partial(_odd_combine_kernel, tr=tr, tp=tp), grid=(n // tr,),
        in_specs=[row, row, row], out_specs=row, out_shape=jax.ShapeDtypeStruct((n, w), BF16),
        compiler_params=_params("parallel"), name="odd_combine")(proj, hf, hb)


def _router_kernel(lg_ref, b_ref, o_ref):
    x = lg_ref[...] + b_ref[...]
    lane = lax.broadcasted_iota(jnp.int32, x.shape, 1)
    big = jnp.int32(2 * LANES)
    gmask = lane < N_GROUPS
    gmax = jnp.max(jnp.where(gmask, x, -jnp.inf), axis=1, keepdims=True)
    ge = jnp.where(gmask, jnp.exp(x - gmax), 0.0)
    gp = ge / jnp.sum(ge, axis=1, keepdims=True)
    gval = jnp.max(gp, axis=1, keepdims=True)
    gidx = jnp.min(jnp.where(jnp.logical_and(gmask, gp == gval), lane, big), axis=1, keepdims=True)
    lo = N_GROUPS + gidx * EXPERTS_PER_GROUP
    emask = jnp.logical_and(lane >= lo, lane < lo + EXPERTS_PER_GROUP)
    emax = jnp.max(jnp.where(emask, x, -jnp.inf), axis=1, keepdims=True)
    ee = jnp.where(emask, jnp.exp(x - emax), 0.0)
    ep = ee / jnp.sum(ee, axis=1, keepdims=True)
    v1 = jnp.max(jnp.where(emask, ep, -1.0), axis=1, keepdims=True)
    i1 = jnp.min(jnp.where(jnp.logical_and(emask, ep == v1), lane, big), axis=1, keepdims=True)
    rest = jnp.logical_and(emask, lane != i1)
    v2 = jnp.max(jnp.where(rest, ep, -1.0), axis=1, keepdims=True)
    i2 = jnp.min(jnp.where(jnp.logical_and(rest, ep == v2), lane, big), axis=1, keepdims=True)
    tot = v1 + v2
    o_ref[...] = jnp.where(lane == i1, v1 / tot * gval, jnp.where(lane == i2, v2 / tot * gval, 0.0))


def _router(logits, bg, be):
    n = logits.shape[0]
    tr = _tile(n, 1376, SUBLANES)
    bias = jnp.concatenate([bg.astype(F32), be.astype(F32), jnp.zeros((LANES - N_GROUPS - N_EXPERTS,), F32)])
    row = pl.BlockSpec((tr, LANES), lambda i: (i, 0))
    return pl.pallas_call(
        _router_kernel, grid=(n // tr,), in_specs=[row, pl.BlockSpec((1, LANES), lambda i: (0, 0))],
        out_specs=row, out_shape=jax.ShapeDtypeStruct((n, LANES), F32),
        compiler_params=_params("parallel"), name="router")(logits, bias.reshape(1, LANES))


def _moe_kernel(a_ref, c_ref, w1_ref, w3_ref, w2_ref, o_ref, acc_s):
    e = pl.program_id(1)

    @pl.when(e == 0)
    def _():
        acc_s[...] = jnp.zeros_like(acc_s)

    a = a_ref[...]
    h1 = _dot(a, w1_ref[0])
    h3 = _dot(a, w3_ref[0])
    comb = c_ref[...]
    lane = lax.broadcasted_iota(jnp.int32, comb.shape, 1)
    c = jnp.sum(jnp.where(lane == e + N_GROUPS, comb, 0.0), axis=1, keepdims=True)
    hid = (h1 * jax.nn.sigmoid(h1) * h3 * c).astype(BF16)
    acc_s[...] += _dot(hid, w2_ref[0])

    @pl.when(e == pl.num_programs(1) - 1)
    def _():
        o_ref[...] = acc_s[...].astype(o_ref.dtype)


def _moe(xn, comb, w1, w3, w2):
    n, d = xn.shape
    ne, _, f = w1.shape
    tm = _tile(n, 688, BF16_ROWS)
    return pl.pallas_call(
        _moe_kernel, grid=(n // tm, ne),
        in_specs=[pl.BlockSpec((tm, d), lambda i, e: (i, 0)), pl.BlockSpec((tm, LANES), lambda i, e: (i, 0)),
                  pl.BlockSpec((1, d, f), lambda i, e: (e, 0, 0)), pl.BlockSpec((1, d, f), lambda i, e: (e, 0, 0)),
                  pl.BlockSpec((1, f, d), lambda i, e: (e, 0, 0))],
        out_specs=pl.BlockSpec((tm, d), lambda i, e: (i, 0)),
        out_shape=jax.ShapeDtypeStruct((n, d), BF16),
        scratch_shapes=[pltpu.VMEM((tm, d), F32)],
        compiler_params=_params("parallel", "arbitrary"), name="moe")(xn, comb, w1, w3, w2)


def _moe_layer(h, ffn_g, wg, bg, we, be, w1, w3, w2):
    d = h.shape[1]
    wr = jnp.concatenate([wg, we, jnp.zeros((d, LANES - N_GROUPS - N_EXPERTS), F32)], axis=1)
    _, xn, logits = _norm(h, ffn_g, router_w=wr)
    comb = _router(logits, bg, be)
    f = w1.shape[-1]
    y = _moe(xn, comb, w1.reshape(N_EXPERTS, d, f).astype(BF16), w3.reshape(N_EXPERTS, d, f).astype(BF16),
             w2.reshape(N_EXPERTS, f, d).astype(BF16))
    return y


def _even_layer(h, delta, g, w_in, gate_bias, qk_conv, lr_up, lr_bias, norm_a, norm_b, w_out, batch):
    n, d = h.shape
    dk, dv = d // 16, d // 8
    qk_w, v_w = HEADS * dk, HEADS * dv
    a_end = 2 * qk_w + 2 * v_w
    b_start = a_end + GATE_COLS
    b_end = b_start + 2 * qk_w + 2 * v_w
    w_main = jnp.concatenate([w_in[:, :a_end], w_in[:, b_start:b_end]], axis=1).astype(BF16)
    w_small = jnp.concatenate([w_in[:, a_end:b_start], w_in[:, b_end:],
                               jnp.zeros((d, LANES - GATE_COLS - 2 * B_RANK), F32)], axis=1).astype(BF16)
    if delta is None:
        _, hn, _ = _norm(h, g)
    else:
        h, hn, _ = _norm(h, g, delta=delta)
    proj = _matmul(hn, w_main, BF16)
    small = _matmul(hn, w_small, F32, tn_target=LANES)
    gates_t = small[:, :GATE_COLS].reshape(n // CHUNK, CHUNK, GATE_COLS).transpose(0, 2, 1)
    qk = _qk_conv(proj, qk_conv, batch, dk)
    va_blk, oa_blk = 2 * qk_w // v_w, (2 * qk_w + v_w) // v_w
    b0 = a_end
    qb_blk, kb_blk = b0 // qk_w, (b0 + qk_w) // qk_w
    vb_blk, gb_blk = (b0 + 2 * qk_w) // v_w, (b0 + 2 * qk_w + v_w) // v_w
    ha, hb = [], []
    for rev in (False, True):
        ha.append(_mlstm(qk, proj, small, gates_t, gate_bias, batch, dk, dv, va_blk, rev))
        hb.append(_gla(proj, small, lr_up[int(rev)], lr_bias[int(rev)], batch, dk, dv, (qb_blk, kb_blk, vb_blk), rev))
    y = _even_combine(ha[0], ha[1], hb[0], hb[1], proj, norm_a, norm_b, batch, dv, oa_blk, gb_blk)
    return _matmul(y, w_out.astype(BF16), F32, res=h)


def _odd_layer(h, delta, g, w_in, conv_w, conv_b, w_r, b_r, w_i, b_i, lam, w_out, batch):
    d_rnn = w_out.shape[0]
    h, hn, _ = _norm(h, g, delta=delta)
    proj = _matmul(hn, w_in.astype(BF16), BF16)
    hs = [_rglru(proj, conv_w, conv_b, w_r[i], b_r[i], w_i[i], b_i[i], lam[i], batch, d_rnn, bool(i)) for i in (0, 1)]
    y = _odd_combine(proj, hs[0], hs[1], batch)
    return _matmul(y, w_out.astype(BF16), F32, res=h)


def kernel(x, meta_tokens, mix_norm, ffn_norm, final_norm, ev_w_in, ev_gate_bias, ev_qk_conv, ev_lr_up, ev_lr_bias, ev_norm_a, ev_norm_b, ev_w_out, od_w_in, od_conv, od_conv_bias, od_w_r, od_b_r, od_w_i, od_b_i, od_lambda, od_w_out, moe_wg, moe_bg, moe_we, moe_be, moe_w1, moe_w3, moe_w2):
    batch, seq, d = x.shape
    depth = mix_norm.shape[0]
    assert seq % CHUNK == 0 and d % 16 == 0
    frame = jnp.concatenate([jnp.zeros((batch, PAD, d), x.dtype),
                             jnp.broadcast_to(meta_tokens.astype(x.dtype)[None], (batch, N_META, d)), x], axis=1)
    h = frame.reshape(batch * (PAD + N_META + seq), d)
    delta = None
    for layer in range(depth):
        if layer % 2 == 0:
            e = layer // 2
            h = _even_layer(h, delta, mix_norm[layer], ev_w_in[e], ev_gate_bias[e], ev_qk_conv[e], ev_lr_up[e],
                            ev_lr_bias[e], ev_norm_a[e], ev_norm_b[e], ev_w_out[e], batch)
        else:
            o = layer // 2
            h = _odd_layer(h, delta, mix_norm[layer], od_w_in[o], od_conv[o], od_conv_bias[o], od_w_r[o], od_b_r[o],
                           od_w_i[o], od_b_i[o], od_lambda[o], od_w_out[o], batch)
        delta = _moe_layer(h, ffn_norm[layer], moe_wg[layer], moe_bg[layer], moe_we[layer], moe_be[layer],
                              moe_w1[layer], moe_w3[layer], moe_w2[layer])
    out = _final_norm(h, delta, final_norm, batch, seq)
    return out.reshape(batch, seq, d)
```

```python
import functools

import jax
import jax.numpy as jnp
from jax import lax
from jax.experimental import pallas as pl
from jax.experimental.pallas import tpu as pltpu

F32 = jnp.float32
BF16 = jnp.bfloat16
HIGHEST = lax.Precision.HIGHEST

N_META = 16
CHUNK = 64
PAD = CHUNK - N_META
SUB = 16
EPS = 1e-6
NEG = -1e30
HEADS = 4
GATE_COLS = 4 * HEADS
B_RANK = 16
B_TAU = 16.0
RNN_BLOCKS = 16
RNN_C = 8.0
N_GROUPS = 4
EXPERTS_PER_GROUP = 8
N_EXPERTS = N_GROUPS * EXPERTS_PER_GROUP
LANES = 128
SUBLANES = 8
BF16_ROWS = 16
VMEM_LIMIT = 56 * 1024 * 1024


def _params(*sem):
    return pltpu.CompilerParams(dimension_semantics=sem, vmem_limit_bytes=VMEM_LIMIT)


def _tile(n, target, mult):
    best = None
    for t in range(mult, min(n, target) + 1, mult):
        if n % t == 0:
            best = t
    assert best is not None, (n, target, mult)
    return best


def _log_sigmoid(x):
    return jnp.minimum(x, 0.0) - jnp.log1p(jnp.exp(-jnp.abs(x)))


def _dot(a, b):
    return jnp.dot(a, b, preferred_element_type=F32)


def _dot_t(a, b):
    return lax.dot_general(a, b, (((1,), (1,)), ((), ())), preferred_element_type=F32)


def _tdot(a, b, precision=None):
    return lax.dot_general(a, b, (((0,), (0,)), ((), ())), preferred_element_type=F32,
                           precision=precision)


def _norm_kernel(*refs, has_delta, has_router):
    it = iter(refs)
    h_ref = next(it)
    d_ref = next(it) if has_delta else None
    g_ref = next(it)
    if has_router:
        w2_ref, wh_ref = next(it), next(it)
    hnew_ref = next(it) if has_delta else None
    hn_ref = next(it)
    lg_ref = next(it) if has_router else None

    x = h_ref[...]
    if has_delta:
        x = x + d_ref[...].astype(F32)
        hnew_ref[...] = x
    y = x * lax.rsqrt(jnp.mean(x * x, axis=-1, keepdims=True) + EPS) * g_ref[...]
    yh = y.astype(BF16)
    hn_ref[...] = yh
    if has_router:
        yl = (y - yh.astype(F32)).astype(BF16)
        r1 = _dot(yh, w2_ref[...])
        lg_ref[...] = r1[:, :LANES] + r1[:, LANES:] + _dot(yl, wh_ref[...])


def _norm(h, g, delta=None, router_w=None):
    n, d = h.shape
    tr = _tile(n, 192, BF16_ROWS)
    row = pl.BlockSpec((tr, d), lambda i: (i, 0))
    in_specs, args = [row], [h]
    if delta is not None:
        in_specs.append(row)
        args.append(delta)
    in_specs.append(pl.BlockSpec((1, d), lambda i: (0, 0)))
    args.append(g.reshape(1, d).astype(F32))
    if router_w is not None:
        wh = router_w.astype(BF16)
        wl = (router_w - wh.astype(F32)).astype(BF16)
        in_specs += [pl.BlockSpec((d, 2 * LANES), lambda i: (0, 0)), pl.BlockSpec((d, LANES), lambda i: (0, 0))]
        args += [jnp.concatenate([wh, wl], axis=1), wh]
    out_specs, out_shape = [], []
    if delta is not None:
        out_specs.append(row)
        out_shape.append(jax.ShapeDtypeStruct((n, d), F32))
    out_specs.append(row)
    out_shape.append(jax.ShapeDtypeStruct((n, d), BF16))
    if router_w is not None:
        out_specs.append(pl.BlockSpec((tr, LANES), lambda i: (i, 0)))
        out_shape.append(jax.ShapeDtypeStruct((n, LANES), F32))
    outs = pl.pallas_call(
        functools.partial(_norm_kernel, has_delta=delta is not None, has_router=router_w is not None),
        grid=(n // tr,), in_specs=in_specs, out_specs=out_specs, out_shape=out_shape,
        compiler_params=_params("parallel"), name="norm")(*args)
    outs = list(outs)
    h_new = outs.pop(0) if delta is not None else None
    hn = outs.pop(0)
    lg = outs.pop(0) if router_w is not None else None
    return h_new, hn, lg


def _final_norm_kernel(h_ref, d_ref, g_ref, o_ref):
    x = h_ref[...] + d_ref[...].astype(F32)
    o_ref[...] = x * lax.rsqrt(jnp.mean(x * x, axis=-1, keepdims=True) + EPS) * g_ref[...]


def _final_norm(h, delta, g, batch, seq):
    n, d = h.shape
    nc = n // batch // CHUNK
    src = pl.BlockSpec((CHUNK, d), lambda b, i: (b * nc + 1 + i, 0))
    return pl.pallas_call(
        _final_norm_kernel, grid=(batch, nc - 1),
        in_specs=[src, src, pl.BlockSpec((1, d), lambda b, i: (0, 0))],
        out_specs=pl.BlockSpec((CHUNK, d), lambda b, i: (b * (nc - 1) + i, 0)),
        out_shape=jax.ShapeDtypeStruct((batch * seq, d), F32),
        compiler_params=_params("parallel", "parallel"), name="final_norm")(h, delta, g.reshape(1, d).astype(F32))


def _mm_kernel(*refs, has_res):
    if has_res:
        a_ref, w_ref, r_ref, o_ref = refs
    else:
        a_ref, w_ref, o_ref = refs
    acc = _dot(a_ref[...], w_ref[...])
    if has_res:
        acc = acc + r_ref[...]
    o_ref[...] = acc.astype(o_ref.dtype)


def _matmul(a, w, out_dtype, res=None, tm_target=688, tn_target=512):
    n, k = a.shape
    m = w.shape[1]
    tm = _tile(n, tm_target, BF16_ROWS)
    tn = _tile(m, tn_target, LANES)
    in_specs = [pl.BlockSpec((tm, k), lambda i, j: (i, 0)), pl.BlockSpec((k, tn), lambda i, j: (0, j))]
    args = [a, w]
    if res is not None:
        in_specs.append(pl.BlockSpec((tm, tn), lambda i, j: (i, j)))
        args.append(res)
    return pl.pallas_call(
        functools.partial(_mm_kernel, has_res=res is not None),
        grid=(n // tm, m // tn), in_specs=in_specs,
        out_specs=pl.BlockSpec((tm, tn), lambda i, j: (i, j)),
        out_shape=jax.ShapeDtypeStruct((n, m), out_dtype),
        compiler_params=_params("parallel", "arbitrary"), name="matmul")(*args)


def _conv_taps(ext_s, cw_ref, tt):
    out = cw_ref[0:1, :] * ext_s[pl.ds(SUBLANES - 2, tt), :]
    for j in range(1, 4):
        out = out + cw_ref[j:j + 1, :] * ext_s[pl.ds(SUBLANES - 2 + j, tt), :]
    return out


def _fill_ext(ext_s, cur_ref, prev_ref, next_ref, row0, tt, has_next):
    rows = row0 + lax.broadcasted_iota(jnp.int32, (tt, 1), 0)
    ext_s[pl.ds(SUBLANES, tt), :] = jnp.where(rows >= PAD, cur_ref[...].astype(F32), 0.0)
    prow = row0 - SUBLANES + lax.broadcasted_iota(jnp.int32, (SUBLANES, 1), 0)
    ext_s[pl.ds(0, SUBLANES), :] = jnp.where(prow >= PAD, prev_ref[...].astype(F32)[SUBLANES:, :], 0.0)
    ext_s[pl.ds(SUBLANES + tt, SUBLANES), :] = jnp.where(has_next, next_ref[...].astype(F32)[:SUBLANES, :], 0.0)


def _qkconv_kernel(cur_ref, prev_ref, next_ref, cw_ref, o_ref, ext_s, *, tt, nt, kscale, half):
    t = pl.program_id(1)
    row0 = t * tt
    _fill_ext(ext_s, cur_ref, prev_ref, next_ref, row0, tt, t < nt - 1)
    y = _conv_taps(ext_s, cw_ref, tt)
    y = y * jax.nn.sigmoid(y)
    col = lax.broadcasted_iota(jnp.int32, (1, 2 * half), 1)
    y = y * jnp.where(col >= half, kscale, 1.0)
    rows = row0 + lax.broadcasted_iota(jnp.int32, (tt, 1), 0)
    o_ref[...] = jnp.where(rows >= PAD, y, 0.0).astype(o_ref.dtype)


def _halo_specs(width, col_block, tt, tp, n):
    per_b, per_t = tp // BF16_ROWS, tt // BF16_ROWS
    last = n // BF16_ROWS - 1

    def make(tmap):
        cur = pl.BlockSpec((tt, width), lambda b, t, *_: (b * (tp // tt) + tmap(t), col_block(*_)))
        prev = pl.BlockSpec((BF16_ROWS, width),
                            lambda b, t, *_: (jnp.maximum(b * per_b + tmap(t) * per_t - 1, 0), col_block(*_)))
        nxt = pl.BlockSpec((BF16_ROWS, width),
                           lambda b, t, *_: (jnp.minimum(b * per_b + (tmap(t) + 1) * per_t, last), col_block(*_)))
        return cur, prev, nxt
    return make


def _qk_conv(proj, conv_w, batch, dk):
    n = proj.shape[0]
    tp = n // batch
    width = 2 * HEADS * dk
    tt = _tile(tp, 688, BF16_ROWS)
    nt = tp // tt
    cur, prev, nxt = _halo_specs(width, lambda: 0, tt, tp, n)(lambda t: t)
    return pl.pallas_call(
        functools.partial(_qkconv_kernel, tt=tt, nt=nt, kscale=dk ** -0.5, half=HEADS * dk),
        grid=(batch, nt),
        in_specs=[cur, prev, nxt, pl.BlockSpec((4, width), lambda b, t: (0, 0))],
        out_specs=pl.BlockSpec((tt, width), lambda b, t: (b * nt + t, 0)),
        out_shape=jax.ShapeDtypeStruct((n, width), BF16),
        scratch_shapes=[pltpu.VMEM((tt + 2 * SUBLANES, width), F32)],
        compiler_params=_params("parallel", "parallel"), name="qk_conv")(proj, proj, proj, conv_w.astype(F32))


def _mlstm_kernel(q_ref, k_ref, v_ref, g_ref, gt_ref, gb_ref, gbt_ref, o_ref, c_s, m_s, *, reverse, dk, dv, nc):
    step = pl.program_id(1)

    @pl.when(step == 0)
    def _():
        c_s[...] = jnp.zeros_like(c_s)
        m_s[...] = jnp.zeros_like(m_s)

    chunk = nc - 1 - step if reverse else step
    L = CHUNK
    real = chunk > 0
    valid_c = jnp.logical_or(real, lax.broadcasted_iota(jnp.int32, (L, 1), 0) >= PAD)
    valid_r = jnp.logical_or(real, lax.broadcasted_iota(jnp.int32, (1, L), 1) >= PAD)
    off = 2 * HEADS if reverse else 0
    g = g_ref[:, :GATE_COLS] + gb_ref[...]
    gt = gt_ref[0] + gbt_ref[...]
    li_c = jnp.where(valid_c, g[:, off:off + HEADS], NEG)
    lf_c = jnp.where(valid_c, _log_sigmoid(g[:, off + HEADS:off + 2 * HEADS]), 0.0)
    li_r = jnp.where(valid_r, gt[off:off + HEADS, :], NEG)
    lf_r = jnp.where(valid_r, _log_sigmoid(gt[off + HEADS:off + 2 * HEADS, :]), 0.0)
    ri = lax.broadcasted_iota(jnp.int32, (L, L), 0)
    ci = lax.broadcasted_iota(jnp.int32, (L, L), 1)
    mask = (ci >= ri) if reverse else (ci <= ri)
    cum_c = jnp.dot(mask.astype(F32), lf_c, precision=HIGHEST, preferred_element_type=F32)
    inc = (ri >= ci) if reverse else (ri <= ci)
    cum_r = jnp.dot(lf_r, inc.astype(F32), precision=HIGHEST, preferred_element_type=F32)
    last = 0 if reverse else L - 1
    ones_col = jnp.where(lax.broadcasted_iota(jnp.int32, (L, LANES), 1) == 0, 1.0, 0.0).astype(BF16)

    for h in range(HEADS):
        cc = cum_c[:, h:h + 1]
        cr = cum_r[h:h + 1, :]
        lic = li_c[:, h:h + 1]
        lir = li_r[h:h + 1, :]
        tot = cc[last:last + 1, :]
        m = m_s[h, 0:1, 0:1]
        qh = q_ref[:, h * dk:(h + 1) * dk]
        kh = k_ref[:, h * dk:(h + 1) * dk]
        vh = jnp.where(valid_c, v_ref[:, h * dv:(h + 1) * dv], 0.0).astype(BF16)
        vaug = jnp.concatenate([vh, ones_col], axis=1)

        d_mat = jnp.where(mask, cc - cr + lir, NEG)
        inter = cc + m
        m_t = jnp.maximum(inter, jnp.max(d_mat, axis=1, keepdims=True))
        w_inter = jnp.exp(inter - m_t)
        s = _dot_t(qh, kh) * jnp.exp(d_mat - m_t)
        haug = w_inter * _dot(qh, c_s[h].astype(BF16)) + _dot(s.astype(BF16), vaug)
        den = haug[:, dv:dv + 1]
        o_ref[:, h * dv:(h + 1) * dv] = (haug[:, :dv] / jnp.maximum(jnp.abs(den), jnp.exp(-m_t))).astype(o_ref.dtype)

        gs = tot - cc + lic
        m_new = jnp.maximum(tot + m, jnp.max(gs, axis=0, keepdims=True))
        decay = jnp.exp(tot + m - m_new)
        ks = (kh.astype(F32) * jnp.exp(gs - m_new)).astype(BF16)
        c_s[h] = decay * c_s[h] + _tdot(ks, vaug)
        m_s[h] = jnp.broadcast_to(m_new, m_s.shape[1:])


def _mlstm(qk, proj, gates, gates_t, gate_bias, batch, dk, dv, v_block, reverse):
    n = qk.shape[0]
    nc = n // batch // CHUNK
    cmap = (lambda b, i: b * nc + nc - 1 - i) if reverse else (lambda b, i: b * nc + i)
    gb = gate_bias.reshape(1, GATE_COLS).astype(F32)
    return pl.pallas_call(
        functools.partial(_mlstm_kernel, reverse=reverse, dk=dk, dv=dv, nc=nc),
        grid=(batch, nc),
        in_specs=[pl.BlockSpec((CHUNK, HEADS * dk), lambda b, i: (cmap(b, i), 0)),
                  pl.BlockSpec((CHUNK, HEADS * dk), lambda b, i: (cmap(b, i), 1)),
                  pl.BlockSpec((CHUNK, HEADS * dv), lambda b, i: (cmap(b, i), v_block)),
                  pl.BlockSpec((CHUNK, LANES), lambda b, i: (cmap(b, i), 0)),
                  pl.BlockSpec((1, GATE_COLS, CHUNK), lambda b, i: (cmap(b, i), 0, 0)),
                  pl.BlockSpec((1, GATE_COLS), lambda b, i: (0, 0)),
                  pl.BlockSpec((GATE_COLS, 1), lambda b, i: (0, 0))],
        out_specs=pl.BlockSpec((CHUNK, HEADS * dv), lambda b, i: (cmap(b, i), 0)),
        out_shape=jax.ShapeDtypeStruct((n, HEADS * dv), BF16),
        scratch_shapes=[pltpu.VMEM((HEADS, dk, dv + LANES), F32), pltpu.VMEM((HEADS, SUBLANES, LANES), F32)],
        compiler_params=_params("parallel", "arbitrary"),
        name="mlstm_bwd" if reverse else "mlstm_fwd")(qk, qk, proj, gates, gates_t, gb, gb.reshape(GATE_COLS, 1))


def _gla_kernel(q_ref, k_ref, v_ref, lr_ref, up_ref, ub_ref, o_ref, s_s, *, reverse, dk, dv, nc):
    step = pl.program_id(1)

    @pl.when(step == 0)
    def _():
        s_s[...] = jnp.zeros_like(s_s)

    chunk = nc - 1 - step if reverse else step
    L = CHUNK
    valid_c = jnp.logical_or(chunk > 0, lax.broadcasted_iota(jnp.int32, (L, 1), 0) >= PAD)
    off = GATE_COLS + (B_RANK if reverse else 0)
    z = jnp.dot(lr_ref[:, off:off + B_RANK], up_ref[...], precision=HIGHEST, preferred_element_type=F32)
    la = jnp.where(valid_c, _log_sigmoid(z + ub_ref[...]) / B_TAU, 0.0)
    ri = lax.broadcasted_iota(jnp.int32, (L, L), 0)
    ci = lax.broadcasted_iota(jnp.int32, (L, L), 1)
    mask = (ci >= ri) if reverse else (ci <= ri)
    cum = jnp.dot(mask.astype(F32), la, precision=HIGHEST, preferred_element_type=F32)
    tot_col = _tdot(la, jnp.ones((L, LANES), F32), precision=HIGHEST)
    last = 0 if reverse else L - 1
    nsub = L // SUB
    sub_lane = lax.broadcasted_iota(jnp.int32, (SUB, L), 1)
    sub_row = lax.broadcasted_iota(jnp.int32, (SUB, 1), 0)

    for h in range(HEADS):
        sl = slice(h * dk, (h + 1) * dk)
        q = jnp.where(valid_c, q_ref[:, sl], 0.0).astype(F32) * dk ** -0.5
        k = jnp.where(valid_c, k_ref[:, sl], 0.0).astype(F32)
        v = jnp.where(valid_c, v_ref[:, h * dv:(h + 1) * dv], 0.0).astype(BF16)
        cumh = cum[:, sl]
        tot = cumh[last:last + 1, :]
        state = s_s[h]
        o_inter = _dot((q * jnp.exp(cumh)).astype(BF16), state.astype(BF16))

        for blk in range(nsub):
            r0 = blk * SUB
            if reverse:
                cs = cumh[r0 + SUB:r0 + SUB + 1, :] if blk < nsub - 1 else jnp.zeros((1, dk), F32)
                earlier = sub_lane >= r0 + SUB
            else:
                cs = cumh[r0 - 1:r0, :] if blk > 0 else jnp.zeros((1, dk), F32)
                earlier = sub_lane < r0
            q_b = q[r0:r0 + SUB, :]
            cum_b = cumh[r0:r0 + SUB, :]
            qd = (q_b * jnp.exp(cum_b - cs)).astype(BF16)
            kd = (k * jnp.exp(jnp.minimum(cs - cumh, 0.0))).astype(BF16)
            att = jnp.where(earlier, _dot_t(qd, kd), 0.0)
            for j in range(SUB):
                s_idx = r0 + j
                tmask = (sub_row <= j) if reverse else (sub_row >= j)
                e = jnp.where(tmask, cum_b - cumh[s_idx:s_idx + 1, :], NEG)
                col = jnp.sum(q_b * k[s_idx:s_idx + 1, :] * jnp.exp(e), axis=1, keepdims=True)
                att = jnp.where(sub_lane == s_idx, col, att)
            o_b = o_inter[r0:r0 + SUB, :] + _dot(att.astype(BF16), v)
            o_ref[r0:r0 + SUB, h * dv:(h + 1) * dv] = o_b.astype(o_ref.dtype)

        kdec = (k * jnp.exp(tot - cumh)).astype(BF16)
        s_s[h] = jnp.exp(tot_col[sl, 0:1]) * state + _tdot(kdec, v)


def _gla(proj, small, lr_up, lr_bias, batch, dk, dv, qkv_blocks, reverse):
    n = proj.shape[0]
    nc = n // batch // CHUNK
    cmap = (lambda b, i: b * nc + nc - 1 - i) if reverse else (lambda b, i: b * nc + i)
    qb, kb, vb = qkv_blocks
    return pl.pallas_call(
        functools.partial(_gla_kernel, reverse=reverse, dk=dk, dv=dv, nc=nc),
        grid=(batch, nc),
        in_specs=[pl.BlockSpec((CHUNK, HEADS * dk), lambda b, i: (cmap(b, i), qb)),
                  pl.BlockSpec((CHUNK, HEADS * dk), lambda b, i: (cmap(b, i), kb)),
                  pl.BlockSpec((CHUNK, HEADS * dv), lambda b, i: (cmap(b, i), vb)),
                  pl.BlockSpec((CHUNK, LANES), lambda b, i: (cmap(b, i), 0)),
                  pl.BlockSpec((B_RANK, HEADS * dk), lambda b, i: (0, 0)),
                  pl.BlockSpec((1, HEADS * dk), lambda b, i: (0, 0))],
        out_specs=pl.BlockSpec((CHUNK, HEADS * dv), lambda b, i: (cmap(b, i), 0)),
        out_shape=jax.ShapeDtypeStruct((n, HEADS * dv), BF16),
        scratch_shapes=[pltpu.VMEM((HEADS, dk, dv), F32)],
        compiler_params=_params("parallel", "arbitrary"),
        name="gla_bwd" if reverse else "gla_fwd")(
            proj, proj, proj, small, lr_up.astype(F32), lr_bias.reshape(1, -1).astype(F32))


def _head_norm(x, g, dv):
    parts = []
    for h in range(HEADS):
        xh = x[:, h * dv:(h + 1) * dv]
        parts.append(xh * lax.rsqrt(jnp.mean(xh * xh, axis=-1, keepdims=True) + EPS))
    return jnp.concatenate(parts, axis=1) * g


def _even_combine_kernel(af_ref, ab_ref, bf_ref, bb_ref, oa_ref, gb_ref, na_ref, nb_ref, o_ref, *, tr, tp, dv):
    rows = (pl.program_id(0) * tr) % tp + lax.broadcasted_iota(jnp.int32, (tr, 1), 0)
    valid = rows >= PAD
    w = HEADS * dv
    ha = af_ref[...].astype(F32) + ab_ref[...].astype(F32)
    ya = jax.nn.sigmoid(oa_ref[...].astype(F32)) * _head_norm(ha, na_ref[...], dv)
    o_ref[:, :w] = jnp.where(valid, ya, 0.0).astype(o_ref.dtype)
    hb = bf_ref[...].astype(F32) + bb_ref[...].astype(F32)
    gb = gb_ref[...].astype(F32)
    yb = gb * jax.nn.sigmoid(gb) * _head_norm(hb, nb_ref[...], dv)
    o_ref[:, w:] = jnp.where(valid, yb, 0.0).astype(o_ref.dtype)


def _even_combine(ha_f, ha_b, hb_f, hb_b, proj, norm_a, norm_b, batch, dv, oa_block, gb_block):
    n, w = ha_f.shape
    tp = n // batch
    tr = _tile(tp, 384, BF16_ROWS)
    row = pl.BlockSpec((tr, w), lambda i: (i, 0))
    vec = pl.BlockSpec((1, w), lambda i: (0, 0))
    return pl.pallas_call(
        functools.partial(_even_combine_kernel, tr=tr, tp=tp, dv=dv),
        grid=(n // tr,),
        in_specs=[row, row, row, row, pl.BlockSpec((tr, w), lambda i: (i, oa_block)),
                  pl.BlockSpec((tr, w), lambda i: (i, gb_block)), vec, vec],
        out_specs=pl.BlockSpec((tr, 2 * w), lambda i: (i, 0)),
        out_shape=jax.ShapeDtypeStruct((n, 2 * w), BF16),
        compiler_params=_params("parallel"), name="even_combine")(
            ha_f, ha_b, hb_f, hb_b, proj, proj, norm_a.reshape(1, w).astype(F32), norm_b.reshape(1, w).astype(F32))


def _block_scan(a, b, reverse):
    sub = lax.broadcasted_iota(jnp.int32, a.shape, 1)
    for k in (1, 2, 4):
        if reverse:
            a_sh, b_sh, m = pltpu.roll(a, SUBLANES - k, 1), pltpu.roll(b, SUBLANES - k, 1), sub < SUBLANES - k
        else:
            a_sh, b_sh, m = pltpu.roll(a, k, 1), pltpu.roll(b, k, 1), sub >= k
        b = jnp.where(m, a * b_sh + b, b)
        a = jnp.where(m, a * a_sh, a)
    return a, b


def _rglru_kernel(cur_ref, prev_ref, next_ref, cw_ref, cb_ref, wr_ref, br_ref, wi_ref, bi_ref, lam_ref,
                  o_ref, ext_s, a_s, b_s, h_s, carry_s, *, reverse, tt, nt):
    step = pl.program_id(2)

    @pl.when(step == 0)
    def _():
        carry_s[...] = jnp.zeros_like(carry_s)

    t = nt - 1 - step if reverse else step
    row0 = t * tt
    _fill_ext(ext_s, cur_ref, prev_ref, next_ref, row0, tt, t < nt - 1)
    u = _conv_taps(ext_s, cw_ref, tt) + cb_ref[...]
    ub = u.astype(BF16)
    r = jax.nn.sigmoid(_dot(ub, wr_ref[0]) + br_ref[...])
    gi = jax.nn.sigmoid(_dot(ub, wi_ref[0]) + bi_ref[...])
    lam = lam_ref[...]
    softplus = jnp.maximum(-lam, 0.0) + jnp.log1p(jnp.exp(-jnp.abs(lam)))
    log_a = -RNN_C * r * softplus
    a = jnp.exp(log_a)
    rows = row0 + lax.broadcasted_iota(jnp.int32, (tt, 1), 0)
    inp = jnp.where(rows >= PAD, jnp.sqrt(1.0 - a * a) * (gi * u), 0.0)
    c = a.shape[1]
    ng = tt // SUBLANES
    a_g, b_g = _block_scan(a.reshape(ng, SUBLANES, c), inp.reshape(ng, SUBLANES, c), reverse)
    a_s[...] = a_g.reshape(tt, c)
    b_s[...] = b_g.reshape(tt, c)
    out_row = 0 if reverse else SUBLANES - 1

    def body(i, carry):
        g = ng - 1 - i if reverse else i
        r0 = pl.multiple_of(g * SUBLANES, SUBLANES)
        hh = b_s[pl.ds(r0, SUBLANES), :] + a_s[pl.ds(r0, SUBLANES), :] * carry
        h_s[pl.ds(r0, SUBLANES), :] = hh
        return hh[out_row:out_row + 1, :]

    carry_s[...] = lax.fori_loop(0, ng, body, carry_s[...])
    o_ref[...] = h_s[...].astype(o_ref.dtype)


def _pair_blocks(w):
    nb, r, _ = w.shape
    z = jnp.zeros((nb // 2, r, r), w.dtype)
    top = jnp.concatenate([w[0::2], z], axis=2)
    bot = jnp.concatenate([z, w[1::2]], axis=2)
    return jnp.concatenate([top, bot], axis=1).astype(BF16)


def _rglru(proj, conv_w, conv_b, w_r, b_r, w_i, b_i, lam, batch, d_rnn, reverse):
    n = proj.shape[0]
    tp = n // batch
    cw = 2 * d_rnn // RNN_BLOCKS
    ncb = d_rnn // cw
    tt = _tile(tp, 688, BF16_ROWS)
    nt = tp // tt
    tmap = (lambda t: nt - 1 - t) if reverse else (lambda t: t)
    cur, prev, nxt = _halo_specs(cw, lambda j: ncb + j, tt, tp, n)(tmap)
    def swap(spec):
        f = spec.index_map
        return pl.BlockSpec(spec.block_shape, lambda b, j, t: f(b, t, j))
    vec = pl.BlockSpec((1, cw), lambda b, j, t: (0, j))
    wspec = pl.BlockSpec((1, cw, cw), lambda b, j, t: (j, 0, 0))
    row = lambda x: x.reshape(1, d_rnn).astype(F32)
    return pl.pallas_call(
        functools.partial(_rglru_kernel, reverse=reverse, tt=tt, nt=nt),
        grid=(batch, ncb, nt),
        in_specs=[swap(cur), swap(prev), swap(nxt), pl.BlockSpec((4, cw), lambda b, j, t: (0, j)), vec,
                  wspec, vec, wspec, vec, vec],
        out_specs=pl.BlockSpec((tt, cw), lambda b, j, t: (b * nt + tmap(t), j)),
        out_shape=jax.ShapeDtypeStruct((n, d_rnn), BF16),
        scratch_shapes=[pltpu.VMEM((tt + 2 * SUBLANES, cw), F32), pltpu.VMEM((tt, cw), F32),
                        pltpu.VMEM((tt, cw), F32), pltpu.VMEM((tt, cw), F32), pltpu.VMEM((1, cw), F32)],
        compiler_params=_params("parallel", "parallel", "arbitrary"),
        name="rglru_bwd" if reverse else "rglru_fwd")(
            proj, proj, proj, conv_w.astype(F32), row(conv_b), _pair_blocks(w_r), row(b_r),
            _pair_blocks(w_i), row(b_i), row(lam))


def _odd_combine_kernel(g_ref, hf_ref, hb_ref, o_ref, *, tr, tp):
    rows = (pl.program_id(0) * tr) % tp + lax.broadcasted_iota(jnp.int32, (tr, 1), 0)
    y = jax.nn.gelu(g_ref[...].astype(F32)) * (hf_ref[...].astype(F32) + hb_ref[...].astype(F32))
    o_ref[...] = jnp.where(rows >= PAD, y, 0.0).astype(o_ref.dtype)


def _odd_combine(proj, hf, hb, batch):
    n, w = hf.shape
    tp = n // batch
    tr = _tile(tp, 384, BF16_ROWS)
    row = pl.BlockSpec((tr, w), lambda i: (i, 0))
    return pl.pallas_call(
        functools.partial(_odd_combine_kernel, tr=tr, tp=tp), grid=(n // tr,),
        in_specs=[row, row, row], out_specs=row, out_shape=jax.ShapeDtypeStruct((n, w), BF16),
        compiler_params=_params("parallel"), name="odd_combine")(proj, hf, hb)


def _route(x):
    lane = lax.broadcasted_iota(jnp.int32, x.shape, 1)
    big = jnp.int32(2 * LANES)
    gmask = lane < N_GROUPS
    gmax = jnp.max(jnp.where(gmask, x, -jnp.inf), axis=1, keepdims=True)
    ge = jnp.where(gmask, jnp.exp(x - gmax), 0.0)
    gp = ge / jnp.sum(ge, axis=1, keepdims=True)
    gval = jnp.max(gp, axis=1, keepdims=True)
    gidx = jnp.min(jnp.where(jnp.logical_and(gmask, gp == gval), lane, big), axis=1, keepdims=True)
    lo = N_GROUPS + gidx * EXPERTS_PER_GROUP
    emask = jnp.logical_and(lane >= lo, lane < lo + EXPERTS_PER_GROUP)
    emax = jnp.max(jnp.where(emask, x, -jnp.inf), axis=1, keepdims=True)
    ee = jnp.where(emask, jnp.exp(x - emax), 0.0)
    ep = ee / jnp.sum(ee, axis=1, keepdims=True)
    v1 = jnp.max(jnp.where(emask, ep, -1.0), axis=1, keepdims=True)
    i1 = jnp.min(jnp.where(jnp.logical_and(emask, ep == v1), lane, big), axis=1, keepdims=True)
    rest = jnp.logical_and(emask, lane != i1)
    v2 = jnp.max(jnp.where(rest, ep, -1.0), axis=1, keepdims=True)
    i2 = jnp.min(jnp.where(jnp.logical_and(rest, ep == v2), lane, big), axis=1, keepdims=True)
    tot = v1 + v2
    comb = jnp.where(lane == i1, v1 / tot * gval, jnp.where(lane == i2, v2 / tot * gval, 0.0))
    return comb, gidx


def _norm_route_kernel(h_ref, g_ref, w2_ref, wh_ref, b_ref, slab_ref, oh_ref, *, nchunk):
    x = h_ref[...]
    y = x * lax.rsqrt(jnp.mean(x * x, axis=-1, keepdims=True) + EPS) * g_ref[...]
    yh = y.astype(BF16)
    yl = (y - yh.astype(F32)).astype(BF16)
    r1 = _dot(yh, w2_ref[...])
    logits = r1[:, :LANES] + r1[:, LANES:] + _dot(yl, wh_ref[...]) + b_ref[...]
    comb, gidx = _route(logits)
    for j in range(nchunk):
        slab_ref[:, j, :] = y[:, j * LANES:(j + 1) * LANES]
    slab_ref[:, nchunk, :] = comb
    for j in range(nchunk + 1, slab_ref.shape[1]):
        slab_ref[:, j, :] = jnp.zeros_like(comb)
    lane = lax.broadcasted_iota(jnp.int32, comb.shape, 1)
    oh_ref[...] = jnp.where(lane == gidx, 1.0, 0.0).astype(oh_ref.dtype)


def _norm_route(h, g, wg, bg, we, be):
    n, d = h.shape
    nchunk = d // LANES
    srows = (nchunk + 1 + SUBLANES - 1) // SUBLANES * SUBLANES
    tr = _tile(n, 192, BF16_ROWS)
    zpad = LANES - N_GROUPS - N_EXPERTS
    wr = jnp.concatenate([wg, we, jnp.zeros((d, zpad), F32)], axis=1)
    wh = wr.astype(BF16)
    wl = (wr - wh.astype(F32)).astype(BF16)
    bias = jnp.concatenate([bg.astype(F32), be.astype(F32), jnp.zeros((zpad,), F32)]).reshape(1, LANES)
    return pl.pallas_call(
        functools.partial(_norm_route_kernel, nchunk=nchunk), grid=(n // tr,),
        in_specs=[pl.BlockSpec((tr, d), lambda i: (i, 0)), pl.BlockSpec((1, d), lambda i: (0, 0)),
                  pl.BlockSpec((d, 2 * LANES), lambda i: (0, 0)), pl.BlockSpec((d, LANES), lambda i: (0, 0)),
                  pl.BlockSpec((1, LANES), lambda i: (0, 0))],
        out_specs=[pl.BlockSpec((tr, srows, LANES), lambda i: (i, 0, 0)), pl.BlockSpec((tr, LANES), lambda i: (i, 0))],
        out_shape=[jax.ShapeDtypeStruct((n, srows, LANES), F32), jax.ShapeDtypeStruct((n, LANES), BF16)],
        compiler_params=_params("parallel"), name="norm_route")(
            h, g.reshape(1, d).astype(F32), jnp.concatenate([wh, wl], axis=1), wh, bias)


def _rank_kernel(oh_ref, g_ref, rank_ref, cnt_ref, carry_s):
    @pl.when(pl.program_id(0) == 0)
    def _():
        carry_s[...] = jnp.zeros_like(carry_s)

    tr = oh_ref.shape[0]
    sel = jnp.where(lax.broadcasted_iota(jnp.int32, (SUBLANES, LANES), 0) ==
                    lax.broadcasted_iota(jnp.int32, (SUBLANES, LANES), 1), 1.0, 0.0).astype(BF16)
    oh_t = _dot_t(sel, oh_ref[...])
    before = (lax.broadcasted_iota(jnp.int32, (tr, tr), 0) < lax.broadcasted_iota(jnp.int32, (tr, tr), 1))
    cum = _dot(oh_t.astype(BF16), jnp.where(before, 1.0, 0.0).astype(BF16)) + carry_s[:, 0:1]
    gid = lax.broadcasted_iota(jnp.int32, (SUBLANES, tr), 0).astype(F32)
    rank_ref[0] = jnp.sum(oh_t * cum, axis=0, keepdims=True).astype(jnp.int32)
    g_ref[0] = jnp.sum(oh_t * gid, axis=0, keepdims=True).astype(jnp.int32)
    carry_s[...] = carry_s[...] + jnp.sum(oh_t, axis=1, keepdims=True)
    cnt_ref[...] = carry_s[...]


def _rank(onehot):
    n = onehot.shape[0]
    tr = _tile(n, 384, LANES)
    row = pl.BlockSpec((1, 1, tr), lambda i: (i, 0, 0))
    g, rank, cnt = pl.pallas_call(
        _rank_kernel, grid=(n // tr,), in_specs=[pl.BlockSpec((tr, LANES), lambda i: (i, 0))],
        out_specs=[row, row, pl.BlockSpec((SUBLANES, LANES), lambda i: (0, 0))],
        out_shape=[jax.ShapeDtypeStruct((n // tr, 1, tr), jnp.int32), jax.ShapeDtypeStruct((n // tr, 1, tr), jnp.int32),
                   jax.ShapeDtypeStruct((SUBLANES, LANES), F32)],
        scratch_shapes=[pltpu.VMEM((SUBLANES, LANES), F32)],
        compiler_params=_params("arbitrary"), name="rank")(onehot)
    return g.reshape(n), rank.reshape(n), cnt[:N_GROUPS, 0].astype(jnp.int32)


def _invert_kernel(g_ref, rank_ref, cnt_ref, pos_ref, idx_ref, tg_ref, *, n, tm, ntiles):
    bases = [jnp.int32(0)]
    for g in range(N_GROUPS - 1):
        bases.append(bases[-1] + (cnt_ref[g] + tm - 1) // tm * tm)

    def zero(i, c):
        idx_ref[i] = 0
        return c
    lax.fori_loop(0, ntiles * tm, zero, 0)

    def place(t, c):
        g = g_ref[t]
        base = bases[0]
        for k in range(1, N_GROUPS):
            base = jnp.where(g == k, bases[k], base)
        p = base + rank_ref[t]
        pos_ref[t] = p
        idx_ref[p] = t
        return c
    lax.fori_loop(0, n, place, 0)

    def tile_group(i, c):
        r = i * tm
        tg = jnp.int32(0)
        for k in range(1, N_GROUPS):
            tg = tg + (r >= bases[k]).astype(jnp.int32)
        tg_ref[i] = tg
        return c
    lax.fori_loop(0, ntiles, tile_group, 0)


def _invert(g, rank, cnt, tm, ntiles):
    n = g.shape[0]
    smem = pl.BlockSpec(memory_space=pltpu.SMEM)
    return pl.pallas_call(
        functools.partial(_invert_kernel, n=n, tm=tm, ntiles=ntiles),
        in_specs=[smem, smem, smem], out_specs=[smem, smem, smem],
        out_shape=[jax.ShapeDtypeStruct((n,), jnp.int32), jax.ShapeDtypeStruct((ntiles * tm,), jnp.int32),
                   jax.ShapeDtypeStruct((ntiles,), jnp.int32)],
        name="invert")(g, rank, cnt)


def _row_copy(src_hbm, buf, sem, src_row, slot, dst_row):
    return pltpu.make_async_copy(src_hbm.at[pl.ds(src_row, 1)], buf.at[slot, pl.ds(dst_row, 1)], sem.at[slot])


def _gather_rows(index_ref, src_hbm, buf, sem, rows):
    i = pl.program_id(0)
    steps = pl.num_programs(0)

    def issue(step, slot):
        def body(r, c):
            _row_copy(src_hbm, buf, sem, index_ref[step * rows + r], slot, r).start()
            return c
        lax.fori_loop(0, rows, body, 0)

    @pl.when(i == 0)
    def _():
        issue(0, 0)

    @pl.when(i + 1 < steps)
    def _():
        issue(i + 1, (i + 1) % 2)

    slot = i % 2

    def wait(r, c):
        _row_copy(src_hbm, buf, sem, 0, slot, r).wait()
        return c
    lax.fori_loop(0, rows, wait, 0)
    return slot


def _dispatch_kernel(idx_ref, slab_hbm, xs_ref, cs_ref, buf, sem, *, nchunk):
    slot = _gather_rows(idx_ref, slab_hbm, buf, sem, xs_ref.shape[0])
    for j in range(nchunk):
        xs_ref[:, j * LANES:(j + 1) * LANES] = buf[slot, :, j, :].astype(xs_ref.dtype)
    cs_ref[...] = buf[slot, :, nchunk, :]


def _dispatch(idx, slab, tm, ntiles, d):
    srows = slab.shape[1]
    return pl.pallas_call(
        functools.partial(_dispatch_kernel, nchunk=d // LANES),
        grid_spec=pltpu.PrefetchScalarGridSpec(
            num_scalar_prefetch=1, grid=(ntiles,),
            in_specs=[pl.BlockSpec(memory_space=pl.ANY)],
            out_specs=[pl.BlockSpec((tm, d), lambda i, idx: (i, 0)), pl.BlockSpec((tm, LANES), lambda i, idx: (i, 0))],
            scratch_shapes=[pltpu.VMEM((2, tm, srows, LANES), F32), pltpu.SemaphoreType.DMA((2,))]),
        out_shape=[jax.ShapeDtypeStruct((ntiles * tm, d), BF16), jax.ShapeDtypeStruct((ntiles * tm, LANES), F32)],
        compiler_params=_params("arbitrary"), name="dispatch")(idx, slab)


def _expert_kernel(tg_ref, x_ref, c_ref, w1_ref, w3_ref, w2_ref, o_ref, acc_s, *, nchunk):
    i = pl.program_id(0)
    e = pl.program_id(1)

    @pl.when(e == 0)
    def _():
        acc_s[...] = jnp.zeros_like(acc_s)

    a = x_ref[...]
    h1 = _dot(a, w1_ref[0])
    h3 = _dot(a, w3_ref[0])
    comb = c_ref[...]
    lane = lax.broadcasted_iota(jnp.int32, comb.shape, 1)
    c = jnp.sum(jnp.where(lane == N_GROUPS + tg_ref[i] * EXPERTS_PER_GROUP + e, comb, 0.0), axis=1, keepdims=True)
    hid = (h1 * jax.nn.sigmoid(h1) * h3 * c).astype(BF16)
    acc_s[...] += _dot(hid, w2_ref[0])

    @pl.when(e == EXPERTS_PER_GROUP - 1)
    def _():
        for j in range(nchunk):
            o_ref[:, j, :] = acc_s[:, j * LANES:(j + 1) * LANES]


def _experts(tg, xs, cs, w1, w3, w2, tm):
    rows, d = xs.shape
    f = w1.shape[-1]
    nchunk = d // LANES
    wmap = lambda i, e, tg: (tg[i] * EXPERTS_PER_GROUP + e, 0, 0)
    return pl.pallas_call(
        functools.partial(_expert_kernel, nchunk=nchunk),
        grid_spec=pltpu.PrefetchScalarGridSpec(
            num_scalar_prefetch=1, grid=(rows // tm, EXPERTS_PER_GROUP),
            in_specs=[pl.BlockSpec((tm, d), lambda i, e, tg: (i, 0)), pl.BlockSpec((tm, LANES), lambda i, e, tg: (i, 0)),
                      pl.BlockSpec((1, d, f), wmap), pl.BlockSpec((1, d, f), wmap), pl.BlockSpec((1, f, d), wmap)],
            out_specs=pl.BlockSpec((tm, nchunk, LANES), lambda i, e, tg: (i, 0, 0)),
            scratch_shapes=[pltpu.VMEM((tm, d), F32)]),
        out_shape=jax.ShapeDtypeStruct((rows, nchunk, LANES), F32),
        compiler_params=_params("parallel", "arbitrary"), name="experts")(tg, xs, cs, w1, w3, w2)


def _collect_kernel(pos_ref, ys_hbm, o_ref, buf, sem, *, nchunk):
    slot = _gather_rows(pos_ref, ys_hbm, buf, sem, o_ref.shape[0])
    for j in range(nchunk):
        o_ref[:, j * LANES:(j + 1) * LANES] = buf[slot, :, j, :].astype(o_ref.dtype)


def _collect(pos, ys):
    n = pos.shape[0]
    _, nchunk, _ = ys.shape
    d = nchunk * LANES
    tr = _tile(n, 384, BF16_ROWS)
    return pl.pallas_call(
        functools.partial(_collect_kernel, nchunk=nchunk),
        grid_spec=pltpu.PrefetchScalarGridSpec(
            num_scalar_prefetch=1, grid=(n // tr,),
            in_specs=[pl.BlockSpec(memory_space=pl.ANY)],
            out_specs=pl.BlockSpec((tr, d), lambda i, pos: (i, 0)),
            scratch_shapes=[pltpu.VMEM((2, tr, nchunk, LANES), F32), pltpu.SemaphoreType.DMA((2,))]),
        out_shape=jax.ShapeDtypeStruct((n, d), BF16),
        compiler_params=_params("arbitrary"), name="collect")(pos, ys)


def _moe_layer(h, ffn_g, wg, bg, we, be, w1, w3, w2):
    n, d = h.shape
    f = w1.shape[-1]
    tm = 512 if n >= 4096 else 128
    ntiles = (n + N_GROUPS * (tm - 1)) // tm
    slab, onehot = _norm_route(h, ffn_g, wg, bg, we, be)
    g, rank, cnt = _rank(onehot)
    pos, idx, tg = _invert(g, rank, cnt, tm, ntiles)
    xs, cs = _dispatch(idx, slab, tm, ntiles, d)
    ys = _experts(tg, xs, cs, w1.reshape(N_EXPERTS, d, f).astype(BF16), w3.reshape(N_EXPERTS, d, f).astype(BF16),
                  w2.reshape(N_EXPERTS, f, d).astype(BF16), tm)
    return _collect(pos, ys)


def _even_layer(h, delta, g, w_in, gate_bias, qk_conv, lr_up, lr_bias, norm_a, norm_b, w_out, batch):
    n, d = h.shape
    dk, dv = d // 16, d // 8
    qk_w, v_w = HEADS * dk, HEADS * dv
    a_end = 2 * qk_w + 2 * v_w
    b_start = a_end + GATE_COLS
    b_end = b_start + 2 * qk_w + 2 * v_w
    w_main = jnp.concatenate([w_in[:, :a_end], w_in[:, b_start:b_end]], axis=1).astype(BF16)
    w_small = jnp.concatenate([w_in[:, a_end:b_start], w_in[:, b_end:],
                               jnp.zeros((d, LANES - GATE_COLS - 2 * B_RANK), F32)], axis=1).astype(BF16)
    if delta is None:
        _, hn, _ = _norm(h, g)
    else:
        h, hn, _ = _norm(h, g, delta=delta)
    proj = _matmul(hn, w_main, BF16)
    small = _matmul(hn, w_small, F32, tn_target=LANES)
    gates_t = small[:, :GATE_COLS].reshape(n // CHUNK, CHUNK, GATE_COLS).transpose(0, 2, 1)
    qk = _qk_conv(proj, qk_conv, batch, dk)
    va_blk, oa_blk = 2 * qk_w // v_w, (2 * qk_w + v_w) // v_w
    b0 = a_end
    qb_blk, kb_blk = b0 // qk_w, (b0 + qk_w) // qk_w
    vb_blk, gb_blk = (b0 + 2 * qk_w) // v_w, (b0 + 2 * qk_w + v_w) // v_w
    ha, hb = [], []
    for rev in (False, True):
        ha.append(_mlstm(qk, proj, small, gates_t, gate_bias, batch, dk, dv, va_blk, rev))
        hb.append(_gla(proj, small, lr_up[int(rev)], lr_bias[int(rev)], batch, dk, dv, (qb_blk, kb_blk, vb_blk), rev))
    y = _even_combine(ha[0], ha[1], hb[0], hb[1], proj, norm_a, norm_b, batch, dv, oa_blk, gb_blk)
    return _matmul(y, w_out.astype(BF16), F32, res=h)


def _odd_layer(h, delta, g, w_in, conv_w, conv_b, w_r, b_r, w_i, b_i, lam, w_out, batch):
    d_rnn = w_out.shape[0]
    h, hn, _ = _norm(h, g, delta=delta)
    proj = _matmul(hn, w_in.astype(BF16), BF16)
    hs = [_rglru(proj, conv_w, conv_b, w_r[i], b_r[i], w_i[i], b_i[i], lam[i], batch, d_rnn, bool(i)) for i in (0, 1)]
    y = _odd_combine(proj, hs[0], hs[1], batch)
    return _matmul(y, w_out.astype(BF16), F32, res=h)


def kernel(x, meta_tokens, mix_norm, ffn_norm, final_norm, ev_w_in, ev_gate_bias, ev_qk_conv, ev_lr_up, ev_lr_bias, ev_norm_a, ev_norm_b, ev_w_out, od_w_in, od_conv, od_conv_bias, od_w_r, od_b_r, od_w_i, od_b_i, od_lambda, od_w_out, moe_wg, moe_bg, moe_we, moe_be, moe_w1, moe_w3, moe_w2):
    batch, seq, d = x.shape
    depth = mix_norm.shape[0]
    assert seq % CHUNK == 0 and d % 16 == 0
    frame = jnp.concatenate([jnp.zeros((batch, PAD, d), x.dtype),
                             jnp.broadcast_to(meta_tokens.astype(x.dtype)[None], (batch, N_META, d)), x], axis=1)
    h = frame.reshape(batch * (PAD + N_META + seq), d)
    delta = None
    for layer in range(depth):
        if layer % 2 == 0:
            e = layer // 2
            h = _even_layer(h, delta, mix_norm[layer], ev_w_in[e], ev_gate_bias[e], ev_qk_conv[e], ev_lr_up[e],
                            ev_lr_bias[e], ev_norm_a[e], ev_norm_b[e], ev_w_out[e], batch)
        else:
            o = layer // 2
            h = _odd_layer(h, delta, mix_norm[layer], od_w_in[o], od_conv[o], od_conv_bias[o], od_w_r[o], od_b_r[o],
                           od_w_i[o], od_b_i[o], od_lambda[o], od_w_out[o], batch)
        delta = _moe_layer(h, ffn_norm[layer], moe_wg[layer], moe_bg[layer], moe_we[layer], moe_be[layer],
                              moe_w1[layer], moe_w3[layer], moe_w2[layer])
    out = _final_norm(h, delta, final_norm, batch, seq)
    return out.reshape(batch, seq, d)
```

```python
import functools

import jax
import jax.numpy as jnp
from jax import lax
from jax.experimental import pallas as pl
from jax.experimental.pallas import tpu as pltpu

F32 = jnp.float32
BF16 = jnp.bfloat16
HIGHEST = lax.Precision.HIGHEST

N_META = 16
CHUNK = 64
PAD = CHUNK - N_META
SUB = 16
EPS = 1e-6
NEG = -1e30
HEADS = 4
GATE_COLS = 4 * HEADS
B_RANK = 16
B_TAU = 16.0
GLA_MAX_CHUNK_DECAY = 80.0
RNN_BLOCKS = 16
RNN_C = 8.0
N_GROUPS = 4
EXPERTS_PER_GROUP = 8
N_EXPERTS = N_GROUPS * EXPERTS_PER_GROUP
LANES = 128
SUBLANES = 8
BF16_ROWS = 16
VMEM_LIMIT = 56 * 1024 * 1024


def _params(*sem):
    return pltpu.CompilerParams(dimension_semantics=sem, vmem_limit_bytes=VMEM_LIMIT)


def _tile(n, target, mult):
    best = None
    for t in range(mult, min(n, target) + 1, mult):
        if n % t == 0:
            best = t
    assert best is not None, (n, target, mult)
    return best


def _log_sigmoid(x):
    return jnp.minimum(x, 0.0) - jnp.log1p(jnp.exp(-jnp.abs(x)))


def _sigmoid(x):
    return 0.5 * jnp.tanh(0.5 * x) + 0.5


def _dot(a, b):
    return jnp.dot(a, b, preferred_element_type=F32)


def _dot_t(a, b):
    return lax.dot_general(a, b, (((1,), (1,)), ((), ())), preferred_element_type=F32)


def _tdot(a, b, precision=None):
    return lax.dot_general(a, b, (((0,), (0,)), ((), ())), preferred_element_type=F32,
                           precision=precision)


def _norm_kernel(*refs, has_delta, has_router):
    it = iter(refs)
    h_ref = next(it)
    d_ref = next(it) if has_delta else None
    g_ref = next(it)
    if has_router:
        w2_ref, wh_ref = next(it), next(it)
    hnew_ref = next(it) if has_delta else None
    hn_ref = next(it)
    lg_ref = next(it) if has_router else None

    x = h_ref[...]
    if has_delta:
        x = x + d_ref[...].astype(F32)
        hnew_ref[...] = x
    y = x * lax.rsqrt(jnp.mean(x * x, axis=-1, keepdims=True) + EPS) * g_ref[...]
    yh = y.astype(BF16)
    hn_ref[...] = yh
    if has_router:
        yl = (y - yh.astype(F32)).astype(BF16)
        r1 = _dot(yh, w2_ref[...])
        lg_ref[...] = r1[:, :LANES] + r1[:, LANES:] + _dot(yl, wh_ref[...])


def _norm(h, g, delta=None, router_w=None):
    n, d = h.shape
    tr = _tile(n, 192, BF16_ROWS)
    row = pl.BlockSpec((tr, d), lambda i: (i, 0))
    in_specs, args = [row], [h]
    if delta is not None:
        in_specs.append(row)
        args.append(delta)
    in_specs.append(pl.BlockSpec((1, d), lambda i: (0, 0)))
    args.append(g.reshape(1, d).astype(F32))
    if router_w is not None:
        wh = router_w.astype(BF16)
        wl = (router_w - wh.astype(F32)).astype(BF16)
        in_specs += [pl.BlockSpec((d, 2 * LANES), lambda i: (0, 0)), pl.BlockSpec((d, LANES), lambda i: (0, 0))]
        args += [jnp.concatenate([wh, wl], axis=1), wh]
    out_specs, out_shape = [], []
    if delta is not None:
        out_specs.append(row)
        out_shape.append(jax.ShapeDtypeStruct((n, d), F32))
    out_specs.append(row)
    out_shape.append(jax.ShapeDtypeStruct((n, d), BF16))
    if router_w is not None:
        out_specs.append(pl.BlockSpec((tr, LANES), lambda i: (i, 0)))
        out_shape.append(jax.ShapeDtypeStruct((n, LANES), F32))
    outs = pl.pallas_call(
        functools.partial(_norm_kernel, has_delta=delta is not None, has_router=router_w is not None),
        grid=(n // tr,), in_specs=in_specs, out_specs=out_specs, out_shape=out_shape,
        compiler_params=_params("parallel"), name="norm")(*args)
    outs = list(outs)
    h_new = outs.pop(0) if delta is not None else None
    hn = outs.pop(0)
    lg = outs.pop(0) if router_w is not None else None
    return h_new, hn, lg


def _final_norm_kernel(h_ref, d_ref, g_ref, o_ref):
    x = h_ref[...] + d_ref[...].astype(F32)
    o_ref[...] = x * lax.rsqrt(jnp.mean(x * x, axis=-1, keepdims=True) + EPS) * g_ref[...]


def _final_norm(h, delta, g, batch, seq):
    n, d = h.shape
    nc = n // batch // CHUNK
    src = pl.BlockSpec((CHUNK, d), lambda b, i: (b * nc + 1 + i, 0))
    return pl.pallas_call(
        _final_norm_kernel, grid=(batch, nc - 1),
        in_specs=[src, src, pl.BlockSpec((1, d), lambda b, i: (0, 0))],
        out_specs=pl.BlockSpec((CHUNK, d), lambda b, i: (b * (nc - 1) + i, 0)),
        out_shape=jax.ShapeDtypeStruct((batch * seq, d), F32),
        compiler_params=_params("parallel", "parallel"), name="final_norm")(h, delta, g.reshape(1, d).astype(F32))


def _mm_kernel(*refs, has_res):
    if has_res:
        a_ref, w_ref, r_ref, o_ref = refs
    else:
        a_ref, w_ref, o_ref = refs
    acc = _dot(a_ref[...], w_ref[...])
    if has_res:
        acc = acc + r_ref[...]
    o_ref[...] = acc.astype(o_ref.dtype)


def _matmul(a, w, out_dtype, res=None, tm_target=688, tn_target=512):
    n, k = a.shape
    m = w.shape[1]
    tm = _tile(n, tm_target, BF16_ROWS)
    tn = _tile(m, tn_target, LANES)
    in_specs = [pl.BlockSpec((tm, k), lambda i, j: (i, 0)), pl.BlockSpec((k, tn), lambda i, j: (0, j))]
    args = [a, w]
    if res is not None:
        in_specs.append(pl.BlockSpec((tm, tn), lambda i, j: (i, j)))
        args.append(res)
    return pl.pallas_call(
        functools.partial(_mm_kernel, has_res=res is not None),
        grid=(n // tm, m // tn), in_specs=in_specs,
        out_specs=pl.BlockSpec((tm, tn), lambda i, j: (i, j)),
        out_shape=jax.ShapeDtypeStruct((n, m), out_dtype),
        compiler_params=_params("parallel", "arbitrary"), name="matmul")(*args)


def _conv_taps(ext_s, cw_ref, tt):
    out = cw_ref[0:1, :] * ext_s[pl.ds(SUBLANES - 2, tt), :]
    for j in range(1, 4):
        out = out + cw_ref[j:j + 1, :] * ext_s[pl.ds(SUBLANES - 2 + j, tt), :]
    return out


def _fill_ext(ext_s, cur_ref, prev_ref, next_ref, row0, tt, has_next):
    rows = row0 + lax.broadcasted_iota(jnp.int32, (tt, 1), 0)
    ext_s[pl.ds(SUBLANES, tt), :] = jnp.where(rows >= PAD, cur_ref[...].astype(F32), 0.0)
    prow = row0 - SUBLANES + lax.broadcasted_iota(jnp.int32, (SUBLANES, 1), 0)
    ext_s[pl.ds(0, SUBLANES), :] = jnp.where(prow >= PAD, prev_ref[...].astype(F32)[SUBLANES:, :], 0.0)
    ext_s[pl.ds(SUBLANES + tt, SUBLANES), :] = jnp.where(has_next, next_ref[...].astype(F32)[:SUBLANES, :], 0.0)


def _qkconv_kernel(cur_ref, prev_ref, next_ref, cw_ref, o_ref, ext_s, *, tt, nt, kscale, half):
    t = pl.program_id(1)
    row0 = t * tt
    _fill_ext(ext_s, cur_ref, prev_ref, next_ref, row0, tt, t < nt - 1)
    y = _conv_taps(ext_s, cw_ref, tt)
    y = y * jax.nn.sigmoid(y)
    col = lax.broadcasted_iota(jnp.int32, (1, 2 * half), 1)
    y = y * jnp.where(col >= half, kscale, 1.0)
    rows = row0 + lax.broadcasted_iota(jnp.int32, (tt, 1), 0)
    o_ref[...] = jnp.where(rows >= PAD, y, 0.0).astype(o_ref.dtype)


def _halo_specs(width, col_block, tt, tp, n):
    per_b, per_t = tp // BF16_ROWS, tt // BF16_ROWS
    last = n // BF16_ROWS - 1

    def make(tmap):
        cur = pl.BlockSpec((tt, width), lambda b, t, *_: (b * (tp // tt) + tmap(t), col_block(*_)))
        prev = pl.BlockSpec((BF16_ROWS, width),
                            lambda b, t, *_: (jnp.maximum(b * per_b + tmap(t) * per_t - 1, 0), col_block(*_)))
        nxt = pl.BlockSpec((BF16_ROWS, width),
                           lambda b, t, *_: (jnp.minimum(b * per_b + (tmap(t) + 1) * per_t, last), col_block(*_)))
        return cur, prev, nxt
    return make


def _qk_conv(proj, conv_w, batch, dk):
    n = proj.shape[0]
    tp = n // batch
    width = 2 * HEADS * dk
    tt = _tile(tp, 688, BF16_ROWS)
    nt = tp // tt
    cur, prev, nxt = _halo_specs(width, lambda: 0, tt, tp, n)(lambda t: t)
    return pl.pallas_call(
        functools.partial(_qkconv_kernel, tt=tt, nt=nt, kscale=dk ** -0.5, half=HEADS * dk),
        grid=(batch, nt),
        in_specs=[cur, prev, nxt, pl.BlockSpec((4, width), lambda b, t: (0, 0))],
        out_specs=pl.BlockSpec((tt, width), lambda b, t: (b * nt + t, 0)),
        out_shape=jax.ShapeDtypeStruct((n, width), BF16),
        scratch_shapes=[pltpu.VMEM((tt + 2 * SUBLANES, width), F32)],
        compiler_params=_params("parallel", "parallel"), name="qk_conv")(proj, proj, proj, conv_w.astype(F32))


def _mlstm_kernel(q_ref, k_ref, v_ref, g_ref, gt_ref, gb_ref, gbt_ref, o_ref, c_s, m_s, *, reverse, dk, dv, nc, batch):
    step = pl.program_id(0)

    @pl.when(step == 0)
    def _():
        c_s[...] = jnp.zeros_like(c_s)
        m_s[...] = jnp.zeros_like(m_s)

    chunk = nc - 1 - step if reverse else step
    L = CHUNK
    real = chunk > 0
    valid_c = jnp.logical_or(real, lax.broadcasted_iota(jnp.int32, (L, 1), 0) >= PAD)
    valid_r = jnp.logical_or(real, lax.broadcasted_iota(jnp.int32, (1, L), 1) >= PAD)
    off = 2 * HEADS if reverse else 0
    ri = lax.broadcasted_iota(jnp.int32, (L, L), 0)
    ci = lax.broadcasted_iota(jnp.int32, (L, L), 1)
    mask = (ci >= ri) if reverse else (ci <= ri)
    inc = (ri >= ci) if reverse else (ri <= ci)
    last = 0 if reverse else L - 1
    ones_col = jnp.where(lax.broadcasted_iota(jnp.int32, (L, LANES), 1) == 0, 1.0, 0.0).astype(BF16)

    for b in range(batch):
        g = g_ref[b, :, :GATE_COLS] + gb_ref[...]
        gt = gt_ref[b, 0] + gbt_ref[...]
        li_c = jnp.where(valid_c, g[:, off:off + HEADS], NEG)
        lf_c = jnp.where(valid_c, _log_sigmoid(g[:, off + HEADS:off + 2 * HEADS]), 0.0)
        li_r = jnp.where(valid_r, gt[off:off + HEADS, :], NEG)
        lf_r = jnp.where(valid_r, _log_sigmoid(gt[off + HEADS:off + 2 * HEADS, :]), 0.0)
        cum_c = jnp.dot(mask.astype(F32), lf_c, precision=HIGHEST, preferred_element_type=F32)
        cum_r = jnp.dot(lf_r, inc.astype(F32), precision=HIGHEST, preferred_element_type=F32)

        for h in range(HEADS):
            sh = b * HEADS + h
            cc = cum_c[:, h:h + 1]
            cr = cum_r[h:h + 1, :]
            lic = li_c[:, h:h + 1]
            lir = li_r[h:h + 1, :]
            tot = cc[last:last + 1, :]
            m = m_s[sh, 0:1, 0:1]
            qh = q_ref[b, :, h * dk:(h + 1) * dk]
            kh = k_ref[b, :, h * dk:(h + 1) * dk]
            vh = jnp.where(valid_c, v_ref[b, :, h * dv:(h + 1) * dv], 0.0).astype(BF16)
            vaug = jnp.concatenate([vh, ones_col], axis=1)

            d_mat = jnp.where(mask, cc - cr + lir, NEG)
            inter = cc + m
            m_t = jnp.maximum(inter, jnp.max(d_mat, axis=1, keepdims=True))
            w_inter = jnp.exp(inter - m_t)
            s = _dot_t(qh, kh) * jnp.exp(d_mat - m_t)
            haug = w_inter * _dot(qh, c_s[sh].astype(BF16)) + _dot(s.astype(BF16), vaug)
            den = haug[:, dv:dv + 1]
            o_ref[b, :, h * dv:(h + 1) * dv] = (
                haug[:, :dv] / jnp.maximum(jnp.abs(den), jnp.exp(-m_t))).astype(o_ref.dtype)

            gs = tot - cc + lic
            m_new = jnp.maximum(tot + m, jnp.max(gs, axis=0, keepdims=True))
            decay = jnp.exp(tot + m - m_new)
            ks = (kh.astype(F32) * jnp.exp(gs - m_new)).astype(BF16)
            c_s[sh] = decay * c_s[sh] + _tdot(ks, vaug)
            m_s[sh] = jnp.broadcast_to(m_new, m_s.shape[1:])


def _chunk_spec(batch, width, col_block, nc, reverse):
    cidx = (lambda i: nc - 1 - i) if reverse else (lambda i: i)
    return pl.BlockSpec((batch, CHUNK, width), lambda i: (0, cidx(i), col_block))


def _mlstm(qk, proj, gates, gates_t, gate_bias, batch, dk, dv, v_block, reverse):
    n = qk.shape[0]
    tp = n // batch
    nc = tp // CHUNK
    cidx = (lambda i: nc - 1 - i) if reverse else (lambda i: i)
    gb = gate_bias.reshape(1, GATE_COLS).astype(F32)
    view = lambda a: a.reshape(batch, tp, a.shape[-1])
    out = pl.pallas_call(
        functools.partial(_mlstm_kernel, reverse=reverse, dk=dk, dv=dv, nc=nc, batch=batch),
        grid=(nc,),
        in_specs=[_chunk_spec(batch, HEADS * dk, 0, nc, reverse), _chunk_spec(batch, HEADS * dk, 1, nc, reverse),
                  _chunk_spec(batch, HEADS * dv, v_block, nc, reverse), _chunk_spec(batch, LANES, 0, nc, reverse),
                  pl.BlockSpec((batch, 1, GATE_COLS, CHUNK), lambda i: (0, cidx(i), 0, 0)),
                  pl.BlockSpec((1, GATE_COLS), lambda i: (0, 0)),
                  pl.BlockSpec((GATE_COLS, 1), lambda i: (0, 0))],
        out_specs=_chunk_spec(batch, HEADS * dv, 0, nc, reverse),
        out_shape=jax.ShapeDtypeStruct((batch, tp, HEADS * dv), BF16),
        scratch_shapes=[pltpu.VMEM((batch * HEADS, dk, dv + LANES), F32),
                        pltpu.VMEM((batch * HEADS, SUBLANES, LANES), F32)],
        compiler_params=_params("arbitrary"),
        name="mlstm_bwd" if reverse else "mlstm_fwd")(
            view(qk), view(qk), view(proj), view(gates), gates_t.reshape(batch, nc, GATE_COLS, CHUNK), gb,
            gb.reshape(GATE_COLS, 1))
    return out.reshape(n, HEADS * dv)


def _gla_head_exact(q, k, v, cumh, state, o_ref, b, h, *, reverse, dk, dv):
    L = CHUNK
    nsub = L // SUB
    sub_lane = lax.broadcasted_iota(jnp.int32, (SUB, L), 1)
    sub_row = lax.broadcasted_iota(jnp.int32, (SUB, 1), 0)
    o_inter = _dot((q * jnp.exp(cumh)).astype(BF16), state.astype(BF16))
    for blk in range(nsub):
        r0 = blk * SUB
        if reverse:
            cs = cumh[r0 + SUB:r0 + SUB + 1, :] if blk < nsub - 1 else jnp.zeros((1, dk), F32)
            earlier = sub_lane >= r0 + SUB
        else:
            cs = cumh[r0 - 1:r0, :] if blk > 0 else jnp.zeros((1, dk), F32)
            earlier = sub_lane < r0
        q_b = q[r0:r0 + SUB, :]
        cum_b = cumh[r0:r0 + SUB, :]
        qd = (q_b * jnp.exp(cum_b - cs)).astype(BF16)
        kd = (k * jnp.exp(jnp.minimum(cs - cumh, 0.0))).astype(BF16)
        att = jnp.where(earlier, _dot_t(qd, kd), 0.0)
        for j in range(SUB):
            s_idx = r0 + j
            tmask = (sub_row <= j) if reverse else (sub_row >= j)
            e = jnp.where(tmask, cum_b - cumh[s_idx:s_idx + 1, :], NEG)
            col = jnp.sum(q_b * k[s_idx:s_idx + 1, :] * jnp.exp(e), axis=1, keepdims=True)
            att = jnp.where(sub_lane == s_idx, col, att)
        o_b = o_inter[r0:r0 + SUB, :] + _dot(att.astype(BF16), v)
        o_ref[b, r0:r0 + SUB, h * dv:(h + 1) * dv] = o_b.astype(o_ref.dtype)


def _gla_head_factored(q, k, v, cumh, state, mask, o_ref, b, h, *, dv):
    qe = (q * jnp.exp(cumh)).astype(BF16)
    ke = (k * jnp.exp(-cumh)).astype(BF16)
    att = jnp.where(mask, _dot_t(qe, ke), 0.0)
    o = _dot(qe, state.astype(BF16)) + _dot(att.astype(BF16), v)
    o_ref[b, :, h * dv:(h + 1) * dv] = o.astype(o_ref.dtype)


def _gla_kernel(q_ref, k_ref, v_ref, lr_ref, up_ref, ub_ref, o_ref, s_s, *, reverse, dk, dv, nc, batch):
    step = pl.program_id(0)

    @pl.when(step == 0)
    def _():
        s_s[...] = jnp.zeros_like(s_s)

    chunk = nc - 1 - step if reverse else step
    L = CHUNK
    valid_c = jnp.logical_or(chunk > 0, lax.broadcasted_iota(jnp.int32, (L, 1), 0) >= PAD)
    off = GATE_COLS + (B_RANK if reverse else 0)
    ri = lax.broadcasted_iota(jnp.int32, (L, L), 0)
    ci = lax.broadcasted_iota(jnp.int32, (L, L), 1)
    mask = (ci >= ri) if reverse else (ci <= ri)
    last = 0 if reverse else L - 1
    cums, tot_cols = [], []
    for b in range(batch):
        z = jnp.dot(lr_ref[b, :, off:off + B_RANK], up_ref[...], precision=HIGHEST, preferred_element_type=F32)
        la = jnp.where(valid_c, _log_sigmoid(z + ub_ref[...]) / B_TAU, 0.0)
        cums.append(jnp.dot(mask.astype(F32), la, precision=HIGHEST, preferred_element_type=F32))
        tot_cols.append(_tdot(la, jnp.ones((L, LANES), F32), precision=HIGHEST))
    lowest = jnp.min(jnp.concatenate([c[last:last + 1, :] for c in cums], axis=0))
    factorable = lowest >= -GLA_MAX_CHUNK_DECAY

    def run(factored):
        for b in range(batch):
            for h in range(HEADS):
                sl = slice(h * dk, (h + 1) * dk)
                q = jnp.where(valid_c, q_ref[b, :, sl], 0.0).astype(F32) * dk ** -0.5
                k = jnp.where(valid_c, k_ref[b, :, sl], 0.0).astype(F32)
                v = jnp.where(valid_c, v_ref[b, :, h * dv:(h + 1) * dv], 0.0).astype(BF16)
                cumh = cums[b][:, sl]
                tot = cumh[last:last + 1, :]
                state = s_s[b * HEADS + h]
                if factored:
                    _gla_head_factored(q, k, v, cumh, state, mask, o_ref, b, h, dv=dv)
                else:
                    _gla_head_exact(q, k, v, cumh, state, o_ref, b, h, reverse=reverse, dk=dk, dv=dv)
                kdec = (k * jnp.exp(tot - cumh)).astype(BF16)
                s_s[b * HEADS + h] = jnp.exp(tot_cols[b][sl, 0:1]) * state + _tdot(kdec, v)

    @pl.when(factorable)
    def _():
        run(True)

    @pl.when(jnp.logical_not(factorable))
    def _():
        run(False)


def _gla(proj, small, lr_up, lr_bias, batch, dk, dv, qkv_blocks, reverse):
    n = proj.shape[0]
    tp = n // batch
    nc = tp // CHUNK
    qb, kb, vb = qkv_blocks
    view = lambda a: a.reshape(batch, tp, a.shape[-1])
    out = pl.pallas_call(
        functools.partial(_gla_kernel, reverse=reverse, dk=dk, dv=dv, nc=nc, batch=batch),
        grid=(nc,),
        in_specs=[_chunk_spec(batch, HEADS * dk, qb, nc, reverse), _chunk_spec(batch, HEADS * dk, kb, nc, reverse),
                  _chunk_spec(batch, HEADS * dv, vb, nc, reverse), _chunk_spec(batch, LANES, 0, nc, reverse),
                  pl.BlockSpec((B_RANK, HEADS * dk), lambda i: (0, 0)),
                  pl.BlockSpec((1, HEADS * dk), lambda i: (0, 0))],
        out_specs=_chunk_spec(batch, HEADS * dv, 0, nc, reverse),
        out_shape=jax.ShapeDtypeStruct((batch, tp, HEADS * dv), BF16),
        scratch_shapes=[pltpu.VMEM((batch * HEADS, dk, dv), F32)],
        compiler_params=_params("arbitrary"),
        name="gla_bwd" if reverse else "gla_fwd")(
            view(proj), view(proj), view(proj), view(small), lr_up.astype(F32), lr_bias.reshape(1, -1).astype(F32))
    return out.reshape(n, HEADS * dv)


def _head_norm(x, g, dv):
    parts = []
    for h in range(HEADS):
        xh = x[:, h * dv:(h + 1) * dv]
        parts.append(xh * lax.rsqrt(jnp.mean(xh * xh, axis=-1, keepdims=True) + EPS))
    return jnp.concatenate(parts, axis=1) * g


def _even_combine_kernel(af_ref, ab_ref, bf_ref, bb_ref, oa_ref, gb_ref, na_ref, nb_ref, o_ref, *, tr, tp, dv):
    rows = (pl.program_id(0) * tr) % tp + lax.broadcasted_iota(jnp.int32, (tr, 1), 0)
    valid = rows >= PAD
    w = HEADS * dv
    ha = af_ref[...].astype(F32) + ab_ref[...].astype(F32)
    ya = jax.nn.sigmoid(oa_ref[...].astype(F32)) * _head_norm(ha, na_ref[...], dv)
    o_ref[:, :w] = jnp.where(valid, ya, 0.0).astype(o_ref.dtype)
    hb = bf_ref[...].astype(F32) + bb_ref[...].astype(F32)
    gb = gb_ref[...].astype(F32)
    yb = gb * jax.nn.sigmoid(gb) * _head_norm(hb, nb_ref[...], dv)
    o_ref[:, w:] = jnp.where(valid, yb, 0.0).astype(o_ref.dtype)


def _even_combine(ha_f, ha_b, hb_f, hb_b, proj, norm_a, norm_b, batch, dv, oa_block, gb_block):
    n, w = ha_f.shape
    tp = n // batch
    tr = _tile(tp, 384, BF16_ROWS)
    row = pl.BlockSpec((tr, w), lambda i: (i, 0))
    vec = pl.BlockSpec((1, w), lambda i: (0, 0))
    return pl.pallas_call(
        functools.partial(_even_combine_kernel, tr=tr, tp=tp, dv=dv),
        grid=(n // tr,),
        in_specs=[row, row, row, row, pl.BlockSpec((tr, w), lambda i: (i, oa_block)),
                  pl.BlockSpec((tr, w), lambda i: (i, gb_block)), vec, vec],
        out_specs=pl.BlockSpec((tr, 2 * w), lambda i: (i, 0)),
        out_shape=jax.ShapeDtypeStruct((n, 2 * w), BF16),
        compiler_params=_params("parallel"), name="even_combine")(
            ha_f, ha_b, hb_f, hb_b, proj, proj, norm_a.reshape(1, w).astype(F32), norm_b.reshape(1, w).astype(F32))


def _block_scan(a, b, reverse):
    sub = lax.broadcasted_iota(jnp.int32, a.shape, 1)
    for k in (1, 2, 4):
        if reverse:
            a_sh, b_sh, m = pltpu.roll(a, SUBLANES - k, 1), pltpu.roll(b, SUBLANES - k, 1), sub < SUBLANES - k
        else:
            a_sh, b_sh, m = pltpu.roll(a, k, 1), pltpu.roll(b, k, 1), sub >= k
        b = jnp.where(m, a * b_sh + b, b)
        a = jnp.where(m, a * a_sh, a)
    return a, b


def _rglru_kernel(cur_ref, prev_ref, next_ref, cw_ref, cb_ref, wr_ref, br_ref, wi_ref, bi_ref, lam_ref,
                  o_ref, ext_s, a_s, b_s, h_s, carry_s, *, reverse, tt, nt):
    step = pl.program_id(2)

    @pl.when(step == 0)
    def _():
        carry_s[...] = jnp.zeros_like(carry_s)

    t = nt - 1 - step if reverse else step
    row0 = t * tt
    _fill_ext(ext_s, cur_ref, prev_ref, next_ref, row0, tt, t < nt - 1)
    u = _conv_taps(ext_s, cw_ref, tt) + cb_ref[...]
    ub = u.astype(BF16)
    r = _sigmoid(_dot(ub, wr_ref[0]) + br_ref[...])
    gi = _sigmoid(_dot(ub, wi_ref[0]) + bi_ref[...])
    lam = lam_ref[...]
    softplus = jnp.maximum(-lam, 0.0) + jnp.log1p(jnp.exp(-jnp.abs(lam)))
    log_a = -RNN_C * r * softplus
    a = jnp.exp(log_a)
    rows = row0 + lax.broadcasted_iota(jnp.int32, (tt, 1), 0)
    inp = jnp.where(rows >= PAD, jnp.sqrt(1.0 - a * a) * (gi * u), 0.0)
    c = a.shape[1]
    ng = tt // SUBLANES
    a_g, b_g = _block_scan(a.reshape(ng, SUBLANES, c), inp.reshape(ng, SUBLANES, c), reverse)
    a_s[...] = a_g.reshape(tt, c)
    b_s[...] = b_g.reshape(tt, c)
    out_row = 0 if reverse else SUBLANES - 1

    def body(i, carry):
        g = ng - 1 - i if reverse else i
        r0 = pl.multiple_of(g * SUBLANES, SUBLANES)
        hh = b_s[pl.ds(r0, SUBLANES), :] + a_s[pl.ds(r0, SUBLANES), :] * carry
        h_s[pl.ds(r0, SUBLANES), :] = hh
        return hh[out_row:out_row + 1, :]

    carry_s[...] = lax.fori_loop(0, ng, body, carry_s[...])
    o_ref[...] = h_s[...].astype(o_ref.dtype)


def _pair_blocks(w):
    nb, r, _ = w.shape
    z = jnp.zeros((nb // 2, r, r), w.dtype)
    top = jnp.concatenate([w[0::2], z], axis=2)
    bot = jnp.concatenate([z, w[1::2]], axis=2)
    return jnp.concatenate([top, bot], axis=1).astype(BF16)


def _rglru(proj, conv_w, conv_b, w_r, b_r, w_i, b_i, lam, batch, d_rnn, reverse):
    n = proj.shape[0]
    tp = n // batch
    cw = 2 * d_rnn // RNN_BLOCKS
    ncb = d_rnn // cw
    tt = _tile(tp, 688, BF16_ROWS)
    nt = tp // tt
    tmap = (lambda t: nt - 1 - t) if reverse else (lambda t: t)
    cur, prev, nxt = _halo_specs(cw, lambda j: ncb + j, tt, tp, n)(tmap)
    def swap(spec):
        f = spec.index_map
        return pl.BlockSpec(spec.block_shape, lambda b, j, t: f(b, t, j))
    vec = pl.BlockSpec((1, cw), lambda b, j, t: (0, j))
    wspec = pl.BlockSpec((1, cw, cw), lambda b, j, t: (j, 0, 0))
    row = lambda x: x.reshape(1, d_rnn).astype(F32)
    return pl.pallas_call(
        functools.partial(_rglru_kernel, reverse=reverse, tt=tt, nt=nt),
        grid=(batch, ncb, nt),
        in_specs=[swap(cur), swap(prev), swap(nxt), pl.BlockSpec((4, cw), lambda b, j, t: (0, j)), vec,
                  wspec, vec, wspec, vec, vec],
        out_specs=pl.BlockSpec((tt, cw), lambda b, j, t: (b * nt + tmap(t), j)),
        out_shape=jax.ShapeDtypeStruct((n, d_rnn), BF16),
        scratch_shapes=[pltpu.VMEM((tt + 2 * SUBLANES, cw), F32), pltpu.VMEM((tt, cw), F32),
                        pltpu.VMEM((tt, cw), F32), pltpu.VMEM((tt, cw), F32), pltpu.VMEM((1, cw), F32)],
        compiler_params=_params("parallel", "parallel", "arbitrary"),
        name="rglru_bwd" if reverse else "rglru_fwd")(
            proj, proj, proj, conv_w.astype(F32), row(conv_b), _pair_blocks(w_r), row(b_r),
            _pair_blocks(w_i), row(b_i), row(lam))


def _odd_combine_kernel(g_ref, hf_ref, hb_ref, o_ref, *, tr, tp):
    rows = (pl.program_id(0) * tr) % tp + lax.broadcasted_iota(jnp.int32, (tr, 1), 0)
    y = jax.nn.gelu(g_ref[...].astype(F32)) * (hf_ref[...].astype(F32) + hb_ref[...].astype(F32))
    o_ref[...] = jnp.where(rows >= PAD, y, 0.0).astype(o_ref.dtype)


def _odd_combine(proj, hf, hb, batch):
    n, w = hf.shape
    tp = n // batch
    tr = _tile(tp, 384, BF16_ROWS)
    row = pl.BlockSpec((tr, w), lambda i: (i, 0))
    return pl.pallas_call(
        functools.partial(_odd_combine_kernel, tr=tr, tp=tp), grid=(n // tr,),
        in_specs=[row, row, row], out_specs=row, out_shape=jax.ShapeDtypeStruct((n, w), BF16),
        compiler_params=_params("parallel"), name="odd_combine")(proj, hf, hb)


def _route(x):
    lane = lax.broadcasted_iota(jnp.int32, x.shape, 1)
    big = jnp.int32(2 * LANES)
    gmask = lane < N_GROUPS
    gmax = jnp.max(jnp.where(gmask, x, -jnp.inf), axis=1, keepdims=True)
    ge = jnp.where(gmask, jnp.exp(x - gmax), 0.0)
    gp = ge / jnp.sum(ge, axis=1, keepdims=True)
    gval = jnp.max(gp, axis=1, keepdims=True)
    gidx = jnp.min(jnp.where(jnp.logical_and(gmask, gp == gval), lane, big), axis=1, keepdims=True)
    lo = N_GROUPS + gidx * EXPERTS_PER_GROUP
    emask = jnp.logical_and(lane >= lo, lane < lo + EXPERTS_PER_GROUP)
    emax = jnp.max(jnp.where(emask, x, -jnp.inf), axis=1, keepdims=True)
    ee = jnp.where(emask, jnp.exp(x - emax), 0.0)
    ep = ee / jnp.sum(ee, axis=1, keepdims=True)
    v1 = jnp.max(jnp.where(emask, ep, -1.0), axis=1, keepdims=True)
    i1 = jnp.min(jnp.where(jnp.logical_and(emask, ep == v1), lane, big), axis=1, keepdims=True)
    rest = jnp.logical_and(emask, lane != i1)
    v2 = jnp.max(jnp.where(rest, ep, -1.0), axis=1, keepdims=True)
    i2 = jnp.min(jnp.where(jnp.logical_and(rest, ep == v2), lane, big), axis=1, keepdims=True)
    tot = v1 + v2
    comb = jnp.where(lane == i1, v1 / tot * gval, jnp.where(lane == i2, v2 / tot * gval, 0.0))
    return comb, gidx


def _norm_route_kernel(h_ref, g_ref, w2_ref, wh_ref, b_ref, slab_ref, oh_ref, *, nchunk):
    x = h_ref[...]
    y = x * lax.rsqrt(jnp.mean(x * x, axis=-1, keepdims=True) + EPS) * g_ref[...]
    yh = y.astype(BF16)
    yl = (y - yh.astype(F32)).astype(BF16)
    r1 = _dot(yh, w2_ref[...])
    logits = r1[:, :LANES] + r1[:, LANES:] + _dot(yl, wh_ref[...]) + b_ref[...]
    comb, gidx = _route(logits)
    for j in range(nchunk):
        slab_ref[:, j, :] = y[:, j * LANES:(j + 1) * LANES]
    slab_ref[:, nchunk, :] = comb
    for j in range(nchunk + 1, slab_ref.shape[1]):
        slab_ref[:, j, :] = jnp.zeros_like(comb)
    lane = lax.broadcasted_iota(jnp.int32, comb.shape, 1)
    oh_ref[...] = jnp.where(lane == gidx, 1.0, 0.0).astype(oh_ref.dtype)


def _norm_route(h, g, wg, bg, we, be):
    n, d = h.shape
    nchunk = d // LANES
    srows = (nchunk + 1 + SUBLANES - 1) // SUBLANES * SUBLANES
    tr = _tile(n, 192, BF16_ROWS)
    zpad = LANES - N_GROUPS - N_EXPERTS
    wr = jnp.concatenate([wg, we, jnp.zeros((d, zpad), F32)], axis=1)
    wh = wr.astype(BF16)
    wl = (wr - wh.astype(F32)).astype(BF16)
    bias = jnp.concatenate([bg.astype(F32), be.astype(F32), jnp.zeros((zpad,), F32)]).reshape(1, LANES)
    return pl.pallas_call(
        functools.partial(_norm_route_kernel, nchunk=nchunk), grid=(n // tr,),
        in_specs=[pl.BlockSpec((tr, d), lambda i: (i, 0)), pl.BlockSpec((1, d), lambda i: (0, 0)),
                  pl.BlockSpec((d, 2 * LANES), lambda i: (0, 0)), pl.BlockSpec((d, LANES), lambda i: (0, 0)),
                  pl.BlockSpec((1, LANES), lambda i: (0, 0))],
        out_specs=[pl.BlockSpec((tr, srows, LANES), lambda i: (i, 0, 0)), pl.BlockSpec((tr, LANES), lambda i: (i, 0))],
        out_shape=[jax.ShapeDtypeStruct((n, srows, LANES), F32), jax.ShapeDtypeStruct((n, LANES), BF16)],
        compiler_params=_params("parallel"), name="norm_route")(
            h, g.reshape(1, d).astype(F32), jnp.concatenate([wh, wl], axis=1), wh, bias)


def _rank_kernel(oh_ref, g_ref, rank_ref, cnt_ref, carry_s):
    @pl.when(pl.program_id(0) == 0)
    def _():
        carry_s[...] = jnp.zeros_like(carry_s)

    tr = oh_ref.shape[0]
    sel = jnp.where(lax.broadcasted_iota(jnp.int32, (SUBLANES, LANES), 0) ==
                    lax.broadcasted_iota(jnp.int32, (SUBLANES, LANES), 1), 1.0, 0.0).astype(BF16)
    oh_t = _dot_t(sel, oh_ref[...])
    before = (lax.broadcasted_iota(jnp.int32, (tr, tr), 0) < lax.broadcasted_iota(jnp.int32, (tr, tr), 1))
    cum = _dot(oh_t.astype(BF16), jnp.where(before, 1.0, 0.0).astype(BF16)) + carry_s[:, 0:1]
    gid = lax.broadcasted_iota(jnp.int32, (SUBLANES, tr), 0).astype(F32)
    rank_ref[0] = jnp.sum(oh_t * cum, axis=0, keepdims=True).astype(jnp.int32)
    g_ref[0] = jnp.sum(oh_t * gid, axis=0, keepdims=True).astype(jnp.int32)
    carry_s[...] = carry_s[...] + jnp.sum(oh_t, axis=1, keepdims=True)
    cnt_ref[...] = carry_s[...]


def _rank(onehot):
    n = onehot.shape[0]
    tr = _tile(n, 384, LANES)
    row = pl.BlockSpec((1, 1, tr), lambda i: (i, 0, 0))
    g, rank, cnt = pl.pallas_call(
        _rank_kernel, grid=(n // tr,), in_specs=[pl.BlockSpec((tr, LANES), lambda i: (i, 0))],
        out_specs=[row, row, pl.BlockSpec((SUBLANES, LANES), lambda i: (0, 0))],
        out_shape=[jax.ShapeDtypeStruct((n // tr, 1, tr), jnp.int32), jax.ShapeDtypeStruct((n // tr, 1, tr), jnp.int32),
                   jax.ShapeDtypeStruct((SUBLANES, LANES), F32)],
        scratch_shapes=[pltpu.VMEM((SUBLANES, LANES), F32)],
        compiler_params=_params("arbitrary"), name="rank")(onehot)
    return g.reshape(n), rank.reshape(n), cnt[:N_GROUPS, 0].astype(jnp.int32)


def _invert_kernel(g_ref, rank_ref, cnt_ref, pos_ref, idx_ref, tg_ref, *, n, tm, ntiles):
    bases = [jnp.int32(0)]
    for g in range(N_GROUPS - 1):
        bases.append(bases[-1] + (cnt_ref[g] + tm - 1) // tm * tm)

    def zero(i, c):
        idx_ref[i] = 0
        return c
    lax.fori_loop(0, ntiles * tm, zero, 0, unroll=8)

    def place(t, c):
        g = g_ref[t]
        base = bases[0]
        for k in range(1, N_GROUPS):
            base = jnp.where(g == k, bases[k], base)
        p = base + rank_ref[t]
        pos_ref[t] = p
        idx_ref[p] = t
        return c
    lax.fori_loop(0, n, place, 0, unroll=8)

    def tile_group(i, c):
        r = i * tm
        tg = jnp.int32(0)
        for k in range(1, N_GROUPS):
            tg = tg + (r >= bases[k]).astype(jnp.int32)
        tg_ref[i] = tg
        return c
    lax.fori_loop(0, ntiles, tile_group, 0)


def _invert(g, rank, cnt, tm, ntiles):
    n = g.shape[0]
    smem = pl.BlockSpec(memory_space=pltpu.SMEM)
    return pl.pallas_call(
        functools.partial(_invert_kernel, n=n, tm=tm, ntiles=ntiles),
        in_specs=[smem, smem, smem], out_specs=[smem, smem, smem],
        out_shape=[jax.ShapeDtypeStruct((n,), jnp.int32), jax.ShapeDtypeStruct((ntiles * tm,), jnp.int32),
                   jax.ShapeDtypeStruct((ntiles,), jnp.int32)],
        name="invert")(g, rank, cnt)


def _row_copy(src_hbm, buf, sem, src_row, slot, dst_row):
    return pltpu.make_async_copy(src_hbm.at[pl.ds(src_row, 1)], buf.at[slot, pl.ds(dst_row, 1)], sem.at[slot])


def _gather_rows(index_ref, src_hbm, buf, sem, rows):
    i = pl.program_id(0)
    steps = pl.num_programs(0)

    def issue(step, slot):
        def body(r, c):
            _row_copy(src_hbm, buf, sem, index_ref[step * rows + r], slot, r).start()
            return c
        lax.fori_loop(0, rows, body, 0, unroll=8)

    @pl.when(i == 0)
    def _():
        issue(0, 0)

    @pl.when(i + 1 < steps)
    def _():
        issue(i + 1, (i + 1) % 2)

    slot = i % 2

    def wait(r, c):
        _row_copy(src_hbm, buf, sem, 0, slot, r).wait()
        return c
    lax.fori_loop(0, rows, wait, 0, unroll=8)
    return slot


def _dispatch_kernel(idx_ref, slab_hbm, xs_ref, cs_ref, buf, sem, *, nchunk):
    slot = _gather_rows(idx_ref, slab_hbm, buf, sem, xs_ref.shape[0])
    for j in range(nchunk):
        xs_ref[:, j * LANES:(j + 1) * LANES] = buf[slot, :, j, :].astype(xs_ref.dtype)
    cs_ref[...] = buf[slot, :, nchunk, :]


def _dispatch(idx, slab, tm, ntiles, d):
    srows = slab.shape[1]
    return pl.pallas_call(
        functools.partial(_dispatch_kernel, nchunk=d // LANES),
        grid_spec=pltpu.PrefetchScalarGridSpec(
            num_scalar_prefetch=1, grid=(ntiles,),
            in_specs=[pl.BlockSpec(memory_space=pl.ANY)],
            out_specs=[pl.BlockSpec((tm, d), lambda i, idx: (i, 0)), pl.BlockSpec((tm, LANES), lambda i, idx: (i, 0))],
            scratch_shapes=[pltpu.VMEM((2, tm, srows, LANES), F32), pltpu.SemaphoreType.DMA((2,))]),
        out_shape=[jax.ShapeDtypeStruct((ntiles * tm, d), BF16), jax.ShapeDtypeStruct((ntiles * tm, LANES), F32)],
        compiler_params=_params("arbitrary"), name="dispatch")(idx, slab)


def _expert_kernel(tg_ref, x_ref, c_ref, w1_ref, w3_ref, w2_ref, o_ref, acc_s, *, nchunk):
    i = pl.program_id(0)
    e = pl.program_id(1)

    @pl.when(e == 0)
    def _():
        acc_s[...] = jnp.zeros_like(acc_s)

    a = x_ref[...]
    h1 = _dot(a, w1_ref[0])
    h3 = _dot(a, w3_ref[0])
    comb = c_ref[...]
    lane = lax.broadcasted_iota(jnp.int32, comb.shape, 1)
    c = jnp.sum(jnp.where(lane == N_GROUPS + tg_ref[i] * EXPERTS_PER_GROUP + e, comb, 0.0), axis=1, keepdims=True)
    hid = (h1 * jax.nn.sigmoid(h1) * h3 * c).astype(BF16)
    acc_s[...] += _dot(hid, w2_ref[0])

    @pl.when(e == EXPERTS_PER_GROUP - 1)
    def _():
        for j in range(nchunk):
            o_ref[:, j, :] = acc_s[:, j * LANES:(j + 1) * LANES]


def _experts(tg, xs, cs, w1, w3, w2, tm):
    rows, d = xs.shape
    f = w1.shape[-1]
    nchunk = d // LANES
    wmap = lambda i, e, tg: (tg[i] * EXPERTS_PER_GROUP + e, 0, 0)
    return pl.pallas_call(
        functools.partial(_expert_kernel, nchunk=nchunk),
        grid_spec=pltpu.PrefetchScalarGridSpec(
            num_scalar_prefetch=1, grid=(rows // tm, EXPERTS_PER_GROUP),
            in_specs=[pl.BlockSpec((tm, d), lambda i, e, tg: (i, 0)), pl.BlockSpec((tm, LANES), lambda i, e, tg: (i, 0)),
                      pl.BlockSpec((1, d, f), wmap), pl.BlockSpec((1, d, f), wmap), pl.BlockSpec((1, f, d), wmap)],
            out_specs=pl.BlockSpec((tm, nchunk, LANES), lambda i, e, tg: (i, 0, 0)),
            scratch_shapes=[pltpu.VMEM((tm, d), F32)]),
        out_shape=jax.ShapeDtypeStruct((rows, nchunk, LANES), F32),
        compiler_params=_params("parallel", "arbitrary"), name="experts")(tg, xs, cs, w1, w3, w2)


def _collect_kernel(pos_ref, ys_hbm, o_ref, buf, sem, *, nchunk):
    slot = _gather_rows(pos_ref, ys_hbm, buf, sem, o_ref.shape[0])
    for j in range(nchunk):
        o_ref[:, j * LANES:(j + 1) * LANES] = buf[slot, :, j, :].astype(o_ref.dtype)


def _collect(pos, ys):
    n = pos.shape[0]
    _, nchunk, _ = ys.shape
    d = nchunk * LANES
    tr = _tile(n, 384, BF16_ROWS)
    return pl.pallas_call(
        functools.partial(_collect_kernel, nchunk=nchunk),
        grid_spec=pltpu.PrefetchScalarGridSpec(
            num_scalar_prefetch=1, grid=(n // tr,),
            in_specs=[pl.BlockSpec(memory_space=pl.ANY)],
            out_specs=pl.BlockSpec((tr, d), lambda i, pos: (i, 0)),
            scratch_shapes=[pltpu.VMEM((2, tr, nchunk, LANES), F32), pltpu.SemaphoreType.DMA((2,))]),
        out_shape=jax.ShapeDtypeStruct((n, d), BF16),
        compiler_params=_params("arbitrary"), name="collect")(pos, ys)


def _moe_layer(h, ffn_g, wg, bg, we, be, w1, w3, w2):
    n, d = h.shape
    f = w1.shape[-1]
    tm = 512 if n >= 4096 else 128
    ntiles = (n + N_GROUPS * (tm - 1)) // tm
    slab, onehot = _norm_route(h, ffn_g, wg, bg, we, be)
    g, rank, cnt = _rank(onehot)
    pos, idx, tg = _invert(g, rank, cnt, tm, ntiles)
    xs, cs = _dispatch(idx, slab, tm, ntiles, d)
    ys = _experts(tg, xs, cs, w1.reshape(N_EXPERTS, d, f).astype(BF16), w3.reshape(N_EXPERTS, d, f).astype(BF16),
                  w2.reshape(N_EXPERTS, f, d).astype(BF16), tm)
    return _collect(pos, ys)


def _even_layer(h, delta, g, w_in, gate_bias, qk_conv, lr_up, lr_bias, norm_a, norm_b, w_out, batch):
    n, d = h.shape
    dk, dv = d // 16, d // 8
    qk_w, v_w = HEADS * dk, HEADS * dv
    a_end = 2 * qk_w + 2 * v_w
    b_start = a_end + GATE_COLS
    b_end = b_start + 2 * qk_w + 2 * v_w
    w_main = jnp.concatenate([w_in[:, :a_end], w_in[:, b_start:b_end]], axis=1).astype(BF16)
    w_small = jnp.concatenate([w_in[:, a_end:b_start], w_in[:, b_end:],
                               jnp.zeros((d, LANES - GATE_COLS - 2 * B_RANK), F32)], axis=1).astype(BF16)
    if delta is None:
        _, hn, _ = _norm(h, g)
    else:
        h, hn, _ = _norm(h, g, delta=delta)
    proj = _matmul(hn, w_main, BF16)
    small = _matmul(hn, w_small, F32, tn_target=LANES)
    gates_t = small[:, :GATE_COLS].reshape(n // CHUNK, CHUNK, GATE_COLS).transpose(0, 2, 1)
    qk = _qk_conv(proj, qk_conv, batch, dk)
    va_blk, oa_blk = 2 * qk_w // v_w, (2 * qk_w + v_w) // v_w
    b0 = a_end
    qb_blk, kb_blk = b0 // qk_w, (b0 + qk_w) // qk_w
    vb_blk, gb_blk = (b0 + 2 * qk_w) // v_w, (b0 + 2 * qk_w + v_w) // v_w
    ha, hb = [], []
    for rev in (False, True):
        ha.append(_mlstm(qk, proj, small, gates_t, gate_bias, batch, dk, dv, va_blk, rev))
        hb.append(_gla(proj, small, lr_up[int(rev)], lr_bias[int(rev)], batch, dk, dv, (qb_blk, kb_blk, vb_blk), rev))
    y = _even_combine(ha[0], ha[1], hb[0], hb[1], proj, norm_a, norm_b, batch, dv, oa_blk, gb_blk)
    return _matmul(y, w_out.astype(BF16), F32, res=h)


def _odd_layer(h, delta, g, w_in, conv_w, conv_b, w_r, b_r, w_i, b_i, lam, w_out, batch):
    d_rnn = w_out.shape[0]
    h, hn, _ = _norm(h, g, delta=delta)
    proj = _matmul(hn, w_in.astype(BF16), BF16)
    hs = [_rglru(proj, conv_w, conv_b, w_r[i], b_r[i], w_i[i], b_i[i], lam[i], batch, d_rnn, bool(i)) for i in (0, 1)]
    y = _odd_combine(proj, hs[0], hs[1], batch)
    return _matmul(y, w_out.astype(BF16), F32, res=h)


def kernel(x, meta_tokens, mix_norm, ffn_norm, final_norm, ev_w_in, ev_gate_bias, ev_qk_conv, ev_lr_up, ev_lr_bias, ev_norm_a, ev_norm_b, ev_w_out, od_w_in, od_conv, od_conv_bias, od_w_r, od_b_r, od_w_i, od_b_i, od_lambda, od_w_out, moe_wg, moe_bg, moe_we, moe_be, moe_w1, moe_w3, moe_w2):
    batch, seq, d = x.shape
    depth = mix_norm.shape[0]
    assert seq % CHUNK == 0 and d % 16 == 0
    frame = jnp.concatenate([jnp.zeros((batch, PAD, d), x.dtype),
                             jnp.broadcast_to(meta_tokens.astype(x.dtype)[None], (batch, N_META, d)), x], axis=1)
    h = frame.reshape(batch * (PAD + N_META + seq), d)
    delta = None
    for layer in range(depth):
        if layer % 2 == 0:
            e = layer // 2
            h = _even_layer(h, delta, mix_norm[layer], ev_w_in[e], ev_gate_bias[e], ev_qk_conv[e], ev_lr_up[e],
                            ev_lr_bias[e], ev_norm_a[e], ev_norm_b[e], ev_w_out[e], batch)
        else:
            o = layer // 2
            h = _odd_layer(h, delta, mix_norm[layer], od_w_in[o], od_conv[o], od_conv_bias[o], od_w_r[o], od_b_r[o],
                           od_w_i[o], od_b_i[o], od_lambda[o], od_w_out[o], batch)
        delta = _moe_layer(h, ffn_norm[layer], moe_wg[layer], moe_bg[layer], moe_we[layer], moe_be[layer],
                              moe_w1[layer], moe_w3[layer], moe_w2[layer])
    out = _final_norm(h, delta, final_norm, batch, seq)
    return out.reshape(batch, seq, d)
```

```python
import functools

import jax
import jax.numpy as jnp
from jax import lax
from jax.experimental import pallas as pl
from jax.experimental.pallas import tpu as pltpu

F32 = jnp.float32
BF16 = jnp.bfloat16
HIGHEST = lax.Precision.HIGHEST

N_META = 16
CHUNK = 64
PAD = CHUNK - N_META
SUB = 16
EPS = 1e-6
NEG = -1e30
HEADS = 4
GATE_COLS = 4 * HEADS
B_RANK = 16
B_TAU = 16.0
GLA_MAX_CHUNK_DECAY = 80.0
RNN_BLOCKS = 16
RNN_C = 8.0
N_GROUPS = 4
EXPERTS_PER_GROUP = 8
N_EXPERTS = N_GROUPS * EXPERTS_PER_GROUP
LANES = 128
SUBLANES = 8
BF16_ROWS = 16
VMEM_LIMIT = 56 * 1024 * 1024


def _params(*sem):
    return pltpu.CompilerParams(dimension_semantics=sem, vmem_limit_bytes=VMEM_LIMIT)


def _tile(n, target, mult):
    best = None
    for t in range(mult, min(n, target) + 1, mult):
        if n % t == 0:
            best = t
    assert best is not None, (n, target, mult)
    return best


def _log_sigmoid(x):
    return jnp.minimum(x, 0.0) - jnp.log1p(jnp.exp(-jnp.abs(x)))


def _sigmoid(x):
    return 0.5 * jnp.tanh(0.5 * x) + 0.5


def _dot(a, b):
    return jnp.dot(a, b, preferred_element_type=F32)


def _dot_t(a, b):
    return lax.dot_general(a, b, (((1,), (1,)), ((), ())), preferred_element_type=F32)


def _tdot(a, b, precision=None):
    return lax.dot_general(a, b, (((0,), (0,)), ((), ())), preferred_element_type=F32,
                           precision=precision)


def _norm_kernel(h_ref, d_ref, g_ref, hnew_ref, hn_ref):
    x = h_ref[...] + d_ref[...].astype(F32)
    hnew_ref[...] = x
    y = x * lax.rsqrt(jnp.mean(x * x, axis=-1, keepdims=True) + EPS) * g_ref[...]
    hn_ref[...] = y.astype(hn_ref.dtype)


def _norm(h, g, delta):
    n, d = h.shape
    tr = _tile(n, 192, BF16_ROWS)
    row = pl.BlockSpec((tr, d), lambda i: (i, 0))
    return pl.pallas_call(
        _norm_kernel, grid=(n // tr,),
        in_specs=[row, row, pl.BlockSpec((1, d), lambda i: (0, 0))], out_specs=[row, row],
        out_shape=[jax.ShapeDtypeStruct((n, d), F32), jax.ShapeDtypeStruct((n, d), BF16)],
        compiler_params=_params("parallel"), name="norm")(h, delta, g.reshape(1, d).astype(F32))


def _frame_norm_kernel(x_ref, meta_ref, g_ref, h_ref, hn_ref):
    d = h_ref.shape[1]
    first = jnp.concatenate([jnp.zeros((PAD, d), F32), meta_ref[...]], axis=0)
    x = jnp.where(pl.program_id(1) == 0, first, x_ref[0])
    h_ref[...] = x
    y = x * lax.rsqrt(jnp.mean(x * x, axis=-1, keepdims=True) + EPS) * g_ref[...]
    hn_ref[...] = y.astype(hn_ref.dtype)


def _frame_norm(x, meta, g):
    batch, seq, d = x.shape
    nc = (PAD + N_META + seq) // CHUNK
    out = pl.BlockSpec((CHUNK, d), lambda b, i: (b * nc + i, 0))
    return pl.pallas_call(
        _frame_norm_kernel, grid=(batch, nc),
        in_specs=[pl.BlockSpec((1, CHUNK, d), lambda b, i: (b, jnp.maximum(i - 1, 0), 0)),
                  pl.BlockSpec((N_META, d), lambda b, i: (0, 0)), pl.BlockSpec((1, d), lambda b, i: (0, 0))],
        out_specs=[out, out],
        out_shape=[jax.ShapeDtypeStruct((batch * nc * CHUNK, d), F32), jax.ShapeDtypeStruct((batch * nc * CHUNK, d), BF16)],
        compiler_params=_params("parallel", "arbitrary"), name="frame_norm")(
            x, meta.astype(F32), g.reshape(1, d).astype(F32))


def _final_norm_kernel(h_ref, d_ref, g_ref, o_ref):
    x = h_ref[...] + d_ref[...].astype(F32)
    o_ref[...] = x * lax.rsqrt(jnp.mean(x * x, axis=-1, keepdims=True) + EPS) * g_ref[...]


def _final_norm(h, delta, g, batch, seq):
    n, d = h.shape
    nc = n // batch // CHUNK
    src = pl.BlockSpec((CHUNK, d), lambda b, i: (b * nc + 1 + i, 0))
    return pl.pallas_call(
        _final_norm_kernel, grid=(batch, nc - 1),
        in_specs=[src, src, pl.BlockSpec((1, d), lambda b, i: (0, 0))],
        out_specs=pl.BlockSpec((CHUNK, d), lambda b, i: (b * (nc - 1) + i, 0)),
        out_shape=jax.ShapeDtypeStruct((batch * seq, d), F32),
        compiler_params=_params("parallel", "parallel"), name="final_norm")(h, delta, g.reshape(1, d).astype(F32))


def _mm_kernel(*refs, has_res):
    if has_res:
        a_ref, w_ref, r_ref, o_ref = refs
    else:
        a_ref, w_ref, o_ref = refs
    acc = _dot(a_ref[...], w_ref[...])
    if has_res:
        acc = acc + r_ref[...]
    o_ref[...] = acc.astype(o_ref.dtype)


def _matmul(a, w, out_dtype, res=None, tm_target=688, tn_target=512):
    n, k = a.shape
    m = w.shape[1]
    tm = _tile(n, tm_target, BF16_ROWS)
    tn = _tile(m, tn_target, LANES)
    in_specs = [pl.BlockSpec((tm, k), lambda i, j: (i, 0)), pl.BlockSpec((k, tn), lambda i, j: (0, j))]
    args = [a, w]
    if res is not None:
        in_specs.append(pl.BlockSpec((tm, tn), lambda i, j: (i, j)))
        args.append(res)
    return pl.pallas_call(
        functools.partial(_mm_kernel, has_res=res is not None),
        grid=(n // tm, m // tn), in_specs=in_specs,
        out_specs=pl.BlockSpec((tm, tn), lambda i, j: (i, j)),
        out_shape=jax.ShapeDtypeStruct((n, m), out_dtype),
        compiler_params=_params("parallel", "arbitrary"), name="matmul")(*args)


def _conv_taps(ext_s, cw_ref, tt):
    out = cw_ref[0:1, :] * ext_s[pl.ds(SUBLANES - 2, tt), :]
    for j in range(1, 4):
        out = out + cw_ref[j:j + 1, :] * ext_s[pl.ds(SUBLANES - 2 + j, tt), :]
    return out


def _fill_ext(ext_s, cur_ref, prev_ref, next_ref, row0, tt, has_next):
    rows = row0 + lax.broadcasted_iota(jnp.int32, (tt, 1), 0)
    ext_s[pl.ds(SUBLANES, tt), :] = jnp.where(rows >= PAD, cur_ref[...].astype(F32), 0.0)
    prow = row0 - SUBLANES + lax.broadcasted_iota(jnp.int32, (SUBLANES, 1), 0)
    ext_s[pl.ds(0, SUBLANES), :] = jnp.where(prow >= PAD, prev_ref[...].astype(F32)[SUBLANES:, :], 0.0)
    ext_s[pl.ds(SUBLANES + tt, SUBLANES), :] = jnp.where(has_next, next_ref[...].astype(F32)[:SUBLANES, :], 0.0)


def _qkconv_kernel(cur_ref, prev_ref, next_ref, cw_ref, o_ref, ext_s, *, tt, nt, kscale, half):
    t = pl.program_id(1)
    row0 = t * tt
    _fill_ext(ext_s, cur_ref, prev_ref, next_ref, row0, tt, t < nt - 1)
    y = _conv_taps(ext_s, cw_ref, tt)
    y = y * jax.nn.sigmoid(y)
    col = lax.broadcasted_iota(jnp.int32, (1, 2 * half), 1)
    y = y * jnp.where(col >= half, kscale, 1.0)
    rows = row0 + lax.broadcasted_iota(jnp.int32, (tt, 1), 0)
    o_ref[...] = jnp.where(rows >= PAD, y, 0.0).astype(o_ref.dtype)


def _halo_specs(width, col_block, tt, tp, n):
    per_b, per_t = tp // BF16_ROWS, tt // BF16_ROWS
    last = n // BF16_ROWS - 1

    def make(tmap):
        cur = pl.BlockSpec((tt, width), lambda b, t, *_: (b * (tp // tt) + tmap(t), col_block(*_)))
        prev = pl.BlockSpec((BF16_ROWS, width),
                            lambda b, t, *_: (jnp.maximum(b * per_b + tmap(t) * per_t - 1, 0), col_block(*_)))
        nxt = pl.BlockSpec((BF16_ROWS, width),
                           lambda b, t, *_: (jnp.minimum(b * per_b + (tmap(t) + 1) * per_t, last), col_block(*_)))
        return cur, prev, nxt
    return make


def _qk_conv(proj, conv_w, batch, dk):
    n = proj.shape[0]
    tp = n // batch
    width = 2 * HEADS * dk
    tt = _tile(tp, 688, BF16_ROWS)
    nt = tp // tt
    cur, prev, nxt = _halo_specs(width, lambda: 0, tt, tp, n)(lambda t: t)
    return pl.pallas_call(
        functools.partial(_qkconv_kernel, tt=tt, nt=nt, kscale=dk ** -0.5, half=HEADS * dk),
        grid=(batch, nt),
        in_specs=[cur, prev, nxt, pl.BlockSpec((4, width), lambda b, t: (0, 0))],
        out_specs=pl.BlockSpec((tt, width), lambda b, t: (b * nt + t, 0)),
        out_shape=jax.ShapeDtypeStruct((n, width), BF16),
        scratch_shapes=[pltpu.VMEM((tt + 2 * SUBLANES, width), F32)],
        compiler_params=_params("parallel", "parallel"), name="qk_conv")(proj, proj, proj, conv_w.astype(F32))


def _mlstm_kernel(q_ref, k_ref, v_ref, g_ref, gt_ref, gb_ref, gbt_ref, o_ref, c_s, m_s, *, reverse, dk, dv, nc, batch):
    step = pl.program_id(0)

    @pl.when(step == 0)
    def _():
        c_s[...] = jnp.zeros_like(c_s)
        m_s[...] = jnp.zeros_like(m_s)

    chunk = nc - 1 - step if reverse else step
    L = CHUNK
    real = chunk > 0
    valid_c = jnp.logical_or(real, lax.broadcasted_iota(jnp.int32, (L, 1), 0) >= PAD)
    valid_r = jnp.logical_or(real, lax.broadcasted_iota(jnp.int32, (1, L), 1) >= PAD)
    off = 2 * HEADS if reverse else 0
    ri = lax.broadcasted_iota(jnp.int32, (L, L), 0)
    ci = lax.broadcasted_iota(jnp.int32, (L, L), 1)
    mask = (ci >= ri) if reverse else (ci <= ri)
    inc = (ri >= ci) if reverse else (ri <= ci)
    last = 0 if reverse else L - 1
    ones_col = jnp.where(lax.broadcasted_iota(jnp.int32, (L, LANES), 1) == 0, 1.0, 0.0).astype(BF16)

    for b in range(batch):
        g = g_ref[b, :, :GATE_COLS] + gb_ref[...]
        gt = gt_ref[b, 0] + gbt_ref[...]
        li_c = jnp.where(valid_c, g[:, off:off + HEADS], NEG)
        lf_c = jnp.where(valid_c, _log_sigmoid(g[:, off + HEADS:off + 2 * HEADS]), 0.0)
        li_r = jnp.where(valid_r, gt[off:off + HEADS, :], NEG)
        lf_r = jnp.where(valid_r, _log_sigmoid(gt[off + HEADS:off + 2 * HEADS, :]), 0.0)
        cum_c = jnp.dot(mask.astype(F32), lf_c, precision=HIGHEST, preferred_element_type=F32)
        cum_r = jnp.dot(lf_r, inc.astype(F32), precision=HIGHEST, preferred_element_type=F32)

        for h in range(HEADS):
            sh = b * HEADS + h
            cc = cum_c[:, h:h + 1]
            cr = cum_r[h:h + 1, :]
            lic = li_c[:, h:h + 1]
            lir = li_r[h:h + 1, :]
            tot = cc[last:last + 1, :]
            m = m_s[sh, 0:1, 0:1]
            qh = q_ref[b, :, h * dk:(h + 1) * dk]
            kh = k_ref[b, :, h * dk:(h + 1) * dk]
            vh = jnp.where(valid_c, v_ref[b, :, h * dv:(h + 1) * dv], 0.0).astype(BF16)
            vaug = jnp.concatenate([vh, ones_col], axis=1)

            d_mat = jnp.where(mask, cc - cr + lir, NEG)
            inter = cc + m
            m_t = jnp.maximum(inter, jnp.max(d_mat, axis=1, keepdims=True))
            w_inter = jnp.exp(inter - m_t)
            s = _dot_t(qh, kh) * jnp.exp(d_mat - m_t)
            haug = w_inter * _dot(qh, c_s[sh].astype(BF16)) + _dot(s.astype(BF16), vaug)
            den = haug[:, dv:dv + 1]
            o_ref[b, :, h * dv:(h + 1) * dv] = (
                haug[:, :dv] / jnp.maximum(jnp.abs(den), jnp.exp(-m_t))).astype(o_ref.dtype)

            gs = tot - cc + lic
            m_new = jnp.maximum(tot + m, jnp.max(gs, axis=0, keepdims=True))
            decay = jnp.exp(tot + m - m_new)
            ks = (kh.astype(F32) * jnp.exp(gs - m_new)).astype(BF16)
            c_s[sh] = decay * c_s[sh] + _tdot(ks, vaug)
            m_s[sh] = jnp.broadcast_to(m_new, m_s.shape[1:])


def _chunk_spec(batch, width, col_block, nc, reverse):
    cidx = (lambda i: nc - 1 - i) if reverse else (lambda i: i)
    return pl.BlockSpec((batch, CHUNK, width), lambda i: (0, cidx(i), col_block))


def _mlstm(qk, proj, gates, gates_t, gate_bias, batch, dk, dv, v_block, reverse):
    n = qk.shape[0]
    tp = n // batch
    nc = tp // CHUNK
    cidx = (lambda i: nc - 1 - i) if reverse else (lambda i: i)
    gb = gate_bias.reshape(1, GATE_COLS).astype(F32)
    view = lambda a: a.reshape(batch, tp, a.shape[-1])
    out = pl.pallas_call(
        functools.partial(_mlstm_kernel, reverse=reverse, dk=dk, dv=dv, nc=nc, batch=batch),
        grid=(nc,),
        in_specs=[_chunk_spec(batch, HEADS * dk, 0, nc, reverse), _chunk_spec(batch, HEADS * dk, 1, nc, reverse),
                  _chunk_spec(batch, HEADS * dv, v_block, nc, reverse), _chunk_spec(batch, LANES, 0, nc, reverse),
                  pl.BlockSpec((batch, 1, GATE_COLS, CHUNK), lambda i: (0, cidx(i), 0, 0)),
                  pl.BlockSpec((1, GATE_COLS), lambda i: (0, 0)),
                  pl.BlockSpec((GATE_COLS, 1), lambda i: (0, 0))],
        out_specs=_chunk_spec(batch, HEADS * dv, 0, nc, reverse),
        out_shape=jax.ShapeDtypeStruct((batch, tp, HEADS * dv), BF16),
        scratch_shapes=[pltpu.VMEM((batch * HEADS, dk, dv + LANES), F32),
                        pltpu.VMEM((batch * HEADS, SUBLANES, LANES), F32)],
        compiler_params=_params("arbitrary"),
        name="mlstm_bwd" if reverse else "mlstm_fwd")(
            view(qk), view(qk), view(proj), view(gates), gates_t.reshape(batch, nc, GATE_COLS, CHUNK), gb,
            gb.reshape(GATE_COLS, 1))
    return out.reshape(n, HEADS * dv)


def _gla_head_exact(q, k, v, cumh, state, o_ref, b, h, *, reverse, dk, dv):
    L = CHUNK
    nsub = L // SUB
    sub_lane = lax.broadcasted_iota(jnp.int32, (SUB, L), 1)
    sub_row = lax.broadcasted_iota(jnp.int32, (SUB, 1), 0)
    o_inter = _dot_t((q * jnp.exp(cumh)).astype(BF16), state.astype(BF16))
    for blk in range(nsub):
        r0 = blk * SUB
        if reverse:
            cs = cumh[r0 + SUB:r0 + SUB + 1, :] if blk < nsub - 1 else jnp.zeros((1, dk), F32)
            earlier = sub_lane >= r0 + SUB
        else:
            cs = cumh[r0 - 1:r0, :] if blk > 0 else jnp.zeros((1, dk), F32)
            earlier = sub_lane < r0
        q_b = q[r0:r0 + SUB, :]
        cum_b = cumh[r0:r0 + SUB, :]
        qd = (q_b * jnp.exp(cum_b - cs)).astype(BF16)
        kd = (k * jnp.exp(jnp.minimum(cs - cumh, 0.0))).astype(BF16)
        att = jnp.where(earlier, _dot_t(qd, kd), 0.0)
        for j in range(SUB):
            s_idx = r0 + j
            tmask = (sub_row <= j) if reverse else (sub_row >= j)
            e = jnp.where(tmask, cum_b - cumh[s_idx:s_idx + 1, :], NEG)
            col = jnp.sum(q_b * k[s_idx:s_idx + 1, :] * jnp.exp(e), axis=1, keepdims=True)
            att = jnp.where(sub_lane == s_idx, col, att)
        o_b = o_inter[r0:r0 + SUB, :] + _dot(att.astype(BF16), v)
        o_ref[b, r0:r0 + SUB, h * dv:(h + 1) * dv] = o_b.astype(o_ref.dtype)


def _gla_head_factored(q, k, v, cumh, state, mask, o_ref, b, h, *, dv):
    qe = (q * jnp.exp(cumh)).astype(BF16)
    ke = (k * jnp.exp(-cumh)).astype(BF16)
    att = jnp.where(mask, _dot_t(qe, ke), 0.0)
    o = _dot_t(qe, state.astype(BF16)) + _dot(att.astype(BF16), v)
    o_ref[b, :, h * dv:(h + 1) * dv] = o.astype(o_ref.dtype)


def _gla_kernel(q_ref, k_ref, v_ref, lr_ref, up_ref, ub_ref, o_ref, s_s, *, reverse, dk, dv, nc, batch):
    step = pl.program_id(0)

    @pl.when(step == 0)
    def _():
        s_s[...] = jnp.zeros_like(s_s)

    chunk = nc - 1 - step if reverse else step
    L = CHUNK
    valid_c = jnp.logical_or(chunk > 0, lax.broadcasted_iota(jnp.int32, (L, 1), 0) >= PAD)
    off = GATE_COLS + (B_RANK if reverse else 0)
    ri = lax.broadcasted_iota(jnp.int32, (L, L), 0)
    ci = lax.broadcasted_iota(jnp.int32, (L, L), 1)
    mask = (ci >= ri) if reverse else (ci <= ri)
    last = 0 if reverse else L - 1
    cums = []
    for b in range(batch):
        z = jnp.dot(lr_ref[b, :, off:off + B_RANK], up_ref[...], precision=HIGHEST, preferred_element_type=F32)
        la = jnp.where(valid_c, _log_sigmoid(z + ub_ref[...]) / B_TAU, 0.0)
        cums.append(jnp.dot(mask.astype(F32), la, precision=HIGHEST, preferred_element_type=F32))
    lowest = jnp.min(jnp.concatenate([c[last:last + 1, :] for c in cums], axis=0))
    factorable = lowest >= -GLA_MAX_CHUNK_DECAY

    def run(factored):
        for b in range(batch):
            for h in range(HEADS):
                sl = slice(h * dk, (h + 1) * dk)
                q = jnp.where(valid_c, q_ref[b, :, sl], 0.0).astype(F32) * dk ** -0.5
                k = jnp.where(valid_c, k_ref[b, :, sl], 0.0).astype(F32)
                v = jnp.where(valid_c, v_ref[b, :, h * dv:(h + 1) * dv], 0.0).astype(BF16)
                cumh = cums[b][:, sl]
                tot = cumh[last:last + 1, :]
                state = s_s[b * HEADS + h]
                if factored:
                    _gla_head_factored(q, k, v, cumh, state, mask, o_ref, b, h, dv=dv)
                else:
                    _gla_head_exact(q, k, v, cumh, state, o_ref, b, h, reverse=reverse, dk=dk, dv=dv)
                kdec = (k * jnp.exp(tot - cumh)).astype(BF16)
                s_s[b * HEADS + h] = jnp.exp(tot) * state + _tdot(v, kdec)

    @pl.when(factorable)
    def _():
        run(True)

    @pl.when(jnp.logical_not(factorable))
    def _():
        run(False)


def _gla(proj, small, lr_up, lr_bias, batch, dk, dv, qkv_blocks, reverse):
    n = proj.shape[0]
    tp = n // batch
    nc = tp // CHUNK
    qb, kb, vb = qkv_blocks
    view = lambda a: a.reshape(batch, tp, a.shape[-1])
    out = pl.pallas_call(
        functools.partial(_gla_kernel, reverse=reverse, dk=dk, dv=dv, nc=nc, batch=batch),
        grid=(nc,),
        in_specs=[_chunk_spec(batch, HEADS * dk, qb, nc, reverse), _chunk_spec(batch, HEADS * dk, kb, nc, reverse),
                  _chunk_spec(batch, HEADS * dv, vb, nc, reverse), _chunk_spec(batch, LANES, 0, nc, reverse),
                  pl.BlockSpec((B_RANK, HEADS * dk), lambda i: (0, 0)),
                  pl.BlockSpec((1, HEADS * dk), lambda i: (0, 0))],
        out_specs=_chunk_spec(batch, HEADS * dv, 0, nc, reverse),
        out_shape=jax.ShapeDtypeStruct((batch, tp, HEADS * dv), BF16),
        scratch_shapes=[pltpu.VMEM((batch * HEADS, dv, dk), F32)],
        compiler_params=_params("arbitrary"),
        name="gla_bwd" if reverse else "gla_fwd")(
            view(proj), view(proj), view(proj), view(small), lr_up.astype(F32), lr_bias.reshape(1, -1).astype(F32))
    return out.reshape(n, HEADS * dv)


def _head_norm(x, g, dv):
    parts = []
    for h in range(HEADS):
        xh = x[:, h * dv:(h + 1) * dv]
        parts.append(xh * lax.rsqrt(jnp.mean(xh * xh, axis=-1, keepdims=True) + EPS))
    return jnp.concatenate(parts, axis=1) * g


def _even_combine_kernel(af_ref, ab_ref, bf_ref, bb_ref, oa_ref, gb_ref, na_ref, nb_ref, o_ref, *, tr, tp, dv):
    rows = (pl.program_id(0) * tr) % tp + lax.broadcasted_iota(jnp.int32, (tr, 1), 0)
    valid = rows >= PAD
    w = HEADS * dv
    ha = af_ref[...].astype(F32) + ab_ref[...].astype(F32)
    ya = jax.nn.sigmoid(oa_ref[...].astype(F32)) * _head_norm(ha, na_ref[...], dv)
    o_ref[:, :w] = jnp.where(valid, ya, 0.0).astype(o_ref.dtype)
    hb = bf_ref[...].astype(F32) + bb_ref[...].astype(F32)
    gb = gb_ref[...].astype(F32)
    yb = gb * jax.nn.sigmoid(gb) * _head_norm(hb, nb_ref[...], dv)
    o_ref[:, w:] = jnp.where(valid, yb, 0.0).astype(o_ref.dtype)


def _even_combine(ha_f, ha_b, hb_f, hb_b, proj, norm_a, norm_b, batch, dv, oa_block, gb_block):
    n, w = ha_f.shape
    tp = n // batch
    tr = _tile(tp, 384, BF16_ROWS)
    row = pl.BlockSpec((tr, w), lambda i: (i, 0))
    vec = pl.BlockSpec((1, w), lambda i: (0, 0))
    return pl.pallas_call(
        functools.partial(_even_combine_kernel, tr=tr, tp=tp, dv=dv),
        grid=(n // tr,),
        in_specs=[row, row, row, row, pl.BlockSpec((tr, w), lambda i: (i, oa_block)),
                  pl.BlockSpec((tr, w), lambda i: (i, gb_block)), vec, vec],
        out_specs=pl.BlockSpec((tr, 2 * w), lambda i: (i, 0)),
        out_shape=jax.ShapeDtypeStruct((n, 2 * w), BF16),
        compiler_params=_params("parallel"), name="even_combine")(
            ha_f, ha_b, hb_f, hb_b, proj, proj, norm_a.reshape(1, w).astype(F32), norm_b.reshape(1, w).astype(F32))


def _block_scan(a, b, reverse):
    sub = lax.broadcasted_iota(jnp.int32, a.shape, 1)
    for k in (1, 2, 4):
        if reverse:
            a_sh, b_sh, m = pltpu.roll(a, SUBLANES - k, 1), pltpu.roll(b, SUBLANES - k, 1), sub < SUBLANES - k
        else:
            a_sh, b_sh, m = pltpu.roll(a, k, 1), pltpu.roll(b, k, 1), sub >= k
        b = jnp.where(m, a * b_sh + b, b)
        a = jnp.where(m, a * a_sh, a)
    return a, b


def _rglru_kernel(cur_ref, prev_ref, next_ref, cw_ref, cb_ref, wr_ref, br_ref, wi_ref, bi_ref, lam_ref,
                  o_ref, ext_s, a_s, b_s, h_s, carry_s, *, reverse, tt, nt):
    step = pl.program_id(2)

    @pl.when(step == 0)
    def _():
        carry_s[...] = jnp.zeros_like(carry_s)

    t = nt - 1 - step if reverse else step
    row0 = t * tt
    _fill_ext(ext_s, cur_ref, prev_ref, next_ref, row0, tt, t < nt - 1)
    u = _conv_taps(ext_s, cw_ref, tt) + cb_ref[...]
    ub = u.astype(BF16)
    r = _sigmoid(_dot(ub, wr_ref[0]) + br_ref[...])
    gi = _sigmoid(_dot(ub, wi_ref[0]) + bi_ref[...])
    lam = lam_ref[...]
    softplus = jnp.maximum(-lam, 0.0) + jnp.log1p(jnp.exp(-jnp.abs(lam)))
    log_a = -RNN_C * r * softplus
    a = jnp.exp(log_a)
    rows = row0 + lax.broadcasted_iota(jnp.int32, (tt, 1), 0)
    inp = jnp.where(rows >= PAD, jnp.sqrt(1.0 - a * a) * (gi * u), 0.0)
    c = a.shape[1]
    ng = tt // SUBLANES
    a_g, b_g = _block_scan(a.reshape(ng, SUBLANES, c), inp.reshape(ng, SUBLANES, c), reverse)
    a_s[...] = a_g.reshape(tt, c)
    b_s[...] = b_g.reshape(tt, c)
    out_row = 0 if reverse else SUBLANES - 1

    def body(i, carry):
        g = ng - 1 - i if reverse else i
        r0 = pl.multiple_of(g * SUBLANES, SUBLANES)
        hh = b_s[pl.ds(r0, SUBLANES), :] + a_s[pl.ds(r0, SUBLANES), :] * carry
        h_s[pl.ds(r0, SUBLANES), :] = hh
        return hh[out_row:out_row + 1, :]

    carry_s[...] = lax.fori_loop(0, ng, body, carry_s[...])
    o_ref[...] = h_s[...].astype(o_ref.dtype)


def _pair_blocks(w):
    nb, r, _ = w.shape
    z = jnp.zeros((nb // 2, r, r), w.dtype)
    top = jnp.concatenate([w[0::2], z], axis=2)
    bot = jnp.concatenate([z, w[1::2]], axis=2)
    return jnp.concatenate([top, bot], axis=1).astype(BF16)


def _rglru(proj, conv_w, conv_b, w_r, b_r, w_i, b_i, lam, batch, d_rnn, reverse):
    n = proj.shape[0]
    tp = n // batch
    cw = 2 * d_rnn // RNN_BLOCKS
    ncb = d_rnn // cw
    tt = _tile(tp, 688, BF16_ROWS)
    nt = tp // tt
    tmap = (lambda t: nt - 1 - t) if reverse else (lambda t: t)
    cur, prev, nxt = _halo_specs(cw, lambda j: ncb + j, tt, tp, n)(tmap)
    def swap(spec):
        f = spec.index_map
        return pl.BlockSpec(spec.block_shape, lambda b, j, t: f(b, t, j))
    vec = pl.BlockSpec((1, cw), lambda b, j, t: (0, j))
    wspec = pl.BlockSpec((1, cw, cw), lambda b, j, t: (j, 0, 0))
    row = lambda x: x.reshape(1, d_rnn).astype(F32)
    return pl.pallas_call(
        functools.partial(_rglru_kernel, reverse=reverse, tt=tt, nt=nt),
        grid=(batch, ncb, nt),
        in_specs=[swap(cur), swap(prev), swap(nxt), pl.BlockSpec((4, cw), lambda b, j, t: (0, j)), vec,
                  wspec, vec, wspec, vec, vec],
        out_specs=pl.BlockSpec((tt, cw), lambda b, j, t: (b * nt + tmap(t), j)),
        out_shape=jax.ShapeDtypeStruct((n, d_rnn), BF16),
        scratch_shapes=[pltpu.VMEM((tt + 2 * SUBLANES, cw), F32), pltpu.VMEM((tt, cw), F32),
                        pltpu.VMEM((tt, cw), F32), pltpu.VMEM((tt, cw), F32), pltpu.VMEM((1, cw), F32)],
        compiler_params=_params("parallel", "parallel", "arbitrary"),
        name="rglru_bwd" if reverse else "rglru_fwd")(
            proj, proj, proj, conv_w.astype(F32), row(conv_b), _pair_blocks(w_r), row(b_r),
            _pair_blocks(w_i), row(b_i), row(lam))


def _odd_combine_kernel(g_ref, hf_ref, hb_ref, o_ref, *, tr, tp):
    rows = (pl.program_id(0) * tr) % tp + lax.broadcasted_iota(jnp.int32, (tr, 1), 0)
    y = jax.nn.gelu(g_ref[...].astype(F32)) * (hf_ref[...].astype(F32) + hb_ref[...].astype(F32))
    o_ref[...] = jnp.where(rows >= PAD, y, 0.0).astype(o_ref.dtype)


def _odd_combine(proj, hf, hb, batch):
    n, w = hf.shape
    tp = n // batch
    tr = _tile(tp, 384, BF16_ROWS)
    row = pl.BlockSpec((tr, w), lambda i: (i, 0))
    return pl.pallas_call(
        functools.partial(_odd_combine_kernel, tr=tr, tp=tp), grid=(n // tr,),
        in_specs=[row, row, row], out_specs=row, out_shape=jax.ShapeDtypeStruct((n, w), BF16),
        compiler_params=_params("parallel"), name="odd_combine")(proj, hf, hb)


def _route(x):
    lane = lax.broadcasted_iota(jnp.int32, x.shape, 1)
    big = jnp.int32(2 * LANES)
    gmask = lane < N_GROUPS
    gmax = jnp.max(jnp.where(gmask, x, -jnp.inf), axis=1, keepdims=True)
    ge = jnp.where(gmask, jnp.exp(x - gmax), 0.0)
    gp = ge / jnp.sum(ge, axis=1, keepdims=True)
    gval = jnp.max(gp, axis=1, keepdims=True)
    gidx = jnp.min(jnp.where(jnp.logical_and(gmask, gp == gval), lane, big), axis=1, keepdims=True)
    lo = N_GROUPS + gidx * EXPERTS_PER_GROUP
    emask = jnp.logical_and(lane >= lo, lane < lo + EXPERTS_PER_GROUP)
    emax = jnp.max(jnp.where(emask, x, -jnp.inf), axis=1, keepdims=True)
    ee = jnp.where(emask, jnp.exp(x - emax), 0.0)
    ep = ee / jnp.sum(ee, axis=1, keepdims=True)
    v1 = jnp.max(jnp.where(emask, ep, -1.0), axis=1, keepdims=True)
    i1 = jnp.min(jnp.where(jnp.logical_and(emask, ep == v1), lane, big), axis=1, keepdims=True)
    rest = jnp.logical_and(emask, lane != i1)
    v2 = jnp.max(jnp.where(rest, ep, -1.0), axis=1, keepdims=True)
    i2 = jnp.min(jnp.where(jnp.logical_and(rest, ep == v2), lane, big), axis=1, keepdims=True)
    tot = v1 + v2
    comb = jnp.where(lane == i1, v1 / tot * gval, jnp.where(lane == i2, v2 / tot * gval, 0.0))
    return comb, gidx


def _pack_pair(hi, lo):
    bits = lambda v: lax.bitcast_convert_type(v.astype(BF16).astype(F32), jnp.uint32)
    return bits(hi) | (bits(lo) >> 16)


def _unpack_pair(w):
    hi = lax.bitcast_convert_type(w & jnp.uint32(0xFFFF0000), F32)
    lo = lax.bitcast_convert_type(w << 16, F32)
    return hi.astype(BF16), lo.astype(BF16)


def _store_packed_rows(ref, x, npack):
    half = npack * LANES
    for j in range(npack):
        ref[:, j, :] = _pack_pair(x[:, j * LANES:(j + 1) * LANES], x[:, half + j * LANES:half + (j + 1) * LANES])


def _load_packed_rows(src, dense_s, dst_ref, npack):
    half = npack * LANES
    for j in range(npack):
        dense_s[:, j * LANES:(j + 1) * LANES] = src(j)
    for j in range(npack):
        hi, lo = _unpack_pair(dense_s[:, j * LANES:(j + 1) * LANES])
        dst_ref[:, j * LANES:(j + 1) * LANES] = hi.astype(dst_ref.dtype)
        dst_ref[:, half + j * LANES:half + (j + 1) * LANES] = lo.astype(dst_ref.dtype)


def _norm_route_kernel(h_ref, g_ref, w2_ref, wh_ref, b_ref, slab_ref, oh_ref, *, npack):
    x = h_ref[...]
    y = x * lax.rsqrt(jnp.mean(x * x, axis=-1, keepdims=True) + EPS) * g_ref[...]
    yh = y.astype(BF16)
    yl = (y - yh.astype(F32)).astype(BF16)
    r1 = _dot(yh, w2_ref[...])
    logits = r1[:, :LANES] + r1[:, LANES:] + _dot(yl, wh_ref[...]) + b_ref[...]
    comb, gidx = _route(logits)
    _store_packed_rows(slab_ref, y, npack)
    slab_ref[:, npack, :] = lax.bitcast_convert_type(comb, jnp.uint32)
    for j in range(npack + 1, slab_ref.shape[1]):
        slab_ref[:, j, :] = jnp.zeros(comb.shape, jnp.uint32)
    lane = lax.broadcasted_iota(jnp.int32, comb.shape, 1)
    oh_ref[...] = jnp.where(lane == gidx, 1.0, 0.0).astype(oh_ref.dtype)


def _norm_route(h, g, wg, bg, we, be):
    n, d = h.shape
    npack = d // (2 * LANES)
    srows = (npack + 1 + SUBLANES - 1) // SUBLANES * SUBLANES
    tr = _tile(n, 192, BF16_ROWS)
    zpad = LANES - N_GROUPS - N_EXPERTS
    wr = jnp.concatenate([wg, we, jnp.zeros((d, zpad), F32)], axis=1)
    wh = wr.astype(BF16)
    wl = (wr - wh.astype(F32)).astype(BF16)
    bias = jnp.concatenate([bg.astype(F32), be.astype(F32), jnp.zeros((zpad,), F32)]).reshape(1, LANES)
    return pl.pallas_call(
        functools.partial(_norm_route_kernel, npack=npack), grid=(n // tr,),
        in_specs=[pl.BlockSpec((tr, d), lambda i: (i, 0)), pl.BlockSpec((1, d), lambda i: (0, 0)),
                  pl.BlockSpec((d, 2 * LANES), lambda i: (0, 0)), pl.BlockSpec((d, LANES), lambda i: (0, 0)),
                  pl.BlockSpec((1, LANES), lambda i: (0, 0))],
        out_specs=[pl.BlockSpec((tr, srows, LANES), lambda i: (i, 0, 0)), pl.BlockSpec((tr, LANES), lambda i: (i, 0))],
        out_shape=[jax.ShapeDtypeStruct((n, srows, LANES), jnp.uint32), jax.ShapeDtypeStruct((n, LANES), BF16)],
        compiler_params=_params("parallel"), name="norm_route")(
            h, g.reshape(1, d).astype(F32), jnp.concatenate([wh, wl], axis=1), wh, bias)


def _rank_kernel(oh_ref, g_ref, rank_ref, cnt_ref, carry_s):
    @pl.when(pl.program_id(0) == 0)
    def _():
        carry_s[...] = jnp.zeros_like(carry_s)

    tr = oh_ref.shape[0]
    sel = jnp.where(lax.broadcasted_iota(jnp.int32, (SUBLANES, LANES), 0) ==
                    lax.broadcasted_iota(jnp.int32, (SUBLANES, LANES), 1), 1.0, 0.0).astype(BF16)
    oh_t = _dot_t(sel, oh_ref[...])
    before = (lax.broadcasted_iota(jnp.int32, (tr, tr), 0) < lax.broadcasted_iota(jnp.int32, (tr, tr), 1))
    cum = _dot(oh_t.astype(BF16), jnp.where(before, 1.0, 0.0).astype(BF16)) + carry_s[:, 0:1]
    gid = lax.broadcasted_iota(jnp.int32, (SUBLANES, tr), 0).astype(F32)
    rank_ref[0] = jnp.sum(oh_t * cum, axis=0, keepdims=True).astype(jnp.int32)
    g_ref[0] = jnp.sum(oh_t * gid, axis=0, keepdims=True).astype(jnp.int32)
    carry_s[...] = carry_s[...] + jnp.sum(oh_t, axis=1, keepdims=True)
    cnt_ref[...] = carry_s[...]


def _rank(onehot):
    n = onehot.shape[0]
    tr = _tile(n, 384, LANES)
    row = pl.BlockSpec((1, 1, tr), lambda i: (i, 0, 0))
    g, rank, cnt = pl.pallas_call(
        _rank_kernel, grid=(n // tr,), in_specs=[pl.BlockSpec((tr, LANES), lambda i: (i, 0))],
        out_specs=[row, row, pl.BlockSpec((SUBLANES, LANES), lambda i: (0, 0))],
        out_shape=[jax.ShapeDtypeStruct((n // tr, 1, tr), jnp.int32), jax.ShapeDtypeStruct((n // tr, 1, tr), jnp.int32),
                   jax.ShapeDtypeStruct((SUBLANES, LANES), F32)],
        scratch_shapes=[pltpu.VMEM((SUBLANES, LANES), F32)],
        compiler_params=_params("arbitrary"), name="rank")(onehot)
    return g.reshape(n), rank.reshape(n), cnt[:N_GROUPS, 0].astype(jnp.int32)


def _invert_kernel(g_ref, rank_ref, cnt_ref, pos_ref, idx_ref, tg_ref, *, n, tm, ntiles):
    bases = [jnp.int32(0)]
    for g in range(N_GROUPS - 1):
        bases.append(bases[-1] + (cnt_ref[g] + tm - 1) // tm * tm)

    def zero(i, c):
        idx_ref[i] = 0
        return c
    lax.fori_loop(0, ntiles * tm, zero, 0, unroll=8)

    def place(t, c):
        g = g_ref[t]
        base = bases[0]
        for k in range(1, N_GROUPS):
            base = jnp.where(g == k, bases[k], base)
        p = base + rank_ref[t]
        pos_ref[t] = p
        idx_ref[p] = t
        return c
    lax.fori_loop(0, n, place, 0, unroll=8)

    def tile_group(i, c):
        r = i * tm
        tg = jnp.int32(0)
        for k in range(1, N_GROUPS):
            tg = tg + (r >= bases[k]).astype(jnp.int32)
        tg_ref[i] = tg
        return c
    lax.fori_loop(0, ntiles, tile_group, 0)


def _invert(g, rank, cnt, tm, ntiles):
    n = g.shape[0]
    smem = pl.BlockSpec(memory_space=pltpu.SMEM)
    return pl.pallas_call(
        functools.partial(_invert_kernel, n=n, tm=tm, ntiles=ntiles),
        in_specs=[smem, smem, smem], out_specs=[smem, smem, smem],
        out_shape=[jax.ShapeDtypeStruct((n,), jnp.int32), jax.ShapeDtypeStruct((ntiles * tm,), jnp.int32),
                   jax.ShapeDtypeStruct((ntiles,), jnp.int32)],
        name="invert")(g, rank, cnt)


def _row_copy(src_hbm, buf, sem, src_row, slot, dst_row):
    return pltpu.make_async_copy(src_hbm.at[pl.ds(src_row, 1)], buf.at[slot, pl.ds(dst_row, 1)], sem.at[slot])


def _gather_rows(index_ref, src_hbm, buf, sem, rows):
    i = pl.program_id(0)
    steps = pl.num_programs(0)

    def issue(step, slot):
        def body(r, c):
            _row_copy(src_hbm, buf, sem, index_ref[step * rows + r], slot, r).start()
            return c
        lax.fori_loop(0, rows, body, 0, unroll=8)

    @pl.when(i == 0)
    def _():
        issue(0, 0)

    @pl.when(i + 1 < steps)
    def _():
        issue(i + 1, (i + 1) % 2)

    slot = i % 2

    def wait(r, c):
        _row_copy(src_hbm, buf, sem, 0, slot, r).wait()
        return c
    lax.fori_loop(0, rows, wait, 0, unroll=8)
    return slot


def _dispatch_kernel(idx_ref, slab_hbm, xs_ref, cs_ref, buf, dense_s, sem, *, npack):
    slot = _gather_rows(idx_ref, slab_hbm, buf, sem, xs_ref.shape[0])
    _load_packed_rows(lambda j: buf[slot, :, j, :], dense_s, xs_ref, npack)
    cs_ref[...] = lax.bitcast_convert_type(buf[slot, :, npack, :], F32)


def _dispatch(idx, slab, tm, ntiles, d):
    srows = slab.shape[1]
    return pl.pallas_call(
        functools.partial(_dispatch_kernel, npack=d // (2 * LANES)),
        grid_spec=pltpu.PrefetchScalarGridSpec(
            num_scalar_prefetch=1, grid=(ntiles,),
            in_specs=[pl.BlockSpec(memory_space=pl.ANY)],
            out_specs=[pl.BlockSpec((tm, d), lambda i, idx: (i, 0)), pl.BlockSpec((tm, LANES), lambda i, idx: (i, 0))],
            scratch_shapes=[pltpu.VMEM((2, tm, srows, LANES), jnp.uint32), pltpu.VMEM((tm, d // 2), jnp.uint32),
                            pltpu.SemaphoreType.DMA((2,))]),
        out_shape=[jax.ShapeDtypeStruct((ntiles * tm, d), BF16), jax.ShapeDtypeStruct((ntiles * tm, LANES), F32)],
        compiler_params=_params("arbitrary"), name="dispatch")(idx, slab)


def _expert_kernel(tg_ref, x_ref, c_ref, w1_ref, w3_ref, w2_ref, o_ref, acc_s, *, npack):
    i = pl.program_id(0)
    e = pl.program_id(1)

    @pl.when(e == 0)
    def _():
        acc_s[...] = jnp.zeros_like(acc_s)

    a = x_ref[...]
    h1 = _dot(a, w1_ref[0])
    h3 = _dot(a, w3_ref[0])
    comb = c_ref[...]
    lane = lax.broadcasted_iota(jnp.int32, comb.shape, 1)
    c = jnp.sum(jnp.where(lane == N_GROUPS + tg_ref[i] * EXPERTS_PER_GROUP + e, comb, 0.0), axis=1, keepdims=True)
    hid = (h1 * jax.nn.sigmoid(h1) * h3 * c).astype(BF16)
    acc_s[...] += _dot(hid, w2_ref[0])

    @pl.when(e == EXPERTS_PER_GROUP - 1)
    def _():
        _store_packed_rows(o_ref, acc_s, npack)


def _experts(tg, xs, cs, w1, w3, w2, tm):
    rows, d = xs.shape
    f = w1.shape[-1]
    npack = d // (2 * LANES)
    wmap = lambda i, e, tg: (tg[i] * EXPERTS_PER_GROUP + e, 0, 0)
    return pl.pallas_call(
        functools.partial(_expert_kernel, npack=npack),
        grid_spec=pltpu.PrefetchScalarGridSpec(
            num_scalar_prefetch=1, grid=(rows // tm, EXPERTS_PER_GROUP),
            in_specs=[pl.BlockSpec((tm, d), lambda i, e, tg: (i, 0)), pl.BlockSpec((tm, LANES), lambda i, e, tg: (i, 0)),
                      pl.BlockSpec((1, d, f), wmap), pl.BlockSpec((1, d, f), wmap), pl.BlockSpec((1, f, d), wmap)],
            out_specs=pl.BlockSpec((tm, npack, LANES), lambda i, e, tg: (i, 0, 0)),
            scratch_shapes=[pltpu.VMEM((tm, d), F32)]),
        out_shape=jax.ShapeDtypeStruct((rows, npack, LANES), jnp.uint32),
        compiler_params=_params("parallel", "arbitrary"), name="experts")(tg, xs, cs, w1, w3, w2)


def _collect_kernel(pos_ref, ys_hbm, o_ref, buf, dense_s, sem, *, npack):
    slot = _gather_rows(pos_ref, ys_hbm, buf, sem, o_ref.shape[0])
    _load_packed_rows(lambda j: buf[slot, :, j, :], dense_s, o_ref, npack)


def _collect(pos, ys):
    n = pos.shape[0]
    _, npack, _ = ys.shape
    d = 2 * npack * LANES
    tr = _tile(n, 384, BF16_ROWS)
    return pl.pallas_call(
        functools.partial(_collect_kernel, npack=npack),
        grid_spec=pltpu.PrefetchScalarGridSpec(
            num_scalar_prefetch=1, grid=(n // tr,),
            in_specs=[pl.BlockSpec(memory_space=pl.ANY)],
            out_specs=pl.BlockSpec((tr, d), lambda i, pos: (i, 0)),
            scratch_shapes=[pltpu.VMEM((2, tr, npack, LANES), jnp.uint32), pltpu.VMEM((tr, d // 2), jnp.uint32),
                            pltpu.SemaphoreType.DMA((2,))]),
        out_shape=jax.ShapeDtypeStruct((n, d), BF16),
        compiler_params=_params("arbitrary"), name="collect")(pos, ys)


def _moe_layer(h, ffn_g, wg, bg, we, be, w1, w3, w2):
    n, d = h.shape
    f = w1.shape[-1]
    tm = 512 if n >= 4096 else 128
    ntiles = (n + N_GROUPS * (tm - 1)) // tm
    slab, onehot = _norm_route(h, ffn_g, wg, bg, we, be)
    g, rank, cnt = _rank(onehot)
    pos, idx, tg = _invert(g, rank, cnt, tm, ntiles)
    xs, cs = _dispatch(idx, slab, tm, ntiles, d)
    ys = _experts(tg, xs, cs, w1.reshape(N_EXPERTS, d, f).astype(BF16), w3.reshape(N_EXPERTS, d, f).astype(BF16),
                  w2.reshape(N_EXPERTS, f, d).astype(BF16), tm)
    return _collect(pos, ys)


def _even_layer(h, hn, w_in, gate_bias, qk_conv, lr_up, lr_bias, norm_a, norm_b, w_out, batch):
    n, d = h.shape
    dk, dv = d // 16, d // 8
    qk_w, v_w = HEADS * dk, HEADS * dv
    a_end = 2 * qk_w + 2 * v_w
    b_start = a_end + GATE_COLS
    b_end = b_start + 2 * qk_w + 2 * v_w
    w_main = jnp.concatenate([w_in[:, :a_end], w_in[:, b_start:b_end]], axis=1).astype(BF16)
    w_small = jnp.concatenate([w_in[:, a_end:b_start], w_in[:, b_end:],
                               jnp.zeros((d, LANES - GATE_COLS - 2 * B_RANK), F32)], axis=1).astype(BF16)
    proj = _matmul(hn, w_main, BF16, tm_target=1376)
    small = _matmul(hn, w_small, F32, tn_target=LANES)
    gates_t = small[:, :GATE_COLS].reshape(n // CHUNK, CHUNK, GATE_COLS).transpose(0, 2, 1)
    qk = _qk_conv(proj, qk_conv, batch, dk)
    va_blk, oa_blk = 2 * qk_w // v_w, (2 * qk_w + v_w) // v_w
    b0 = a_end
    qb_blk, kb_blk = b0 // qk_w, (b0 + qk_w) // qk_w
    vb_blk, gb_blk = (b0 + 2 * qk_w) // v_w, (b0 + 2 * qk_w + v_w) // v_w
    ha, hb = [], []
    for rev in (False, True):
        ha.append(_mlstm(qk, proj, small, gates_t, gate_bias, batch, dk, dv, va_blk, rev))
        hb.append(_gla(proj, small, lr_up[int(rev)], lr_bias[int(rev)], batch, dk, dv, (qb_blk, kb_blk, vb_blk), rev))
    y = _even_combine(ha[0], ha[1], hb[0], hb[1], proj, norm_a, norm_b, batch, dv, oa_blk, gb_blk)
    return _matmul(y, w_out.astype(BF16), F32, res=h)


def _odd_layer(h, hn, w_in, conv_w, conv_b, w_r, b_r, w_i, b_i, lam, w_out, batch):
    d_rnn = w_out.shape[0]
    proj = _matmul(hn, w_in.astype(BF16), BF16, tm_target=1376)
    hs = [_rglru(proj, conv_w, conv_b, w_r[i], b_r[i], w_i[i], b_i[i], lam[i], batch, d_rnn, bool(i)) for i in (0, 1)]
    y = _odd_combine(proj, hs[0], hs[1], batch)
    return _matmul(y, w_out.astype(BF16), F32, res=h)


def kernel(x, meta_tokens, mix_norm, ffn_norm, final_norm, ev_w_in, ev_gate_bias, ev_qk_conv, ev_lr_up, ev_lr_bias, ev_norm_a, ev_norm_b, ev_w_out, od_w_in, od_conv, od_conv_bias, od_w_r, od_b_r, od_w_i, od_b_i, od_lambda, od_w_out, moe_wg, moe_bg, moe_we, moe_be, moe_w1, moe_w3, moe_w2):
    batch, seq, d = x.shape
    depth = mix_norm.shape[0]
    assert seq % CHUNK == 0 and d % 16 == 0
    for layer in range(depth):
        if layer == 0:
            h, hn = _frame_norm(x, meta_tokens, mix_norm[0])
        else:
            h, hn = _norm(h, mix_norm[layer], delta)
        if layer % 2 == 0:
            e = layer // 2
            h = _even_layer(h, hn, ev_w_in[e], ev_gate_bias[e], ev_qk_conv[e], ev_lr_up[e],
                            ev_lr_bias[e], ev_norm_a[e], ev_norm_b[e], ev_w_out[e], batch)
        else:
            o = layer // 2
            h = _odd_layer(h, hn, od_w_in[o], od_conv[o], od_conv_bias[o], od_w_r[o], od_b_r[o],
                           od_w_i[o], od_b_i[o], od_lambda[o], od_w_out[o], batch)
        delta = _moe_layer(h, ffn_norm[layer], moe_wg[layer], moe_bg[layer], moe_we[layer], moe_be[layer],
                              moe_w1[layer], moe_w3[layer], moe_w2[layer])
    out = _final_norm(h, delta, final_norm, batch, seq)
    return out.reshape(batch, seq, d)
```

```python
import functools

import jax
import jax.numpy as jnp
from jax import lax
from jax.experimental import pallas as pl
from jax.experimental.pallas import tpu as pltpu

F32 = jnp.float32
BF16 = jnp.bfloat16
HIGHEST = lax.Precision.HIGHEST

N_META = 16
CHUNK = 64
PAD = CHUNK - N_META
SUB = 16
EPS = 1e-6
NEG = -1e30
HEADS = 4
GATE_COLS = 4 * HEADS
B_RANK = 16
B_TAU = 16.0
GLA_MAX_CHUNK_DECAY = 80.0
RNN_BLOCKS = 16
RNN_C = 8.0
N_GROUPS = 4
EXPERTS_PER_GROUP = 8
N_EXPERTS = N_GROUPS * EXPERTS_PER_GROUP
LANES = 128
SUBLANES = 8
BF16_ROWS = 16
VMEM_LIMIT = 56 * 1024 * 1024


def _params(*sem):
    return pltpu.CompilerParams(dimension_semantics=sem, vmem_limit_bytes=VMEM_LIMIT)


def _tile(n, target, mult):
    best = None
    for t in range(mult, min(n, target) + 1, mult):
        if n % t == 0:
            best = t
    assert best is not None, (n, target, mult)
    return best


def _log_sigmoid(x):
    return jnp.minimum(x, 0.0) - jnp.log1p(jnp.exp(-jnp.abs(x)))


def _sigmoid(x):
    return 0.5 * jnp.tanh(0.5 * x) + 0.5


def _split3(x):
    hi = x.astype(BF16)
    r1 = x - hi.astype(F32)
    mid = r1.astype(BF16)
    lo = (r1 - mid.astype(F32)).astype(BF16)
    return hi, mid, lo


def _dot(a, b):
    return jnp.dot(a, b, preferred_element_type=F32)


def _dot_t(a, b):
    return lax.dot_general(a, b, (((1,), (1,)), ((), ())), preferred_element_type=F32)


def _tdot(a, b, precision=None):
    return lax.dot_general(a, b, (((0,), (0,)), ((), ())), preferred_element_type=F32,
                           precision=precision)


def _norm_kernel(h_ref, d_ref, g_ref, hnew_ref, hn_ref):
    x = h_ref[...] + d_ref[...].astype(F32)
    hnew_ref[...] = x
    y = x * lax.rsqrt(jnp.mean(x * x, axis=-1, keepdims=True) + EPS) * g_ref[...]
    hn_ref[...] = y.astype(hn_ref.dtype)


def _norm(h, g, delta):
    n, d = h.shape
    tr = _tile(n, 192, BF16_ROWS)
    row = pl.BlockSpec((tr, d), lambda i: (i, 0))
    return pl.pallas_call(
        _norm_kernel, grid=(n // tr,),
        in_specs=[row, row, pl.BlockSpec((1, d), lambda i: (0, 0))], out_specs=[row, row],
        out_shape=[jax.ShapeDtypeStruct((n, d), F32), jax.ShapeDtypeStruct((n, d), BF16)],
        compiler_params=_params("parallel"), name="norm")(h, delta, g.reshape(1, d).astype(F32))


def _frame_norm_kernel(x_ref, meta_ref, g_ref, h_ref, hn_ref):
    tr, d = h_ref.shape
    xb = x_ref[...]
    first = jnp.concatenate([jnp.zeros((PAD, d), F32), meta_ref[...], xb[:tr - CHUNK]], axis=0)
    x = jnp.where(pl.program_id(1) == 0, first, xb)
    h_ref[...] = x
    y = x * lax.rsqrt(jnp.mean(x * x, axis=-1, keepdims=True) + EPS) * g_ref[...]
    hn_ref[...] = y.astype(hn_ref.dtype)


def _frame_norm(x, meta, g):
    batch, seq, d = x.shape
    tp = PAD + N_META + seq
    nc = tp // CHUNK
    tr = CHUNK * _tile(nc, min(4, seq // CHUNK), 1)
    nt = tp // tr
    out = pl.BlockSpec((tr, d), lambda b, i: (b * nt + i, 0))
    src = pl.BlockSpec((pl.Element(tr), pl.Element(d)),
                       lambda b, i: (pl.multiple_of(b * seq + jnp.maximum(i * tr - CHUNK, 0), CHUNK), 0))
    return pl.pallas_call(
        _frame_norm_kernel, grid=(batch, nt),
        in_specs=[src, pl.BlockSpec((N_META, d), lambda b, i: (0, 0)), pl.BlockSpec((1, d), lambda b, i: (0, 0))],
        out_specs=[out, out],
        out_shape=[jax.ShapeDtypeStruct((batch * tp, d), F32), jax.ShapeDtypeStruct((batch * tp, d), BF16)],
        compiler_params=_params("parallel", "parallel"), name="frame_norm")(
            x.reshape(batch * seq, d), meta.astype(F32), g.reshape(1, d).astype(F32))


def _final_norm_kernel(h_ref, d_ref, g_ref, o_ref):
    x = h_ref[...] + d_ref[...].astype(F32)
    o_ref[...] = x * lax.rsqrt(jnp.mean(x * x, axis=-1, keepdims=True) + EPS) * g_ref[...]


def _final_norm(h, delta, g, batch, seq):
    n, d = h.shape
    tp = n // batch
    tr = _tile(seq, 512, BF16_ROWS)
    nt = seq // tr
    src = pl.BlockSpec((pl.Element(tr), pl.Element(d)),
                       lambda b, i: (pl.multiple_of(b * tp + CHUNK + i * tr, CHUNK), 0))
    return pl.pallas_call(
        _final_norm_kernel, grid=(batch, nt),
        in_specs=[src, src, pl.BlockSpec((1, d), lambda b, i: (0, 0))],
        out_specs=pl.BlockSpec((tr, d), lambda b, i: (b * nt + i, 0)),
        out_shape=jax.ShapeDtypeStruct((batch * seq, d), F32),
        compiler_params=_params("parallel", "parallel"), name="final_norm")(h, delta, g.reshape(1, d).astype(F32))


def _cast_kernel(w_ref, o_ref):
    o_ref[...] = w_ref[...].astype(o_ref.dtype)


def _cast_bf16(w, block_bytes=8 * 1024 * 1024):
    shape = w.shape
    w2 = w.reshape(-1, shape[-1])
    rows, cols = w2.shape
    tr = _tile(rows, max(BF16_ROWS, block_bytes // (4 * cols)), BF16_ROWS)
    spec = pl.BlockSpec((tr, cols), lambda i: (i, 0))
    out = pl.pallas_call(
        _cast_kernel, grid=(rows // tr,), in_specs=[spec], out_specs=spec,
        out_shape=jax.ShapeDtypeStruct((rows, cols), BF16),
        compiler_params=_params("parallel"), name="cast_bf16")(w2)
    return out.reshape(shape)


def _split_cast_kernel(w_ref, main_ref, small_ref, *, a_end, b_start, b_end):
    w = w_ref[...]
    main_ref[:, :a_end] = w[:, :a_end].astype(main_ref.dtype)
    main_ref[:, a_end:] = w[:, b_start:b_end].astype(main_ref.dtype)
    rows = w.shape[0]
    narrow = jnp.concatenate([w[:, a_end:b_start], w[:, b_end:]], axis=1)
    pad = jnp.zeros((rows, small_ref.shape[1] - narrow.shape[1]), F32)
    small_ref[...] = jnp.concatenate([narrow, pad], axis=1).astype(small_ref.dtype)


def _split_cast(w_in, a_end, b_start, b_end):
    d, cols = w_in.shape
    tr = _tile(d, 256, BF16_ROWS)
    wide = a_end + b_end - b_start
    return pl.pallas_call(
        functools.partial(_split_cast_kernel, a_end=a_end, b_start=b_start, b_end=b_end), grid=(d // tr,),
        in_specs=[pl.BlockSpec((tr, cols), lambda i: (i, 0))],
        out_specs=[pl.BlockSpec((tr, wide), lambda i: (i, 0)), pl.BlockSpec((tr, LANES), lambda i: (i, 0))],
        out_shape=[jax.ShapeDtypeStruct((d, wide), BF16), jax.ShapeDtypeStruct((d, LANES), BF16)],
        compiler_params=_params("parallel"), name="split_cast")(w_in)


def _mm_kernel(*refs, has_res):
    if has_res:
        a_ref, w_ref, r_ref, o_ref = refs
    else:
        a_ref, w_ref, o_ref = refs
    acc = _dot(a_ref[...], w_ref[...])
    if has_res:
        acc = acc + r_ref[...]
    o_ref[...] = acc.astype(o_ref.dtype)


def _matmul(a, w, out_dtype, res=None, tm_target=688, tn_target=512):
    n, k = a.shape
    m = w.shape[1]
    tm = _tile(n, tm_target, BF16_ROWS)
    tn = _tile(m, tn_target, LANES)
    in_specs = [pl.BlockSpec((tm, k), lambda i, j: (i, 0)), pl.BlockSpec((k, tn), lambda i, j: (0, j))]
    args = [a, w]
    if res is not None:
        in_specs.append(pl.BlockSpec((tm, tn), lambda i, j: (i, j)))
        args.append(res)
    return pl.pallas_call(
        functools.partial(_mm_kernel, has_res=res is not None),
        grid=(n // tm, m // tn), in_specs=in_specs,
        out_specs=pl.BlockSpec((tm, tn), lambda i, j: (i, j)),
        out_shape=jax.ShapeDtypeStruct((n, m), out_dtype),
        compiler_params=_params("parallel", "arbitrary"), name="matmul")(*args)


def _conv_taps(ext_s, cw_ref, tt):
    out = cw_ref[0:1, :] * ext_s[pl.ds(SUBLANES - 2, tt), :]
    for j in range(1, 4):
        out = out + cw_ref[j:j + 1, :] * ext_s[pl.ds(SUBLANES - 2 + j, tt), :]
    return out


def _fill_ext(ext_s, cur_ref, prev_ref, next_ref, row0, tt, has_next):
    rows = row0 + lax.broadcasted_iota(jnp.int32, (tt, 1), 0)
    ext_s[pl.ds(SUBLANES, tt), :] = jnp.where(rows >= PAD, cur_ref[...].astype(F32), 0.0)
    prow = row0 - SUBLANES + lax.broadcasted_iota(jnp.int32, (SUBLANES, 1), 0)
    ext_s[pl.ds(0, SUBLANES), :] = jnp.where(prow >= PAD, prev_ref[...].astype(F32)[SUBLANES:, :], 0.0)
    ext_s[pl.ds(SUBLANES + tt, SUBLANES), :] = jnp.where(has_next, next_ref[...].astype(F32)[:SUBLANES, :], 0.0)


def _qkconv_kernel(cur_ref, prev_ref, next_ref, cw_ref, o_ref, ext_s, *, tt, nt, kscale, half):
    t = pl.program_id(1)
    row0 = t * tt
    _fill_ext(ext_s, cur_ref, prev_ref, next_ref, row0, tt, t < nt - 1)
    y = _conv_taps(ext_s, cw_ref, tt)
    y = y * jax.nn.sigmoid(y)
    col = lax.broadcasted_iota(jnp.int32, (1, 2 * half), 1)
    y = y * jnp.where(col >= half, kscale, 1.0)
    rows = row0 + lax.broadcasted_iota(jnp.int32, (tt, 1), 0)
    o_ref[...] = jnp.where(rows >= PAD, y, 0.0).astype(o_ref.dtype)


def _halo_specs(width, col_block, tt, tp, n):
    per_b, per_t = tp // BF16_ROWS, tt // BF16_ROWS
    last = n // BF16_ROWS - 1

    def make(tmap):
        cur = pl.BlockSpec((tt, width), lambda b, t, *_: (b * (tp // tt) + tmap(t), col_block(*_)))
        prev = pl.BlockSpec((BF16_ROWS, width),
                            lambda b, t, *_: (jnp.maximum(b * per_b + tmap(t) * per_t - 1, 0), col_block(*_)))
        nxt = pl.BlockSpec((BF16_ROWS, width),
                           lambda b, t, *_: (jnp.minimum(b * per_b + (tmap(t) + 1) * per_t, last), col_block(*_)))
        return cur, prev, nxt
    return make


def _qk_conv(proj, conv_w, batch, dk):
    n = proj.shape[0]
    tp = n // batch
    width = 2 * HEADS * dk
    tt = _tile(tp, 688, BF16_ROWS)
    nt = tp // tt
    cur, prev, nxt = _halo_specs(width, lambda: 0, tt, tp, n)(lambda t: t)
    return pl.pallas_call(
        functools.partial(_qkconv_kernel, tt=tt, nt=nt, kscale=dk ** -0.5, half=HEADS * dk),
        grid=(batch, nt),
        in_specs=[cur, prev, nxt, pl.BlockSpec((4, width), lambda b, t: (0, 0))],
        out_specs=pl.BlockSpec((tt, width), lambda b, t: (b * nt + t, 0)),
        out_shape=jax.ShapeDtypeStruct((n, width), BF16),
        scratch_shapes=[pltpu.VMEM((tt + 2 * SUBLANES, width), F32)],
        compiler_params=_params("parallel", "parallel"), name="qk_conv")(proj, proj, proj, conv_w.astype(F32))


def _mlstm_kernel(q_ref, k_ref, v_ref, g_ref, gt_ref, gb_ref, gbt_ref, o_ref, c_s, m_s, *, reverse, dk, dv, nc, batch):
    step = pl.program_id(0)

    @pl.when(step == 0)
    def _():
        c_s[...] = jnp.zeros_like(c_s)
        m_s[...] = jnp.zeros_like(m_s)

    chunk = nc - 1 - step if reverse else step
    L = CHUNK
    real = chunk > 0
    valid_c = jnp.logical_or(real, lax.broadcasted_iota(jnp.int32, (L, 1), 0) >= PAD)
    valid_r = jnp.logical_or(real, lax.broadcasted_iota(jnp.int32, (1, L), 1) >= PAD)
    off = 2 * HEADS if reverse else 0
    ri = lax.broadcasted_iota(jnp.int32, (L, L), 0)
    ci = lax.broadcasted_iota(jnp.int32, (L, L), 1)
    mask = (ci >= ri) if reverse else (ci <= ri)
    inc = (ri >= ci) if reverse else (ri <= ci)
    last = 0 if reverse else L - 1
    ones_col = jnp.where(lax.broadcasted_iota(jnp.int32, (L, LANES), 1) == 0, 1.0, 0.0).astype(BF16)
    lane_pad = jnp.zeros((dk, LANES - L), BF16)

    lf_c, lf_r, li_r = [], [], []
    for b in range(batch):
        g = g_ref[b, :, :GATE_COLS] + gb_ref[...]
        gt = gt_ref[b, 0] + gbt_ref[...]
        lf_c.append(jnp.where(valid_c, _log_sigmoid(g), 0.0))
        lf_r.append(jnp.where(valid_r, _log_sigmoid(gt), 0.0))
        li_r.append(jnp.where(valid_r, gt[off:off + HEADS, :], NEG))
    terms_c = [t for x in lf_c for t in _split3(x)]
    sums_c = _dot(mask.astype(BF16), jnp.concatenate(terms_c, axis=1))
    terms_r = [t for x in lf_r for t in _split3(x)]
    sums_r = _dot(jnp.concatenate(terms_r, axis=0), inc.astype(BF16))
    w = GATE_COLS

    for b in range(batch):
        cum_c = sum(sums_c[:, (3 * b + j) * w:(3 * b + j + 1) * w] for j in range(3))
        cum_r = sum(sums_r[(3 * b + j) * w:(3 * b + j + 1) * w, :] for j in range(3))
        for h in range(HEADS):
            sh = b * HEADS + h
            col = off + HEADS + h
            cc = cum_c[:, col:col + 1]
            cr = cum_r[col:col + 1, :]
            lir = li_r[b][h:h + 1, :]
            tot = cc[last:last + 1, :]
            m = m_s[sh, 0:1, 0:1]
            qh = q_ref[b, :, h * dk:(h + 1) * dk]
            k_t = k_ref[b, :, h * dk:(h + 1) * dk].astype(F32).T
            vh = jnp.where(valid_c, v_ref[b, :, h * dv:(h + 1) * dv], 0.0).astype(BF16)
            vaug = jnp.concatenate([vh, ones_col], axis=1)

            rhs = jnp.concatenate([k_t.astype(BF16), lane_pad, c_s[sh].astype(BF16)], axis=1)
            qkc = _dot(qh, rhs)
            d_mat = jnp.where(mask, cc - cr + lir, NEG)
            inter = cc + m
            m_t = jnp.maximum(inter, jnp.max(d_mat, axis=1, keepdims=True))
            w_inter = jnp.exp(inter - m_t)
            s = qkc[:, :L] * jnp.exp(d_mat - m_t)

            gs = tot - cr + lir
            m_new = jnp.maximum(tot + m, jnp.max(gs, axis=1, keepdims=True))
            decay = jnp.exp(tot + m - m_new)
            ks_t = (k_t * jnp.exp(gs - m_new)).astype(BF16)
            sv = _dot(jnp.concatenate([s.astype(BF16), ks_t], axis=0), vaug)
            haug = w_inter * qkc[:, LANES:] + sv[:L]
            den = haug[:, dv:dv + 1]
            o_ref[b, :, h * dv:(h + 1) * dv] = (
                haug[:, :dv] / jnp.maximum(jnp.abs(den), jnp.exp(-m_t))).astype(o_ref.dtype)
            c_s[sh] = decay * c_s[sh] + sv[L:]
            m_s[sh] = jnp.broadcast_to(m_new, m_s.shape[1:])


def _chunk_spec(batch, width, col_block, nc, reverse):
    cidx = (lambda i: nc - 1 - i) if reverse else (lambda i: i)
    return pl.BlockSpec((batch, CHUNK, width), lambda i: (0, cidx(i), col_block))


def _mlstm(qk, proj, gates, gates_t, gate_bias, batch, dk, dv, v_block, reverse):
    n = qk.shape[0]
    tp = n // batch
    nc = tp // CHUNK
    cidx = (lambda i: nc - 1 - i) if reverse else (lambda i: i)
    gb = gate_bias.reshape(1, GATE_COLS).astype(F32)
    view = lambda a: a.reshape(batch, tp, a.shape[-1])
    out = pl.pallas_call(
        functools.partial(_mlstm_kernel, reverse=reverse, dk=dk, dv=dv, nc=nc, batch=batch),
        grid=(nc,),
        in_specs=[_chunk_spec(batch, HEADS * dk, 0, nc, reverse), _chunk_spec(batch, HEADS * dk, 1, nc, reverse),
                  _chunk_spec(batch, HEADS * dv, v_block, nc, reverse), _chunk_spec(batch, LANES, 0, nc, reverse),
                  pl.BlockSpec((batch, 1, GATE_COLS, CHUNK), lambda i: (0, cidx(i), 0, 0)),
                  pl.BlockSpec((1, GATE_COLS), lambda i: (0, 0)),
                  pl.BlockSpec((GATE_COLS, 1), lambda i: (0, 0))],
        out_specs=_chunk_spec(batch, HEADS * dv, 0, nc, reverse),
        out_shape=jax.ShapeDtypeStruct((batch, tp, HEADS * dv), BF16),
        scratch_shapes=[pltpu.VMEM((batch * HEADS, dk, dv + LANES), F32),
                        pltpu.VMEM((batch * HEADS, SUBLANES, LANES), F32)],
        compiler_params=_params("arbitrary"),
        name="mlstm_bwd" if reverse else "mlstm_fwd")(
            view(qk), view(qk), view(proj), view(gates), gates_t.reshape(batch, nc, GATE_COLS, CHUNK), gb,
            gb.reshape(GATE_COLS, 1))
    return out.reshape(n, HEADS * dv)


def _gla_head_exact(q, k, v, cumh, state, o_ref, b, h, *, reverse, dk, dv):
    L = CHUNK
    nsub = L // SUB
    sub_lane = lax.broadcasted_iota(jnp.int32, (SUB, L), 1)
    sub_row = lax.broadcasted_iota(jnp.int32, (SUB, 1), 0)
    o_inter = _dot_t((q * jnp.exp(cumh)).astype(BF16), state.astype(BF16))
    for blk in range(nsub):
        r0 = blk * SUB
        if reverse:
            cs = cumh[r0 + SUB:r0 + SUB + 1, :] if blk < nsub - 1 else jnp.zeros((1, dk), F32)
            earlier = sub_lane >= r0 + SUB
        else:
            cs = cumh[r0 - 1:r0, :] if blk > 0 else jnp.zeros((1, dk), F32)
            earlier = sub_lane < r0
        q_b = q[r0:r0 + SUB, :]
        cum_b = cumh[r0:r0 + SUB, :]
        qd = (q_b * jnp.exp(cum_b - cs)).astype(BF16)
        kd = (k * jnp.exp(jnp.minimum(cs - cumh, 0.0))).astype(BF16)
        att = jnp.where(earlier, _dot_t(qd, kd), 0.0)
        for j in range(SUB):
            s_idx = r0 + j
            tmask = (sub_row <= j) if reverse else (sub_row >= j)
            e = jnp.where(tmask, cum_b - cumh[s_idx:s_idx + 1, :], NEG)
            col = jnp.sum(q_b * k[s_idx:s_idx + 1, :] * jnp.exp(e), axis=1, keepdims=True)
            att = jnp.where(sub_lane == s_idx, col, att)
        o_b = o_inter[r0:r0 + SUB, :] + _dot(att.astype(BF16), v)
        o_ref[b, r0:r0 + SUB, h * dv:(h + 1) * dv] = o_b.astype(o_ref.dtype)


def _gla_head_factored(q, k, v, cumh, state, mask, o_ref, b, h, *, dv):
    qe = (q * jnp.exp(cumh)).astype(BF16)
    ke = (k * jnp.exp(-cumh)).astype(BF16)
    att = jnp.where(mask, _dot_t(qe, ke), 0.0)
    o = _dot_t(qe, state.astype(BF16)) + _dot(att.astype(BF16), v)
    o_ref[b, :, h * dv:(h + 1) * dv] = o.astype(o_ref.dtype)


def _gla_kernel(q_ref, k_ref, v_ref, lr_ref, up_ref, ub_ref, o_ref, s_s, *, reverse, dk, dv, nc, batch):
    step = pl.program_id(0)

    @pl.when(step == 0)
    def _():
        s_s[...] = jnp.zeros_like(s_s)

    chunk = nc - 1 - step if reverse else step
    L = CHUNK
    valid_c = jnp.logical_or(chunk > 0, lax.broadcasted_iota(jnp.int32, (L, 1), 0) >= PAD)
    off = GATE_COLS + (B_RANK if reverse else 0)
    ri = lax.broadcasted_iota(jnp.int32, (L, L), 0)
    ci = lax.broadcasted_iota(jnp.int32, (L, L), 1)
    mask = (ci >= ri) if reverse else (ci <= ri)
    last = 0 if reverse else L - 1
    cums = []
    for b in range(batch):
        z = jnp.dot(lr_ref[b, :, off:off + B_RANK], up_ref[...], precision=HIGHEST, preferred_element_type=F32)
        la = jnp.where(valid_c, _log_sigmoid(z + ub_ref[...]) / B_TAU, 0.0)
        cums.append(sum(_dot(mask.astype(BF16), t) for t in _split3(la)))
    lowest = jnp.min(jnp.concatenate([c[last:last + 1, :] for c in cums], axis=0))
    factorable = lowest >= -GLA_MAX_CHUNK_DECAY

    def run(factored):
        for b in range(batch):
            for h in range(HEADS):
                sl = slice(h * dk, (h + 1) * dk)
                q = jnp.where(valid_c, q_ref[b, :, sl], 0.0).astype(F32) * dk ** -0.5
                k = jnp.where(valid_c, k_ref[b, :, sl], 0.0).astype(F32)
                v = jnp.where(valid_c, v_ref[b, :, h * dv:(h + 1) * dv], 0.0).astype(BF16)
                cumh = cums[b][:, sl]
                tot = cumh[last:last + 1, :]
                state = s_s[b * HEADS + h]
                if factored:
                    _gla_head_factored(q, k, v, cumh, state, mask, o_ref, b, h, dv=dv)
                else:
                    _gla_head_exact(q, k, v, cumh, state, o_ref, b, h, reverse=reverse, dk=dk, dv=dv)
                kdec = (k * jnp.exp(tot - cumh)).astype(BF16)
                s_s[b * HEADS + h] = jnp.exp(tot) * state + _tdot(v, kdec)

    @pl.when(factorable)
    def _():
        run(True)

    @pl.when(jnp.logical_not(factorable))
    def _():
        run(False)


def _gla(proj, small, lr_up, lr_bias, batch, dk, dv, qkv_blocks, reverse):
    n = proj.shape[0]
    tp = n // batch
    nc = tp // CHUNK
    qb, kb, vb = qkv_blocks
    view = lambda a: a.reshape(batch, tp, a.shape[-1])
    out = pl.pallas_call(
        functools.partial(_gla_kernel, reverse=reverse, dk=dk, dv=dv, nc=nc, batch=batch),
        grid=(nc,),
        in_specs=[_chunk_spec(batch, HEADS * dk, qb, nc, reverse), _chunk_spec(batch, HEADS * dk, kb, nc, reverse),
                  _chunk_spec(batch, HEADS * dv, vb, nc, reverse), _chunk_spec(batch, LANES, 0, nc, reverse),
                  pl.BlockSpec((B_RANK, HEADS * dk), lambda i: (0, 0)),
                  pl.BlockSpec((1, HEADS * dk), lambda i: (0, 0))],
        out_specs=_chunk_spec(batch, HEADS * dv, 0, nc, reverse),
        out_shape=jax.ShapeDtypeStruct((batch, tp, HEADS * dv), BF16),
        scratch_shapes=[pltpu.VMEM((batch * HEADS, dv, dk), F32)],
        compiler_params=_params("arbitrary"),
        name="gla_bwd" if reverse else "gla_fwd")(
            view(proj), view(proj), view(proj), view(small), lr_up.astype(F32), lr_bias.reshape(1, -1).astype(F32))
    return out.reshape(n, HEADS * dv)


def _head_norm(x, g, dv):
    parts = []
    for h in range(HEADS):
        xh = x[:, h * dv:(h + 1) * dv]
        parts.append(xh * lax.rsqrt(jnp.mean(xh * xh, axis=-1, keepdims=True) + EPS))
    return jnp.concatenate(parts, axis=1) * g


def _even_combine_kernel(af_ref, ab_ref, bf_ref, bb_ref, oa_ref, gb_ref, na_ref, nb_ref, o_ref, *, tr, tp, dv):
    rows = (pl.program_id(0) * tr) % tp + lax.broadcasted_iota(jnp.int32, (tr, 1), 0)
    valid = rows >= PAD
    w = HEADS * dv
    ha = af_ref[...].astype(F32) + ab_ref[...].astype(F32)
    ya = jax.nn.sigmoid(oa_ref[...].astype(F32)) * _head_norm(ha, na_ref[...], dv)
    o_ref[:, :w] = jnp.where(valid, ya, 0.0).astype(o_ref.dtype)
    hb = bf_ref[...].astype(F32) + bb_ref[...].astype(F32)
    gb = gb_ref[...].astype(F32)
    yb = gb * jax.nn.sigmoid(gb) * _head_norm(hb, nb_ref[...], dv)
    o_ref[:, w:] = jnp.where(valid, yb, 0.0).astype(o_ref.dtype)


def _even_combine(ha_f, ha_b, hb_f, hb_b, proj, norm_a, norm_b, batch, dv, oa_block, gb_block):
    n, w = ha_f.shape
    tp = n // batch
    tr = _tile(tp, 384, BF16_ROWS)
    row = pl.BlockSpec((tr, w), lambda i: (i, 0))
    vec = pl.BlockSpec((1, w), lambda i: (0, 0))
    return pl.pallas_call(
        functools.partial(_even_combine_kernel, tr=tr, tp=tp, dv=dv),
        grid=(n // tr,),
        in_specs=[row, row, row, row, pl.BlockSpec((tr, w), lambda i: (i, oa_block)),
                  pl.BlockSpec((tr, w), lambda i: (i, gb_block)), vec, vec],
        out_specs=pl.BlockSpec((tr, 2 * w), lambda i: (i, 0)),
        out_shape=jax.ShapeDtypeStruct((n, 2 * w), BF16),
        compiler_params=_params("parallel"), name="even_combine")(
            ha_f, ha_b, hb_f, hb_b, proj, proj, norm_a.reshape(1, w).astype(F32), norm_b.reshape(1, w).astype(F32))


def _block_scan(a, b, reverse):
    sub = lax.broadcasted_iota(jnp.int32, a.shape, 1)
    for k in (1, 2, 4):
        if reverse:
            a_sh, b_sh, m = pltpu.roll(a, SUBLANES - k, 1), pltpu.roll(b, SUBLANES - k, 1), sub < SUBLANES - k
        else:
            a_sh, b_sh, m = pltpu.roll(a, k, 1), pltpu.roll(b, k, 1), sub >= k
        b = jnp.where(m, a * b_sh + b, b)
        a = jnp.where(m, a * a_sh, a)
    return a, b


def _rglru_kernel(cur_ref, prev_ref, next_ref, cw_ref, cb_ref, wr_ref, br_ref, wi_ref, bi_ref, lam_ref,
                  o_ref, ext_s, a_s, b_s, h_s, carry_s, *, reverse, tt, nt):
    step = pl.program_id(2)

    @pl.when(step == 0)
    def _():
        carry_s[...] = jnp.zeros_like(carry_s)

    t = nt - 1 - step if reverse else step
    row0 = t * tt
    _fill_ext(ext_s, cur_ref, prev_ref, next_ref, row0, tt, t < nt - 1)
    u = _conv_taps(ext_s, cw_ref, tt) + cb_ref[...]
    ub = u.astype(BF16)
    r = _sigmoid(_dot(ub, wr_ref[0]) + br_ref[...])
    gi = _sigmoid(_dot(ub, wi_ref[0]) + bi_ref[...])
    lam = lam_ref[...]
    softplus = jnp.maximum(-lam, 0.0) + jnp.log1p(jnp.exp(-jnp.abs(lam)))
    log_a = -RNN_C * r * softplus
    a = jnp.exp(log_a)
    rows = row0 + lax.broadcasted_iota(jnp.int32, (tt, 1), 0)
    inp = jnp.where(rows >= PAD, jnp.sqrt(1.0 - a * a) * (gi * u), 0.0)
    c = a.shape[1]
    ng = tt // SUBLANES
    a_g, b_g = _block_scan(a.reshape(ng, SUBLANES, c), inp.reshape(ng, SUBLANES, c), reverse)
    a_s[...] = a_g.reshape(tt, c)
    b_s[...] = b_g.reshape(tt, c)
    out_row = 0 if reverse else SUBLANES - 1

    def body(i, carry):
        g = ng - 1 - i if reverse else i
        r0 = pl.multiple_of(g * SUBLANES, SUBLANES)
        hh = b_s[pl.ds(r0, SUBLANES), :] + a_s[pl.ds(r0, SUBLANES), :] * carry
        h_s[pl.ds(r0, SUBLANES), :] = hh
        return hh[out_row:out_row + 1, :]

    carry_s[...] = lax.fori_loop(0, ng, body, carry_s[...])
    o_ref[...] = h_s[...].astype(o_ref.dtype)


def _pair_blocks(w):
    nb, r, _ = w.shape
    z = jnp.zeros((nb // 2, r, r), w.dtype)
    top = jnp.concatenate([w[0::2], z], axis=2)
    bot = jnp.concatenate([z, w[1::2]], axis=2)
    return jnp.concatenate([top, bot], axis=1).astype(BF16)


def _rglru(proj, conv_w, conv_b, w_r, b_r, w_i, b_i, lam, batch, d_rnn, reverse):
    n = proj.shape[0]
    tp = n // batch
    cw = 2 * d_rnn // RNN_BLOCKS
    ncb = d_rnn // cw
    tt = _tile(tp, 688, BF16_ROWS)
    nt = tp // tt
    tmap = (lambda t: nt - 1 - t) if reverse else (lambda t: t)
    cur, prev, nxt = _halo_specs(cw, lambda j: ncb + j, tt, tp, n)(tmap)
    def swap(spec):
        f = spec.index_map
        return pl.BlockSpec(spec.block_shape, lambda b, j, t: f(b, t, j))
    vec = pl.BlockSpec((1, cw), lambda b, j, t: (0, j))
    wspec = pl.BlockSpec((1, cw, cw), lambda b, j, t: (j, 0, 0))
    row = lambda x: x.reshape(1, d_rnn).astype(F32)
    return pl.pallas_call(
        functools.partial(_rglru_kernel, reverse=reverse, tt=tt, nt=nt),
        grid=(batch, ncb, nt),
        in_specs=[swap(cur), swap(prev), swap(nxt), pl.BlockSpec((4, cw), lambda b, j, t: (0, j)), vec,
                  wspec, vec, wspec, vec, vec],
        out_specs=pl.BlockSpec((tt, cw), lambda b, j, t: (b * nt + tmap(t), j)),
        out_shape=jax.ShapeDtypeStruct((n, d_rnn), BF16),
        scratch_shapes=[pltpu.VMEM((tt + 2 * SUBLANES, cw), F32), pltpu.VMEM((tt, cw), F32),
                        pltpu.VMEM((tt, cw), F32), pltpu.VMEM((tt, cw), F32), pltpu.VMEM((1, cw), F32)],
        compiler_params=_params("parallel", "parallel", "arbitrary"),
        name="rglru_bwd" if reverse else "rglru_fwd")(
            proj, proj, proj, conv_w.astype(F32), row(conv_b), _pair_blocks(w_r), row(b_r),
            _pair_blocks(w_i), row(b_i), row(lam))


def _odd_combine_kernel(g_ref, hf_ref, hb_ref, o_ref, *, tr, tp):
    rows = (pl.program_id(0) * tr) % tp + lax.broadcasted_iota(jnp.int32, (tr, 1), 0)
    y = jax.nn.gelu(g_ref[...].astype(F32)) * (hf_ref[...].astype(F32) + hb_ref[...].astype(F32))
    o_ref[...] = jnp.where(rows >= PAD, y, 0.0).astype(o_ref.dtype)


def _odd_combine(proj, hf, hb, batch):
    n, w = hf.shape
    tp = n // batch
    tr = _tile(tp, 384, BF16_ROWS)
    row = pl.BlockSpec((tr, w), lambda i: (i, 0))
    return pl.pallas_call(
        functools.partial(_odd_combine_kernel, tr=tr, tp=tp), grid=(n // tr,),
        in_specs=[row, row, row], out_specs=row, out_shape=jax.ShapeDtypeStruct((n, w), BF16),
        compiler_params=_params("parallel"), name="odd_combine")(proj, hf, hb)


def _route(x):
    lane = lax.broadcasted_iota(jnp.int32, x.shape, 1)
    big = jnp.int32(2 * LANES)
    gmask = lane < N_GROUPS
    gmax = jnp.max(jnp.where(gmask, x, -jnp.inf), axis=1, keepdims=True)
    ge = jnp.where(gmask, jnp.exp(x - gmax), 0.0)
    gp = ge / jnp.sum(ge, axis=1, keepdims=True)
    gval = jnp.max(gp, axis=1, keepdims=True)
    gidx = jnp.min(jnp.where(jnp.logical_and(gmask, gp == gval), lane, big), axis=1, keepdims=True)
    lo = N_GROUPS + gidx * EXPERTS_PER_GROUP
    emask = jnp.logical_and(lane >= lo, lane < lo + EXPERTS_PER_GROUP)
    emax = jnp.max(jnp.where(emask, x, -jnp.inf), axis=1, keepdims=True)
    ee = jnp.where(emask, jnp.exp(x - emax), 0.0)
    ep = ee / jnp.sum(ee, axis=1, keepdims=True)
    v1 = jnp.max(jnp.where(emask, ep, -1.0), axis=1, keepdims=True)
    i1 = jnp.min(jnp.where(jnp.logical_and(emask, ep == v1), lane, big), axis=1, keepdims=True)
    rest = jnp.logical_and(emask, lane != i1)
    v2 = jnp.max(jnp.where(rest, ep, -1.0), axis=1, keepdims=True)
    i2 = jnp.min(jnp.where(jnp.logical_and(rest, ep == v2), lane, big), axis=1, keepdims=True)
    tot = v1 + v2
    comb = jnp.where(lane == i1, v1 / tot * gval, jnp.where(lane == i2, v2 / tot * gval, 0.0))
    return comb, gidx


def _pack_pair(hi, lo):
    bits = lambda v: lax.bitcast_convert_type(v.astype(BF16).astype(F32), jnp.uint32)
    return bits(hi) | (bits(lo) >> 16)


def _unpack_pair(w):
    hi = lax.bitcast_convert_type(w & jnp.uint32(0xFFFF0000), F32)
    lo = lax.bitcast_convert_type(w << 16, F32)
    return hi.astype(BF16), lo.astype(BF16)


def _store_packed_rows(ref, x, npack):
    half = npack * LANES
    for j in range(npack):
        ref[:, j, :] = _pack_pair(x[:, j * LANES:(j + 1) * LANES], x[:, half + j * LANES:half + (j + 1) * LANES])


def _load_packed_rows(src, dense_s, dst_ref, npack):
    half = npack * LANES
    for j in range(npack):
        dense_s[:, j * LANES:(j + 1) * LANES] = src(j)
    for j in range(npack):
        hi, lo = _unpack_pair(dense_s[:, j * LANES:(j + 1) * LANES])
        dst_ref[:, j * LANES:(j + 1) * LANES] = hi.astype(dst_ref.dtype)
        dst_ref[:, half + j * LANES:half + (j + 1) * LANES] = lo.astype(dst_ref.dtype)


def _norm_route_kernel(h_ref, g_ref, w2_ref, wh_ref, b_ref, slab_ref, oh_ref, *, npack):
    x = h_ref[...]
    y = x * lax.rsqrt(jnp.mean(x * x, axis=-1, keepdims=True) + EPS) * g_ref[...]
    yh = y.astype(BF16)
    yl = (y - yh.astype(F32)).astype(BF16)
    r1 = _dot(yh, w2_ref[...])
    logits = r1[:, :LANES] + r1[:, LANES:] + _dot(yl, wh_ref[...]) + b_ref[...]
    comb, gidx = _route(logits)
    _store_packed_rows(slab_ref, y, npack)
    slab_ref[:, npack, :] = lax.bitcast_convert_type(comb, jnp.uint32)
    for j in range(npack + 1, slab_ref.shape[1]):
        slab_ref[:, j, :] = jnp.zeros(comb.shape, jnp.uint32)
    lane = lax.broadcasted_iota(jnp.int32, comb.shape, 1)
    oh_ref[...] = jnp.where(lane == gidx, 1.0, 0.0).astype(oh_ref.dtype)


def _norm_route(h, g, wg, bg, we, be):
    n, d = h.shape
    npack = d // (2 * LANES)
    srows = (npack + 1 + SUBLANES - 1) // SUBLANES * SUBLANES
    tr = _tile(n, 192, BF16_ROWS)
    zpad = LANES - N_GROUPS - N_EXPERTS
    wr = jnp.concatenate([wg, we, jnp.zeros((d, zpad), F32)], axis=1)
    wh = wr.astype(BF16)
    wl = (wr - wh.astype(F32)).astype(BF16)
    bias = jnp.concatenate([bg.astype(F32), be.astype(F32), jnp.zeros((zpad,), F32)]).reshape(1, LANES)
    return pl.pallas_call(
        functools.partial(_norm_route_kernel, npack=npack), grid=(n // tr,),
        in_specs=[pl.BlockSpec((tr, d), lambda i: (i, 0)), pl.BlockSpec((1, d), lambda i: (0, 0)),
                  pl.BlockSpec((d, 2 * LANES), lambda i: (0, 0)), pl.BlockSpec((d, LANES), lambda i: (0, 0)),
                  pl.BlockSpec((1, LANES), lambda i: (0, 0))],
        out_specs=[pl.BlockSpec((tr, srows, LANES), lambda i: (i, 0, 0)), pl.BlockSpec((tr, LANES), lambda i: (i, 0))],
        out_shape=[jax.ShapeDtypeStruct((n, srows, LANES), jnp.uint32), jax.ShapeDtypeStruct((n, LANES), BF16)],
        compiler_params=_params("parallel"), name="norm_route")(
            h, g.reshape(1, d).astype(F32), jnp.concatenate([wh, wl], axis=1), wh, bias)


def _rank_kernel(oh_ref, g_ref, rank_ref, cnt_ref, carry_s):
    @pl.when(pl.program_id(0) == 0)
    def _():
        carry_s[...] = jnp.zeros_like(carry_s)

    tr = oh_ref.shape[0]
    sel = jnp.where(lax.broadcasted_iota(jnp.int32, (SUBLANES, LANES), 0) ==
                    lax.broadcasted_iota(jnp.int32, (SUBLANES, LANES), 1), 1.0, 0.0).astype(BF16)
    oh_t = _dot_t(sel, oh_ref[...])
    before = (lax.broadcasted_iota(jnp.int32, (tr, tr), 0) < lax.broadcasted_iota(jnp.int32, (tr, tr), 1))
    cum = _dot(oh_t.astype(BF16), jnp.where(before, 1.0, 0.0).astype(BF16)) + carry_s[:, 0:1]
    gid = lax.broadcasted_iota(jnp.int32, (SUBLANES, tr), 0).astype(F32)
    rank_ref[0] = jnp.sum(oh_t * cum, axis=0, keepdims=True).astype(jnp.int32)
    g_ref[0] = jnp.sum(oh_t * gid, axis=0, keepdims=True).astype(jnp.int32)
    carry_s[...] = carry_s[...] + jnp.sum(oh_t, axis=1, keepdims=True)
    cnt_ref[...] = carry_s[...]


def _rank(onehot):
    n = onehot.shape[0]
    tr = _tile(n, 384, LANES)
    row = pl.BlockSpec((1, 1, tr), lambda i: (i, 0, 0))
    g, rank, cnt = pl.pallas_call(
        _rank_kernel, grid=(n // tr,), in_specs=[pl.BlockSpec((tr, LANES), lambda i: (i, 0))],
        out_specs=[row, row, pl.BlockSpec((SUBLANES, LANES), lambda i: (0, 0))],
        out_shape=[jax.ShapeDtypeStruct((n // tr, 1, tr), jnp.int32), jax.ShapeDtypeStruct((n // tr, 1, tr), jnp.int32),
                   jax.ShapeDtypeStruct((SUBLANES, LANES), F32)],
        scratch_shapes=[pltpu.VMEM((SUBLANES, LANES), F32)],
        compiler_params=_params("arbitrary"), name="rank")(onehot)
    return g.reshape(n), rank.reshape(n), cnt[:N_GROUPS, 0].astype(jnp.int32)


def _invert_kernel(g_ref, rank_ref, cnt_ref, pos_ref, idx_ref, tg_ref, *, n, tm, ntiles):
    bases = [jnp.int32(0)]
    for g in range(N_GROUPS - 1):
        bases.append(bases[-1] + (cnt_ref[g] + tm - 1) // tm * tm)

    def zero(i, c):
        idx_ref[i] = 0
        return c
    lax.fori_loop(0, ntiles * tm, zero, 0, unroll=8)

    def place(t, c):
        g = g_ref[t]
        base = bases[0]
        for k in range(1, N_GROUPS):
            base = jnp.where(g == k, bases[k], base)
        p = base + rank_ref[t]
        pos_ref[t] = p
        idx_ref[p] = t
        return c
    lax.fori_loop(0, n, place, 0, unroll=8)

    def tile_group(i, c):
        r = i * tm
        tg = jnp.int32(0)
        for k in range(1, N_GROUPS):
            tg = tg + (r >= bases[k]).astype(jnp.int32)
        tg_ref[i] = tg
        return c
    lax.fori_loop(0, ntiles, tile_group, 0)


def _invert(g, rank, cnt, tm, ntiles):
    n = g.shape[0]
    smem = pl.BlockSpec(memory_space=pltpu.SMEM)
    return pl.pallas_call(
        functools.partial(_invert_kernel, n=n, tm=tm, ntiles=ntiles),
        in_specs=[smem, smem, smem], out_specs=[smem, smem, smem],
        out_shape=[jax.ShapeDtypeStruct((n,), jnp.int32), jax.ShapeDtypeStruct((ntiles * tm,), jnp.int32),
                   jax.ShapeDtypeStruct((ntiles,), jnp.int32)],
        name="invert")(g, rank, cnt)


def _row_copy(src_hbm, buf, sem, src_row, slot, dst_row):
    return pltpu.make_async_copy(src_hbm.at[pl.ds(src_row, 1)], buf.at[slot, pl.ds(dst_row, 1)], sem.at[slot])


def _gather_rows(index_ref, src_hbm, buf, sem, rows):
    i = pl.program_id(0)
    steps = pl.num_programs(0)

    def issue(step, slot):
        def body(r, c):
            _row_copy(src_hbm, buf, sem, index_ref[step * rows + r], slot, r).start()
            return c
        lax.fori_loop(0, rows, body, 0, unroll=8)

    @pl.when(i == 0)
    def _():
        issue(0, 0)

    @pl.when(i + 1 < steps)
    def _():
        issue(i + 1, (i + 1) % 2)

    slot = i % 2

    def wait(r, c):
        _row_copy(src_hbm, buf, sem, 0, slot, r).wait()
        return c
    lax.fori_loop(0, rows, wait, 0, unroll=8)
    return slot


def _dispatch_kernel(idx_ref, slab_hbm, xs_ref, cs_ref, buf, dense_s, sem, *, npack):
    slot = _gather_rows(idx_ref, slab_hbm, buf, sem, xs_ref.shape[0])
    _load_packed_rows(lambda j: buf[slot, :, j, :], dense_s, xs_ref, npack)
    cs_ref[...] = lax.bitcast_convert_type(buf[slot, :, npack, :], F32)


def _dispatch(idx, slab, tm, ntiles, d):
    srows = slab.shape[1]
    return pl.pallas_call(
        functools.partial(_dispatch_kernel, npack=d // (2 * LANES)),
        grid_spec=pltpu.PrefetchScalarGridSpec(
            num_scalar_prefetch=1, grid=(ntiles,),
            in_specs=[pl.BlockSpec(memory_space=pl.ANY)],
            out_specs=[pl.BlockSpec((tm, d), lambda i, idx: (i, 0)), pl.BlockSpec((tm, LANES), lambda i, idx: (i, 0))],
            scratch_shapes=[pltpu.VMEM((2, tm, srows, LANES), jnp.uint32), pltpu.VMEM((tm, d // 2), jnp.uint32),
                            pltpu.SemaphoreType.DMA((2,))]),
        out_shape=[jax.ShapeDtypeStruct((ntiles * tm, d), BF16), jax.ShapeDtypeStruct((ntiles * tm, LANES), F32)],
        compiler_params=_params("arbitrary"), name="dispatch")(idx, slab)


def _expert_kernel(tg_ref, x_ref, c_ref, w1_ref, w3_ref, w2_ref, o_ref, acc_s, *, npack):
    i = pl.program_id(0)
    e = pl.program_id(1)

    @pl.when(e == 0)
    def _():
        acc_s[...] = jnp.zeros_like(acc_s)

    a = x_ref[...]
    h1 = _dot(a, w1_ref[0])
    h3 = _dot(a, w3_ref[0])
    comb = c_ref[...]
    lane = lax.broadcasted_iota(jnp.int32, comb.shape, 1)
    c = jnp.sum(jnp.where(lane == N_GROUPS + tg_ref[i] * EXPERTS_PER_GROUP + e, comb, 0.0), axis=1, keepdims=True)
    hid = (h1 * jax.nn.sigmoid(h1) * h3 * c).astype(BF16)
    acc_s[...] += _dot(hid, w2_ref[0])

    @pl.when(e == EXPERTS_PER_GROUP - 1)
    def _():
        _store_packed_rows(o_ref, acc_s, npack)


def _experts(tg, xs, cs, w1, w3, w2, tm):
    rows, d = xs.shape
    f = w1.shape[-1]
    npack = d // (2 * LANES)
    wmap = lambda i, e, tg: (tg[i] * EXPERTS_PER_GROUP + e, 0, 0)
    return pl.pallas_call(
        functools.partial(_expert_kernel, npack=npack),
        grid_spec=pltpu.PrefetchScalarGridSpec(
            num_scalar_prefetch=1, grid=(rows // tm, EXPERTS_PER_GROUP),
            in_specs=[pl.BlockSpec((tm, d), lambda i, e, tg: (i, 0)), pl.BlockSpec((tm, LANES), lambda i, e, tg: (i, 0)),
                      pl.BlockSpec((1, d, f), wmap), pl.BlockSpec((1, d, f), wmap), pl.BlockSpec((1, f, d), wmap)],
            out_specs=pl.BlockSpec((tm, npack, LANES), lambda i, e, tg: (i, 0, 0)),
            scratch_shapes=[pltpu.VMEM((tm, d), F32)]),
        out_shape=jax.ShapeDtypeStruct((rows, npack, LANES), jnp.uint32),
        compiler_params=_params("parallel", "arbitrary"), name="experts")(tg, xs, cs, w1, w3, w2)


def _collect_kernel(pos_ref, ys_hbm, o_ref, buf, dense_s, sem, *, npack):
    slot = _gather_rows(pos_ref, ys_hbm, buf, sem, o_ref.shape[0])
    _load_packed_rows(lambda j: buf[slot, :, j, :], dense_s, o_ref, npack)


def _collect(pos, ys):
    n = pos.shape[0]
    _, npack, _ = ys.shape
    d = 2 * npack * LANES
    tr = _tile(n, 384, BF16_ROWS)
    return pl.pallas_call(
        functools.partial(_collect_kernel, npack=npack),
        grid_spec=pltpu.PrefetchScalarGridSpec(
            num_scalar_prefetch=1, grid=(n // tr,),
            in_specs=[pl.BlockSpec(memory_space=pl.ANY)],
            out_specs=pl.BlockSpec((tr, d), lambda i, pos: (i, 0)),
            scratch_shapes=[pltpu.VMEM((2, tr, npack, LANES), jnp.uint32), pltpu.VMEM((tr, d // 2), jnp.uint32),
                            pltpu.SemaphoreType.DMA((2,))]),
        out_shape=jax.ShapeDtypeStruct((n, d), BF16),
        compiler_params=_params("arbitrary"), name="collect")(pos, ys)


def _moe_layer(h, ffn_g, wg, bg, we, be, w1, w3, w2):
    n, d = h.shape
    f = w1.shape[-1]
    tm = 512 if n >= 4096 else 128
    ntiles = (n + N_GROUPS * (tm - 1)) // tm
    slab, onehot = _norm_route(h, ffn_g, wg, bg, we, be)
    g, rank, cnt = _rank(onehot)
    pos, idx, tg = _invert(g, rank, cnt, tm, ntiles)
    xs, cs = _dispatch(idx, slab, tm, ntiles, d)
    ys = _experts(tg, xs, cs, _cast_bf16(w1).reshape(N_EXPERTS, d, f), _cast_bf16(w3).reshape(N_EXPERTS, d, f),
                  _cast_bf16(w2).reshape(N_EXPERTS, f, d), tm)
    return _collect(pos, ys)


def _even_layer(h, hn, w_in, gate_bias, qk_conv, lr_up, lr_bias, norm_a, norm_b, w_out, batch):
    n, d = h.shape
    dk, dv = d // 16, d // 8
    qk_w, v_w = HEADS * dk, HEADS * dv
    a_end = 2 * qk_w + 2 * v_w
    b_start = a_end + GATE_COLS
    b_end = b_start + 2 * qk_w + 2 * v_w
    w_main, w_small = _split_cast(w_in, a_end, b_start, b_end)
    proj = _matmul(hn, w_main, BF16, tm_target=1376)
    small = _matmul(hn, w_small, F32, tn_target=LANES)
    gates_t = small[:, :GATE_COLS].reshape(n // CHUNK, CHUNK, GATE_COLS).transpose(0, 2, 1)
    qk = _qk_conv(proj, qk_conv, batch, dk)
    va_blk, oa_blk = 2 * qk_w // v_w, (2 * qk_w + v_w) // v_w
    b0 = a_end
    qb_blk, kb_blk = b0 // qk_w, (b0 + qk_w) // qk_w
    vb_blk, gb_blk = (b0 + 2 * qk_w) // v_w, (b0 + 2 * qk_w + v_w) // v_w
    ha, hb = [], []
    for rev in (False, True):
        ha.append(_mlstm(qk, proj, small, gates_t, gate_bias, batch, dk, dv, va_blk, rev))
        hb.append(_gla(proj, small, lr_up[int(rev)], lr_bias[int(rev)], batch, dk, dv, (qb_blk, kb_blk, vb_blk), rev))
    y = _even_combine(ha[0], ha[1], hb[0], hb[1], proj, norm_a, norm_b, batch, dv, oa_blk, gb_blk)
    return _matmul(y, _cast_bf16(w_out), F32, res=h)


def _odd_layer(h, hn, w_in, conv_w, conv_b, w_r, b_r, w_i, b_i, lam, w_out, batch):
    d_rnn = w_out.shape[0]
    proj = _matmul(hn, _cast_bf16(w_in), BF16, tm_target=1376)
    hs = [_rglru(proj, conv_w, conv_b, w_r[i], b_r[i], w_i[i], b_i[i], lam[i], batch, d_rnn, bool(i)) for i in (0, 1)]
    y = _odd_combine(proj, hs[0], hs[1], batch)
    return _matmul(y, _cast_bf16(w_out), F32, res=h)


def kernel(x, meta_tokens, mix_norm, ffn_norm, final_norm, ev_w_in, ev_gate_bias, ev_qk_conv, ev_lr_up, ev_lr_bias, ev_norm_a, ev_norm_b, ev_w_out, od_w_in, od_conv, od_conv_bias, od_w_r, od_b_r, od_w_i, od_b_i, od_lambda, od_w_out, moe_wg, moe_bg, moe_we, moe_be, moe_w1, moe_w3, moe_w2):
    batch, seq, d = x.shape
    depth = mix_norm.shape[0]
    assert seq % CHUNK == 0 and d % 16 == 0
    for layer in range(depth):
        if layer == 0:
            h, hn = _frame_norm(x, meta_tokens, mix_norm[0])
        else:
            h, hn = _norm(h, mix_norm[layer], delta)
        if layer % 2 == 0:
            e = layer // 2
            h = _even_layer(h, hn, ev_w_in[e], ev_gate_bias[e], ev_qk_conv[e], ev_lr_up[e],
                            ev_lr_bias[e], ev_norm_a[e], ev_norm_b[e], ev_w_out[e], batch)
        else:
            o = layer // 2
            h = _odd_layer(h, hn, od_w_in[o], od_conv[o], od_conv_bias[o], od_w_r[o], od_b_r[o],
                           od_w_i[o], od_b_i[o], od_lambda[o], od_w_out[o], batch)
        delta = _moe_layer(h, ffn_norm[layer], moe_wg[layer], moe_bg[layer], moe_we[layer], moe_be[layer],
                              moe_w1[layer], moe_w3[layer], moe_w2[layer])
    out = _final_norm(h, delta, final_norm, batch, seq)
    return out.reshape(batch, seq, d)
```

```python
import functools

import jax
import jax.numpy as jnp
from jax import lax
from jax.experimental import pallas as pl
from jax.experimental.pallas import tpu as pltpu

F32 = jnp.float32
BF16 = jnp.bfloat16
HIGHEST = lax.Precision.HIGHEST

N_META = 16
CHUNK = 64
PAD = CHUNK - N_META
SUB = 16
EPS = 1e-6
NEG = -1e30
HEADS = 4
GATE_COLS = 4 * HEADS
B_RANK = 16
B_TAU = 16.0
GLA_MAX_CHUNK_DECAY = 80.0
RNN_BLOCKS = 16
RNN_C = 8.0
N_GROUPS = 4
EXPERTS_PER_GROUP = 8
N_EXPERTS = N_GROUPS * EXPERTS_PER_GROUP
LANES = 128
SUBLANES = 8
BF16_ROWS = 16
VMEM_LIMIT = 56 * 1024 * 1024


def _params(*sem):
    return pltpu.CompilerParams(dimension_semantics=sem, vmem_limit_bytes=VMEM_LIMIT)


def _tile(n, target, mult):
    best = None
    for t in range(mult, min(n, target) + 1, mult):
        if n % t == 0:
            best = t
    assert best is not None, (n, target, mult)
    return best


def _log_sigmoid(x):
    return jnp.minimum(x, 0.0) - jnp.log1p(jnp.exp(-jnp.abs(x)))


def _sigmoid(x):
    return 0.5 * jnp.tanh(0.5 * x) + 0.5


def _split3(x):
    hi = x.astype(BF16)
    r1 = x - hi.astype(F32)
    mid = r1.astype(BF16)
    lo = (r1 - mid.astype(F32)).astype(BF16)
    return hi, mid, lo


def _dot(a, b):
    return jnp.dot(a, b, preferred_element_type=F32)


def _dot_t(a, b):
    return lax.dot_general(a, b, (((1,), (1,)), ((), ())), preferred_element_type=F32)


def _tdot(a, b, precision=None):
    return lax.dot_general(a, b, (((0,), (0,)), ((), ())), preferred_element_type=F32,
                           precision=precision)


def _norm_kernel(h_ref, d_ref, g_ref, hnew_ref, hn_ref):
    x = h_ref[...] + d_ref[...].astype(F32)
    hnew_ref[...] = x
    y = x * lax.rsqrt(jnp.mean(x * x, axis=-1, keepdims=True) + EPS) * g_ref[...]
    hn_ref[...] = y.astype(hn_ref.dtype)


def _norm(h, g, delta):
    n, d = h.shape
    tr = _tile(n, 192, BF16_ROWS)
    row = pl.BlockSpec((tr, d), lambda i: (i, 0))
    return pl.pallas_call(
        _norm_kernel, grid=(n // tr,),
        in_specs=[row, row, pl.BlockSpec((1, d), lambda i: (0, 0))], out_specs=[row, row],
        out_shape=[jax.ShapeDtypeStruct((n, d), F32), jax.ShapeDtypeStruct((n, d), BF16)],
        compiler_params=_params("parallel"), name="norm")(h, delta, g.reshape(1, d).astype(F32))


def _frame_norm_kernel(x_ref, meta_ref, g_ref, h_ref, hn_ref):
    tr, d = h_ref.shape
    xb = x_ref[...]
    first = jnp.concatenate([jnp.zeros((PAD, d), F32), meta_ref[...], xb[:tr - CHUNK]], axis=0)
    x = jnp.where(pl.program_id(1) == 0, first, xb)
    h_ref[...] = x
    y = x * lax.rsqrt(jnp.mean(x * x, axis=-1, keepdims=True) + EPS) * g_ref[...]
    hn_ref[...] = y.astype(hn_ref.dtype)


def _frame_norm(x, meta, g):
    batch, seq, d = x.shape
    tp = PAD + N_META + seq
    nc = tp // CHUNK
    tr = CHUNK * _tile(nc, min(4, seq // CHUNK), 1)
    nt = tp // tr
    out = pl.BlockSpec((tr, d), lambda b, i: (b * nt + i, 0))
    src = pl.BlockSpec((pl.Element(tr), pl.Element(d)),
                       lambda b, i: (pl.multiple_of(b * seq + jnp.maximum(i * tr - CHUNK, 0), CHUNK), 0))
    return pl.pallas_call(
        _frame_norm_kernel, grid=(batch, nt),
        in_specs=[src, pl.BlockSpec((N_META, d), lambda b, i: (0, 0)), pl.BlockSpec((1, d), lambda b, i: (0, 0))],
        out_specs=[out, out],
        out_shape=[jax.ShapeDtypeStruct((batch * tp, d), F32), jax.ShapeDtypeStruct((batch * tp, d), BF16)],
        compiler_params=_params("parallel", "parallel"), name="frame_norm")(
            x.reshape(batch * seq, d), meta.astype(F32), g.reshape(1, d).astype(F32))


def _final_norm_kernel(h_ref, d_ref, g_ref, o_ref):
    x = h_ref[...] + d_ref[...].astype(F32)
    o_ref[...] = x * lax.rsqrt(jnp.mean(x * x, axis=-1, keepdims=True) + EPS) * g_ref[...]


def _final_norm(h, delta, g, batch, seq):
    n, d = h.shape
    tp = n // batch
    tr = _tile(seq, 512, BF16_ROWS)
    nt = seq // tr
    src = pl.BlockSpec((pl.Element(tr), pl.Element(d)),
                       lambda b, i: (pl.multiple_of(b * tp + CHUNK + i * tr, CHUNK), 0))
    return pl.pallas_call(
        _final_norm_kernel, grid=(batch, nt),
        in_specs=[src, src, pl.BlockSpec((1, d), lambda b, i: (0, 0))],
        out_specs=pl.BlockSpec((tr, d), lambda b, i: (b * nt + i, 0)),
        out_shape=jax.ShapeDtypeStruct((batch * seq, d), F32),
        compiler_params=_params("parallel", "parallel"), name="final_norm")(h, delta, g.reshape(1, d).astype(F32))


def _cast_kernel(w_ref, o_ref):
    o_ref[...] = w_ref[...].astype(o_ref.dtype)


def _cast_bf16(w, block_bytes=8 * 1024 * 1024):
    shape = w.shape
    w2 = w.reshape(-1, shape[-1])
    rows, cols = w2.shape
    tr = _tile(rows, max(BF16_ROWS, block_bytes // (4 * cols)), BF16_ROWS)
    spec = pl.BlockSpec((tr, cols), lambda i: (i, 0))
    out = pl.pallas_call(
        _cast_kernel, grid=(rows // tr,), in_specs=[spec], out_specs=spec,
        out_shape=jax.ShapeDtypeStruct((rows, cols), BF16),
        compiler_params=_params("parallel"), name="cast_bf16")(w2)
    return out.reshape(shape)


def _split_cast_kernel(w_ref, main_ref, small_ref, *, a_end, b_start, b_end):
    w = w_ref[...]
    main_ref[:, :a_end] = w[:, :a_end].astype(main_ref.dtype)
    main_ref[:, a_end:] = w[:, b_start:b_end].astype(main_ref.dtype)
    rows = w.shape[0]
    narrow = jnp.concatenate([w[:, a_end:b_start], w[:, b_end:]], axis=1)
    pad = jnp.zeros((rows, small_ref.shape[1] - narrow.shape[1]), F32)
    small_ref[...] = jnp.concatenate([narrow, pad], axis=1).astype(small_ref.dtype)


def _split_cast(w_in, a_end, b_start, b_end):
    d, cols = w_in.shape
    tr = _tile(d, 256, BF16_ROWS)
    wide = a_end + b_end - b_start
    return pl.pallas_call(
        functools.partial(_split_cast_kernel, a_end=a_end, b_start=b_start, b_end=b_end), grid=(d // tr,),
        in_specs=[pl.BlockSpec((tr, cols), lambda i: (i, 0))],
        out_specs=[pl.BlockSpec((tr, wide), lambda i: (i, 0)), pl.BlockSpec((tr, LANES), lambda i: (i, 0))],
        out_shape=[jax.ShapeDtypeStruct((d, wide), BF16), jax.ShapeDtypeStruct((d, LANES), BF16)],
        compiler_params=_params("parallel"), name="split_cast")(w_in)


def _mm_kernel(*refs, has_res):
    if has_res:
        a_ref, w_ref, r_ref, o_ref = refs
    else:
        a_ref, w_ref, o_ref = refs
    acc = _dot(a_ref[...], w_ref[...])
    if has_res:
        acc = acc + r_ref[...]
    o_ref[...] = acc.astype(o_ref.dtype)


def _matmul(a, w, out_dtype, res=None, tm_target=688, tn_target=512):
    n, k = a.shape
    m = w.shape[1]
    tm = _tile(n, tm_target, BF16_ROWS)
    tn = _tile(m, tn_target, LANES)
    in_specs = [pl.BlockSpec((tm, k), lambda i, j: (i, 0)), pl.BlockSpec((k, tn), lambda i, j: (0, j))]
    args = [a, w]
    if res is not None:
        in_specs.append(pl.BlockSpec((tm, tn), lambda i, j: (i, j)))
        args.append(res)
    return pl.pallas_call(
        functools.partial(_mm_kernel, has_res=res is not None),
        grid=(n // tm, m // tn), in_specs=in_specs,
        out_specs=pl.BlockSpec((tm, tn), lambda i, j: (i, j)),
        out_shape=jax.ShapeDtypeStruct((n, m), out_dtype),
        compiler_params=_params("parallel", "arbitrary"), name="matmul")(*args)


def _conv_taps(ext_s, cw_ref, tt):
    out = cw_ref[0:1, :] * ext_s[pl.ds(SUBLANES - 2, tt), :]
    for j in range(1, 4):
        out = out + cw_ref[j:j + 1, :] * ext_s[pl.ds(SUBLANES - 2 + j, tt), :]
    return out


def _fill_ext(ext_s, cur_ref, prev_ref, next_ref, row0, tt, has_next):
    rows = row0 + lax.broadcasted_iota(jnp.int32, (tt, 1), 0)
    ext_s[pl.ds(SUBLANES, tt), :] = jnp.where(rows >= PAD, cur_ref[...].astype(F32), 0.0)
    prow = row0 - SUBLANES + lax.broadcasted_iota(jnp.int32, (SUBLANES, 1), 0)
    ext_s[pl.ds(0, SUBLANES), :] = jnp.where(prow >= PAD, prev_ref[...].astype(F32)[SUBLANES:, :], 0.0)
    ext_s[pl.ds(SUBLANES + tt, SUBLANES), :] = jnp.where(has_next, next_ref[...].astype(F32)[:SUBLANES, :], 0.0)


def _qkconv_kernel(cur_ref, prev_ref, next_ref, cw_ref, o_ref, ext_s, *, tt, nt, kscale, half):
    t = pl.program_id(1)
    row0 = t * tt
    _fill_ext(ext_s, cur_ref, prev_ref, next_ref, row0, tt, t < nt - 1)
    y = _conv_taps(ext_s, cw_ref, tt)
    y = y * jax.nn.sigmoid(y)
    col = lax.broadcasted_iota(jnp.int32, (1, 2 * half), 1)
    y = y * jnp.where(col >= half, kscale, 1.0)
    rows = row0 + lax.broadcasted_iota(jnp.int32, (tt, 1), 0)
    o_ref[...] = jnp.where(rows >= PAD, y, 0.0).astype(o_ref.dtype)


def _halo_specs(width, col_block, tt, tp, n):
    per_b, per_t = tp // BF16_ROWS, tt // BF16_ROWS
    last = n // BF16_ROWS - 1

    def make(tmap):
        cur = pl.BlockSpec((tt, width), lambda b, t, *_: (b * (tp // tt) + tmap(t), col_block(*_)))
        prev = pl.BlockSpec((BF16_ROWS, width),
                            lambda b, t, *_: (jnp.maximum(b * per_b + tmap(t) * per_t - 1, 0), col_block(*_)))
        nxt = pl.BlockSpec((BF16_ROWS, width),
                           lambda b, t, *_: (jnp.minimum(b * per_b + (tmap(t) + 1) * per_t, last), col_block(*_)))
        return cur, prev, nxt
    return make


def _qk_conv(proj, conv_w, batch, dk):
    n = proj.shape[0]
    tp = n // batch
    width = 2 * HEADS * dk
    tt = _tile(tp, 688, BF16_ROWS)
    nt = tp // tt
    cur, prev, nxt = _halo_specs(width, lambda: 0, tt, tp, n)(lambda t: t)
    return pl.pallas_call(
        functools.partial(_qkconv_kernel, tt=tt, nt=nt, kscale=dk ** -0.5, half=HEADS * dk),
        grid=(batch, nt),
        in_specs=[cur, prev, nxt, pl.BlockSpec((4, width), lambda b, t: (0, 0))],
        out_specs=pl.BlockSpec((tt, width), lambda b, t: (b * nt + t, 0)),
        out_shape=jax.ShapeDtypeStruct((n, width), BF16),
        scratch_shapes=[pltpu.VMEM((tt + 2 * SUBLANES, width), F32)],
        compiler_params=_params("parallel", "parallel"), name="qk_conv")(proj, proj, proj, conv_w.astype(F32))


def _mlstm_kernel(q_ref, k_ref, v_ref, g_ref, gt_ref, gb_ref, gbt_ref, o_ref, c_s, m_s, *, reverse, dk, dv, nc, batch):
    step = pl.program_id(0)

    @pl.when(step == 0)
    def _():
        c_s[...] = jnp.zeros_like(c_s)
        m_s[...] = jnp.zeros_like(m_s)

    chunk = nc - 1 - step if reverse else step
    L = CHUNK
    real = chunk > 0
    valid_c = jnp.logical_or(real, lax.broadcasted_iota(jnp.int32, (L, 1), 0) >= PAD)
    valid_r = jnp.logical_or(real, lax.broadcasted_iota(jnp.int32, (1, L), 1) >= PAD)
    off = 2 * HEADS if reverse else 0
    ri = lax.broadcasted_iota(jnp.int32, (L, L), 0)
    ci = lax.broadcasted_iota(jnp.int32, (L, L), 1)
    mask = (ci >= ri) if reverse else (ci <= ri)
    inc = (ri >= ci) if reverse else (ri <= ci)
    last = 0 if reverse else L - 1
    ones_col = jnp.where(lax.broadcasted_iota(jnp.int32, (L, LANES), 1) == 0, 1.0, 0.0).astype(BF16)
    lane_pad = jnp.zeros((dk, LANES - L), BF16)

    lf_c, lf_r, li_r = [], [], []
    for b in range(batch):
        g = g_ref[b, :, :GATE_COLS] + gb_ref[...]
        gt = gt_ref[b, 0] + gbt_ref[...]
        lf_c.append(jnp.where(valid_c, _log_sigmoid(g), 0.0))
        lf_r.append(jnp.where(valid_r, _log_sigmoid(gt), 0.0))
        li_r.append(jnp.where(valid_r, gt[off:off + HEADS, :], NEG))
    terms_c = [t for x in lf_c for t in _split3(x)]
    sums_c = _dot(mask.astype(BF16), jnp.concatenate(terms_c, axis=1))
    terms_r = [t for x in lf_r for t in _split3(x)]
    sums_r = _dot(jnp.concatenate(terms_r, axis=0), inc.astype(BF16))
    w = GATE_COLS

    for b in range(batch):
        cum_c = sum(sums_c[:, (3 * b + j) * w:(3 * b + j + 1) * w] for j in range(3))
        cum_r = sum(sums_r[(3 * b + j) * w:(3 * b + j + 1) * w, :] for j in range(3))
        for h in range(HEADS):
            sh = b * HEADS + h
            col = off + HEADS + h
            cc = cum_c[:, col:col + 1]
            cr = cum_r[col:col + 1, :]
            lir = li_r[b][h:h + 1, :]
            tot = cc[last:last + 1, :]
            m = m_s[sh, 0:1, 0:1]
            qh = q_ref[b, :, h * dk:(h + 1) * dk]
            k_t = k_ref[b, :, h * dk:(h + 1) * dk].astype(F32).T
            vh = jnp.where(valid_c, v_ref[b, :, h * dv:(h + 1) * dv], 0.0).astype(BF16)
            vaug = jnp.concatenate([vh, ones_col], axis=1)

            rhs = jnp.concatenate([k_t.astype(BF16), lane_pad, c_s[sh].astype(BF16)], axis=1)
            qkc = _dot(qh, rhs)
            d_mat = jnp.where(mask, cc - cr + lir, NEG)
            inter = cc + m
            m_t = jnp.maximum(inter, jnp.max(d_mat, axis=1, keepdims=True))
            w_inter = jnp.exp(inter - m_t)
            s = qkc[:, :L] * jnp.exp(d_mat - m_t)

            gs = tot - cr + lir
            m_new = jnp.maximum(tot + m, jnp.max(gs, axis=1, keepdims=True))
            decay = jnp.exp(tot + m - m_new)
            ks_t = (k_t * jnp.exp(gs - m_new)).astype(BF16)
            sv = _dot(jnp.concatenate([s.astype(BF16), ks_t], axis=0), vaug)
            haug = w_inter * qkc[:, LANES:] + sv[:L]
            den = haug[:, dv:dv + 1]
            o_ref[b, :, h * dv:(h + 1) * dv] = (
                haug[:, :dv] / jnp.maximum(jnp.abs(den), jnp.exp(-m_t))).astype(o_ref.dtype)
            c_s[sh] = decay * c_s[sh] + sv[L:]
            m_s[sh] = jnp.broadcast_to(m_new, m_s.shape[1:])


def _chunk_spec(batch, width, col_block, nc, reverse):
    cidx = (lambda i: nc - 1 - i) if reverse else (lambda i: i)
    return pl.BlockSpec((batch, CHUNK, width), lambda i: (0, cidx(i), col_block))


def _mlstm(qk, proj, gates, gates_t, gate_bias, batch, dk, dv, v_block, reverse):
    n = qk.shape[0]
    tp = n // batch
    nc = tp // CHUNK
    cidx = (lambda i: nc - 1 - i) if reverse else (lambda i: i)
    gb = gate_bias.reshape(1, GATE_COLS).astype(F32)
    view = lambda a: a.reshape(batch, tp, a.shape[-1])
    out = pl.pallas_call(
        functools.partial(_mlstm_kernel, reverse=reverse, dk=dk, dv=dv, nc=nc, batch=batch),
        grid=(nc,),
        in_specs=[_chunk_spec(batch, HEADS * dk, 0, nc, reverse), _chunk_spec(batch, HEADS * dk, 1, nc, reverse),
                  _chunk_spec(batch, HEADS * dv, v_block, nc, reverse), _chunk_spec(batch, LANES, 0, nc, reverse),
                  pl.BlockSpec((batch, 1, GATE_COLS, CHUNK), lambda i: (0, cidx(i), 0, 0)),
                  pl.BlockSpec((1, GATE_COLS), lambda i: (0, 0)),
                  pl.BlockSpec((GATE_COLS, 1), lambda i: (0, 0))],
        out_specs=_chunk_spec(batch, HEADS * dv, 0, nc, reverse),
        out_shape=jax.ShapeDtypeStruct((batch, tp, HEADS * dv), BF16),
        scratch_shapes=[pltpu.VMEM((batch * HEADS, dk, dv + LANES), F32),
                        pltpu.VMEM((batch * HEADS, SUBLANES, LANES), F32)],
        compiler_params=_params("arbitrary"),
        name="mlstm_bwd" if reverse else "mlstm_fwd")(
            view(qk), view(qk), view(proj), view(gates), gates_t.reshape(batch, nc, GATE_COLS, CHUNK), gb,
            gb.reshape(GATE_COLS, 1))
    return out.reshape(n, HEADS * dv)


def _gla_head_exact(q, k, v, cumh, state, o_ref, b, h, *, reverse, dk, dv):
    L = CHUNK
    nsub = L // SUB
    sub_lane = lax.broadcasted_iota(jnp.int32, (SUB, L), 1)
    sub_row = lax.broadcasted_iota(jnp.int32, (SUB, 1), 0)
    o_inter = _dot_t((q * jnp.exp(cumh)).astype(BF16), state.astype(BF16))
    for blk in range(nsub):
        r0 = blk * SUB
        if reverse:
            cs = cumh[r0 + SUB:r0 + SUB + 1, :] if blk < nsub - 1 else jnp.zeros((1, dk), F32)
            earlier = sub_lane >= r0 + SUB
        else:
            cs = cumh[r0 - 1:r0, :] if blk > 0 else jnp.zeros((1, dk), F32)
            earlier = sub_lane < r0
        q_b = q[r0:r0 + SUB, :]
        cum_b = cumh[r0:r0 + SUB, :]
        qd = (q_b * jnp.exp(cum_b - cs)).astype(BF16)
        kd = (k * jnp.exp(jnp.minimum(cs - cumh, 0.0))).astype(BF16)
        att = jnp.where(earlier, _dot_t(qd, kd), 0.0)
        for j in range(SUB):
            s_idx = r0 + j
            tmask = (sub_row <= j) if reverse else (sub_row >= j)
            e = jnp.where(tmask, cum_b - cumh[s_idx:s_idx + 1, :], NEG)
            col = jnp.sum(q_b * k[s_idx:s_idx + 1, :] * jnp.exp(e), axis=1, keepdims=True)
            att = jnp.where(sub_lane == s_idx, col, att)
        o_b = o_inter[r0:r0 + SUB, :] + _dot(att.astype(BF16), v)
        o_ref[b, r0:r0 + SUB, h * dv:(h + 1) * dv] = o_b.astype(o_ref.dtype)


def _gla_head_factored(q, k, v, cumh, state, mask, o_ref, b, h, *, dv):
    qe = (q * jnp.exp(cumh)).astype(BF16)
    ke = (k * jnp.exp(-cumh)).astype(BF16)
    att = jnp.where(mask, _dot_t(qe, ke), 0.0)
    o = _dot_t(qe, state.astype(BF16)) + _dot(att.astype(BF16), v)
    o_ref[b, :, h * dv:(h + 1) * dv] = o.astype(o_ref.dtype)


def _gla_kernel(q_ref, k_ref, v_ref, lr_ref, up_ref, ub_ref, o_ref, s_s, *, reverse, dk, dv, nc, batch):
    step = pl.program_id(0)

    @pl.when(step == 0)
    def _():
        s_s[...] = jnp.zeros_like(s_s)

    chunk = nc - 1 - step if reverse else step
    L = CHUNK
    valid_c = jnp.logical_or(chunk > 0, lax.broadcasted_iota(jnp.int32, (L, 1), 0) >= PAD)
    off = GATE_COLS + (B_RANK if reverse else 0)
    ri = lax.broadcasted_iota(jnp.int32, (L, L), 0)
    ci = lax.broadcasted_iota(jnp.int32, (L, L), 1)
    mask = (ci >= ri) if reverse else (ci <= ri)
    last = 0 if reverse else L - 1
    cums = []
    for b in range(batch):
        z = jnp.dot(lr_ref[b, :, off:off + B_RANK], up_ref[...], precision=HIGHEST, preferred_element_type=F32)
        la = jnp.where(valid_c, _log_sigmoid(z + ub_ref[...]) / B_TAU, 0.0)
        cums.append(sum(_dot(mask.astype(BF16), t) for t in _split3(la)))
    lowest = jnp.min(jnp.concatenate([c[last:last + 1, :] for c in cums], axis=0))
    factorable = lowest >= -GLA_MAX_CHUNK_DECAY

    def run(factored):
        for b in range(batch):
            for h in range(HEADS):
                sl = slice(h * dk, (h + 1) * dk)
                q = jnp.where(valid_c, q_ref[b, :, sl], 0.0).astype(F32) * dk ** -0.5
                k = jnp.where(valid_c, k_ref[b, :, sl], 0.0).astype(F32)
                v = jnp.where(valid_c, v_ref[b, :, h * dv:(h + 1) * dv], 0.0).astype(BF16)
                cumh = cums[b][:, sl]
                tot = cumh[last:last + 1, :]
                state = s_s[b * HEADS + h]
                if factored:
                    _gla_head_factored(q, k, v, cumh, state, mask, o_ref, b, h, dv=dv)
                else:
                    _gla_head_exact(q, k, v, cumh, state, o_ref, b, h, reverse=reverse, dk=dk, dv=dv)
                kdec = (k * jnp.exp(tot - cumh)).astype(BF16)
                s_s[b * HEADS + h] = jnp.exp(tot) * state + _tdot(v, kdec)

    @pl.when(factorable)
    def _():
        run(True)

    @pl.when(jnp.logical_not(factorable))
    def _():
        run(False)


def _gla(proj, small, lr_up, lr_bias, batch, dk, dv, qkv_blocks, reverse):
    n = proj.shape[0]
    tp = n // batch
    nc = tp // CHUNK
    qb, kb, vb = qkv_blocks
    view = lambda a: a.reshape(batch, tp, a.shape[-1])
    out = pl.pallas_call(
        functools.partial(_gla_kernel, reverse=reverse, dk=dk, dv=dv, nc=nc, batch=batch),
        grid=(nc,),
        in_specs=[_chunk_spec(batch, HEADS * dk, qb, nc, reverse), _chunk_spec(batch, HEADS * dk, kb, nc, reverse),
                  _chunk_spec(batch, HEADS * dv, vb, nc, reverse), _chunk_spec(batch, LANES, 0, nc, reverse),
                  pl.BlockSpec((B_RANK, HEADS * dk), lambda i: (0, 0)),
                  pl.BlockSpec((1, HEADS * dk), lambda i: (0, 0))],
        out_specs=_chunk_spec(batch, HEADS * dv, 0, nc, reverse),
        out_shape=jax.ShapeDtypeStruct((batch, tp, HEADS * dv), BF16),
        scratch_shapes=[pltpu.VMEM((batch * HEADS, dv, dk), F32)],
        compiler_params=_params("arbitrary"),
        name="gla_bwd" if reverse else "gla_fwd")(
            view(proj), view(proj), view(proj), view(small), lr_up.astype(F32), lr_bias.reshape(1, -1).astype(F32))
    return out.reshape(n, HEADS * dv)


def _head_norm(x, g, dv):
    parts = []
    for h in range(HEADS):
        xh = x[:, h * dv:(h + 1) * dv]
        parts.append(xh * lax.rsqrt(jnp.mean(xh * xh, axis=-1, keepdims=True) + EPS))
    return jnp.concatenate(parts, axis=1) * g


def _even_combine_kernel(af_ref, ab_ref, bf_ref, bb_ref, oa_ref, gb_ref, na_ref, nb_ref, o_ref, *, tr, tp, dv):
    rows = (pl.program_id(0) * tr) % tp + lax.broadcasted_iota(jnp.int32, (tr, 1), 0)
    valid = rows >= PAD
    w = HEADS * dv
    ha = af_ref[...].astype(F32) + ab_ref[...].astype(F32)
    ya = jax.nn.sigmoid(oa_ref[...].astype(F32)) * _head_norm(ha, na_ref[...], dv)
    o_ref[:, :w] = jnp.where(valid, ya, 0.0).astype(o_ref.dtype)
    hb = bf_ref[...].astype(F32) + bb_ref[...].astype(F32)
    gb = gb_ref[...].astype(F32)
    yb = gb * jax.nn.sigmoid(gb) * _head_norm(hb, nb_ref[...], dv)
    o_ref[:, w:] = jnp.where(valid, yb, 0.0).astype(o_ref.dtype)


def _even_combine(ha_f, ha_b, hb_f, hb_b, proj, norm_a, norm_b, batch, dv, oa_block, gb_block):
    n, w = ha_f.shape
    tp = n // batch
    tr = _tile(tp, 384, BF16_ROWS)
    row = pl.BlockSpec((tr, w), lambda i: (i, 0))
    vec = pl.BlockSpec((1, w), lambda i: (0, 0))
    return pl.pallas_call(
        functools.partial(_even_combine_kernel, tr=tr, tp=tp, dv=dv),
        grid=(n // tr,),
        in_specs=[row, row, row, row, pl.BlockSpec((tr, w), lambda i: (i, oa_block)),
                  pl.BlockSpec((tr, w), lambda i: (i, gb_block)), vec, vec],
        out_specs=pl.BlockSpec((tr, 2 * w), lambda i: (i, 0)),
        out_shape=jax.ShapeDtypeStruct((n, 2 * w), BF16),
        compiler_params=_params("parallel"), name="even_combine")(
            ha_f, ha_b, hb_f, hb_b, proj, proj, norm_a.reshape(1, w).astype(F32), norm_b.reshape(1, w).astype(F32))


def _block_scan(a, b, reverse):
    sub = lax.broadcasted_iota(jnp.int32, a.shape, 1)
    for k in (1, 2, 4):
        if reverse:
            a_sh, b_sh, m = pltpu.roll(a, SUBLANES - k, 1), pltpu.roll(b, SUBLANES - k, 1), sub < SUBLANES - k
        else:
            a_sh, b_sh, m = pltpu.roll(a, k, 1), pltpu.roll(b, k, 1), sub >= k
        b = jnp.where(m, a * b_sh + b, b)
        a = jnp.where(m, a * a_sh, a)
    return a, b


def _rglru_kernel(cur_ref, prev_ref, next_ref, cw_ref, cb_ref, wr_ref, br_ref, wi_ref, bi_ref, lam_ref,
                  o_ref, ext_s, a_s, b_s, h_s, carry_s, *, reverse, tt, nt):
    step = pl.program_id(2)

    @pl.when(step == 0)
    def _():
        carry_s[...] = jnp.zeros_like(carry_s)

    t = nt - 1 - step if reverse else step
    row0 = t * tt
    _fill_ext(ext_s, cur_ref, prev_ref, next_ref, row0, tt, t < nt - 1)
    u = _conv_taps(ext_s, cw_ref, tt) + cb_ref[...]
    ub = u.astype(BF16)
    r = _sigmoid(_dot(ub, wr_ref[0]) + br_ref[...])
    gi = _sigmoid(_dot(ub, wi_ref[0]) + bi_ref[...])
    lam = lam_ref[...]
    softplus = jnp.maximum(-lam, 0.0) + jnp.log1p(jnp.exp(-jnp.abs(lam)))
    log_a = -RNN_C * r * softplus
    a = jnp.exp(log_a)
    rows = row0 + lax.broadcasted_iota(jnp.int32, (tt, 1), 0)
    inp = jnp.where(rows >= PAD, jnp.sqrt(1.0 - a * a) * (gi * u), 0.0)
    c = a.shape[1]
    ng = tt // SUBLANES
    a_g, b_g = _block_scan(a.reshape(ng, SUBLANES, c), inp.reshape(ng, SUBLANES, c), reverse)
    a_s[...] = a_g.reshape(tt, c)
    b_s[...] = b_g.reshape(tt, c)
    out_row = 0 if reverse else SUBLANES - 1

    def body(i, carry):
        g = ng - 1 - i if reverse else i
        r0 = pl.multiple_of(g * SUBLANES, SUBLANES)
        hh = b_s[pl.ds(r0, SUBLANES), :] + a_s[pl.ds(r0, SUBLANES), :] * carry
        h_s[pl.ds(r0, SUBLANES), :] = hh
        return hh[out_row:out_row + 1, :]

    carry_s[...] = lax.fori_loop(0, ng, body, carry_s[...])
    o_ref[...] = h_s[...].astype(o_ref.dtype)


def _pair_blocks(w):
    nb, r, _ = w.shape
    z = jnp.zeros((nb // 2, r, r), w.dtype)
    top = jnp.concatenate([w[0::2], z], axis=2)
    bot = jnp.concatenate([z, w[1::2]], axis=2)
    return jnp.concatenate([top, bot], axis=1).astype(BF16)


def _rglru(proj, conv_w, conv_b, w_r, b_r, w_i, b_i, lam, batch, d_rnn, reverse):
    n = proj.shape[0]
    tp = n // batch
    cw = 2 * d_rnn // RNN_BLOCKS
    ncb = d_rnn // cw
    tt = _tile(tp, 688, BF16_ROWS)
    nt = tp // tt
    tmap = (lambda t: nt - 1 - t) if reverse else (lambda t: t)
    cur, prev, nxt = _halo_specs(cw, lambda j: ncb + j, tt, tp, n)(tmap)
    def swap(spec):
        f = spec.index_map
        return pl.BlockSpec(spec.block_shape, lambda b, j, t: f(b, t, j))
    vec = pl.BlockSpec((1, cw), lambda b, j, t: (0, j))
    wspec = pl.BlockSpec((1, cw, cw), lambda b, j, t: (j, 0, 0))
    row = lambda x: x.reshape(1, d_rnn).astype(F32)
    return pl.pallas_call(
        functools.partial(_rglru_kernel, reverse=reverse, tt=tt, nt=nt),
        grid=(batch, ncb, nt),
        in_specs=[swap(cur), swap(prev), swap(nxt), pl.BlockSpec((4, cw), lambda b, j, t: (0, j)), vec,
                  wspec, vec, wspec, vec, vec],
        out_specs=pl.BlockSpec((tt, cw), lambda b, j, t: (b * nt + tmap(t), j)),
        out_shape=jax.ShapeDtypeStruct((n, d_rnn), BF16),
        scratch_shapes=[pltpu.VMEM((tt + 2 * SUBLANES, cw), F32), pltpu.VMEM((tt, cw), F32),
                        pltpu.VMEM((tt, cw), F32), pltpu.VMEM((tt, cw), F32), pltpu.VMEM((1, cw), F32)],
        compiler_params=_params("parallel", "parallel", "arbitrary"),
        name="rglru_bwd" if reverse else "rglru_fwd")(
            proj, proj, proj, conv_w.astype(F32), row(conv_b), _pair_blocks(w_r), row(b_r),
            _pair_blocks(w_i), row(b_i), row(lam))


def _odd_combine_kernel(g_ref, hf_ref, hb_ref, o_ref, *, tr, tp):
    rows = (pl.program_id(0) * tr) % tp + lax.broadcasted_iota(jnp.int32, (tr, 1), 0)
    y = jax.nn.gelu(g_ref[...].astype(F32)) * (hf_ref[...].astype(F32) + hb_ref[...].astype(F32))
    o_ref[...] = jnp.where(rows >= PAD, y, 0.0).astype(o_ref.dtype)


def _odd_combine(proj, hf, hb, batch):
    n, w = hf.shape
    tp = n // batch
    tr = _tile(tp, 384, BF16_ROWS)
    row = pl.BlockSpec((tr, w), lambda i: (i, 0))
    return pl.pallas_call(
        functools.partial(_odd_combine_kernel, tr=tr, tp=tp), grid=(n // tr,),
        in_specs=[row, row, row], out_specs=row, out_shape=jax.ShapeDtypeStruct((n, w), BF16),
        compiler_params=_params("parallel"), name="odd_combine")(proj, hf, hb)


def _route(x):
    lane = lax.broadcasted_iota(jnp.int32, x.shape, 1)
    big = jnp.int32(2 * LANES)
    gmask = lane < N_GROUPS
    gmax = jnp.max(jnp.where(gmask, x, -jnp.inf), axis=1, keepdims=True)
    ge = jnp.where(gmask, jnp.exp(x - gmax), 0.0)
    gp = ge / jnp.sum(ge, axis=1, keepdims=True)
    gval = jnp.max(gp, axis=1, keepdims=True)
    gidx = jnp.min(jnp.where(jnp.logical_and(gmask, gp == gval), lane, big), axis=1, keepdims=True)
    lo = N_GROUPS + gidx * EXPERTS_PER_GROUP
    emask = jnp.logical_and(lane >= lo, lane < lo + EXPERTS_PER_GROUP)
    emax = jnp.max(jnp.where(emask, x, -jnp.inf), axis=1, keepdims=True)
    ee = jnp.where(emask, jnp.exp(x - emax), 0.0)
    ep = ee / jnp.sum(ee, axis=1, keepdims=True)
    v1 = jnp.max(jnp.where(emask, ep, -1.0), axis=1, keepdims=True)
    i1 = jnp.min(jnp.where(jnp.logical_and(emask, ep == v1), lane, big), axis=1, keepdims=True)
    rest = jnp.logical_and(emask, lane != i1)
    v2 = jnp.max(jnp.where(rest, ep, -1.0), axis=1, keepdims=True)
    i2 = jnp.min(jnp.where(jnp.logical_and(rest, ep == v2), lane, big), axis=1, keepdims=True)
    tot = v1 + v2
    comb = jnp.where(lane == i1, v1 / tot * gval, jnp.where(lane == i2, v2 / tot * gval, 0.0))
    return comb, gidx


def _pack_pair(hi, lo):
    bits = lambda v: lax.bitcast_convert_type(v.astype(BF16).astype(F32), jnp.uint32)
    return bits(hi) | (bits(lo) >> 16)


def _unpack_pair(w):
    hi = lax.bitcast_convert_type(w & jnp.uint32(0xFFFF0000), F32)
    lo = lax.bitcast_convert_type(w << 16, F32)
    return hi.astype(BF16), lo.astype(BF16)


def _store_packed_rows(ref, x, npack):
    half = npack * LANES
    for j in range(npack):
        ref[:, j, :] = _pack_pair(x[:, j * LANES:(j + 1) * LANES], x[:, half + j * LANES:half + (j + 1) * LANES])


def _load_packed_rows(src, dense_s, dst_ref, npack):
    half = npack * LANES
    for j in range(npack):
        dense_s[:, j * LANES:(j + 1) * LANES] = src(j)
    for j in range(npack):
        hi, lo = _unpack_pair(dense_s[:, j * LANES:(j + 1) * LANES])
        dst_ref[:, j * LANES:(j + 1) * LANES] = hi.astype(dst_ref.dtype)
        dst_ref[:, half + j * LANES:half + (j + 1) * LANES] = lo.astype(dst_ref.dtype)


def _norm_route_kernel(h_ref, g_ref, w2_ref, wh_ref, b_ref, slab_ref, oh_ref, *, npack):
    x = h_ref[...]
    y = x * lax.rsqrt(jnp.mean(x * x, axis=-1, keepdims=True) + EPS) * g_ref[...]
    yh = y.astype(BF16)
    yl = (y - yh.astype(F32)).astype(BF16)
    r1 = _dot(yh, w2_ref[...])
    logits = r1[:, :LANES] + r1[:, LANES:] + _dot(yl, wh_ref[...]) + b_ref[...]
    comb, gidx = _route(logits)
    _store_packed_rows(slab_ref, y, npack)
    slab_ref[:, npack, :] = lax.bitcast_convert_type(comb, jnp.uint32)
    for j in range(npack + 1, slab_ref.shape[1]):
        slab_ref[:, j, :] = jnp.zeros(comb.shape, jnp.uint32)
    lane = lax.broadcasted_iota(jnp.int32, comb.shape, 1)
    oh_ref[...] = jnp.where(lane == gidx, 1.0, 0.0).astype(oh_ref.dtype)


def _norm_route(h, g, wg, bg, we, be):
    n, d = h.shape
    npack = d // (2 * LANES)
    srows = (npack + 1 + SUBLANES - 1) // SUBLANES * SUBLANES
    tr = _tile(n, 192, BF16_ROWS)
    zpad = LANES - N_GROUPS - N_EXPERTS
    wr = jnp.concatenate([wg, we, jnp.zeros((d, zpad), F32)], axis=1)
    wh = wr.astype(BF16)
    wl = (wr - wh.astype(F32)).astype(BF16)
    bias = jnp.concatenate([bg.astype(F32), be.astype(F32), jnp.zeros((zpad,), F32)]).reshape(1, LANES)
    return pl.pallas_call(
        functools.partial(_norm_route_kernel, npack=npack), grid=(n // tr,),
        in_specs=[pl.BlockSpec((tr, d), lambda i: (i, 0)), pl.BlockSpec((1, d), lambda i: (0, 0)),
                  pl.BlockSpec((d, 2 * LANES), lambda i: (0, 0)), pl.BlockSpec((d, LANES), lambda i: (0, 0)),
                  pl.BlockSpec((1, LANES), lambda i: (0, 0))],
        out_specs=[pl.BlockSpec((tr, srows, LANES), lambda i: (i, 0, 0)), pl.BlockSpec((tr, LANES), lambda i: (i, 0))],
        out_shape=[jax.ShapeDtypeStruct((n, srows, LANES), jnp.uint32), jax.ShapeDtypeStruct((n, LANES), BF16)],
        compiler_params=_params("parallel"), name="norm_route")(
            h, g.reshape(1, d).astype(F32), jnp.concatenate([wh, wl], axis=1), wh, bias)


def _rank_kernel(oh_ref, g_ref, rank_ref, cnt_ref, carry_s):
    @pl.when(pl.program_id(0) == 0)
    def _():
        carry_s[...] = jnp.zeros_like(carry_s)

    tr = oh_ref.shape[0]
    sel = jnp.where(lax.broadcasted_iota(jnp.int32, (SUBLANES, LANES), 0) ==
                    lax.broadcasted_iota(jnp.int32, (SUBLANES, LANES), 1), 1.0, 0.0).astype(BF16)
    oh_t = _dot_t(sel, oh_ref[...])
    before = (lax.broadcasted_iota(jnp.int32, (tr, tr), 0) < lax.broadcasted_iota(jnp.int32, (tr, tr), 1))
    cum = _dot(oh_t.astype(BF16), jnp.where(before, 1.0, 0.0).astype(BF16)) + carry_s[:, 0:1]
    gid = lax.broadcasted_iota(jnp.int32, (SUBLANES, tr), 0).astype(F32)
    rank_ref[0] = jnp.sum(oh_t * cum, axis=0, keepdims=True).astype(jnp.int32)
    g_ref[0] = jnp.sum(oh_t * gid, axis=0, keepdims=True).astype(jnp.int32)
    carry_s[...] = carry_s[...] + jnp.sum(oh_t, axis=1, keepdims=True)
    cnt_ref[...] = carry_s[...]


def _rank(onehot):
    n = onehot.shape[0]
    tr = _tile(n, 384, LANES)
    row = pl.BlockSpec((1, 1, tr), lambda i: (i, 0, 0))
    g, rank, cnt = pl.pallas_call(
        _rank_kernel, grid=(n // tr,), in_specs=[pl.BlockSpec((tr, LANES), lambda i: (i, 0))],
        out_specs=[row, row, pl.BlockSpec((SUBLANES, LANES), lambda i: (0, 0))],
        out_shape=[jax.ShapeDtypeStruct((n // tr, 1, tr), jnp.int32), jax.ShapeDtypeStruct((n // tr, 1, tr), jnp.int32),
                   jax.ShapeDtypeStruct((SUBLANES, LANES), F32)],
        scratch_shapes=[pltpu.VMEM((SUBLANES, LANES), F32)],
        compiler_params=_params("arbitrary"), name="rank")(onehot)
    return g.reshape(n), rank.reshape(n), cnt[:N_GROUPS, 0].astype(jnp.int32)


def _invert_kernel(g_ref, rank_ref, cnt_ref, pos_ref, idx_ref, tg_ref, *, n, tm, ntiles):
    bases = [jnp.int32(0)]
    for g in range(N_GROUPS - 1):
        bases.append(bases[-1] + (cnt_ref[g] + tm - 1) // tm * tm)

    def zero(i, c):
        idx_ref[i] = 0
        return c
    lax.fori_loop(0, ntiles * tm, zero, 0, unroll=8)

    def place(t, c):
        g = g_ref[t]
        base = bases[0]
        for k in range(1, N_GROUPS):
            base = jnp.where(g == k, bases[k], base)
        p = base + rank_ref[t]
        pos_ref[t] = p
        idx_ref[p] = t
        return c
    lax.fori_loop(0, n, place, 0, unroll=8)

    def tile_group(i, c):
        r = i * tm
        tg = jnp.int32(0)
        for k in range(1, N_GROUPS):
            tg = tg + (r >= bases[k]).astype(jnp.int32)
        tg_ref[i] = tg
        return c
    lax.fori_loop(0, ntiles, tile_group, 0)


def _invert(g, rank, cnt, tm, ntiles):
    n = g.shape[0]
    smem = pl.BlockSpec(memory_space=pltpu.SMEM)
    return pl.pallas_call(
        functools.partial(_invert_kernel, n=n, tm=tm, ntiles=ntiles),
        in_specs=[smem, smem, smem], out_specs=[smem, smem, smem],
        out_shape=[jax.ShapeDtypeStruct((n,), jnp.int32), jax.ShapeDtypeStruct((ntiles * tm,), jnp.int32),
                   jax.ShapeDtypeStruct((ntiles,), jnp.int32)],
        name="invert")(g, rank, cnt)


def _row_copy(src_hbm, buf, sem, src_row, slot, dst_row):
    return pltpu.make_async_copy(src_hbm.at[pl.ds(src_row, 1)], buf.at[slot, pl.ds(dst_row, 1)], sem.at[slot])


def _gather_rows(index_ref, src_hbm, buf, sem, rows):
    i = pl.program_id(0)
    steps = pl.num_programs(0)

    def issue(step, slot):
        def body(r, c):
            _row_copy(src_hbm, buf, sem, index_ref[step * rows + r], slot, r).start()
            return c
        lax.fori_loop(0, rows, body, 0, unroll=8)

    @pl.when(i == 0)
    def _():
        issue(0, 0)

    @pl.when(i + 1 < steps)
    def _():
        issue(i + 1, (i + 1) % 2)

    slot = i % 2

    def wait(r, c):
        _row_copy(src_hbm, buf, sem, 0, slot, r).wait()
        return c
    lax.fori_loop(0, rows, wait, 0, unroll=8)
    return slot


def _dispatch_kernel(idx_ref, slab_hbm, xs_ref, cs_ref, buf, dense_s, sem, *, npack):
    slot = _gather_rows(idx_ref, slab_hbm, buf, sem, xs_ref.shape[0])
    _load_packed_rows(lambda j: buf[slot, :, j, :], dense_s, xs_ref, npack)
    cs_ref[...] = lax.bitcast_convert_type(buf[slot, :, npack, :], F32)


def _dispatch(idx, slab, tm, ntiles, d):
    srows = slab.shape[1]
    return pl.pallas_call(
        functools.partial(_dispatch_kernel, npack=d // (2 * LANES)),
        grid_spec=pltpu.PrefetchScalarGridSpec(
            num_scalar_prefetch=1, grid=(ntiles,),
            in_specs=[pl.BlockSpec(memory_space=pl.ANY)],
            out_specs=[pl.BlockSpec((tm, d), lambda i, idx: (i, 0)), pl.BlockSpec((tm, LANES), lambda i, idx: (i, 0))],
            scratch_shapes=[pltpu.VMEM((2, tm, srows, LANES), jnp.uint32), pltpu.VMEM((tm, d // 2), jnp.uint32),
                            pltpu.SemaphoreType.DMA((2,))]),
        out_shape=[jax.ShapeDtypeStruct((ntiles * tm, d), BF16), jax.ShapeDtypeStruct((ntiles * tm, LANES), F32)],
        compiler_params=_params("arbitrary"), name="dispatch")(idx, slab)


def _expert_kernel(tg_ref, x_ref, c_ref, w1_ref, w3_ref, w2_ref, o_ref, acc_s, *, npack):
    i = pl.program_id(0)
    e = pl.program_id(1)

    @pl.when(e == 0)
    def _():
        acc_s[...] = jnp.zeros_like(acc_s)

    a = x_ref[...]
    h1 = _dot(a, w1_ref[0])
    h3 = _dot(a, w3_ref[0])
    comb = c_ref[...]
    lane = lax.broadcasted_iota(jnp.int32, comb.shape, 1)
    c = jnp.sum(jnp.where(lane == N_GROUPS + tg_ref[i] * EXPERTS_PER_GROUP + e, comb, 0.0), axis=1, keepdims=True)
    hid = (h1 * jax.nn.sigmoid(h1) * h3 * c).astype(BF16)
    acc_s[...] += _dot(hid, w2_ref[0])

    @pl.when(e == EXPERTS_PER_GROUP - 1)
    def _():
        _store_packed_rows(o_ref, acc_s, npack)


def _experts(tg, xs, cs, w1, w3, w2, tm, layer):
    rows, d = xs.shape
    f = w1.shape[-1]
    npack = d // (2 * LANES)
    wmap = lambda i, e, tg: (layer * N_EXPERTS + tg[i] * EXPERTS_PER_GROUP + e, 0, 0)
    return pl.pallas_call(
        functools.partial(_expert_kernel, npack=npack),
        grid_spec=pltpu.PrefetchScalarGridSpec(
            num_scalar_prefetch=1, grid=(rows // tm, EXPERTS_PER_GROUP),
            in_specs=[pl.BlockSpec((tm, d), lambda i, e, tg: (i, 0)), pl.BlockSpec((tm, LANES), lambda i, e, tg: (i, 0)),
                      pl.BlockSpec((1, d, f), wmap), pl.BlockSpec((1, d, f), wmap), pl.BlockSpec((1, f, d), wmap)],
            out_specs=pl.BlockSpec((tm, npack, LANES), lambda i, e, tg: (i, 0, 0)),
            scratch_shapes=[pltpu.VMEM((tm, d), F32)]),
        out_shape=jax.ShapeDtypeStruct((rows, npack, LANES), jnp.uint32),
        compiler_params=_params("parallel", "arbitrary"), name="experts")(tg, xs, cs, w1, w3, w2)


def _collect_kernel(pos_ref, ys_hbm, o_ref, buf, dense_s, sem, *, npack):
    slot = _gather_rows(pos_ref, ys_hbm, buf, sem, o_ref.shape[0])
    _load_packed_rows(lambda j: buf[slot, :, j, :], dense_s, o_ref, npack)


def _collect(pos, ys):
    n = pos.shape[0]
    _, npack, _ = ys.shape
    d = 2 * npack * LANES
    tr = _tile(n, 384, BF16_ROWS)
    return pl.pallas_call(
        functools.partial(_collect_kernel, npack=npack),
        grid_spec=pltpu.PrefetchScalarGridSpec(
            num_scalar_prefetch=1, grid=(n // tr,),
            in_specs=[pl.BlockSpec(memory_space=pl.ANY)],
            out_specs=pl.BlockSpec((tr, d), lambda i, pos: (i, 0)),
            scratch_shapes=[pltpu.VMEM((2, tr, npack, LANES), jnp.uint32), pltpu.VMEM((tr, d // 2), jnp.uint32),
                            pltpu.SemaphoreType.DMA((2,))]),
        out_shape=jax.ShapeDtypeStruct((n, d), BF16),
        compiler_params=_params("arbitrary"), name="collect")(pos, ys)


def _moe_layer(h, ffn_g, wg, bg, we, be, w1, w3, w2, layer):
    n, d = h.shape
    tm = 512 if n >= 4096 else 128
    ntiles = (n + N_GROUPS * (tm - 1)) // tm
    slab, onehot = _norm_route(h, ffn_g, wg, bg, we, be)
    g, rank, cnt = _rank(onehot)
    pos, idx, tg = _invert(g, rank, cnt, tm, ntiles)
    xs, cs = _dispatch(idx, slab, tm, ntiles, d)
    ys = _experts(tg, xs, cs, w1, w3, w2, tm, layer)
    return _collect(pos, ys)


def _even_layer(h, hn, w_in, gate_bias, qk_conv, lr_up, lr_bias, norm_a, norm_b, w_out, batch):
    n, d = h.shape
    dk, dv = d // 16, d // 8
    qk_w, v_w = HEADS * dk, HEADS * dv
    a_end = 2 * qk_w + 2 * v_w
    b_start = a_end + GATE_COLS
    b_end = b_start + 2 * qk_w + 2 * v_w
    w_main, w_small = _split_cast(w_in, a_end, b_start, b_end)
    proj = _matmul(hn, w_main, BF16, tm_target=1376)
    small = _matmul(hn, w_small, F32, tn_target=LANES)
    gates_t = small[:, :GATE_COLS].reshape(n // CHUNK, CHUNK, GATE_COLS).transpose(0, 2, 1)
    qk = _qk_conv(proj, qk_conv, batch, dk)
    va_blk, oa_blk = 2 * qk_w // v_w, (2 * qk_w + v_w) // v_w
    b0 = a_end
    qb_blk, kb_blk = b0 // qk_w, (b0 + qk_w) // qk_w
    vb_blk, gb_blk = (b0 + 2 * qk_w) // v_w, (b0 + 2 * qk_w + v_w) // v_w
    ha, hb = [], []
    for rev in (False, True):
        ha.append(_mlstm(qk, proj, small, gates_t, gate_bias, batch, dk, dv, va_blk, rev))
        hb.append(_gla(proj, small, lr_up[int(rev)], lr_bias[int(rev)], batch, dk, dv, (qb_blk, kb_blk, vb_blk), rev))
    y = _even_combine(ha[0], ha[1], hb[0], hb[1], proj, norm_a, norm_b, batch, dv, oa_blk, gb_blk)
    return _matmul(y, _cast_bf16(w_out), F32, res=h)


def _odd_layer(h, hn, w_in, conv_w, conv_b, w_r, b_r, w_i, b_i, lam, w_out, batch):
    d_rnn = w_out.shape[0]
    proj = _matmul(hn, _cast_bf16(w_in), BF16, tm_target=1376)
    hs = [_rglru(proj, conv_w, conv_b, w_r[i], b_r[i], w_i[i], b_i[i], lam[i], batch, d_rnn, bool(i)) for i in (0, 1)]
    y = _odd_combine(proj, hs[0], hs[1], batch)
    return _matmul(y, _cast_bf16(w_out), F32, res=h)


def kernel(x, meta_tokens, mix_norm, ffn_norm, final_norm, ev_w_in, ev_gate_bias, ev_qk_conv, ev_lr_up, ev_lr_bias, ev_norm_a, ev_norm_b, ev_w_out, od_w_in, od_conv, od_conv_bias, od_w_r, od_b_r, od_w_i, od_b_i, od_lambda, od_w_out, moe_wg, moe_bg, moe_we, moe_be, moe_w1, moe_w3, moe_w2):
    batch, seq, d = x.shape
    depth = mix_norm.shape[0]
    assert seq % CHUNK == 0 and d % 16 == 0
    f = moe_w1.shape[-1]
    w1 = _cast_bf16(moe_w1).reshape(depth * N_EXPERTS, d, f)
    w3 = _cast_bf16(moe_w3).reshape(depth * N_EXPERTS, d, f)
    w2 = _cast_bf16(moe_w2).reshape(depth * N_EXPERTS, f, d)
    for layer in range(depth):
        if layer == 0:
            h, hn = _frame_norm(x, meta_tokens, mix_norm[0])
        else:
            h, hn = _norm(h, mix_norm[layer], delta)
        if layer % 2 == 0:
            e = layer // 2
            h = _even_layer(h, hn, ev_w_in[e], ev_gate_bias[e], ev_qk_conv[e], ev_lr_up[e],
                            ev_lr_bias[e], ev_norm_a[e], ev_norm_b[e], ev_w_out[e], batch)
        else:
            o = layer // 2
            h = _odd_layer(h, hn, od_w_in[o], od_conv[o], od_conv_bias[o], od_w_r[o], od_b_r[o],
                           od_w_i[o], od_b_i[o], od_lambda[o], od_w_out[o], batch)
        delta = _moe_layer(h, ffn_norm[layer], moe_wg[layer], moe_bg[layer], moe_we[layer], moe_be[layer],
                           w1, w3, w2, layer)
    out = _final_norm(h, delta, final_norm, batch, seq)
    return out.reshape(batch, seq, d)
```

```python
import functools

import jax
import jax.numpy as jnp
from jax import lax
from jax.experimental import pallas as pl
from jax.experimental.pallas import tpu as pltpu

F32 = jnp.float32
BF16 = jnp.bfloat16
HIGHEST = lax.Precision.HIGHEST

N_META = 16
CHUNK = 128
PAD = CHUNK - N_META
SUB = 16
EPS = 1e-6
NEG = -1e30
HEADS = 4
GATE_COLS = 4 * HEADS
B_RANK = 16
B_TAU = 16.0
GLA_MAX_CHUNK_DECAY = 80.0
RNN_BLOCKS = 16
RNN_C = 8.0
N_GROUPS = 4
EXPERTS_PER_GROUP = 8
N_EXPERTS = N_GROUPS * EXPERTS_PER_GROUP
LANES = 128
SUBLANES = 8
BF16_ROWS = 16
VMEM_LIMIT = 56 * 1024 * 1024


def _params(*sem):
    return pltpu.CompilerParams(dimension_semantics=sem, vmem_limit_bytes=VMEM_LIMIT)


def _tile(n, target, mult):
    best = None
    for t in range(mult, min(n, target) + 1, mult):
        if n % t == 0:
            best = t
    assert best is not None, (n, target, mult)
    return best


def _log_sigmoid(x):
    return jnp.minimum(x, 0.0) - jnp.log1p(jnp.exp(-jnp.abs(x)))


def _sigmoid(x):
    return 0.5 * jnp.tanh(0.5 * x) + 0.5


def _split3(x):
    hi = x.astype(BF16)
    r1 = x - hi.astype(F32)
    mid = r1.astype(BF16)
    lo = (r1 - mid.astype(F32)).astype(BF16)
    return hi, mid, lo


def _dot(a, b):
    return jnp.dot(a, b, preferred_element_type=F32)


def _dot_t(a, b):
    return lax.dot_general(a, b, (((1,), (1,)), ((), ())), preferred_element_type=F32)


def _tdot(a, b, precision=None):
    return lax.dot_general(a, b, (((0,), (0,)), ((), ())), preferred_element_type=F32,
                           precision=precision)


def _norm_kernel(h_ref, d_ref, g_ref, hnew_ref, hn_ref):
    x = h_ref[...] + d_ref[...].astype(F32)
    hnew_ref[...] = x
    y = x * lax.rsqrt(jnp.mean(x * x, axis=-1, keepdims=True) + EPS) * g_ref[...]
    hn_ref[...] = y.astype(hn_ref.dtype)


def _norm(h, g, delta):
    n, d = h.shape
    tr = _tile(n, 192, BF16_ROWS)
    row = pl.BlockSpec((tr, d), lambda i: (i, 0))
    return pl.pallas_call(
        _norm_kernel, grid=(n // tr,),
        in_specs=[row, row, pl.BlockSpec((1, d), lambda i: (0, 0))], out_specs=[row, row],
        out_shape=[jax.ShapeDtypeStruct((n, d), F32), jax.ShapeDtypeStruct((n, d), BF16)],
        compiler_params=_params("parallel"), name="norm")(h, delta, g.reshape(1, d).astype(F32))


def _frame_norm_kernel(x_ref, meta_ref, g_ref, h_ref, hn_ref):
    tr, d = h_ref.shape
    xb = x_ref[...]
    first = jnp.concatenate([jnp.zeros((PAD, d), F32), meta_ref[...]] + ([xb[:tr - CHUNK]] if tr > CHUNK else []), axis=0)
    x = jnp.where(pl.program_id(1) == 0, first, xb)
    h_ref[...] = x
    y = x * lax.rsqrt(jnp.mean(x * x, axis=-1, keepdims=True) + EPS) * g_ref[...]
    hn_ref[...] = y.astype(hn_ref.dtype)


def _frame_norm(x, meta, g):
    batch, seq, d = x.shape
    tp = PAD + N_META + seq
    nc = tp // CHUNK
    tr = CHUNK * _tile(nc, min(3, seq // CHUNK), 1)
    nt = tp // tr
    out = pl.BlockSpec((tr, d), lambda b, i: (b * nt + i, 0))
    src = pl.BlockSpec((pl.Element(tr), pl.Element(d)),
                       lambda b, i: (pl.multiple_of(b * seq + jnp.maximum(i * tr - CHUNK, 0), CHUNK), 0))
    return pl.pallas_call(
        _frame_norm_kernel, grid=(batch, nt),
        in_specs=[src, pl.BlockSpec((N_META, d), lambda b, i: (0, 0)), pl.BlockSpec((1, d), lambda b, i: (0, 0))],
        out_specs=[out, out],
        out_shape=[jax.ShapeDtypeStruct((batch * tp, d), F32), jax.ShapeDtypeStruct((batch * tp, d), BF16)],
        compiler_params=_params("parallel", "parallel"), name="frame_norm")(
            x.reshape(batch * seq, d), meta.astype(F32), g.reshape(1, d).astype(F32))


def _final_norm_kernel(h_ref, d_ref, g_ref, o_ref):
    x = h_ref[...] + d_ref[...].astype(F32)
    o_ref[...] = x * lax.rsqrt(jnp.mean(x * x, axis=-1, keepdims=True) + EPS) * g_ref[...]


def _final_norm(h, delta, g, batch, seq):
    n, d = h.shape
    tp = n // batch
    tr = _tile(seq, 512, BF16_ROWS)
    nt = seq // tr
    src = pl.BlockSpec((pl.Element(tr), pl.Element(d)),
                       lambda b, i: (pl.multiple_of(b * tp + CHUNK + i * tr, CHUNK), 0))
    return pl.pallas_call(
        _final_norm_kernel, grid=(batch, nt),
        in_specs=[src, src, pl.BlockSpec((1, d), lambda b, i: (0, 0))],
        out_specs=pl.BlockSpec((tr, d), lambda b, i: (b * nt + i, 0)),
        out_shape=jax.ShapeDtypeStruct((batch * seq, d), F32),
        compiler_params=_params("parallel", "parallel"), name="final_norm")(h, delta, g.reshape(1, d).astype(F32))


def _cast_kernel(w_ref, o_ref):
    o_ref[...] = w_ref[...].astype(o_ref.dtype)


def _cast_bf16(w, block_bytes=8 * 1024 * 1024):
    shape = w.shape
    w2 = w.reshape(-1, shape[-1])
    rows, cols = w2.shape
    tr = _tile(rows, max(BF16_ROWS, block_bytes // (4 * cols)), BF16_ROWS)
    spec = pl.BlockSpec((tr, cols), lambda i: (i, 0))
    out = pl.pallas_call(
        _cast_kernel, grid=(rows // tr,), in_specs=[spec], out_specs=spec,
        out_shape=jax.ShapeDtypeStruct((rows, cols), BF16),
        compiler_params=_params("parallel"), name="cast_bf16")(w2)
    return out.reshape(shape)


def _split_cast_kernel(w_ref, main_ref, small_ref, *, a_end, b_start, b_end):
    w = w_ref[...]
    main_ref[:, :a_end] = w[:, :a_end].astype(main_ref.dtype)
    main_ref[:, a_end:] = w[:, b_start:b_end].astype(main_ref.dtype)
    rows = w.shape[0]
    narrow = jnp.concatenate([w[:, a_end:b_start], w[:, b_end:]], axis=1)
    pad = jnp.zeros((rows, small_ref.shape[1] - narrow.shape[1]), F32)
    small_ref[...] = jnp.concatenate([narrow, pad], axis=1).astype(small_ref.dtype)


def _split_cast(w_in, a_end, b_start, b_end):
    d, cols = w_in.shape
    tr = _tile(d, 256, BF16_ROWS)
    wide = a_end + b_end - b_start
    return pl.pallas_call(
        functools.partial(_split_cast_kernel, a_end=a_end, b_start=b_start, b_end=b_end), grid=(d // tr,),
        in_specs=[pl.BlockSpec((tr, cols), lambda i: (i, 0))],
        out_specs=[pl.BlockSpec((tr, wide), lambda i: (i, 0)), pl.BlockSpec((tr, LANES), lambda i: (i, 0))],
        out_shape=[jax.ShapeDtypeStruct((d, wide), BF16), jax.ShapeDtypeStruct((d, LANES), BF16)],
        compiler_params=_params("parallel"), name="split_cast")(w_in)


def _mm_kernel(*refs, has_res):
    if has_res:
        a_ref, w_ref, r_ref, o_ref = refs
    else:
        a_ref, w_ref, o_ref = refs
    acc = _dot(a_ref[...], w_ref[...])
    if has_res:
        acc = acc + r_ref[...]
    o_ref[...] = acc.astype(o_ref.dtype)


def _matmul(a, w, out_dtype, res=None, tm_target=688, tn_target=512):
    n, k = a.shape
    m = w.shape[1]
    tm = _tile(n, tm_target, BF16_ROWS)
    tn = _tile(m, tn_target, LANES)
    in_specs = [pl.BlockSpec((tm, k), lambda i, j: (i, 0)), pl.BlockSpec((k, tn), lambda i, j: (0, j))]
    args = [a, w]
    if res is not None:
        in_specs.append(pl.BlockSpec((tm, tn), lambda i, j: (i, j)))
        args.append(res)
    return pl.pallas_call(
        functools.partial(_mm_kernel, has_res=res is not None),
        grid=(n // tm, m // tn), in_specs=in_specs,
        out_specs=pl.BlockSpec((tm, tn), lambda i, j: (i, j)),
        out_shape=jax.ShapeDtypeStruct((n, m), out_dtype),
        compiler_params=_params("parallel", "arbitrary"), name="matmul")(*args)


def _conv_taps(ext_s, cw_ref, tt):
    out = cw_ref[0:1, :] * ext_s[pl.ds(SUBLANES - 2, tt), :]
    for j in range(1, 4):
        out = out + cw_ref[j:j + 1, :] * ext_s[pl.ds(SUBLANES - 2 + j, tt), :]
    return out


def _fill_ext(ext_s, cur_ref, prev_ref, next_ref, row0, tt, has_next):
    rows = row0 + lax.broadcasted_iota(jnp.int32, (tt, 1), 0)
    ext_s[pl.ds(SUBLANES, tt), :] = jnp.where(rows >= PAD, cur_ref[...].astype(F32), 0.0)
    prow = row0 - SUBLANES + lax.broadcasted_iota(jnp.int32, (SUBLANES, 1), 0)
    ext_s[pl.ds(0, SUBLANES), :] = jnp.where(prow >= PAD, prev_ref[...].astype(F32)[SUBLANES:, :], 0.0)
    ext_s[pl.ds(SUBLANES + tt, SUBLANES), :] = jnp.where(has_next, next_ref[...].astype(F32)[:SUBLANES, :], 0.0)


def _qkconv_kernel(cur_ref, prev_ref, next_ref, cw_ref, o_ref, ext_s, *, tt, nt, kscale, half):
    t = pl.program_id(1)
    row0 = t * tt
    _fill_ext(ext_s, cur_ref, prev_ref, next_ref, row0, tt, t < nt - 1)
    y = _conv_taps(ext_s, cw_ref, tt)
    y = y * jax.nn.sigmoid(y)
    col = lax.broadcasted_iota(jnp.int32, (1, 2 * half), 1)
    y = y * jnp.where(col >= half, kscale, 1.0)
    rows = row0 + lax.broadcasted_iota(jnp.int32, (tt, 1), 0)
    o_ref[...] = jnp.where(rows >= PAD, y, 0.0).astype(o_ref.dtype)


def _halo_specs(width, col_block, tt, tp, n):
    per_b, per_t = tp // BF16_ROWS, tt // BF16_ROWS
    last = n // BF16_ROWS - 1

    def make(tmap):
        cur = pl.BlockSpec((tt, width), lambda b, t, *_: (b * (tp // tt) + tmap(t), col_block(*_)))
        prev = pl.BlockSpec((BF16_ROWS, width),
                            lambda b, t, *_: (jnp.maximum(b * per_b + tmap(t) * per_t - 1, 0), col_block(*_)))
        nxt = pl.BlockSpec((BF16_ROWS, width),
                           lambda b, t, *_: (jnp.minimum(b * per_b + (tmap(t) + 1) * per_t, last), col_block(*_)))
        return cur, prev, nxt
    return make


def _qk_conv(proj, conv_w, batch, dk):
    n = proj.shape[0]
    tp = n // batch
    width = 2 * HEADS * dk
    tt = _tile(tp, 688, BF16_ROWS)
    nt = tp // tt
    cur, prev, nxt = _halo_specs(width, lambda: 0, tt, tp, n)(lambda t: t)
    return pl.pallas_call(
        functools.partial(_qkconv_kernel, tt=tt, nt=nt, kscale=dk ** -0.5, half=HEADS * dk),
        grid=(batch, nt),
        in_specs=[cur, prev, nxt, pl.BlockSpec((4, width), lambda b, t: (0, 0))],
        out_specs=pl.BlockSpec((tt, width), lambda b, t: (b * nt + t, 0)),
        out_shape=jax.ShapeDtypeStruct((n, width), BF16),
        scratch_shapes=[pltpu.VMEM((tt + 2 * SUBLANES, width), F32)],
        compiler_params=_params("parallel", "parallel"), name="qk_conv")(proj, proj, proj, conv_w.astype(F32))


def _mlstm_kernel(q_ref, k_ref, v_ref, g_ref, gt_ref, gb_ref, gbt_ref, o_ref, c_s, m_s, *, reverse, dk, dv, nc, batch):
    step = pl.program_id(0)

    @pl.when(step == 0)
    def _():
        c_s[...] = jnp.zeros_like(c_s)
        m_s[...] = jnp.zeros_like(m_s)

    chunk = nc - 1 - step if reverse else step
    L = CHUNK
    real = chunk > 0
    valid_c = jnp.logical_or(real, lax.broadcasted_iota(jnp.int32, (L, 1), 0) >= PAD)
    valid_r = jnp.logical_or(real, lax.broadcasted_iota(jnp.int32, (1, L), 1) >= PAD)
    off = 2 * HEADS if reverse else 0
    ri = lax.broadcasted_iota(jnp.int32, (L, L), 0)
    ci = lax.broadcasted_iota(jnp.int32, (L, L), 1)
    mask = (ci >= ri) if reverse else (ci <= ri)
    inc = (ri >= ci) if reverse else (ri <= ci)
    last = 0 if reverse else L - 1
    ones_col = jnp.where(lax.broadcasted_iota(jnp.int32, (L, LANES), 1) == 0, 1.0, 0.0).astype(BF16)
    lane_pad = [jnp.zeros((dk, -L % LANES), BF16)] if L % LANES else []
    score_w = L + -L % LANES

    lf_c, lf_r, li_r = [], [], []
    for b in range(batch):
        g = g_ref[b, :, :GATE_COLS] + gb_ref[...]
        gt = gt_ref[b, 0] + gbt_ref[...]
        lf_c.append(jnp.where(valid_c, _log_sigmoid(g), 0.0))
        lf_r.append(jnp.where(valid_r, _log_sigmoid(gt), 0.0))
        li_r.append(jnp.where(valid_r, gt[off:off + HEADS, :], NEG))
    terms_c = [t for x in lf_c for t in _split3(x)]
    sums_c = _dot(mask.astype(BF16), jnp.concatenate(terms_c, axis=1))
    terms_r = [t for x in lf_r for t in _split3(x)]
    sums_r = _dot(jnp.concatenate(terms_r, axis=0), inc.astype(BF16))
    w = GATE_COLS

    for b in range(batch):
        cum_c = sum(sums_c[:, (3 * b + j) * w:(3 * b + j + 1) * w] for j in range(3))
        cum_r = sum(sums_r[(3 * b + j) * w:(3 * b + j + 1) * w, :] for j in range(3))
        for h in range(HEADS):
            sh = b * HEADS + h
            col = off + HEADS + h
            cc = cum_c[:, col:col + 1]
            cr = cum_r[col:col + 1, :]
            lir = li_r[b][h:h + 1, :]
            tot = cc[last:last + 1, :]
            m = m_s[sh, 0:1, 0:1]
            qh = q_ref[b, :, h * dk:(h + 1) * dk]
            k_t = k_ref[b, :, h * dk:(h + 1) * dk].astype(F32).T
            vh = jnp.where(valid_c, v_ref[b, :, h * dv:(h + 1) * dv], 0.0).astype(BF16)
            vaug = jnp.concatenate([vh, ones_col], axis=1)

            rhs = jnp.concatenate([k_t.astype(BF16)] + lane_pad + [c_s[sh].astype(BF16)], axis=1)
            qkc = _dot(qh, rhs)
            d_mat = jnp.where(mask, cc - cr + lir, NEG)
            inter = cc + m
            m_t = jnp.maximum(inter, jnp.max(d_mat, axis=1, keepdims=True))
            w_inter = jnp.exp(inter - m_t)
            s = qkc[:, :L] * jnp.exp(d_mat - m_t)

            gs = tot - cr + lir
            m_new = jnp.maximum(tot + m, jnp.max(gs, axis=1, keepdims=True))
            decay = jnp.exp(tot + m - m_new)
            ks_t = (k_t * jnp.exp(gs - m_new)).astype(BF16)
            sv = _dot(jnp.concatenate([s.astype(BF16), ks_t], axis=0), vaug)
            haug = w_inter * qkc[:, score_w:] + sv[:L]
            den = haug[:, dv:dv + 1]
            o_ref[b, :, h * dv:(h + 1) * dv] = (
                haug[:, :dv] / jnp.maximum(jnp.abs(den), jnp.exp(-m_t))).astype(o_ref.dtype)
            c_s[sh] = decay * c_s[sh] + sv[L:]
            m_s[sh] = jnp.broadcast_to(m_new, m_s.shape[1:])


def _chunk_spec(batch, width, col_block, nc, reverse):
    cidx = (lambda i: nc - 1 - i) if reverse else (lambda i: i)
    return pl.BlockSpec((batch, CHUNK, width), lambda i: (0, cidx(i), col_block))


def _mlstm(qk, proj, gates, gates_t, gate_bias, batch, dk, dv, v_block, reverse):
    n = qk.shape[0]
    tp = n // batch
    nc = tp // CHUNK
    cidx = (lambda i: nc - 1 - i) if reverse else (lambda i: i)
    gb = gate_bias.reshape(1, GATE_COLS).astype(F32)
    view = lambda a: a.reshape(batch, tp, a.shape[-1])
    out = pl.pallas_call(
        functools.partial(_mlstm_kernel, reverse=reverse, dk=dk, dv=dv, nc=nc, batch=batch),
        grid=(nc,),
        in_specs=[_chunk_spec(batch, HEADS * dk, 0, nc, reverse), _chunk_spec(batch, HEADS * dk, 1, nc, reverse),
                  _chunk_spec(batch, HEADS * dv, v_block, nc, reverse), _chunk_spec(batch, LANES, 0, nc, reverse),
                  pl.BlockSpec((batch, 1, GATE_COLS, CHUNK), lambda i: (0, cidx(i), 0, 0)),
                  pl.BlockSpec((1, GATE_COLS), lambda i: (0, 0)),
                  pl.BlockSpec((GATE_COLS, 1), lambda i: (0, 0))],
        out_specs=_chunk_spec(batch, HEADS * dv, 0, nc, reverse),
        out_shape=jax.ShapeDtypeStruct((batch, tp, HEADS * dv), BF16),
        scratch_shapes=[pltpu.VMEM((batch * HEADS, dk, dv + LANES), F32),
                        pltpu.VMEM((batch * HEADS, SUBLANES, LANES), F32)],
        compiler_params=_params("arbitrary"),
        name="mlstm_bwd" if reverse else "mlstm_fwd")(
            view(qk), view(qk), view(proj), view(gates), gates_t.reshape(batch, nc, GATE_COLS, CHUNK), gb,
            gb.reshape(GATE_COLS, 1))
    return out.reshape(n, HEADS * dv)


def _gla_head_exact(q, k, v, cumh, state, o_ref, b, h, *, reverse, dk, dv):
    L = CHUNK
    nsub = L // SUB
    sub_lane = lax.broadcasted_iota(jnp.int32, (SUB, L), 1)
    sub_row = lax.broadcasted_iota(jnp.int32, (SUB, 1), 0)
    o_inter = _dot_t((q * jnp.exp(cumh)).astype(BF16), state.astype(BF16))
    for blk in range(nsub):
        r0 = blk * SUB
        if reverse:
            cs = cumh[r0 + SUB:r0 + SUB + 1, :] if blk < nsub - 1 else jnp.zeros((1, dk), F32)
            earlier = sub_lane >= r0 + SUB
        else:
            cs = cumh[r0 - 1:r0, :] if blk > 0 else jnp.zeros((1, dk), F32)
            earlier = sub_lane < r0
        q_b = q[r0:r0 + SUB, :]
        cum_b = cumh[r0:r0 + SUB, :]
        qd = (q_b * jnp.exp(cum_b - cs)).astype(BF16)
        kd = (k * jnp.exp(jnp.minimum(cs - cumh, 0.0))).astype(BF16)
        att = jnp.where(earlier, _dot_t(qd, kd), 0.0)
        for j in range(SUB):
            s_idx = r0 + j
            tmask = (sub_row <= j) if reverse else (sub_row >= j)
            e = jnp.where(tmask, cum_b - cumh[s_idx:s_idx + 1, :], NEG)
            col = jnp.sum(q_b * k[s_idx:s_idx + 1, :] * jnp.exp(e), axis=1, keepdims=True)
            att = jnp.where(sub_lane == s_idx, col, att)
        o_b = o_inter[r0:r0 + SUB, :] + _dot(att.astype(BF16), v)
        o_ref[b, r0:r0 + SUB, h * dv:(h + 1) * dv] = o_b.astype(o_ref.dtype)


def _gla_head_factored(q, k, v, cumh, state, mask, o_ref, b, h, *, dv):
    qe = (q * jnp.exp(cumh)).astype(BF16)
    ke = (k * jnp.exp(-cumh)).astype(BF16)
    att = jnp.where(mask, _dot_t(qe, ke), 0.0)
    o = _dot_t(qe, state.astype(BF16)) + _dot(att.astype(BF16), v)
    o_ref[b, :, h * dv:(h + 1) * dv] = o.astype(o_ref.dtype)


def _gla_kernel(q_ref, k_ref, v_ref, lr_ref, up_ref, ub_ref, o_ref, s_s, *, reverse, dk, dv, nc, batch):
    step = pl.program_id(0)

    @pl.when(step == 0)
    def _():
        s_s[...] = jnp.zeros_like(s_s)

    chunk = nc - 1 - step if reverse else step
    L = CHUNK
    valid_c = jnp.logical_or(chunk > 0, lax.broadcasted_iota(jnp.int32, (L, 1), 0) >= PAD)
    off = GATE_COLS + (B_RANK if reverse else 0)
    ri = lax.broadcasted_iota(jnp.int32, (L, L), 0)
    ci = lax.broadcasted_iota(jnp.int32, (L, L), 1)
    mask = (ci >= ri) if reverse else (ci <= ri)
    last = 0 if reverse else L - 1
    cums = []
    for b in range(batch):
        z = jnp.dot(lr_ref[b, :, off:off + B_RANK], up_ref[...], precision=HIGHEST, preferred_element_type=F32)
        la = jnp.where(valid_c, _log_sigmoid(z + ub_ref[...]) / B_TAU, 0.0)
        cums.append(sum(_dot(mask.astype(BF16), t) for t in _split3(la)))
    lowest = jnp.min(jnp.concatenate([c[last:last + 1, :] for c in cums], axis=0))
    factorable = lowest >= -GLA_MAX_CHUNK_DECAY

    def run(factored):
        for b in range(batch):
            for h in range(HEADS):
                sl = slice(h * dk, (h + 1) * dk)
                q = jnp.where(valid_c, q_ref[b, :, sl], 0.0).astype(F32) * dk ** -0.5
                k = jnp.where(valid_c, k_ref[b, :, sl], 0.0).astype(F32)
                v = jnp.where(valid_c, v_ref[b, :, h * dv:(h + 1) * dv], 0.0).astype(BF16)
                cumh = cums[b][:, sl]
                tot = cumh[last:last + 1, :]
                state = s_s[b * HEADS + h]
                if factored:
                    _gla_head_factored(q, k, v, cumh, state, mask, o_ref, b, h, dv=dv)
                else:
                    _gla_head_exact(q, k, v, cumh, state, o_ref, b, h, reverse=reverse, dk=dk, dv=dv)
                kdec = (k * jnp.exp(tot - cumh)).astype(BF16)
                s_s[b * HEADS + h] = jnp.exp(tot) * state + _tdot(v, kdec)

    @pl.when(factorable)
    def _():
        run(True)

    @pl.when(jnp.logical_not(factorable))
    def _():
        run(False)


def _gla(proj, small, lr_up, lr_bias, batch, dk, dv, qkv_blocks, reverse):
    n = proj.shape[0]
    tp = n // batch
    nc = tp // CHUNK
    qb, kb, vb = qkv_blocks
    view = lambda a: a.reshape(batch, tp, a.shape[-1])
    out = pl.pallas_call(
        functools.partial(_gla_kernel, reverse=reverse, dk=dk, dv=dv, nc=nc, batch=batch),
        grid=(nc,),
        in_specs=[_chunk_spec(batch, HEADS * dk, qb, nc, reverse), _chunk_spec(batch, HEADS * dk, kb, nc, reverse),
                  _chunk_spec(batch, HEADS * dv, vb, nc, reverse), _chunk_spec(batch, LANES, 0, nc, reverse),
                  pl.BlockSpec((B_RANK, HEADS * dk), lambda i: (0, 0)),
                  pl.BlockSpec((1, HEADS * dk), lambda i: (0, 0))],
        out_specs=_chunk_spec(batch, HEADS * dv, 0, nc, reverse),
        out_shape=jax.ShapeDtypeStruct((batch, tp, HEADS * dv), BF16),
        scratch_shapes=[pltpu.VMEM((batch * HEADS, dv, dk), F32)],
        compiler_params=_params("arbitrary"),
        name="gla_bwd" if reverse else "gla_fwd")(
            view(proj), view(proj), view(proj), view(small), lr_up.astype(F32), lr_bias.reshape(1, -1).astype(F32))
    return out.reshape(n, HEADS * dv)


def _head_norm(x, g, dv):
    parts = []
    for h in range(HEADS):
        xh = x[:, h * dv:(h + 1) * dv]
        parts.append(xh * lax.rsqrt(jnp.mean(xh * xh, axis=-1, keepdims=True) + EPS))
    return jnp.concatenate(parts, axis=1) * g


def _even_combine_kernel(af_ref, ab_ref, bf_ref, bb_ref, oa_ref, gb_ref, na_ref, nb_ref, o_ref, *, tr, tp, dv):
    rows = (pl.program_id(0) * tr) % tp + lax.broadcasted_iota(jnp.int32, (tr, 1), 0)
    valid = rows >= PAD
    w = HEADS * dv
    ha = af_ref[...].astype(F32) + ab_ref[...].astype(F32)
    ya = jax.nn.sigmoid(oa_ref[...].astype(F32)) * _head_norm(ha, na_ref[...], dv)
    o_ref[:, :w] = jnp.where(valid, ya, 0.0).astype(o_ref.dtype)
    hb = bf_ref[...].astype(F32) + bb_ref[...].astype(F32)
    gb = gb_ref[...].astype(F32)
    yb = gb * jax.nn.sigmoid(gb) * _head_norm(hb, nb_ref[...], dv)
    o_ref[:, w:] = jnp.where(valid, yb, 0.0).astype(o_ref.dtype)


def _even_combine(ha_f, ha_b, hb_f, hb_b, proj, norm_a, norm_b, batch, dv, oa_block, gb_block):
    n, w = ha_f.shape
    tp = n // batch
    tr = _tile(tp, 384, BF16_ROWS)
    row = pl.BlockSpec((tr, w), lambda i: (i, 0))
    vec = pl.BlockSpec((1, w), lambda i: (0, 0))
    return pl.pallas_call(
        functools.partial(_even_combine_kernel, tr=tr, tp=tp, dv=dv),
        grid=(n // tr,),
        in_specs=[row, row, row, row, pl.BlockSpec((tr, w), lambda i: (i, oa_block)),
                  pl.BlockSpec((tr, w), lambda i: (i, gb_block)), vec, vec],
        out_specs=pl.BlockSpec((tr, 2 * w), lambda i: (i, 0)),
        out_shape=jax.ShapeDtypeStruct((n, 2 * w), BF16),
        compiler_params=_params("parallel"), name="even_combine")(
            ha_f, ha_b, hb_f, hb_b, proj, proj, norm_a.reshape(1, w).astype(F32), norm_b.reshape(1, w).astype(F32))


def _block_scan(a, b, reverse):
    sub = lax.broadcasted_iota(jnp.int32, a.shape, 1)
    for k in (1, 2, 4):
        if reverse:
            a_sh, b_sh, m = pltpu.roll(a, SUBLANES - k, 1), pltpu.roll(b, SUBLANES - k, 1), sub < SUBLANES - k
        else:
            a_sh, b_sh, m = pltpu.roll(a, k, 1), pltpu.roll(b, k, 1), sub >= k
        b = jnp.where(m, a * b_sh + b, b)
        a = jnp.where(m, a * a_sh, a)
    return a, b


def _rglru_kernel(cur_ref, prev_ref, next_ref, cw_ref, cb_ref, wr_ref, br_ref, wi_ref, bi_ref, lam_ref,
                  o_ref, ext_s, a_s, b_s, h_s, carry_s, *, reverse, tt, nt):
    step = pl.program_id(2)

    @pl.when(step == 0)
    def _():
        carry_s[...] = jnp.zeros_like(carry_s)

    t = nt - 1 - step if reverse else step
    row0 = t * tt
    _fill_ext(ext_s, cur_ref, prev_ref, next_ref, row0, tt, t < nt - 1)
    u = _conv_taps(ext_s, cw_ref, tt) + cb_ref[...]
    ub = u.astype(BF16)
    r = _sigmoid(_dot(ub, wr_ref[0]) + br_ref[...])
    gi = _sigmoid(_dot(ub, wi_ref[0]) + bi_ref[...])
    lam = lam_ref[...]
    softplus = jnp.maximum(-lam, 0.0) + jnp.log1p(jnp.exp(-jnp.abs(lam)))
    log_a = -RNN_C * r * softplus
    a = jnp.exp(log_a)
    rows = row0 + lax.broadcasted_iota(jnp.int32, (tt, 1), 0)
    inp = jnp.where(rows >= PAD, jnp.sqrt(1.0 - a * a) * (gi * u), 0.0)
    c = a.shape[1]
    ng = tt // SUBLANES
    a_g, b_g = _block_scan(a.reshape(ng, SUBLANES, c), inp.reshape(ng, SUBLANES, c), reverse)
    a_s[...] = a_g.reshape(tt, c)
    b_s[...] = b_g.reshape(tt, c)
    out_row = 0 if reverse else SUBLANES - 1

    def body(i, carry):
        g = ng - 1 - i if reverse else i
        r0 = pl.multiple_of(g * SUBLANES, SUBLANES)
        hh = b_s[pl.ds(r0, SUBLANES), :] + a_s[pl.ds(r0, SUBLANES), :] * carry
        h_s[pl.ds(r0, SUBLANES), :] = hh
        return hh[out_row:out_row + 1, :]

    carry_s[...] = lax.fori_loop(0, ng, body, carry_s[...])
    o_ref[...] = h_s[...].astype(o_ref.dtype)


def _pair_blocks(w):
    nb, r, _ = w.shape
    z = jnp.zeros((nb // 2, r, r), w.dtype)
    top = jnp.concatenate([w[0::2], z], axis=2)
    bot = jnp.concatenate([z, w[1::2]], axis=2)
    return jnp.concatenate([top, bot], axis=1).astype(BF16)


def _rglru(proj, conv_w, conv_b, w_r, b_r, w_i, b_i, lam, batch, d_rnn, reverse):
    n = proj.shape[0]
    tp = n // batch
    cw = 2 * d_rnn // RNN_BLOCKS
    ncb = d_rnn // cw
    tt = _tile(tp, 688, BF16_ROWS)
    nt = tp // tt
    tmap = (lambda t: nt - 1 - t) if reverse else (lambda t: t)
    cur, prev, nxt = _halo_specs(cw, lambda j: ncb + j, tt, tp, n)(tmap)
    def swap(spec):
        f = spec.index_map
        return pl.BlockSpec(spec.block_shape, lambda b, j, t: f(b, t, j))
    vec = pl.BlockSpec((1, cw), lambda b, j, t: (0, j))
    wspec = pl.BlockSpec((1, cw, cw), lambda b, j, t: (j, 0, 0))
    row = lambda x: x.reshape(1, d_rnn).astype(F32)
    return pl.pallas_call(
        functools.partial(_rglru_kernel, reverse=reverse, tt=tt, nt=nt),
        grid=(batch, ncb, nt),
        in_specs=[swap(cur), swap(prev), swap(nxt), pl.BlockSpec((4, cw), lambda b, j, t: (0, j)), vec,
                  wspec, vec, wspec, vec, vec],
        out_specs=pl.BlockSpec((tt, cw), lambda b, j, t: (b * nt + tmap(t), j)),
        out_shape=jax.ShapeDtypeStruct((n, d_rnn), BF16),
        scratch_shapes=[pltpu.VMEM((tt + 2 * SUBLANES, cw), F32), pltpu.VMEM((tt, cw), F32),
                        pltpu.VMEM((tt, cw), F32), pltpu.VMEM((tt, cw), F32), pltpu.VMEM((1, cw), F32)],
        compiler_params=_params("parallel", "parallel", "arbitrary"),
        name="rglru_bwd" if reverse else "rglru_fwd")(
            proj, proj, proj, conv_w.astype(F32), row(conv_b), _pair_blocks(w_r), row(b_r),
            _pair_blocks(w_i), row(b_i), row(lam))


def _odd_combine_kernel(g_ref, hf_ref, hb_ref, o_ref, *, tr, tp):
    rows = (pl.program_id(0) * tr) % tp + lax.broadcasted_iota(jnp.int32, (tr, 1), 0)
    y = jax.nn.gelu(g_ref[...].astype(F32)) * (hf_ref[...].astype(F32) + hb_ref[...].astype(F32))
    o_ref[...] = jnp.where(rows >= PAD, y, 0.0).astype(o_ref.dtype)


def _odd_combine(proj, hf, hb, batch):
    n, w = hf.shape
    tp = n // batch
    tr = _tile(tp, 384, BF16_ROWS)
    row = pl.BlockSpec((tr, w), lambda i: (i, 0))
    return pl.pallas_call(
        functools.partial(_odd_combine_kernel, tr=tr, tp=tp), grid=(n // tr,),
        in_specs=[row, row, row], out_specs=row, out_shape=jax.ShapeDtypeStruct((n, w), BF16),
        compiler_params=_params("parallel"), name="odd_combine")(proj, hf, hb)


def _route(x):
    lane = lax.broadcasted_iota(jnp.int32, x.shape, 1)
    big = jnp.int32(2 * LANES)
    gmask = lane < N_GROUPS
    gmax = jnp.max(jnp.where(gmask, x, -jnp.inf), axis=1, keepdims=True)
    ge = jnp.where(gmask, jnp.exp(x - gmax), 0.0)
    gp = ge / jnp.sum(ge, axis=1, keepdims=True)
    gval = jnp.max(gp, axis=1, keepdims=True)
    gidx = jnp.min(jnp.where(jnp.logical_and(gmask, gp == gval), lane, big), axis=1, keepdims=True)
    lo = N_GROUPS + gidx * EXPERTS_PER_GROUP
    emask = jnp.logical_and(lane >= lo, lane < lo + EXPERTS_PER_GROUP)
    emax = jnp.max(jnp.where(emask, x, -jnp.inf), axis=1, keepdims=True)
    ee = jnp.where(emask, jnp.exp(x - emax), 0.0)
    ep = ee / jnp.sum(ee, axis=1, keepdims=True)
    v1 = jnp.max(jnp.where(emask, ep, -1.0), axis=1, keepdims=True)
    i1 = jnp.min(jnp.where(jnp.logical_and(emask, ep == v1), lane, big), axis=1, keepdims=True)
    rest = jnp.logical_and(emask, lane != i1)
    v2 = jnp.max(jnp.where(rest, ep, -1.0), axis=1, keepdims=True)
    i2 = jnp.min(jnp.where(jnp.logical_and(rest, ep == v2), lane, big), axis=1, keepdims=True)
    tot = v1 + v2
    comb = jnp.where(lane == i1, v1 / tot * gval, jnp.where(lane == i2, v2 / tot * gval, 0.0))
    return comb, gidx


def _pack_pair(hi, lo):
    bits = lambda v: lax.bitcast_convert_type(v.astype(BF16).astype(F32), jnp.uint32)
    return bits(hi) | (bits(lo) >> 16)


def _unpack_pair(w):
    hi = lax.bitcast_convert_type(w & jnp.uint32(0xFFFF0000), F32)
    lo = lax.bitcast_convert_type(w << 16, F32)
    return hi.astype(BF16), lo.astype(BF16)


def _store_packed_rows(ref, x, npack):
    half = npack * LANES
    for j in range(npack):
        ref[:, j, :] = _pack_pair(x[:, j * LANES:(j + 1) * LANES], x[:, half + j * LANES:half + (j + 1) * LANES])


def _load_packed_rows(src, dense_s, dst_ref, npack):
    half = npack * LANES
    for j in range(npack):
        dense_s[:, j * LANES:(j + 1) * LANES] = src(j)
    for j in range(npack):
        hi, lo = _unpack_pair(dense_s[:, j * LANES:(j + 1) * LANES])
        dst_ref[:, j * LANES:(j + 1) * LANES] = hi.astype(dst_ref.dtype)
        dst_ref[:, half + j * LANES:half + (j + 1) * LANES] = lo.astype(dst_ref.dtype)


def _norm_route_kernel(h_ref, g_ref, w2_ref, wh_ref, b_ref, slab_ref, oh_ref, *, npack):
    x = h_ref[...]
    y = x * lax.rsqrt(jnp.mean(x * x, axis=-1, keepdims=True) + EPS) * g_ref[...]
    yh = y.astype(BF16)
    yl = (y - yh.astype(F32)).astype(BF16)
    r1 = _dot(yh, w2_ref[...])
    logits = r1[:, :LANES] + r1[:, LANES:] + _dot(yl, wh_ref[...]) + b_ref[...]
    comb, gidx = _route(logits)
    _store_packed_rows(slab_ref, y, npack)
    slab_ref[:, npack, :] = lax.bitcast_convert_type(comb, jnp.uint32)
    for j in range(npack + 1, slab_ref.shape[1]):
        slab_ref[:, j, :] = jnp.zeros(comb.shape, jnp.uint32)
    lane = lax.broadcasted_iota(jnp.int32, comb.shape, 1)
    oh_ref[...] = jnp.where(lane == gidx, 1.0, 0.0).astype(oh_ref.dtype)


def _norm_route(h, g, wg, bg, we, be):
    n, d = h.shape
    npack = d // (2 * LANES)
    srows = (npack + 1 + SUBLANES - 1) // SUBLANES * SUBLANES
    tr = _tile(n, 192, BF16_ROWS)
    zpad = LANES - N_GROUPS - N_EXPERTS
    wr = jnp.concatenate([wg, we, jnp.zeros((d, zpad), F32)], axis=1)
    wh = wr.astype(BF16)
    wl = (wr - wh.astype(F32)).astype(BF16)
    bias = jnp.concatenate([bg.astype(F32), be.astype(F32), jnp.zeros((zpad,), F32)]).reshape(1, LANES)
    return pl.pallas_call(
        functools.partial(_norm_route_kernel, npack=npack), grid=(n // tr,),
        in_specs=[pl.BlockSpec((tr, d), lambda i: (i, 0)), pl.BlockSpec((1, d), lambda i: (0, 0)),
                  pl.BlockSpec((d, 2 * LANES), lambda i: (0, 0)), pl.BlockSpec((d, LANES), lambda i: (0, 0)),
                  pl.BlockSpec((1, LANES), lambda i: (0, 0))],
        out_specs=[pl.BlockSpec((tr, srows, LANES), lambda i: (i, 0, 0)), pl.BlockSpec((tr, LANES), lambda i: (i, 0))],
        out_shape=[jax.ShapeDtypeStruct((n, srows, LANES), jnp.uint32), jax.ShapeDtypeStruct((n, LANES), BF16)],
        compiler_params=_params("parallel"), name="norm_route")(
            h, g.reshape(1, d).astype(F32), jnp.concatenate([wh, wl], axis=1), wh, bias)


def _rank_kernel(oh_ref, g_ref, rank_ref, cnt_ref, carry_s):
    @pl.when(pl.program_id(0) == 0)
    def _():
        carry_s[...] = jnp.zeros_like(carry_s)

    tr = oh_ref.shape[0]
    sel = jnp.where(lax.broadcasted_iota(jnp.int32, (SUBLANES, LANES), 0) ==
                    lax.broadcasted_iota(jnp.int32, (SUBLANES, LANES), 1), 1.0, 0.0).astype(BF16)
    oh_t = _dot_t(sel, oh_ref[...])
    before = (lax.broadcasted_iota(jnp.int32, (tr, tr), 0) < lax.broadcasted_iota(jnp.int32, (tr, tr), 1))
    cum = _dot(oh_t.astype(BF16), jnp.where(before, 1.0, 0.0).astype(BF16)) + carry_s[:, 0:1]
    gid = lax.broadcasted_iota(jnp.int32, (SUBLANES, tr), 0).astype(F32)
    rank_ref[0] = jnp.sum(oh_t * cum, axis=0, keepdims=True).astype(jnp.int32)
    g_ref[0] = jnp.sum(oh_t * gid, axis=0, keepdims=True).astype(jnp.int32)
    carry_s[...] = carry_s[...] + jnp.sum(oh_t, axis=1, keepdims=True)
    cnt_ref[...] = carry_s[...]


def _rank(onehot):
    n = onehot.shape[0]
    tr = _tile(n, 384, LANES)
    row = pl.BlockSpec((1, 1, tr), lambda i: (i, 0, 0))
    g, rank, cnt = pl.pallas_call(
        _rank_kernel, grid=(n // tr,), in_specs=[pl.BlockSpec((tr, LANES), lambda i: (i, 0))],
        out_specs=[row, row, pl.BlockSpec((SUBLANES, LANES), lambda i: (0, 0))],
        out_shape=[jax.ShapeDtypeStruct((n // tr, 1, tr), jnp.int32), jax.ShapeDtypeStruct((n // tr, 1, tr), jnp.int32),
                   jax.ShapeDtypeStruct((SUBLANES, LANES), F32)],
        scratch_shapes=[pltpu.VMEM((SUBLANES, LANES), F32)],
        compiler_params=_params("arbitrary"), name="rank")(onehot)
    return g.reshape(n), rank.reshape(n), cnt[:N_GROUPS, 0].astype(jnp.int32)


def _invert_kernel(g_ref, rank_ref, cnt_ref, pos_ref, idx_ref, tg_ref, *, n, tm, ntiles):
    bases = [jnp.int32(0)]
    for g in range(N_GROUPS - 1):
        bases.append(bases[-1] + (cnt_ref[g] + tm - 1) // tm * tm)

    def zero(i, c):
        idx_ref[i] = 0
        return c
    lax.fori_loop(0, ntiles * tm, zero, 0, unroll=8)

    def place(t, c):
        g = g_ref[t]
        base = bases[0]
        for k in range(1, N_GROUPS):
            base = jnp.where(g == k, bases[k], base)
        p = base + rank_ref[t]
        pos_ref[t] = p
        idx_ref[p] = t
        return c
    lax.fori_loop(0, n, place, 0, unroll=8)

    def tile_group(i, c):
        r = i * tm
        tg = jnp.int32(0)
        for k in range(1, N_GROUPS):
            tg = tg + (r >= bases[k]).astype(jnp.int32)
        tg_ref[i] = tg
        return c
    lax.fori_loop(0, ntiles, tile_group, 0)


def _invert(g, rank, cnt, tm, ntiles):
    n = g.shape[0]
    smem = pl.BlockSpec(memory_space=pltpu.SMEM)
    return pl.pallas_call(
        functools.partial(_invert_kernel, n=n, tm=tm, ntiles=ntiles),
        in_specs=[smem, smem, smem], out_specs=[smem, smem, smem],
        out_shape=[jax.ShapeDtypeStruct((n,), jnp.int32), jax.ShapeDtypeStruct((ntiles * tm,), jnp.int32),
                   jax.ShapeDtypeStruct((ntiles,), jnp.int32)],
        name="invert")(g, rank, cnt)


def _row_copy(src_hbm, buf, sem, src_row, slot, dst_row):
    return pltpu.make_async_copy(src_hbm.at[pl.ds(src_row, 1)], buf.at[slot, pl.ds(dst_row, 1)], sem.at[slot])


def _gather_rows(index_ref, src_hbm, buf, sem, rows):
    i = pl.program_id(0)
    steps = pl.num_programs(0)

    def issue(step, slot):
        def body(r, c):
            _row_copy(src_hbm, buf, sem, index_ref[step * rows + r], slot, r).start()
            return c
        lax.fori_loop(0, rows, body, 0, unroll=8)

    @pl.when(i == 0)
    def _():
        issue(0, 0)

    @pl.when(i + 1 < steps)
    def _():
        issue(i + 1, (i + 1) % 2)

    slot = i % 2

    def wait(r, c):
        _row_copy(src_hbm, buf, sem, 0, slot, r).wait()
        return c
    lax.fori_loop(0, rows, wait, 0, unroll=8)
    return slot


def _dispatch_kernel(idx_ref, slab_hbm, xs_ref, cs_ref, buf, dense_s, sem, *, npack):
    slot = _gather_rows(idx_ref, slab_hbm, buf, sem, xs_ref.shape[0])
    _load_packed_rows(lambda j: buf[slot, :, j, :], dense_s, xs_ref, npack)
    cs_ref[...] = lax.bitcast_convert_type(buf[slot, :, npack, :], F32)


def _dispatch(idx, slab, tm, ntiles, d):
    srows = slab.shape[1]
    return pl.pallas_call(
        functools.partial(_dispatch_kernel, npack=d // (2 * LANES)),
        grid_spec=pltpu.PrefetchScalarGridSpec(
            num_scalar_prefetch=1, grid=(ntiles,),
            in_specs=[pl.BlockSpec(memory_space=pl.ANY)],
            out_specs=[pl.BlockSpec((tm, d), lambda i, idx: (i, 0)), pl.BlockSpec((tm, LANES), lambda i, idx: (i, 0))],
            scratch_shapes=[pltpu.VMEM((2, tm, srows, LANES), jnp.uint32), pltpu.VMEM((tm, d // 2), jnp.uint32),
                            pltpu.SemaphoreType.DMA((2,))]),
        out_shape=[jax.ShapeDtypeStruct((ntiles * tm, d), BF16), jax.ShapeDtypeStruct((ntiles * tm, LANES), F32)],
        compiler_params=_params("arbitrary"), name="dispatch")(idx, slab)


def _expert_kernel(tg_ref, x_ref, c_ref, w1_ref, w3_ref, w2_ref, o_ref, acc_s, *, npack):
    i = pl.program_id(0)
    e = pl.program_id(1)

    @pl.when(e == 0)
    def _():
        acc_s[...] = jnp.zeros_like(acc_s)

    a = x_ref[...]
    h1 = _dot(a, w1_ref[0])
    h3 = _dot(a, w3_ref[0])
    comb = c_ref[...]
    lane = lax.broadcasted_iota(jnp.int32, comb.shape, 1)
    c = jnp.sum(jnp.where(lane == N_GROUPS + tg_ref[i] * EXPERTS_PER_GROUP + e, comb, 0.0), axis=1, keepdims=True)
    hid = (h1 * jax.nn.sigmoid(h1) * h3 * c).astype(BF16)
    acc_s[...] += _dot(hid, w2_ref[0])

    @pl.when(e == EXPERTS_PER_GROUP - 1)
    def _():
        _store_packed_rows(o_ref, acc_s, npack)


def _experts(tg, xs, cs, w1, w3, w2, tm, layer):
    rows, d = xs.shape
    f = w1.shape[-1]
    npack = d // (2 * LANES)
    wmap = lambda i, e, tg: (layer * N_EXPERTS + tg[i] * EXPERTS_PER_GROUP + e, 0, 0)
    return pl.pallas_call(
        functools.partial(_expert_kernel, npack=npack),
        grid_spec=pltpu.PrefetchScalarGridSpec(
            num_scalar_prefetch=1, grid=(rows // tm, EXPERTS_PER_GROUP),
            in_specs=[pl.BlockSpec((tm, d), lambda i, e, tg: (i, 0)), pl.BlockSpec((tm, LANES), lambda i, e, tg: (i, 0)),
                      pl.BlockSpec((1, d, f), wmap), pl.BlockSpec((1, d, f), wmap), pl.BlockSpec((1, f, d), wmap)],
            out_specs=pl.BlockSpec((tm, npack, LANES), lambda i, e, tg: (i, 0, 0)),
            scratch_shapes=[pltpu.VMEM((tm, d), F32)]),
        out_shape=jax.ShapeDtypeStruct((rows, npack, LANES), jnp.uint32),
        compiler_params=_params("parallel", "arbitrary"), name="experts")(tg, xs, cs, w1, w3, w2)


def _collect_kernel(pos_ref, ys_hbm, o_ref, buf, dense_s, sem, *, npack):
    slot = _gather_rows(pos_ref, ys_hbm, buf, sem, o_ref.shape[0])
    _load_packed_rows(lambda j: buf[slot, :, j, :], dense_s, o_ref, npack)


def _collect(pos, ys):
    n = pos.shape[0]
    _, npack, _ = ys.shape
    d = 2 * npack * LANES
    tr = _tile(n, 384, BF16_ROWS)
    return pl.pallas_call(
        functools.partial(_collect_kernel, npack=npack),
        grid_spec=pltpu.PrefetchScalarGridSpec(
            num_scalar_prefetch=1, grid=(n // tr,),
            in_specs=[pl.BlockSpec(memory_space=pl.ANY)],
            out_specs=pl.BlockSpec((tr, d), lambda i, pos: (i, 0)),
            scratch_shapes=[pltpu.VMEM((2, tr, npack, LANES), jnp.uint32), pltpu.VMEM((tr, d // 2), jnp.uint32),
                            pltpu.SemaphoreType.DMA((2,))]),
        out_shape=jax.ShapeDtypeStruct((n, d), BF16),
        compiler_params=_params("arbitrary"), name="collect")(pos, ys)


def _moe_layer(h, ffn_g, wg, bg, we, be, w1, w3, w2, layer):
    n, d = h.shape
    tm = 512 if n >= 4096 else 128
    ntiles = (n + N_GROUPS * (tm - 1)) // tm
    slab, onehot = _norm_route(h, ffn_g, wg, bg, we, be)
    g, rank, cnt = _rank(onehot)
    pos, idx, tg = _invert(g, rank, cnt, tm, ntiles)
    xs, cs = _dispatch(idx, slab, tm, ntiles, d)
    ys = _experts(tg, xs, cs, w1, w3, w2, tm, layer)
    return _collect(pos, ys)


def _even_layer(h, hn, w_in, gate_bias, qk_conv, lr_up, lr_bias, norm_a, norm_b, w_out, batch):
    n, d = h.shape
    dk, dv = d // 16, d // 8
    qk_w, v_w = HEADS * dk, HEADS * dv
    a_end = 2 * qk_w + 2 * v_w
    b_start = a_end + GATE_COLS
    b_end = b_start + 2 * qk_w + 2 * v_w
    w_main, w_small = _split_cast(w_in, a_end, b_start, b_end)
    proj = _matmul(hn, w_main, BF16, tm_target=1376)
    small = _matmul(hn, w_small, F32, tn_target=LANES)
    gates_t = small[:, :GATE_COLS].reshape(n // CHUNK, CHUNK, GATE_COLS).transpose(0, 2, 1)
    qk = _qk_conv(proj, qk_conv, batch, dk)
    va_blk, oa_blk = 2 * qk_w // v_w, (2 * qk_w + v_w) // v_w
    b0 = a_end
    qb_blk, kb_blk = b0 // qk_w, (b0 + qk_w) // qk_w
    vb_blk, gb_blk = (b0 + 2 * qk_w) // v_w, (b0 + 2 * qk_w + v_w) // v_w
    ha, hb = [], []
    for rev in (False, True):
        ha.append(_mlstm(qk, proj, small, gates_t, gate_bias, batch, dk, dv, va_blk, rev))
        hb.append(_gla(proj, small, lr_up[int(rev)], lr_bias[int(rev)], batch, dk, dv, (qb_blk, kb_blk, vb_blk), rev))
    y = _even_combine(ha[0], ha[1], hb[0], hb[1], proj, norm_a, norm_b, batch, dv, oa_blk, gb_blk)
    return _matmul(y, _cast_bf16(w_out), F32, res=h)


def _odd_layer(h, hn, w_in, conv_w, conv_b, w_r, b_r, w_i, b_i, lam, w_out, batch):
    d_rnn = w_out.shape[0]
    proj = _matmul(hn, _cast_bf16(w_in), BF16, tm_target=1376)
    hs = [_rglru(proj, conv_w, conv_b, w_r[i], b_r[i], w_i[i], b_i[i], lam[i], batch, d_rnn, bool(i)) for i in (0, 1)]
    y = _odd_combine(proj, hs[0], hs[1], batch)
    return _matmul(y, _cast_bf16(w_out), F32, res=h)


def kernel(x, meta_tokens, mix_norm, ffn_norm, final_norm, ev_w_in, ev_gate_bias, ev_qk_conv, ev_lr_up, ev_lr_bias, ev_norm_a, ev_norm_b, ev_w_out, od_w_in, od_conv, od_conv_bias, od_w_r, od_b_r, od_w_i, od_b_i, od_lambda, od_w_out, moe_wg, moe_bg, moe_we, moe_be, moe_w1, moe_w3, moe_w2):
    batch, seq, d = x.shape
    depth = mix_norm.shape[0]
    assert seq % CHUNK == 0 and d % 16 == 0
    f = moe_w1.shape[-1]
    w1 = _cast_bf16(moe_w1).reshape(depth * N_EXPERTS, d, f)
    w3 = _cast_bf16(moe_w3).reshape(depth * N_EXPERTS, d, f)
    w2 = _cast_bf16(moe_w2).reshape(depth * N_EXPERTS, f, d)
    for layer in range(depth):
        if layer == 0:
            h, hn = _frame_norm(x, meta_tokens, mix_norm[0])
        else:
            h, hn = _norm(h, mix_norm[layer], delta)
        if layer % 2 == 0:
            e = layer // 2
            h = _even_layer(h, hn, ev_w_in[e], ev_gate_bias[e], ev_qk_conv[e], ev_lr_up[e],
                            ev_lr_bias[e], ev_norm_a[e], ev_norm_b[e], ev_w_out[e], batch)
        else:
            o = layer // 2
            h = _odd_layer(h, hn, od_w_in[o], od_conv[o], od_conv_bias[o], od_w_r[o], od_b_r[o],
                           od_w_i[o], od_b_i[o], od_lambda[o], od_w_out[o], batch)
        delta = _moe_layer(h, ffn_norm[layer], moe_wg[layer], moe_bg[layer], moe_we[layer], moe_be[layer],
                           w1, w3, w2, layer)
    out = _final_norm(h, delta, final_norm, batch, seq)
    return out.reshape(batch, seq, d)
```

```python
import functools

import jax
import jax.numpy as jnp
from jax import lax
from jax.experimental import pallas as pl
from jax.experimental.pallas import tpu as pltpu

F32 = jnp.float32
BF16 = jnp.bfloat16
HIGHEST = lax.Precision.HIGHEST

N_META = 16
CHUNK = 128
PAD = CHUNK - N_META
SUB = 16
EPS = 1e-6
NEG = -1e30
HEADS = 4
GATE_COLS = 4 * HEADS
B_RANK = 16
B_TAU = 16.0
GLA_MAX_CHUNK_DECAY = 80.0
RNN_BLOCKS = 16
RNN_C = 8.0
N_GROUPS = 4
EXPERTS_PER_GROUP = 8
N_EXPERTS = N_GROUPS * EXPERTS_PER_GROUP
LANES = 128
SUBLANES = 8
BF16_ROWS = 16
VMEM_LIMIT = 56 * 1024 * 1024


def _params(*sem):
    return pltpu.CompilerParams(dimension_semantics=sem, vmem_limit_bytes=VMEM_LIMIT)


def _tile(n, target, mult):
    best = None
    for t in range(mult, min(n, target) + 1, mult):
        if n % t == 0:
            best = t
    assert best is not None, (n, target, mult)
    return best


def _log_sigmoid(x):
    return jnp.minimum(x, 0.0) - jnp.log1p(jnp.exp(-jnp.abs(x)))


def _split3(x):
    hi = x.astype(BF16)
    r1 = x - hi.astype(F32)
    mid = r1.astype(BF16)
    lo = (r1 - mid.astype(F32)).astype(BF16)
    return hi, mid, lo


def _dot(a, b):
    return jnp.dot(a, b, preferred_element_type=F32)


def _dot_t(a, b):
    return lax.dot_general(a, b, (((1,), (1,)), ((), ())), preferred_element_type=F32)


def _tdot(a, b, precision=None):
    return lax.dot_general(a, b, (((0,), (0,)), ((), ())), preferred_element_type=F32,
                           precision=precision)


def _frame_norm_kernel(x_ref, meta_ref, g_ref, h_ref, hn_ref):
    tr, d = h_ref.shape
    xb = x_ref[...]
    first = jnp.concatenate([jnp.zeros((PAD, d), F32), meta_ref[...]] + ([xb[:tr - CHUNK]] if tr > CHUNK else []), axis=0)
    x = jnp.where(pl.program_id(1) == 0, first, xb)
    h_ref[...] = x
    y = x * lax.rsqrt(jnp.mean(x * x, axis=-1, keepdims=True) + EPS) * g_ref[...]
    hn_ref[...] = y.astype(hn_ref.dtype)


def _frame_norm(x, meta, g):
    batch, seq, d = x.shape
    tp = PAD + N_META + seq
    nc = tp // CHUNK
    tr = CHUNK * _tile(nc, min(3, seq // CHUNK), 1)
    nt = tp // tr
    out = pl.BlockSpec((tr, d), lambda b, i: (b * nt + i, 0))
    src = pl.BlockSpec((pl.Element(tr), pl.Element(d)),
                       lambda b, i: (pl.multiple_of(b * seq + jnp.maximum(i * tr - CHUNK, 0), CHUNK), 0))
    return pl.pallas_call(
        _frame_norm_kernel, grid=(batch, nt),
        in_specs=[src, pl.BlockSpec((N_META, d), lambda b, i: (0, 0)), pl.BlockSpec((1, d), lambda b, i: (0, 0))],
        out_specs=[out, out],
        out_shape=[jax.ShapeDtypeStruct((batch * tp, d), F32), jax.ShapeDtypeStruct((batch * tp, d), BF16)],
        compiler_params=_params("parallel", "parallel"), name="frame_norm")(
            x.reshape(batch * seq, d), meta.astype(F32), g.reshape(1, d).astype(F32))


def _final_norm_kernel(h_ref, d_ref, g_ref, o_ref):
    x = h_ref[...] + d_ref[...].astype(F32)
    o_ref[...] = x * lax.rsqrt(jnp.mean(x * x, axis=-1, keepdims=True) + EPS) * g_ref[...]


def _final_norm(h, delta, g, batch, seq):
    n, d = h.shape
    tp = n // batch
    tr = _tile(seq, 512, BF16_ROWS)
    nt = seq // tr
    src = pl.BlockSpec((pl.Element(tr), pl.Element(d)),
                       lambda b, i: (pl.multiple_of(b * tp + CHUNK + i * tr, CHUNK), 0))
    return pl.pallas_call(
        _final_norm_kernel, grid=(batch, nt),
        in_specs=[src, src, pl.BlockSpec((1, d), lambda b, i: (0, 0))],
        out_specs=pl.BlockSpec((tr, d), lambda b, i: (b * nt + i, 0)),
        out_shape=jax.ShapeDtypeStruct((batch * seq, d), F32),
        compiler_params=_params("parallel", "parallel"), name="final_norm")(h, delta, g.reshape(1, d).astype(F32))


def _cast_kernel(w_ref, o_ref):
    o_ref[...] = w_ref[...].astype(o_ref.dtype)


def _cast_bf16(w, block_bytes=8 * 1024 * 1024):
    shape = w.shape
    w2 = w.reshape(-1, shape[-1])
    rows, cols = w2.shape
    tr = _tile(rows, max(BF16_ROWS, block_bytes // (4 * cols)), BF16_ROWS)
    spec = pl.BlockSpec((tr, cols), lambda i: (i, 0))
    out = pl.pallas_call(
        _cast_kernel, grid=(rows // tr,), in_specs=[spec], out_specs=spec,
        out_shape=jax.ShapeDtypeStruct((rows, cols), BF16),
        compiler_params=_params("parallel"), name="cast_bf16")(w2)
    return out.reshape(shape)


def _cast_pair_kernel(a_ref, b_ref, o_ref):
    f = a_ref.shape[1]
    o_ref[:, :f] = a_ref[...].astype(o_ref.dtype)
    o_ref[:, f:] = b_ref[...].astype(o_ref.dtype)


def _cast_pair_bf16(a, b, block_bytes=4 * 1024 * 1024):
    shape = a.shape
    f = shape[-1]
    a2, b2 = a.reshape(-1, f), b.reshape(-1, f)
    rows = a2.shape[0]
    tr = _tile(rows, max(BF16_ROWS, block_bytes // (4 * f)), BF16_ROWS)
    spec = pl.BlockSpec((tr, f), lambda i: (i, 0))
    out = pl.pallas_call(
        _cast_pair_kernel, grid=(rows // tr,), in_specs=[spec, spec],
        out_specs=pl.BlockSpec((tr, 2 * f), lambda i: (i, 0)),
        out_shape=jax.ShapeDtypeStruct((rows, 2 * f), BF16),
        compiler_params=_params("parallel"), name="cast_pair_bf16")(a2, b2)
    return out.reshape(shape[:-1] + (2 * f,))


def _split_cast_kernel(w_ref, main_ref, small_ref, *, a_end, b_start, b_end):
    w = w_ref[...]
    main_ref[:, :a_end] = w[:, :a_end].astype(main_ref.dtype)
    main_ref[:, a_end:] = w[:, b_start:b_end].astype(main_ref.dtype)
    rows = w.shape[0]
    narrow = jnp.concatenate([w[:, a_end:b_start], w[:, b_end:]], axis=1)
    pad = jnp.zeros((rows, small_ref.shape[1] - narrow.shape[1]), F32)
    small_ref[...] = jnp.concatenate([narrow, pad], axis=1).astype(small_ref.dtype)


def _split_cast(w_in, a_end, b_start, b_end):
    d, cols = w_in.shape
    tr = _tile(d, 256, BF16_ROWS)
    wide = a_end + b_end - b_start
    return pl.pallas_call(
        functools.partial(_split_cast_kernel, a_end=a_end, b_start=b_start, b_end=b_end), grid=(d // tr,),
        in_specs=[pl.BlockSpec((tr, cols), lambda i: (i, 0))],
        out_specs=[pl.BlockSpec((tr, wide), lambda i: (i, 0)), pl.BlockSpec((tr, LANES), lambda i: (i, 0))],
        out_shape=[jax.ShapeDtypeStruct((d, wide), BF16), jax.ShapeDtypeStruct((d, LANES), BF16)],
        compiler_params=_params("parallel"), name="split_cast")(w_in)


def _mm_kernel(*refs, has_res):
    if has_res:
        a_ref, w_ref, r_ref, o_ref = refs
    else:
        a_ref, w_ref, o_ref = refs
    acc = _dot(a_ref[...], w_ref[...])
    if has_res:
        acc = acc + r_ref[...]
    o_ref[...] = acc.astype(o_ref.dtype)


def _matmul(a, w, out_dtype, res=None, tm_target=688, tn_target=512):
    n, k = a.shape
    m = w.shape[1]
    tm = _tile(n, tm_target, BF16_ROWS)
    tn = _tile(m, tn_target, LANES)
    in_specs = [pl.BlockSpec((tm, k), lambda i, j: (i, 0)), pl.BlockSpec((k, tn), lambda i, j: (0, j))]
    args = [a, w]
    if res is not None:
        in_specs.append(pl.BlockSpec((tm, tn), lambda i, j: (i, j)))
        args.append(res)
    return pl.pallas_call(
        functools.partial(_mm_kernel, has_res=res is not None),
        grid=(n // tm, m // tn), in_specs=in_specs,
        out_specs=pl.BlockSpec((tm, tn), lambda i, j: (i, j)),
        out_shape=jax.ShapeDtypeStruct((n, m), out_dtype),
        compiler_params=_params("parallel", "arbitrary"), name="matmul")(*args)


def _conv_taps(ext_s, cw_ref, tt):
    out = cw_ref[0:1, :] * ext_s[pl.ds(SUBLANES - 2, tt), :]
    for j in range(1, 4):
        out = out + cw_ref[j:j + 1, :] * ext_s[pl.ds(SUBLANES - 2 + j, tt), :]
    return out


def _fill_ext(ext_s, cur_ref, prev_ref, next_ref, row0, tt, has_next):
    rows = row0 + lax.broadcasted_iota(jnp.int32, (tt, 1), 0)
    ext_s[pl.ds(SUBLANES, tt), :] = jnp.where(rows >= PAD, cur_ref[...].astype(F32), 0.0)
    prow = row0 - SUBLANES + lax.broadcasted_iota(jnp.int32, (SUBLANES, 1), 0)
    ext_s[pl.ds(0, SUBLANES), :] = jnp.where(prow >= PAD, prev_ref[...].astype(F32)[SUBLANES:, :], 0.0)
    ext_s[pl.ds(SUBLANES + tt, SUBLANES), :] = jnp.where(has_next, next_ref[...].astype(F32)[:SUBLANES, :], 0.0)


def _qkconv_kernel(cur_ref, prev_ref, next_ref, cw_ref, o_ref, ext_s, *, tt, nt, kscale, half):
    t = pl.program_id(1)
    row0 = t * tt
    _fill_ext(ext_s, cur_ref, prev_ref, next_ref, row0, tt, t < nt - 1)
    y = _conv_taps(ext_s, cw_ref, tt)
    y = y * jax.nn.sigmoid(y)
    col = lax.broadcasted_iota(jnp.int32, (1, 2 * half), 1)
    y = y * jnp.where(col >= half, kscale, 1.0)
    rows = row0 + lax.broadcasted_iota(jnp.int32, (tt, 1), 0)
    o_ref[...] = jnp.where(rows >= PAD, y, 0.0).astype(o_ref.dtype)


def _halo_specs(width, col_block, tt, tp, n):
    per_b, per_t = tp // BF16_ROWS, tt // BF16_ROWS
    last = n // BF16_ROWS - 1

    def make(tmap):
        cur = pl.BlockSpec((tt, width), lambda b, t, *_: (b * (tp // tt) + tmap(t), col_block(*_)))
        prev = pl.BlockSpec((BF16_ROWS, width),
                            lambda b, t, *_: (jnp.maximum(b * per_b + tmap(t) * per_t - 1, 0), col_block(*_)))
        nxt = pl.BlockSpec((BF16_ROWS, width),
                           lambda b, t, *_: (jnp.minimum(b * per_b + (tmap(t) + 1) * per_t, last), col_block(*_)))
        return cur, prev, nxt
    return make


def _qk_conv(proj, conv_w, batch, dk):
    n = proj.shape[0]
    tp = n // batch
    width = 2 * HEADS * dk
    tt = _tile(tp, 688, BF16_ROWS)
    nt = tp // tt
    cur, prev, nxt = _halo_specs(width, lambda: 0, tt, tp, n)(lambda t: t)
    return pl.pallas_call(
        functools.partial(_qkconv_kernel, tt=tt, nt=nt, kscale=dk ** -0.5, half=HEADS * dk),
        grid=(batch, nt),
        in_specs=[cur, prev, nxt, pl.BlockSpec((4, width), lambda b, t: (0, 0))],
        out_specs=pl.BlockSpec((tt, width), lambda b, t: (b * nt + t, 0)),
        out_shape=jax.ShapeDtypeStruct((n, width), BF16),
        scratch_shapes=[pltpu.VMEM((tt + 2 * SUBLANES, width), F32)],
        compiler_params=_params("parallel", "parallel"), name="qk_conv")(proj, proj, proj, conv_w.astype(F32))


def _mlstm_kernel(q_ref, k_ref, v_ref, g_ref, gt_ref, gb_ref, gbt_ref, o_ref, c_s, m_s, *, reverse, dk, dv, nc, batch):
    step = pl.program_id(0)

    @pl.when(step == 0)
    def _():
        c_s[...] = jnp.zeros_like(c_s)
        m_s[...] = jnp.zeros_like(m_s)

    chunk = nc - 1 - step if reverse else step
    L = CHUNK
    real = chunk > 0
    valid_c = jnp.logical_or(real, lax.broadcasted_iota(jnp.int32, (L, 1), 0) >= PAD)
    valid_r = jnp.logical_or(real, lax.broadcasted_iota(jnp.int32, (1, L), 1) >= PAD)
    off = 2 * HEADS if reverse else 0
    ri = lax.broadcasted_iota(jnp.int32, (L, L), 0)
    ci = lax.broadcasted_iota(jnp.int32, (L, L), 1)
    mask = (ci >= ri) if reverse else (ci <= ri)
    inc = (ri >= ci) if reverse else (ri <= ci)
    last = 0 if reverse else L - 1
    ones_col = jnp.where(lax.broadcasted_iota(jnp.int32, (L, LANES), 1) == 0, 1.0, 0.0).astype(BF16)
    lane_pad = [jnp.zeros((dk, -L % LANES), BF16)] if L % LANES else []
    score_w = L + -L % LANES

    lf_c, lf_r, li_r = [], [], []
    for b in range(batch):
        g = g_ref[b, :, :GATE_COLS] + gb_ref[...]
        gt = gt_ref[b, 0] + gbt_ref[...]
        lf_c.append(jnp.where(valid_c, _log_sigmoid(g), 0.0))
        lf_r.append(jnp.where(valid_r, _log_sigmoid(gt), 0.0))
        li_r.append(jnp.where(valid_r, gt[off:off + HEADS, :], NEG))
    terms_c = [t for x in lf_c for t in _split3(x)]
    sums_c = _dot(mask.astype(BF16), jnp.concatenate(terms_c, axis=1))
    terms_r = [t for x in lf_r for t in _split3(x)]
    sums_r = _dot(jnp.concatenate(terms_r, axis=0), inc.astype(BF16))
    w = GATE_COLS

    for b in range(batch):
        cum_c = sum(sums_c[:, (3 * b + j) * w:(3 * b + j + 1) * w] for j in range(3))
        cum_r = sum(sums_r[(3 * b + j) * w:(3 * b + j + 1) * w, :] for j in range(3))
        for h in range(HEADS):
            sh = b * HEADS + h
            col = off + HEADS + h
            cc = cum_c[:, col:col + 1]
            cr = cum_r[col:col + 1, :]
            lir = li_r[b][h:h + 1, :]
            tot = cc[last:last + 1, :]
            m = m_s[sh, 0:1, 0:1]
            qh = q_ref[b, :, h * dk:(h + 1) * dk]
            k_t = k_ref[b, :, h * dk:(h + 1) * dk].astype(F32).T
            vh = jnp.where(valid_c, v_ref[b, :, h * dv:(h + 1) * dv], 0.0).astype(BF16)
            vaug = jnp.concatenate([vh, ones_col], axis=1)

            rhs = jnp.concatenate([k_t.astype(BF16)] + lane_pad + [c_s[sh].astype(BF16)], axis=1)
            qkc = _dot(qh, rhs)
            d_mat = jnp.where(mask, cc - cr + lir, NEG)
            inter = cc + m
            m_t = jnp.maximum(inter, jnp.max(d_mat, axis=1, keepdims=True))
            w_inter = jnp.exp(inter - m_t)
            s = qkc[:, :L] * jnp.exp(d_mat - m_t)

            gs = tot - cr + lir
            m_new = jnp.maximum(tot + m, jnp.max(gs, axis=1, keepdims=True))
            decay = jnp.exp(tot + m - m_new)
            ks_t = (k_t * jnp.exp(gs - m_new)).astype(BF16)
            sv = _dot(jnp.concatenate([s.astype(BF16), ks_t], axis=0), vaug)
            haug = w_inter * qkc[:, score_w:] + sv[:L]
            den = haug[:, dv:dv + 1]
            o_ref[b, :, h * dv:(h + 1) * dv] = (
                haug[:, :dv] / jnp.maximum(jnp.abs(den), jnp.exp(-m_t))).astype(o_ref.dtype)
            c_s[sh] = decay * c_s[sh] + sv[L:]
            m_s[sh] = jnp.broadcast_to(m_new, m_s.shape[1:])


def _chunk_spec(batch, width, col_block, nc, reverse):
    cidx = (lambda i: nc - 1 - i) if reverse else (lambda i: i)
    return pl.BlockSpec((batch, CHUNK, width), lambda i: (0, cidx(i), col_block))


def _mlstm(qk, proj, gates, gates_t, gate_bias, batch, dk, dv, v_block, reverse):
    n = qk.shape[0]
    tp = n // batch
    nc = tp // CHUNK
    cidx = (lambda i: nc - 1 - i) if reverse else (lambda i: i)
    gb = gate_bias.reshape(1, GATE_COLS).astype(F32)
    view = lambda a: a.reshape(batch, tp, a.shape[-1])
    out = pl.pallas_call(
        functools.partial(_mlstm_kernel, reverse=reverse, dk=dk, dv=dv, nc=nc, batch=batch),
        grid=(nc,),
        in_specs=[_chunk_spec(batch, HEADS * dk, 0, nc, reverse), _chunk_spec(batch, HEADS * dk, 1, nc, reverse),
                  _chunk_spec(batch, HEADS * dv, v_block, nc, reverse), _chunk_spec(batch, LANES, 0, nc, reverse),
                  pl.BlockSpec((batch, 1, GATE_COLS, CHUNK), lambda i: (0, cidx(i), 0, 0)),
                  pl.BlockSpec((1, GATE_COLS), lambda i: (0, 0)),
                  pl.BlockSpec((GATE_COLS, 1), lambda i: (0, 0))],
        out_specs=_chunk_spec(batch, HEADS * dv, 0, nc, reverse),
        out_shape=jax.ShapeDtypeStruct((batch, tp, HEADS * dv), BF16),
        scratch_shapes=[pltpu.VMEM((batch * HEADS, dk, dv + LANES), F32),
                        pltpu.VMEM((batch * HEADS, SUBLANES, LANES), F32)],
        compiler_params=_params("arbitrary"),
        name="mlstm_bwd" if reverse else "mlstm_fwd")(
            view(qk), view(qk), view(proj), view(gates), gates_t.reshape(batch, nc, GATE_COLS, CHUNK), gb,
            gb.reshape(GATE_COLS, 1))
    return out.reshape(n, HEADS * dv)


def _gla_head_exact(q, k, v, cumh, state, o_ref, b, h, *, reverse, dk, dv):
    L = CHUNK
    nsub = L // SUB
    sub_lane = lax.broadcasted_iota(jnp.int32, (SUB, L), 1)
    sub_row = lax.broadcasted_iota(jnp.int32, (SUB, 1), 0)
    o_inter = _dot_t((q * jnp.exp(cumh)).astype(BF16), state.astype(BF16))
    for blk in range(nsub):
        r0 = blk * SUB
        if reverse:
            cs = cumh[r0 + SUB:r0 + SUB + 1, :] if blk < nsub - 1 else jnp.zeros((1, dk), F32)
            earlier = sub_lane >= r0 + SUB
        else:
            cs = cumh[r0 - 1:r0, :] if blk > 0 else jnp.zeros((1, dk), F32)
            earlier = sub_lane < r0
        q_b = q[r0:r0 + SUB, :]
        cum_b = cumh[r0:r0 + SUB, :]
        qd = (q_b * jnp.exp(cum_b - cs)).astype(BF16)
        kd = (k * jnp.exp(jnp.minimum(cs - cumh, 0.0))).astype(BF16)
        att = jnp.where(earlier, _dot_t(qd, kd), 0.0)
        for j in range(SUB):
            s_idx = r0 + j
            tmask = (sub_row <= j) if reverse else (sub_row >= j)
            e = jnp.where(tmask, cum_b - cumh[s_idx:s_idx + 1, :], NEG)
            col = jnp.sum(q_b * k[s_idx:s_idx + 1, :] * jnp.exp(e), axis=1, keepdims=True)
            att = jnp.where(sub_lane == s_idx, col, att)
        o_b = o_inter[r0:r0 + SUB, :] + _dot(att.astype(BF16), v)
        o_ref[b, r0:r0 + SUB, h * dv:(h + 1) * dv] = o_b.astype(o_ref.dtype)


def _gla_head_factored(q, k, v, cumh, state, mask, o_ref, b, h, *, dv):
    qe = (q * jnp.exp(cumh)).astype(BF16)
    ke = (k * jnp.exp(-cumh)).astype(BF16)
    att = jnp.where(mask, _dot_t(qe, ke), 0.0)
    o = _dot_t(qe, state.astype(BF16)) + _dot(att.astype(BF16), v)
    o_ref[b, :, h * dv:(h + 1) * dv] = o.astype(o_ref.dtype)


def _gla_kernel(q_ref, k_ref, v_ref, lr_ref, up_ref, ub_ref, o_ref, s_s, *, reverse, dk, dv, nc, batch):
    step = pl.program_id(0)

    @pl.when(step == 0)
    def _():
        s_s[...] = jnp.zeros_like(s_s)

    chunk = nc - 1 - step if reverse else step
    L = CHUNK
    valid_c = jnp.logical_or(chunk > 0, lax.broadcasted_iota(jnp.int32, (L, 1), 0) >= PAD)
    off = GATE_COLS + (B_RANK if reverse else 0)
    ri = lax.broadcasted_iota(jnp.int32, (L, L), 0)
    ci = lax.broadcasted_iota(jnp.int32, (L, L), 1)
    mask = (ci >= ri) if reverse else (ci <= ri)
    last = 0 if reverse else L - 1
    cums = []
    for b in range(batch):
        z = jnp.dot(lr_ref[b, :, off:off + B_RANK], up_ref[...], precision=HIGHEST, preferred_element_type=F32)
        la = jnp.where(valid_c, _log_sigmoid(z + ub_ref[...]) / B_TAU, 0.0)
        cums.append(sum(_dot(mask.astype(BF16), t) for t in _split3(la)))
    lowest = jnp.min(jnp.concatenate([c[last:last + 1, :] for c in cums], axis=0))
    factorable = lowest >= -GLA_MAX_CHUNK_DECAY

    def run(factored):
        for b in range(batch):
            for h in range(HEADS):
                sl = slice(h * dk, (h + 1) * dk)
                q = jnp.where(valid_c, q_ref[b, :, sl], 0.0).astype(F32) * dk ** -0.5
                k = jnp.where(valid_c, k_ref[b, :, sl], 0.0).astype(F32)
                v = jnp.where(valid_c, v_ref[b, :, h * dv:(h + 1) * dv], 0.0).astype(BF16)
                cumh = cums[b][:, sl]
                tot = cumh[last:last + 1, :]
                state = s_s[b * HEADS + h]
                if factored:
                    _gla_head_factored(q, k, v, cumh, state, mask, o_ref, b, h, dv=dv)
                else:
                    _gla_head_exact(q, k, v, cumh, state, o_ref, b, h, reverse=reverse, dk=dk, dv=dv)
                kdec = (k * jnp.exp(tot - cumh)).astype(BF16)
                s_s[b * HEADS + h] = jnp.exp(tot) * state + _tdot(v, kdec)

    @pl.when(factorable)
    def _():
        run(True)

    @pl.when(jnp.logical_not(factorable))
    def _():
        run(False)


def _gla(proj, small, lr_up, lr_bias, batch, dk, dv, qkv_blocks, reverse):
    n = proj.shape[0]
    tp = n // batch
    nc = tp // CHUNK
    qb, kb, vb = qkv_blocks
    view = lambda a: a.reshape(batch, tp, a.shape[-1])
    out = pl.pallas_call(
        functools.partial(_gla_kernel, reverse=reverse, dk=dk, dv=dv, nc=nc, batch=batch),
        grid=(nc,),
        in_specs=[_chunk_spec(batch, HEADS * dk, qb, nc, reverse), _chunk_spec(batch, HEADS * dk, kb, nc, reverse),
                  _chunk_spec(batch, HEADS * dv, vb, nc, reverse), _chunk_spec(batch, LANES, 0, nc, reverse),
                  pl.BlockSpec((B_RANK, HEADS * dk), lambda i: (0, 0)),
                  pl.BlockSpec((1, HEADS * dk), lambda i: (0, 0))],
        out_specs=_chunk_spec(batch, HEADS * dv, 0, nc, reverse),
        out_shape=jax.ShapeDtypeStruct((batch, tp, HEADS * dv), BF16),
        scratch_shapes=[pltpu.VMEM((batch * HEADS, dv, dk), F32)],
        compiler_params=_params("arbitrary"),
        name="gla_bwd" if reverse else "gla_fwd")(
            view(proj), view(proj), view(proj), view(small), lr_up.astype(F32), lr_bias.reshape(1, -1).astype(F32))
    return out.reshape(n, HEADS * dv)


def _head_norm(x, g, dv):
    parts = []
    for h in range(HEADS):
        xh = x[:, h * dv:(h + 1) * dv]
        parts.append(xh * lax.rsqrt(jnp.mean(xh * xh, axis=-1, keepdims=True) + EPS))
    return jnp.concatenate(parts, axis=1) * g


def _even_combine_kernel(af_ref, ab_ref, bf_ref, bb_ref, oa_ref, gb_ref, na_ref, nb_ref, o_ref, *, tr, tp, dv):
    rows = (pl.program_id(0) * tr) % tp + lax.broadcasted_iota(jnp.int32, (tr, 1), 0)
    valid = rows >= PAD
    w = HEADS * dv
    ha = af_ref[...].astype(F32) + ab_ref[...].astype(F32)
    ya = jax.nn.sigmoid(oa_ref[...].astype(F32)) * _head_norm(ha, na_ref[...], dv)
    o_ref[:, :w] = jnp.where(valid, ya, 0.0).astype(o_ref.dtype)
    hb = bf_ref[...].astype(F32) + bb_ref[...].astype(F32)
    gb = gb_ref[...].astype(F32)
    yb = gb * jax.nn.sigmoid(gb) * _head_norm(hb, nb_ref[...], dv)
    o_ref[:, w:] = jnp.where(valid, yb, 0.0).astype(o_ref.dtype)


def _even_combine(ha_f, ha_b, hb_f, hb_b, proj, norm_a, norm_b, batch, dv, oa_block, gb_block):
    n, w = ha_f.shape
    tp = n // batch
    tr = _tile(tp, 384, BF16_ROWS)
    row = pl.BlockSpec((tr, w), lambda i: (i, 0))
    vec = pl.BlockSpec((1, w), lambda i: (0, 0))
    return pl.pallas_call(
        functools.partial(_even_combine_kernel, tr=tr, tp=tp, dv=dv),
        grid=(n // tr,),
        in_specs=[row, row, row, row, pl.BlockSpec((tr, w), lambda i: (i, oa_block)),
                  pl.BlockSpec((tr, w), lambda i: (i, gb_block)), vec, vec],
        out_specs=pl.BlockSpec((tr, 2 * w), lambda i: (i, 0)),
        out_shape=jax.ShapeDtypeStruct((n, 2 * w), BF16),
        compiler_params=_params("parallel"), name="even_combine")(
            ha_f, ha_b, hb_f, hb_b, proj, proj, norm_a.reshape(1, w).astype(F32), norm_b.reshape(1, w).astype(F32))


def _block_scan(a, b, reverse):
    sub = lax.broadcasted_iota(jnp.int32, a.shape, 1)
    for k in (1, 2, 4):
        if reverse:
            a_sh, b_sh, m = pltpu.roll(a, SUBLANES - k, 1), pltpu.roll(b, SUBLANES - k, 1), sub < SUBLANES - k
        else:
            a_sh, b_sh, m = pltpu.roll(a, k, 1), pltpu.roll(b, k, 1), sub >= k
        b = jnp.where(m, a * b_sh + b, b)
        a = jnp.where(m, a * a_sh, a)
    return a, b


def _rglru_kernel(cur_ref, prev_ref, next_ref, cw_ref, cb_ref, wr_ref, br_ref, wi_ref, bi_ref, lam_ref,
                  o_ref, ext_s, a_s, b_s, h_s, carry_s, *, reverse, tt, nt):
    step = pl.program_id(2)

    @pl.when(step == 0)
    def _():
        carry_s[...] = jnp.zeros_like(carry_s)

    t = nt - 1 - step if reverse else step
    row0 = t * tt
    _fill_ext(ext_s, cur_ref, prev_ref, next_ref, row0, tt, t < nt - 1)
    u = _conv_taps(ext_s, cw_ref, tt) + cb_ref[...]
    ub = u.astype(BF16)
    tr_ = jnp.tanh(_dot(ub, wr_ref[0]) + 0.5 * br_ref[...])
    ti_ = jnp.tanh(_dot(ub, wi_ref[0]) + 0.5 * bi_ref[...])
    lam = lam_ref[...]
    softplus = jnp.maximum(-lam, 0.0) + jnp.log1p(jnp.exp(-jnp.abs(lam)))
    half_c = (-0.5 * RNN_C) * softplus
    log_a = half_c * tr_ + half_c
    a = jnp.exp(log_a)
    rows = row0 + lax.broadcasted_iota(jnp.int32, (tt, 1), 0)
    inp = jnp.where(rows >= PAD, (0.5 * jnp.sqrt(1.0 - a * a) * u) * (ti_ + 1.0), 0.0)
    c = a.shape[1]
    ng = tt // SUBLANES
    a_g, b_g = _block_scan(a.reshape(ng, SUBLANES, c), inp.reshape(ng, SUBLANES, c), reverse)
    a_s[...] = a_g.reshape(tt, c)
    b_s[...] = b_g.reshape(tt, c)
    out_row = 0 if reverse else SUBLANES - 1

    def body(i, carry):
        g = ng - 1 - i if reverse else i
        r0 = pl.multiple_of(g * SUBLANES, SUBLANES)
        hh = b_s[pl.ds(r0, SUBLANES), :] + a_s[pl.ds(r0, SUBLANES), :] * carry
        h_s[pl.ds(r0, SUBLANES), :] = hh
        return hh[out_row:out_row + 1, :]

    carry_s[...] = lax.fori_loop(0, ng, body, carry_s[...])
    o_ref[...] = h_s[...].astype(o_ref.dtype)


def _pair_blocks(w):
    nb, r, _ = w.shape
    z = jnp.zeros((nb // 2, r, r), w.dtype)
    top = jnp.concatenate([w[0::2], z], axis=2)
    bot = jnp.concatenate([z, w[1::2]], axis=2)
    return (0.5 * jnp.concatenate([top, bot], axis=1)).astype(BF16)


def _rglru(proj, conv_w, conv_b, w_r, b_r, w_i, b_i, lam, batch, d_rnn, reverse):
    n = proj.shape[0]
    tp = n // batch
    cw = 2 * d_rnn // RNN_BLOCKS
    ncb = d_rnn // cw
    tt = _tile(tp, 688, BF16_ROWS)
    nt = tp // tt
    tmap = (lambda t: nt - 1 - t) if reverse else (lambda t: t)
    cur, prev, nxt = _halo_specs(cw, lambda j: ncb + j, tt, tp, n)(tmap)
    def swap(spec):
        f = spec.index_map
        return pl.BlockSpec(spec.block_shape, lambda b, j, t: f(b, t, j))
    vec = pl.BlockSpec((1, cw), lambda b, j, t: (0, j))
    wspec = pl.BlockSpec((1, cw, cw), lambda b, j, t: (j, 0, 0))
    row = lambda x: x.reshape(1, d_rnn).astype(F32)
    return pl.pallas_call(
        functools.partial(_rglru_kernel, reverse=reverse, tt=tt, nt=nt),
        grid=(batch, ncb, nt),
        in_specs=[swap(cur), swap(prev), swap(nxt), pl.BlockSpec((4, cw), lambda b, j, t: (0, j)), vec,
                  wspec, vec, wspec, vec, vec],
        out_specs=pl.BlockSpec((tt, cw), lambda b, j, t: (b * nt + tmap(t), j)),
        out_shape=jax.ShapeDtypeStruct((n, d_rnn), BF16),
        scratch_shapes=[pltpu.VMEM((tt + 2 * SUBLANES, cw), F32), pltpu.VMEM((tt, cw), F32),
                        pltpu.VMEM((tt, cw), F32), pltpu.VMEM((tt, cw), F32), pltpu.VMEM((1, cw), F32)],
        compiler_params=_params("parallel", "parallel", "arbitrary"),
        name="rglru_bwd" if reverse else "rglru_fwd")(
            proj, proj, proj, conv_w.astype(F32), row(conv_b), _pair_blocks(w_r), row(b_r),
            _pair_blocks(w_i), row(b_i), row(lam))


def _odd_combine_kernel(g_ref, hf_ref, hb_ref, o_ref, *, tr, tp):
    rows = (pl.program_id(0) * tr) % tp + lax.broadcasted_iota(jnp.int32, (tr, 1), 0)
    y = jax.nn.gelu(g_ref[...].astype(F32)) * (hf_ref[...].astype(F32) + hb_ref[...].astype(F32))
    o_ref[...] = jnp.where(rows >= PAD, y, 0.0).astype(o_ref.dtype)


def _odd_combine(proj, hf, hb, batch):
    n, w = hf.shape
    tp = n // batch
    tr = _tile(tp, 384, BF16_ROWS)
    row = pl.BlockSpec((tr, w), lambda i: (i, 0))
    return pl.pallas_call(
        functools.partial(_odd_combine_kernel, tr=tr, tp=tp), grid=(n // tr,),
        in_specs=[row, row, row], out_specs=row, out_shape=jax.ShapeDtypeStruct((n, w), BF16),
        compiler_params=_params("parallel"), name="odd_combine")(proj, hf, hb)


def _route(x):
    lane = lax.broadcasted_iota(jnp.int32, x.shape, 1)
    big = jnp.int32(2 * LANES)
    gmask = lane < N_GROUPS
    gmax = jnp.max(jnp.where(gmask, x, -jnp.inf), axis=1, keepdims=True)
    ge = jnp.where(gmask, jnp.exp(x - gmax), 0.0)
    gp = ge / jnp.sum(ge, axis=1, keepdims=True)
    gval = jnp.max(gp, axis=1, keepdims=True)
    gidx = jnp.min(jnp.where(jnp.logical_and(gmask, gp == gval), lane, big), axis=1, keepdims=True)
    lo = N_GROUPS + gidx * EXPERTS_PER_GROUP
    emask = jnp.logical_and(lane >= lo, lane < lo + EXPERTS_PER_GROUP)
    emax = jnp.max(jnp.where(emask, x, -jnp.inf), axis=1, keepdims=True)
    ee = jnp.where(emask, jnp.exp(x - emax), 0.0)
    ep = ee / jnp.sum(ee, axis=1, keepdims=True)
    v1 = jnp.max(jnp.where(emask, ep, -1.0), axis=1, keepdims=True)
    i1 = jnp.min(jnp.where(jnp.logical_and(emask, ep == v1), lane, big), axis=1, keepdims=True)
    rest = jnp.logical_and(emask, lane != i1)
    v2 = jnp.max(jnp.where(rest, ep, -1.0), axis=1, keepdims=True)
    i2 = jnp.min(jnp.where(jnp.logical_and(rest, ep == v2), lane, big), axis=1, keepdims=True)
    tot = v1 + v2
    comb = jnp.where(lane == i1, v1 / tot * gval, jnp.where(lane == i2, v2 / tot * gval, 0.0))
    return comb, gidx


def _pack_pair(hi, lo):
    bits = lambda v: lax.bitcast_convert_type(v.astype(BF16).astype(F32), jnp.uint32)
    return bits(hi) | (bits(lo) >> 16)


def _unpack_pair(w):
    hi = lax.bitcast_convert_type(w & jnp.uint32(0xFFFF0000), F32)
    lo = lax.bitcast_convert_type(w << 16, F32)
    return hi.astype(BF16), lo.astype(BF16)


def _store_packed_rows(ref, x, npack):
    half = npack * LANES
    for j in range(npack):
        ref[:, j, :] = _pack_pair(x[:, j * LANES:(j + 1) * LANES], x[:, half + j * LANES:half + (j + 1) * LANES])


def _load_packed_rows(src, dense_s, dst_ref, npack):
    half = npack * LANES
    for j in range(npack):
        dense_s[:, j * LANES:(j + 1) * LANES] = src(j)
    for j in range(npack):
        hi, lo = _unpack_pair(dense_s[:, j * LANES:(j + 1) * LANES])
        dst_ref[:, j * LANES:(j + 1) * LANES] = hi.astype(dst_ref.dtype)
        dst_ref[:, half + j * LANES:half + (j + 1) * LANES] = lo.astype(dst_ref.dtype)


def _norm_route_kernel(h_ref, g_ref, w2_ref, wh_ref, b_ref, slab_ref, oh_ref, *, npack):
    x = h_ref[...]
    y = x * lax.rsqrt(jnp.mean(x * x, axis=-1, keepdims=True) + EPS) * g_ref[...]
    yh = y.astype(BF16)
    yl = (y - yh.astype(F32)).astype(BF16)
    r1 = _dot(yh, w2_ref[...])
    logits = r1[:, :LANES] + r1[:, LANES:] + _dot(yl, wh_ref[...]) + b_ref[...]
    comb, gidx = _route(logits)
    _store_packed_rows(slab_ref, y, npack)
    slab_ref[:, npack, :] = lax.bitcast_convert_type(comb, jnp.uint32)
    for j in range(npack + 1, slab_ref.shape[1]):
        slab_ref[:, j, :] = jnp.zeros(comb.shape, jnp.uint32)
    lane = lax.broadcasted_iota(jnp.int32, comb.shape, 1)
    oh_ref[...] = jnp.where(lane == gidx, 1.0, 0.0).astype(oh_ref.dtype)


def _norm_route(h, g, wg, bg, we, be):
    n, d = h.shape
    npack = d // (2 * LANES)
    srows = (npack + 1 + SUBLANES - 1) // SUBLANES * SUBLANES
    tr = _tile(n, 192, BF16_ROWS)
    zpad = LANES - N_GROUPS - N_EXPERTS
    wr = jnp.concatenate([wg, we, jnp.zeros((d, zpad), F32)], axis=1)
    wh = wr.astype(BF16)
    wl = (wr - wh.astype(F32)).astype(BF16)
    bias = jnp.concatenate([bg.astype(F32), be.astype(F32), jnp.zeros((zpad,), F32)]).reshape(1, LANES)
    return pl.pallas_call(
        functools.partial(_norm_route_kernel, npack=npack), grid=(n // tr,),
        in_specs=[pl.BlockSpec((tr, d), lambda i: (i, 0)), pl.BlockSpec((1, d), lambda i: (0, 0)),
                  pl.BlockSpec((d, 2 * LANES), lambda i: (0, 0)), pl.BlockSpec((d, LANES), lambda i: (0, 0)),
                  pl.BlockSpec((1, LANES), lambda i: (0, 0))],
        out_specs=[pl.BlockSpec((tr, srows, LANES), lambda i: (i, 0, 0)), pl.BlockSpec((tr, LANES), lambda i: (i, 0))],
        out_shape=[jax.ShapeDtypeStruct((n, srows, LANES), jnp.uint32), jax.ShapeDtypeStruct((n, LANES), BF16)],
        compiler_params=_params("parallel"), name="norm_route")(
            h, g.reshape(1, d).astype(F32), jnp.concatenate([wh, wl], axis=1), wh, bias)


def _rank_kernel(oh_ref, g_ref, rank_ref, cnt_ref, carry_s):
    @pl.when(pl.program_id(0) == 0)
    def _():
        carry_s[...] = jnp.zeros_like(carry_s)

    tr = oh_ref.shape[0]
    sel = jnp.where(lax.broadcasted_iota(jnp.int32, (SUBLANES, LANES), 0) ==
                    lax.broadcasted_iota(jnp.int32, (SUBLANES, LANES), 1), 1.0, 0.0).astype(BF16)
    oh_t = _dot_t(sel, oh_ref[...])
    before = (lax.broadcasted_iota(jnp.int32, (tr, tr), 0) < lax.broadcasted_iota(jnp.int32, (tr, tr), 1))
    cum = _dot(oh_t.astype(BF16), jnp.where(before, 1.0, 0.0).astype(BF16)) + carry_s[:, 0:1]
    gid = lax.broadcasted_iota(jnp.int32, (SUBLANES, tr), 0).astype(F32)
    rank_ref[0] = jnp.sum(oh_t * cum, axis=0, keepdims=True).astype(jnp.int32)
    g_ref[0] = jnp.sum(oh_t * gid, axis=0, keepdims=True).astype(jnp.int32)
    carry_s[...] = carry_s[...] + jnp.sum(oh_t, axis=1, keepdims=True)
    cnt_ref[...] = carry_s[...]


def _rank(onehot):
    n = onehot.shape[0]
    tr = _tile(n, 384, LANES)
    row = pl.BlockSpec((1, 1, tr), lambda i: (i, 0, 0))
    g, rank, cnt = pl.pallas_call(
        _rank_kernel, grid=(n // tr,), in_specs=[pl.BlockSpec((tr, LANES), lambda i: (i, 0))],
        out_specs=[row, row, pl.BlockSpec((SUBLANES, LANES), lambda i: (0, 0))],
        out_shape=[jax.ShapeDtypeStruct((n // tr, 1, tr), jnp.int32), jax.ShapeDtypeStruct((n // tr, 1, tr), jnp.int32),
                   jax.ShapeDtypeStruct((SUBLANES, LANES), F32)],
        scratch_shapes=[pltpu.VMEM((SUBLANES, LANES), F32)],
        compiler_params=_params("arbitrary"), name="rank")(onehot)
    return g.reshape(n), rank.reshape(n), cnt[:N_GROUPS, 0].astype(jnp.int32)


def _invert_kernel(g_ref, rank_ref, cnt_ref, pos_ref, idx_ref, tg_ref, *, n, tm, ntiles):
    bases = [jnp.int32(0)]
    for g in range(N_GROUPS - 1):
        bases.append(bases[-1] + (cnt_ref[g] + tm - 1) // tm * tm)

    def zero(i, c):
        idx_ref[i] = 0
        return c
    lax.fori_loop(0, ntiles * tm, zero, 0, unroll=8)

    def place(t, c):
        g = g_ref[t]
        base = bases[0]
        for k in range(1, N_GROUPS):
            base = jnp.where(g == k, bases[k], base)
        p = base + rank_ref[t]
        pos_ref[t] = p
        idx_ref[p] = t
        return c
    lax.fori_loop(0, n, place, 0, unroll=8)

    def tile_group(i, c):
        r = i * tm
        tg = jnp.int32(0)
        for k in range(1, N_GROUPS):
            tg = tg + (r >= bases[k]).astype(jnp.int32)
        tg_ref[i] = tg
        return c
    lax.fori_loop(0, ntiles, tile_group, 0)


def _invert(g, rank, cnt, tm, ntiles):
    n = g.shape[0]
    smem = pl.BlockSpec(memory_space=pltpu.SMEM)
    return pl.pallas_call(
        functools.partial(_invert_kernel, n=n, tm=tm, ntiles=ntiles),
        in_specs=[smem, smem, smem], out_specs=[smem, smem, smem],
        out_shape=[jax.ShapeDtypeStruct((n,), jnp.int32), jax.ShapeDtypeStruct((ntiles * tm,), jnp.int32),
                   jax.ShapeDtypeStruct((ntiles,), jnp.int32)],
        name="invert")(g, rank, cnt)


def _row_copy(src_hbm, buf, sem, src_row, slot, dst_row):
    return pltpu.make_async_copy(src_hbm.at[pl.ds(src_row, 1)], buf.at[slot, pl.ds(dst_row, 1)], sem.at[slot])


def _gather_rows(index_ref, src_hbm, buf, sem, rows):
    i = pl.program_id(0)
    steps = pl.num_programs(0)

    def issue(step, slot):
        def body(r, c):
            _row_copy(src_hbm, buf, sem, index_ref[step * rows + r], slot, r).start()
            return c
        lax.fori_loop(0, rows, body, 0, unroll=8)

    @pl.when(i == 0)
    def _():
        issue(0, 0)

    @pl.when(i + 1 < steps)
    def _():
        issue(i + 1, (i + 1) % 2)

    slot = i % 2

    def wait(r, c):
        _row_copy(src_hbm, buf, sem, 0, slot, r).wait()
        return c
    lax.fori_loop(0, rows, wait, 0, unroll=8)
    return slot


def _dispatch_kernel(idx_ref, slab_hbm, xs_ref, cs_ref, buf, dense_s, sem, *, npack):
    slot = _gather_rows(idx_ref, slab_hbm, buf, sem, xs_ref.shape[0])
    _load_packed_rows(lambda j: buf[slot, :, j, :], dense_s, xs_ref, npack)
    cs_ref[...] = lax.bitcast_convert_type(buf[slot, :, npack, :], F32)


def _dispatch(idx, slab, tm, ntiles, d):
    srows = slab.shape[1]
    return pl.pallas_call(
        functools.partial(_dispatch_kernel, npack=d // (2 * LANES)),
        grid_spec=pltpu.PrefetchScalarGridSpec(
            num_scalar_prefetch=1, grid=(ntiles,),
            in_specs=[pl.BlockSpec(memory_space=pl.ANY)],
            out_specs=[pl.BlockSpec((tm, d), lambda i, idx: (i, 0)), pl.BlockSpec((tm, LANES), lambda i, idx: (i, 0))],
            scratch_shapes=[pltpu.VMEM((2, tm, srows, LANES), jnp.uint32), pltpu.VMEM((tm, d // 2), jnp.uint32),
                            pltpu.SemaphoreType.DMA((2,))]),
        out_shape=[jax.ShapeDtypeStruct((ntiles * tm, d), BF16), jax.ShapeDtypeStruct((ntiles * tm, LANES), F32)],
        compiler_params=_params("arbitrary"), name="dispatch")(idx, slab)


def _expert_kernel(tg_ref, x_ref, c_ref, w13_ref, w2_ref, o_ref, acc_s, *, npack):
    i = pl.program_id(0)
    e = pl.program_id(1)

    @pl.when(e == 0)
    def _():
        acc_s[...] = jnp.zeros_like(acc_s)

    f = w2_ref.shape[1]
    h13 = _dot(x_ref[...], w13_ref[0])
    h1, h3 = h13[:, :f], h13[:, f:]
    comb = c_ref[...]
    lane = lax.broadcasted_iota(jnp.int32, comb.shape, 1)
    c = jnp.sum(jnp.where(lane == N_GROUPS + tg_ref[i] * EXPERTS_PER_GROUP + e, comb, 0.0), axis=1, keepdims=True)
    hid = (h1 * jax.nn.sigmoid(h1) * h3 * c).astype(BF16)
    acc_s[...] += _dot(hid, w2_ref[0])

    @pl.when(e == EXPERTS_PER_GROUP - 1)
    def _():
        _store_packed_rows(o_ref, acc_s, npack)


def _experts(tg, xs, cs, w13, w2, tm, layer):
    rows, d = xs.shape
    f = w2.shape[1]
    npack = d // (2 * LANES)
    wmap = lambda i, e, tg: (layer * N_EXPERTS + tg[i] * EXPERTS_PER_GROUP + e, 0, 0)
    return pl.pallas_call(
        functools.partial(_expert_kernel, npack=npack),
        grid_spec=pltpu.PrefetchScalarGridSpec(
            num_scalar_prefetch=1, grid=(rows // tm, EXPERTS_PER_GROUP),
            in_specs=[pl.BlockSpec((tm, d), lambda i, e, tg: (i, 0)), pl.BlockSpec((tm, LANES), lambda i, e, tg: (i, 0)),
                      pl.BlockSpec((1, d, 2 * f), wmap), pl.BlockSpec((1, f, d), wmap)],
            out_specs=pl.BlockSpec((tm, npack, LANES), lambda i, e, tg: (i, 0, 0)),
            scratch_shapes=[pltpu.VMEM((tm, d), F32)]),
        out_shape=jax.ShapeDtypeStruct((rows, npack, LANES), jnp.uint32),
        compiler_params=_params("parallel", "arbitrary"), name="experts")(tg, xs, cs, w13, w2)


def _collect_kernel(pos_ref, ys_hbm, o_ref, buf, dense_s, sem, *, npack):
    slot = _gather_rows(pos_ref, ys_hbm, buf, sem, o_ref.shape[0])
    _load_packed_rows(lambda j: buf[slot, :, j, :], dense_s, o_ref, npack)


def _collect_norm_kernel(pos_ref, ys_hbm, h_ref, g_ref, hnew_ref, hn_ref, buf, dense_s, delta_s, sem, *, npack):
    slot = _gather_rows(pos_ref, ys_hbm, buf, sem, h_ref.shape[0])
    _load_packed_rows(lambda j: buf[slot, :, j, :], dense_s, delta_s, npack)
    x = h_ref[...] + delta_s[...].astype(F32)
    hnew_ref[...] = x
    y = x * lax.rsqrt(jnp.mean(x * x, axis=-1, keepdims=True) + EPS) * g_ref[...]
    hn_ref[...] = y.astype(hn_ref.dtype)


def _collect(pos, ys, h=None, g=None):
    n = pos.shape[0]
    _, npack, _ = ys.shape
    d = 2 * npack * LANES
    tr = _tile(n, 384, BF16_ROWS)
    row = pl.BlockSpec((tr, d), lambda i, pos: (i, 0))
    scratch = [pltpu.VMEM((2, tr, npack, LANES), jnp.uint32), pltpu.VMEM((tr, d // 2), jnp.uint32)]
    sem = pltpu.SemaphoreType.DMA((2,))
    if h is None:
        return pl.pallas_call(
            functools.partial(_collect_kernel, npack=npack),
            grid_spec=pltpu.PrefetchScalarGridSpec(
                num_scalar_prefetch=1, grid=(n // tr,), in_specs=[pl.BlockSpec(memory_space=pl.ANY)],
                out_specs=row, scratch_shapes=scratch + [sem]),
            out_shape=jax.ShapeDtypeStruct((n, d), BF16),
            compiler_params=_params("arbitrary"), name="collect")(pos, ys)
    return pl.pallas_call(
        functools.partial(_collect_norm_kernel, npack=npack),
        grid_spec=pltpu.PrefetchScalarGridSpec(
            num_scalar_prefetch=1, grid=(n // tr,),
            in_specs=[pl.BlockSpec(memory_space=pl.ANY), row, pl.BlockSpec((1, d), lambda i, pos: (0, 0))],
            out_specs=[row, row], scratch_shapes=scratch + [pltpu.VMEM((tr, d), BF16), sem]),
        out_shape=[jax.ShapeDtypeStruct((n, d), F32), jax.ShapeDtypeStruct((n, d), BF16)],
        compiler_params=_params("arbitrary"), name="collect_norm")(pos, ys, h, g.reshape(1, d).astype(F32))


def _moe_layer(h, ffn_g, wg, bg, we, be, w13, w2, layer, next_norm=None):
    n, d = h.shape
    tm = 512 if n >= 4096 else 128
    ntiles = (n + N_GROUPS * (tm - 1)) // tm
    slab, onehot = _norm_route(h, ffn_g, wg, bg, we, be)
    g, rank, cnt = _rank(onehot)
    pos, idx, tg = _invert(g, rank, cnt, tm, ntiles)
    xs, cs = _dispatch(idx, slab, tm, ntiles, d)
    ys = _experts(tg, xs, cs, w13, w2, tm, layer)
    if next_norm is None:
        return _collect(pos, ys)
    return _collect(pos, ys, h, next_norm)


def _even_layer(h, hn, w_in, gate_bias, qk_conv, lr_up, lr_bias, norm_a, norm_b, w_out, batch):
    n, d = h.shape
    dk, dv = d // 16, d // 8
    qk_w, v_w = HEADS * dk, HEADS * dv
    a_end = 2 * qk_w + 2 * v_w
    b_start = a_end + GATE_COLS
    b_end = b_start + 2 * qk_w + 2 * v_w
    w_main, w_small = _split_cast(w_in, a_end, b_start, b_end)
    proj = _matmul(hn, w_main, BF16, tm_target=1376)
    small = _matmul(hn, w_small, F32, tn_target=LANES)
    gates_t = small[:, :GATE_COLS].reshape(n // CHUNK, CHUNK, GATE_COLS).transpose(0, 2, 1)
    qk = _qk_conv(proj, qk_conv, batch, dk)
    va_blk, oa_blk = 2 * qk_w // v_w, (2 * qk_w + v_w) // v_w
    b0 = a_end
    qb_blk, kb_blk = b0 // qk_w, (b0 + qk_w) // qk_w
    vb_blk, gb_blk = (b0 + 2 * qk_w) // v_w, (b0 + 2 * qk_w + v_w) // v_w
    ha, hb = [], []
    for rev in (False, True):
        ha.append(_mlstm(qk, proj, small, gates_t, gate_bias, batch, dk, dv, va_blk, rev))
        hb.append(_gla(proj, small, lr_up[int(rev)], lr_bias[int(rev)], batch, dk, dv, (qb_blk, kb_blk, vb_blk), rev))
    y = _even_combine(ha[0], ha[1], hb[0], hb[1], proj, norm_a, norm_b, batch, dv, oa_blk, gb_blk)
    return _matmul(y, _cast_bf16(w_out), F32, res=h)


def _odd_layer(h, hn, w_in, conv_w, conv_b, w_r, b_r, w_i, b_i, lam, w_out, batch):
    d_rnn = w_out.shape[0]
    proj = _matmul(hn, _cast_bf16(w_in), BF16, tm_target=1376)
    hs = [_rglru(proj, conv_w, conv_b, w_r[i], b_r[i], w_i[i], b_i[i], lam[i], batch, d_rnn, bool(i)) for i in (0, 1)]
    y = _odd_combine(proj, hs[0], hs[1], batch)
    return _matmul(y, _cast_bf16(w_out), F32, res=h)


def kernel(x, meta_tokens, mix_norm, ffn_norm, final_norm, ev_w_in, ev_gate_bias, ev_qk_conv, ev_lr_up, ev_lr_bias, ev_norm_a, ev_norm_b, ev_w_out, od_w_in, od_conv, od_conv_bias, od_w_r, od_b_r, od_w_i, od_b_i, od_lambda, od_w_out, moe_wg, moe_bg, moe_we, moe_be, moe_w1, moe_w3, moe_w2):
    batch, seq, d = x.shape
    depth = mix_norm.shape[0]
    assert seq % CHUNK == 0 and d % 16 == 0
    f = moe_w1.shape[-1]
    w13 = _cast_pair_bf16(moe_w1, moe_w3).reshape(depth * N_EXPERTS, d, 2 * f)
    w2 = _cast_bf16(moe_w2).reshape(depth * N_EXPERTS, f, d)
    h, hn = _frame_norm(x, meta_tokens, mix_norm[0])
    for layer in range(depth):
        if layer % 2 == 0:
            e = layer // 2
            h = _even_layer(h, hn, ev_w_in[e], ev_gate_bias[e], ev_qk_conv[e], ev_lr_up[e],
                            ev_lr_bias[e], ev_norm_a[e], ev_norm_b[e], ev_w_out[e], batch)
        else:
            o = layer // 2
            h = _odd_layer(h, hn, od_w_in[o], od_conv[o], od_conv_bias[o], od_w_r[o], od_b_r[o],
                           od_w_i[o], od_b_i[o], od_lambda[o], od_w_out[o], batch)
        moe_args = (h, ffn_norm[layer], moe_wg[layer], moe_bg[layer], moe_we[layer], moe_be[layer], w13, w2, layer)
        if layer + 1 < depth:
            h, hn = _moe_layer(*moe_args, next_norm=mix_norm[layer + 1])
        else:
            delta = _moe_layer(*moe_args)
    out = _final_norm(h, delta, final_norm, batch, seq)
    return out.reshape(batch, seq, d)
```

```python
import functools

import jax
import jax.numpy as jnp
from jax import lax
from jax.experimental import pallas as pl
from jax.experimental.pallas import tpu as pltpu

F32 = jnp.float32
BF16 = jnp.bfloat16
HIGHEST = lax.Precision.HIGHEST

N_META = 16
CHUNK = 128
PAD = CHUNK - N_META
SUB = 16
EPS = 1e-6
NEG = -1e30
HEADS = 4
GATE_COLS = 4 * HEADS
B_RANK = 16
B_TAU = 16.0
GLA_MAX_CHUNK_DECAY = 80.0
RNN_BLOCKS = 16
RNN_C = 8.0
N_GROUPS = 4
EXPERTS_PER_GROUP = 8
N_EXPERTS = N_GROUPS * EXPERTS_PER_GROUP
LANES = 128
SUBLANES = 8
BF16_ROWS = 16
VMEM_LIMIT = 56 * 1024 * 1024


def _params(*sem):
    return pltpu.CompilerParams(dimension_semantics=sem, vmem_limit_bytes=VMEM_LIMIT)


def _tile(n, target, mult):
    best = None
    for t in range(mult, min(n, target) + 1, mult):
        if n % t == 0:
            best = t
    assert best is not None, (n, target, mult)
    return best


def _log_sigmoid(x):
    return jnp.minimum(x, 0.0) - jnp.log1p(jnp.exp(-jnp.abs(x)))


def _split3(x):
    hi = x.astype(BF16)
    r1 = x - hi.astype(F32)
    mid = r1.astype(BF16)
    lo = (r1 - mid.astype(F32)).astype(BF16)
    return hi, mid, lo


def _dot(a, b):
    return jnp.dot(a, b, preferred_element_type=F32)


def _dot_t(a, b):
    return lax.dot_general(a, b, (((1,), (1,)), ((), ())), preferred_element_type=F32)


def _tdot(a, b, precision=None):
    return lax.dot_general(a, b, (((0,), (0,)), ((), ())), preferred_element_type=F32,
                           precision=precision)


def _frame_norm_kernel(x_ref, meta_ref, g_ref, h_ref, hn_ref):
    tr, d = h_ref.shape
    xb = x_ref[...]
    first = jnp.concatenate([jnp.zeros((PAD, d), F32), meta_ref[...]] + ([xb[:tr - CHUNK]] if tr > CHUNK else []), axis=0)
    x = jnp.where(pl.program_id(1) == 0, first, xb)
    h_ref[...] = x
    y = x * lax.rsqrt(jnp.mean(x * x, axis=-1, keepdims=True) + EPS) * g_ref[...]
    hn_ref[...] = y.astype(hn_ref.dtype)


def _frame_norm(x, meta, g):
    batch, seq, d = x.shape
    tp = PAD + N_META + seq
    nc = tp // CHUNK
    tr = CHUNK * _tile(nc, min(3, seq // CHUNK), 1)
    nt = tp // tr
    out = pl.BlockSpec((tr, d), lambda b, i: (b * nt + i, 0))
    src = pl.BlockSpec((pl.Element(tr), pl.Element(d)),
                       lambda b, i: (pl.multiple_of(b * seq + jnp.maximum(i * tr - CHUNK, 0), CHUNK), 0))
    return pl.pallas_call(
        _frame_norm_kernel, grid=(batch, nt),
        in_specs=[src, pl.BlockSpec((N_META, d), lambda b, i: (0, 0)), pl.BlockSpec((1, d), lambda b, i: (0, 0))],
        out_specs=[out, out],
        out_shape=[jax.ShapeDtypeStruct((batch * tp, d), F32), jax.ShapeDtypeStruct((batch * tp, d), BF16)],
        compiler_params=_params("parallel", "parallel"), name="frame_norm")(
            x.reshape(batch * seq, d), meta.astype(F32), g.reshape(1, d).astype(F32))


def _final_norm_kernel(h_ref, d_ref, g_ref, o_ref):
    x = h_ref[...] + d_ref[...].astype(F32)
    o_ref[...] = x * lax.rsqrt(jnp.mean(x * x, axis=-1, keepdims=True) + EPS) * g_ref[...]


def _final_norm(h, delta, g, batch, seq):
    n, d = h.shape
    tp = n // batch
    tr = _tile(seq, 512, BF16_ROWS)
    nt = seq // tr
    src = pl.BlockSpec((pl.Element(tr), pl.Element(d)),
                       lambda b, i: (pl.multiple_of(b * tp + CHUNK + i * tr, CHUNK), 0))
    return pl.pallas_call(
        _final_norm_kernel, grid=(batch, nt),
        in_specs=[src, src, pl.BlockSpec((1, d), lambda b, i: (0, 0))],
        out_specs=pl.BlockSpec((tr, d), lambda b, i: (b * nt + i, 0)),
        out_shape=jax.ShapeDtypeStruct((batch * seq, d), F32),
        compiler_params=_params("parallel", "parallel"), name="final_norm")(h, delta, g.reshape(1, d).astype(F32))


def _cast_kernel(w_ref, o_ref):
    o_ref[...] = w_ref[...].astype(o_ref.dtype)


def _cast_bf16(w, block_bytes=8 * 1024 * 1024):
    shape = w.shape
    w2 = w.reshape(-1, shape[-1])
    rows, cols = w2.shape
    tr = _tile(rows, max(BF16_ROWS, block_bytes // (4 * cols)), BF16_ROWS)
    spec = pl.BlockSpec((tr, cols), lambda i: (i, 0))
    out = pl.pallas_call(
        _cast_kernel, grid=(rows // tr,), in_specs=[spec], out_specs=spec,
        out_shape=jax.ShapeDtypeStruct((rows, cols), BF16),
        compiler_params=_params("parallel"), name="cast_bf16")(w2)
    return out.reshape(shape)


def _cast_pair_kernel(a_ref, b_ref, o_ref):
    f = a_ref.shape[1]
    o_ref[:, :f] = a_ref[...].astype(o_ref.dtype)
    o_ref[:, f:] = b_ref[...].astype(o_ref.dtype)


def _cast_pair_bf16(a, b, block_bytes=4 * 1024 * 1024):
    shape = a.shape
    f = shape[-1]
    a2, b2 = a.reshape(-1, f), b.reshape(-1, f)
    rows = a2.shape[0]
    tr = _tile(rows, max(BF16_ROWS, block_bytes // (4 * f)), BF16_ROWS)
    spec = pl.BlockSpec((tr, f), lambda i: (i, 0))
    out = pl.pallas_call(
        _cast_pair_kernel, grid=(rows // tr,), in_specs=[spec, spec],
        out_specs=pl.BlockSpec((tr, 2 * f), lambda i: (i, 0)),
        out_shape=jax.ShapeDtypeStruct((rows, 2 * f), BF16),
        compiler_params=_params("parallel"), name="cast_pair_bf16")(a2, b2)
    return out.reshape(shape[:-1] + (2 * f,))


def _split_cast_kernel(w_ref, main_ref, small_ref, *, a_end, b_start, b_end):
    w = w_ref[...]
    main_ref[:, :a_end] = w[:, :a_end].astype(main_ref.dtype)
    main_ref[:, a_end:] = w[:, b_start:b_end].astype(main_ref.dtype)
    rows = w.shape[0]
    narrow = jnp.concatenate([w[:, a_end:b_start], w[:, b_end:]], axis=1)
    pad = jnp.zeros((rows, small_ref.shape[1] - narrow.shape[1]), F32)
    small_ref[...] = jnp.concatenate([narrow, pad], axis=1).astype(small_ref.dtype)


def _split_cast(w_in, a_end, b_start, b_end):
    d, cols = w_in.shape
    tr = _tile(d, 256, BF16_ROWS)
    wide = a_end + b_end - b_start
    return pl.pallas_call(
        functools.partial(_split_cast_kernel, a_end=a_end, b_start=b_start, b_end=b_end), grid=(d // tr,),
        in_specs=[pl.BlockSpec((tr, cols), lambda i: (i, 0))],
        out_specs=[pl.BlockSpec((tr, wide), lambda i: (i, 0)), pl.BlockSpec((tr, LANES), lambda i: (i, 0))],
        out_shape=[jax.ShapeDtypeStruct((d, wide), BF16), jax.ShapeDtypeStruct((d, LANES), BF16)],
        compiler_params=_params("parallel"), name="split_cast")(w_in)


def _mm_kernel(*refs, has_res):
    if has_res:
        a_ref, w_ref, r_ref, o_ref = refs
    else:
        a_ref, w_ref, o_ref = refs
    acc = _dot(a_ref[...], w_ref[...])
    if has_res:
        acc = acc + r_ref[...]
    o_ref[...] = acc.astype(o_ref.dtype)


def _matmul(a, w, out_dtype, res=None, tm_target=1376, tn_target=512):
    n, k = a.shape
    m = w.shape[1]
    tm = _tile(n, tm_target, BF16_ROWS)
    tn = _tile(m, tn_target, LANES)
    in_specs = [pl.BlockSpec((tm, k), lambda i, j: (i, 0)), pl.BlockSpec((k, tn), lambda i, j: (0, j))]
    args = [a, w]
    if res is not None:
        in_specs.append(pl.BlockSpec((tm, tn), lambda i, j: (i, j)))
        args.append(res)
    return pl.pallas_call(
        functools.partial(_mm_kernel, has_res=res is not None),
        grid=(n // tm, m // tn), in_specs=in_specs,
        out_specs=pl.BlockSpec((tm, tn), lambda i, j: (i, j)),
        out_shape=jax.ShapeDtypeStruct((n, m), out_dtype),
        compiler_params=_params("parallel", "arbitrary"), name="matmul")(*args)


def _conv_taps(ext_s, cw_ref, tt):
    out = cw_ref[0:1, :] * ext_s[pl.ds(SUBLANES - 2, tt), :]
    for j in range(1, 4):
        out = out + cw_ref[j:j + 1, :] * ext_s[pl.ds(SUBLANES - 2 + j, tt), :]
    return out


def _fill_ext(ext_s, cur_ref, prev_ref, next_ref, row0, tt, has_next):
    rows = row0 + lax.broadcasted_iota(jnp.int32, (tt, 1), 0)
    ext_s[pl.ds(SUBLANES, tt), :] = jnp.where(rows >= PAD, cur_ref[...].astype(F32), 0.0)
    prow = row0 - SUBLANES + lax.broadcasted_iota(jnp.int32, (SUBLANES, 1), 0)
    ext_s[pl.ds(0, SUBLANES), :] = jnp.where(prow >= PAD, prev_ref[...].astype(F32)[SUBLANES:, :], 0.0)
    ext_s[pl.ds(SUBLANES + tt, SUBLANES), :] = jnp.where(has_next, next_ref[...].astype(F32)[:SUBLANES, :], 0.0)


def _qkconv_kernel(cur_ref, prev_ref, next_ref, cw_ref, o_ref, ext_s, *, tt, nt, kscale, half):
    t = pl.program_id(1)
    row0 = t * tt
    _fill_ext(ext_s, cur_ref, prev_ref, next_ref, row0, tt, t < nt - 1)
    y = _conv_taps(ext_s, cw_ref, tt)
    y = y * jax.nn.sigmoid(y)
    col = lax.broadcasted_iota(jnp.int32, (1, 2 * half), 1)
    y = y * jnp.where(col >= half, kscale, 1.0)
    rows = row0 + lax.broadcasted_iota(jnp.int32, (tt, 1), 0)
    o_ref[...] = jnp.where(rows >= PAD, y, 0.0).astype(o_ref.dtype)


def _halo_specs(width, col_block, tt, tp, n):
    per_b, per_t = tp // BF16_ROWS, tt // BF16_ROWS
    last = n // BF16_ROWS - 1

    def make(tmap):
        cur = pl.BlockSpec((tt, width), lambda b, t, *_: (b * (tp // tt) + tmap(t), col_block(*_)))
        prev = pl.BlockSpec((BF16_ROWS, width),
                            lambda b, t, *_: (jnp.maximum(b * per_b + tmap(t) * per_t - 1, 0), col_block(*_)))
        nxt = pl.BlockSpec((BF16_ROWS, width),
                           lambda b, t, *_: (jnp.minimum(b * per_b + (tmap(t) + 1) * per_t, last), col_block(*_)))
        return cur, prev, nxt
    return make


def _qk_conv(proj, conv_w, batch, dk):
    n = proj.shape[0]
    tp = n // batch
    width = 2 * HEADS * dk
    tt = _tile(tp, 688, BF16_ROWS)
    nt = tp // tt
    cur, prev, nxt = _halo_specs(width, lambda: 0, tt, tp, n)(lambda t: t)
    return pl.pallas_call(
        functools.partial(_qkconv_kernel, tt=tt, nt=nt, kscale=dk ** -0.5, half=HEADS * dk),
        grid=(batch, nt),
        in_specs=[cur, prev, nxt, pl.BlockSpec((4, width), lambda b, t: (0, 0))],
        out_specs=pl.BlockSpec((tt, width), lambda b, t: (b * nt + t, 0)),
        out_shape=jax.ShapeDtypeStruct((n, width), BF16),
        scratch_shapes=[pltpu.VMEM((tt + 2 * SUBLANES, width), F32)],
        compiler_params=_params("parallel", "parallel"), name="qk_conv")(proj, proj, proj, conv_w.astype(F32))


def _mlstm_kernel(q_ref, k_ref, v_ref, g_ref, gt_ref, gb_ref, gbt_ref, o_ref, c_s, m_s, *, reverse, dk, dv, nc, batch):
    step = pl.program_id(0)

    @pl.when(step == 0)
    def _():
        c_s[...] = jnp.zeros_like(c_s)
        m_s[...] = jnp.zeros_like(m_s)

    chunk = nc - 1 - step if reverse else step
    L = CHUNK
    real = chunk > 0
    valid_c = jnp.logical_or(real, lax.broadcasted_iota(jnp.int32, (L, 1), 0) >= PAD)
    valid_r = jnp.logical_or(real, lax.broadcasted_iota(jnp.int32, (1, L), 1) >= PAD)
    off = 2 * HEADS if reverse else 0
    ri = lax.broadcasted_iota(jnp.int32, (L, L), 0)
    ci = lax.broadcasted_iota(jnp.int32, (L, L), 1)
    mask = (ci >= ri) if reverse else (ci <= ri)
    inc = (ri >= ci) if reverse else (ri <= ci)
    last = 0 if reverse else L - 1
    ones_col = jnp.where(lax.broadcasted_iota(jnp.int32, (L, LANES), 1) == 0, 1.0, 0.0).astype(BF16)
    lane_pad = [jnp.zeros((dk, -L % LANES), BF16)] if L % LANES else []
    score_w = L + -L % LANES

    lf_c, lf_r, li_r = [], [], []
    for b in range(batch):
        g = g_ref[b, :, :GATE_COLS] + gb_ref[...]
        gt = gt_ref[b, 0] + gbt_ref[...]
        lf_c.append(jnp.where(valid_c, _log_sigmoid(g), 0.0))
        lf_r.append(jnp.where(valid_r, _log_sigmoid(gt), 0.0))
        li_r.append(jnp.where(valid_r, gt[off:off + HEADS, :], NEG))
    terms_c = [t for x in lf_c for t in _split3(x)]
    sums_c = _dot(mask.astype(BF16), jnp.concatenate(terms_c, axis=1))
    terms_r = [t for x in lf_r for t in _split3(x)]
    sums_r = _dot(jnp.concatenate(terms_r, axis=0), inc.astype(BF16))
    w = GATE_COLS

    for b in range(batch):
        cum_c = sum(sums_c[:, (3 * b + j) * w:(3 * b + j + 1) * w] for j in range(3))
        cum_r = sum(sums_r[(3 * b + j) * w:(3 * b + j + 1) * w, :] for j in range(3))
        for h in range(HEADS):
            sh = b * HEADS + h
            col = off + HEADS + h
            cc = cum_c[:, col:col + 1]
            cr = cum_r[col:col + 1, :]
            lir = li_r[b][h:h + 1, :]
            tot = cc[last:last + 1, :]
            m = m_s[sh, 0:1, 0:1]
            qh = q_ref[b, :, h * dk:(h + 1) * dk]
            k_t = k_ref[b, :, h * dk:(h + 1) * dk].astype(F32).T
            vh = jnp.where(valid_c, v_ref[b, :, h * dv:(h + 1) * dv], 0.0).astype(BF16)
            vaug = jnp.concatenate([vh, ones_col], axis=1)

            rhs = jnp.concatenate([k_t.astype(BF16)] + lane_pad + [c_s[sh].astype(BF16)], axis=1)
            qkc = _dot(qh, rhs)
            d_mat = jnp.where(mask, cc - cr + lir, NEG)
            inter = cc + m
            m_t = jnp.maximum(inter, jnp.max(d_mat, axis=1, keepdims=True))
            w_inter = jnp.exp(inter - m_t)
            s = qkc[:, :L] * jnp.exp(d_mat - m_t)

            gs = tot - cr + lir
            m_new = jnp.maximum(tot + m, jnp.max(gs, axis=1, keepdims=True))
            decay = jnp.exp(tot + m - m_new)
            ks_t = (k_t * jnp.exp(gs - m_new)).astype(BF16)
            sv = _dot(jnp.concatenate([s.astype(BF16), ks_t], axis=0), vaug)
            haug = w_inter * qkc[:, score_w:] + sv[:L]
            den = haug[:, dv:dv + 1]
            o_ref[b, :, h * dv:(h + 1) * dv] = (
                haug[:, :dv] / jnp.maximum(jnp.abs(den), jnp.exp(-m_t))).astype(o_ref.dtype)
            c_s[sh] = decay * c_s[sh] + sv[L:]
            m_s[sh] = jnp.broadcast_to(m_new, m_s.shape[1:])


def _chunk_spec(batch, width, col_block, nc, reverse):
    cidx = (lambda i: nc - 1 - i) if reverse else (lambda i: i)
    return pl.BlockSpec((batch, CHUNK, width), lambda i: (0, cidx(i), col_block))


def _mlstm(qk, proj, gates, gates_t, gate_bias, batch, dk, dv, v_block, reverse):
    n = qk.shape[0]
    tp = n // batch
    nc = tp // CHUNK
    cidx = (lambda i: nc - 1 - i) if reverse else (lambda i: i)
    gb = gate_bias.reshape(1, GATE_COLS).astype(F32)
    view = lambda a: a.reshape(batch, tp, a.shape[-1])
    out = pl.pallas_call(
        functools.partial(_mlstm_kernel, reverse=reverse, dk=dk, dv=dv, nc=nc, batch=batch),
        grid=(nc,),
        in_specs=[_chunk_spec(batch, HEADS * dk, 0, nc, reverse), _chunk_spec(batch, HEADS * dk, 1, nc, reverse),
                  _chunk_spec(batch, HEADS * dv, v_block, nc, reverse), _chunk_spec(batch, LANES, 0, nc, reverse),
                  pl.BlockSpec((batch, 1, GATE_COLS, CHUNK), lambda i: (0, cidx(i), 0, 0)),
                  pl.BlockSpec((1, GATE_COLS), lambda i: (0, 0)),
                  pl.BlockSpec((GATE_COLS, 1), lambda i: (0, 0))],
        out_specs=_chunk_spec(batch, HEADS * dv, 0, nc, reverse),
        out_shape=jax.ShapeDtypeStruct((batch, tp, HEADS * dv), BF16),
        scratch_shapes=[pltpu.VMEM((batch * HEADS, dk, dv + LANES), F32),
                        pltpu.VMEM((batch * HEADS, SUBLANES, LANES), F32)],
        compiler_params=_params("arbitrary"),
        name="mlstm_bwd" if reverse else "mlstm_fwd")(
            view(qk), view(qk), view(proj), view(gates), gates_t.reshape(batch, nc, GATE_COLS, CHUNK), gb,
            gb.reshape(GATE_COLS, 1))
    return out.reshape(n, HEADS * dv)


def _gla_head_exact(q, k, v, cumh, state, o_ref, b, h, *, reverse, dk, dv):
    L = CHUNK
    nsub = L // SUB
    sub_lane = lax.broadcasted_iota(jnp.int32, (SUB, L), 1)
    sub_row = lax.broadcasted_iota(jnp.int32, (SUB, 1), 0)
    o_inter = _dot_t((q * jnp.exp(cumh)).astype(BF16), state.astype(BF16))
    for blk in range(nsub):
        r0 = blk * SUB
        if reverse:
            cs = cumh[r0 + SUB:r0 + SUB + 1, :] if blk < nsub - 1 else jnp.zeros((1, dk), F32)
            earlier = sub_lane >= r0 + SUB
        else:
            cs = cumh[r0 - 1:r0, :] if blk > 0 else jnp.zeros((1, dk), F32)
            earlier = sub_lane < r0
        q_b = q[r0:r0 + SUB, :]
        cum_b = cumh[r0:r0 + SUB, :]
        qd = (q_b * jnp.exp(cum_b - cs)).astype(BF16)
        kd = (k * jnp.exp(jnp.minimum(cs - cumh, 0.0))).astype(BF16)
        att = jnp.where(earlier, _dot_t(qd, kd), 0.0)
        for j in range(SUB):
            s_idx = r0 + j
            tmask = (sub_row <= j) if reverse else (sub_row >= j)
            e = jnp.where(tmask, cum_b - cumh[s_idx:s_idx + 1, :], NEG)
            col = jnp.sum(q_b * k[s_idx:s_idx + 1, :] * jnp.exp(e), axis=1, keepdims=True)
            att = jnp.where(sub_lane == s_idx, col, att)
        o_b = o_inter[r0:r0 + SUB, :] + _dot(att.astype(BF16), v)
        o_ref[b, r0:r0 + SUB, h * dv:(h + 1) * dv] = o_b.astype(o_ref.dtype)


def _gla_head_factored(q, k, v, cumh, state, mask, o_ref, b, h, *, dv):
    qe = (q * jnp.exp(cumh)).astype(BF16)
    ke = (k * jnp.exp(-cumh)).astype(BF16)
    att = jnp.where(mask, _dot_t(qe, ke), 0.0)
    o = _dot_t(qe, state.astype(BF16)) + _dot(att.astype(BF16), v)
    o_ref[b, :, h * dv:(h + 1) * dv] = o.astype(o_ref.dtype)


def _gla_kernel(q_ref, k_ref, v_ref, lr_ref, up_ref, ub_ref, o_ref, s_s, *, reverse, dk, dv, nc, batch):
    step = pl.program_id(0)

    @pl.when(step == 0)
    def _():
        s_s[...] = jnp.zeros_like(s_s)

    chunk = nc - 1 - step if reverse else step
    L = CHUNK
    valid_c = jnp.logical_or(chunk > 0, lax.broadcasted_iota(jnp.int32, (L, 1), 0) >= PAD)
    off = GATE_COLS + (B_RANK if reverse else 0)
    ri = lax.broadcasted_iota(jnp.int32, (L, L), 0)
    ci = lax.broadcasted_iota(jnp.int32, (L, L), 1)
    mask = (ci >= ri) if reverse else (ci <= ri)
    last = 0 if reverse else L - 1
    cums = []
    for b in range(batch):
        z = jnp.dot(lr_ref[b, :, off:off + B_RANK], up_ref[...], precision=HIGHEST, preferred_element_type=F32)
        la = jnp.where(valid_c, _log_sigmoid(z + ub_ref[...]) / B_TAU, 0.0)
        cums.append(sum(_dot(mask.astype(BF16), t) for t in _split3(la)))
    lowest = jnp.min(jnp.concatenate([c[last:last + 1, :] for c in cums], axis=0))
    factorable = lowest >= -GLA_MAX_CHUNK_DECAY

    def run(factored):
        for b in range(batch):
            for h in range(HEADS):
                sl = slice(h * dk, (h + 1) * dk)
                q = jnp.where(valid_c, q_ref[b, :, sl], 0.0).astype(F32) * dk ** -0.5
                k = jnp.where(valid_c, k_ref[b, :, sl], 0.0).astype(F32)
                v = jnp.where(valid_c, v_ref[b, :, h * dv:(h + 1) * dv], 0.0).astype(BF16)
                cumh = cums[b][:, sl]
                tot = cumh[last:last + 1, :]
                state = s_s[b * HEADS + h]
                if factored:
                    _gla_head_factored(q, k, v, cumh, state, mask, o_ref, b, h, dv=dv)
                else:
                    _gla_head_exact(q, k, v, cumh, state, o_ref, b, h, reverse=reverse, dk=dk, dv=dv)
                kdec = (k * jnp.exp(tot - cumh)).astype(BF16)
                s_s[b * HEADS + h] = jnp.exp(tot) * state + _tdot(v, kdec)

    @pl.when(factorable)
    def _():
        run(True)

    @pl.when(jnp.logical_not(factorable))
    def _():
        run(False)


def _gla(proj, small, lr_up, lr_bias, batch, dk, dv, qkv_blocks, reverse):
    n = proj.shape[0]
    tp = n // batch
    nc = tp // CHUNK
    qb, kb, vb = qkv_blocks
    view = lambda a: a.reshape(batch, tp, a.shape[-1])
    out = pl.pallas_call(
        functools.partial(_gla_kernel, reverse=reverse, dk=dk, dv=dv, nc=nc, batch=batch),
        grid=(nc,),
        in_specs=[_chunk_spec(batch, HEADS * dk, qb, nc, reverse), _chunk_spec(batch, HEADS * dk, kb, nc, reverse),
                  _chunk_spec(batch, HEADS * dv, vb, nc, reverse), _chunk_spec(batch, LANES, 0, nc, reverse),
                  pl.BlockSpec((B_RANK, HEADS * dk), lambda i: (0, 0)),
                  pl.BlockSpec((1, HEADS * dk), lambda i: (0, 0))],
        out_specs=_chunk_spec(batch, HEADS * dv, 0, nc, reverse),
        out_shape=jax.ShapeDtypeStruct((batch, tp, HEADS * dv), BF16),
        scratch_shapes=[pltpu.VMEM((batch * HEADS, dv, dk), F32)],
        compiler_params=_params("arbitrary"),
        name="gla_bwd" if reverse else "gla_fwd")(
            view(proj), view(proj), view(proj), view(small), lr_up.astype(F32), lr_bias.reshape(1, -1).astype(F32))
    return out.reshape(n, HEADS * dv)


def _head_norm(x, g, dv):
    parts = []
    for h in range(HEADS):
        xh = x[:, h * dv:(h + 1) * dv]
        parts.append(xh * lax.rsqrt(jnp.mean(xh * xh, axis=-1, keepdims=True) + EPS))
    return jnp.concatenate(parts, axis=1) * g


def _even_combine_kernel(af_ref, ab_ref, bf_ref, bb_ref, oa_ref, gb_ref, na_ref, nb_ref, o_ref, *, tr, tp, dv):
    rows = (pl.program_id(0) * tr) % tp + lax.broadcasted_iota(jnp.int32, (tr, 1), 0)
    valid = rows >= PAD
    w = HEADS * dv
    ha = af_ref[...].astype(F32) + ab_ref[...].astype(F32)
    ya = jax.nn.sigmoid(oa_ref[...].astype(F32)) * _head_norm(ha, na_ref[...], dv)
    o_ref[:, :w] = jnp.where(valid, ya, 0.0).astype(o_ref.dtype)
    hb = bf_ref[...].astype(F32) + bb_ref[...].astype(F32)
    gb = gb_ref[...].astype(F32)
    yb = gb * jax.nn.sigmoid(gb) * _head_norm(hb, nb_ref[...], dv)
    o_ref[:, w:] = jnp.where(valid, yb, 0.0).astype(o_ref.dtype)


def _even_combine(ha_f, ha_b, hb_f, hb_b, proj, norm_a, norm_b, batch, dv, oa_block, gb_block):
    n, w = ha_f.shape
    tp = n // batch
    tr = _tile(tp, 384, BF16_ROWS)
    row = pl.BlockSpec((tr, w), lambda i: (i, 0))
    vec = pl.BlockSpec((1, w), lambda i: (0, 0))
    return pl.pallas_call(
        functools.partial(_even_combine_kernel, tr=tr, tp=tp, dv=dv),
        grid=(n // tr,),
        in_specs=[row, row, row, row, pl.BlockSpec((tr, w), lambda i: (i, oa_block)),
                  pl.BlockSpec((tr, w), lambda i: (i, gb_block)), vec, vec],
        out_specs=pl.BlockSpec((tr, 2 * w), lambda i: (i, 0)),
        out_shape=jax.ShapeDtypeStruct((n, 2 * w), BF16),
        compiler_params=_params("parallel"), name="even_combine")(
            ha_f, ha_b, hb_f, hb_b, proj, proj, norm_a.reshape(1, w).astype(F32), norm_b.reshape(1, w).astype(F32))


def _block_scan(a, b, reverse):
    sub = lax.broadcasted_iota(jnp.int32, a.shape, 1)
    for k in (1, 2, 4):
        if reverse:
            a_sh, b_sh, m = pltpu.roll(a, SUBLANES - k, 1), pltpu.roll(b, SUBLANES - k, 1), sub < SUBLANES - k
        else:
            a_sh, b_sh, m = pltpu.roll(a, k, 1), pltpu.roll(b, k, 1), sub >= k
        b = jnp.where(m, a * b_sh + b, b)
        a = jnp.where(m, a * a_sh, a)
    return a, b


def _rglru_kernel(cur_ref, prev_ref, next_ref, cw_ref, cb_ref, wr_ref, br_ref, wi_ref, bi_ref, lam_ref,
                  o_ref, ext_s, a_s, b_s, h_s, carry_s, *, reverse, tt, nt):
    step = pl.program_id(2)

    @pl.when(step == 0)
    def _():
        carry_s[...] = jnp.zeros_like(carry_s)

    t = nt - 1 - step if reverse else step
    row0 = t * tt
    _fill_ext(ext_s, cur_ref, prev_ref, next_ref, row0, tt, t < nt - 1)
    u = _conv_taps(ext_s, cw_ref, tt) + cb_ref[...]
    ub = u.astype(BF16)
    tr_ = jnp.tanh(_dot(ub, wr_ref[0]) + 0.5 * br_ref[...])
    ti_ = jnp.tanh(_dot(ub, wi_ref[0]) + 0.5 * bi_ref[...])
    lam = lam_ref[...]
    softplus = jnp.maximum(-lam, 0.0) + jnp.log1p(jnp.exp(-jnp.abs(lam)))
    half_c = (-0.5 * RNN_C) * softplus
    log_a = half_c * tr_ + half_c
    a = jnp.exp(log_a)
    rows = row0 + lax.broadcasted_iota(jnp.int32, (tt, 1), 0)
    inp = jnp.where(rows >= PAD, (0.5 * jnp.sqrt(1.0 - a * a) * u) * (ti_ + 1.0), 0.0)
    c = a.shape[1]
    ng = tt // SUBLANES
    a_g, b_g = _block_scan(a.reshape(ng, SUBLANES, c), inp.reshape(ng, SUBLANES, c), reverse)
    a_s[...] = a_g.reshape(tt, c)
    b_s[...] = b_g.reshape(tt, c)
    out_row = 0 if reverse else SUBLANES - 1

    def body(i, carry):
        g = ng - 1 - i if reverse else i
        r0 = pl.multiple_of(g * SUBLANES, SUBLANES)
        hh = b_s[pl.ds(r0, SUBLANES), :] + a_s[pl.ds(r0, SUBLANES), :] * carry
        h_s[pl.ds(r0, SUBLANES), :] = hh
        return hh[out_row:out_row + 1, :]

    carry_s[...] = lax.fori_loop(0, ng, body, carry_s[...])
    o_ref[...] = h_s[...].astype(o_ref.dtype)


def _pair_blocks(w):
    nb, r, _ = w.shape
    z = jnp.zeros((nb // 2, r, r), w.dtype)
    top = jnp.concatenate([w[0::2], z], axis=2)
    bot = jnp.concatenate([z, w[1::2]], axis=2)
    return (0.5 * jnp.concatenate([top, bot], axis=1)).astype(BF16)


def _rglru(proj, conv_w, conv_b, w_r, b_r, w_i, b_i, lam, batch, d_rnn, reverse):
    n = proj.shape[0]
    tp = n // batch
    cw = 2 * d_rnn // RNN_BLOCKS
    ncb = d_rnn // cw
    tt = _tile(tp, 688, BF16_ROWS)
    nt = tp // tt
    tmap = (lambda t: nt - 1 - t) if reverse else (lambda t: t)
    cur, prev, nxt = _halo_specs(cw, lambda j: ncb + j, tt, tp, n)(tmap)
    def swap(spec):
        f = spec.index_map
        return pl.BlockSpec(spec.block_shape, lambda b, j, t: f(b, t, j))
    vec = pl.BlockSpec((1, cw), lambda b, j, t: (0, j))
    wspec = pl.BlockSpec((1, cw, cw), lambda b, j, t: (j, 0, 0))
    row = lambda x: x.reshape(1, d_rnn).astype(F32)
    return pl.pallas_call(
        functools.partial(_rglru_kernel, reverse=reverse, tt=tt, nt=nt),
        grid=(batch, ncb, nt),
        in_specs=[swap(cur), swap(prev), swap(nxt), pl.BlockSpec((4, cw), lambda b, j, t: (0, j)), vec,
                  wspec, vec, wspec, vec, vec],
        out_specs=pl.BlockSpec((tt, cw), lambda b, j, t: (b * nt + tmap(t), j)),
        out_shape=jax.ShapeDtypeStruct((n, d_rnn), BF16),
        scratch_shapes=[pltpu.VMEM((tt + 2 * SUBLANES, cw), F32), pltpu.VMEM((tt, cw), F32),
                        pltpu.VMEM((tt, cw), F32), pltpu.VMEM((tt, cw), F32), pltpu.VMEM((1, cw), F32)],
        compiler_params=_params("parallel", "parallel", "arbitrary"),
        name="rglru_bwd" if reverse else "rglru_fwd")(
            proj, proj, proj, conv_w.astype(F32), row(conv_b), _pair_blocks(w_r), row(b_r),
            _pair_blocks(w_i), row(b_i), row(lam))


def _odd_combine_kernel(g_ref, hf_ref, hb_ref, o_ref, *, tr, tp):
    rows = (pl.program_id(0) * tr) % tp + lax.broadcasted_iota(jnp.int32, (tr, 1), 0)
    y = jax.nn.gelu(g_ref[...].astype(F32)) * (hf_ref[...].astype(F32) + hb_ref[...].astype(F32))
    o_ref[...] = jnp.where(rows >= PAD, y, 0.0).astype(o_ref.dtype)


def _odd_combine(proj, hf, hb, batch):
    n, w = hf.shape
    tp = n // batch
    tr = _tile(tp, 384, BF16_ROWS)
    row = pl.BlockSpec((tr, w), lambda i: (i, 0))
    return pl.pallas_call(
        functools.partial(_odd_combine_kernel, tr=tr, tp=tp), grid=(n // tr,),
        in_specs=[row, row, row], out_specs=row, out_shape=jax.ShapeDtypeStruct((n, w), BF16),
        compiler_params=_params("parallel"), name="odd_combine")(proj, hf, hb)


def _route(x):
    lane = lax.broadcasted_iota(jnp.int32, x.shape, 1)
    big = jnp.int32(2 * LANES)
    gmask = lane < N_GROUPS
    gmax = jnp.max(jnp.where(gmask, x, -jnp.inf), axis=1, keepdims=True)
    ge = jnp.where(gmask, jnp.exp(x - gmax), 0.0)
    gp = ge / jnp.sum(ge, axis=1, keepdims=True)
    gval = jnp.max(gp, axis=1, keepdims=True)
    gidx = jnp.min(jnp.where(jnp.logical_and(gmask, gp == gval), lane, big), axis=1, keepdims=True)
    lo = N_GROUPS + gidx * EXPERTS_PER_GROUP
    emask = jnp.logical_and(lane >= lo, lane < lo + EXPERTS_PER_GROUP)
    emax = jnp.max(jnp.where(emask, x, -jnp.inf), axis=1, keepdims=True)
    ee = jnp.where(emask, jnp.exp(x - emax), 0.0)
    ep = ee / jnp.sum(ee, axis=1, keepdims=True)
    v1 = jnp.max(jnp.where(emask, ep, -1.0), axis=1, keepdims=True)
    i1 = jnp.min(jnp.where(jnp.logical_and(emask, ep == v1), lane, big), axis=1, keepdims=True)
    rest = jnp.logical_and(emask, lane != i1)
    v2 = jnp.max(jnp.where(rest, ep, -1.0), axis=1, keepdims=True)
    i2 = jnp.min(jnp.where(jnp.logical_and(rest, ep == v2), lane, big), axis=1, keepdims=True)
    tot = v1 + v2
    comb = jnp.where(lane == i1, v1 / tot * gval, jnp.where(lane == i2, v2 / tot * gval, 0.0))
    return comb, gidx


def _pack_pair(hi, lo):
    bits = lambda v: lax.bitcast_convert_type(v.astype(BF16).astype(F32), jnp.uint32)
    return bits(hi) | (bits(lo) >> 16)


def _unpack_pair(w):
    hi = lax.bitcast_convert_type(w & jnp.uint32(0xFFFF0000), F32)
    lo = lax.bitcast_convert_type(w << 16, F32)
    return hi.astype(BF16), lo.astype(BF16)


def _store_packed_rows(ref, x, npack):
    half = npack * LANES
    for j in range(npack):
        ref[:, j, :] = _pack_pair(x[:, j * LANES:(j + 1) * LANES], x[:, half + j * LANES:half + (j + 1) * LANES])


def _load_packed_rows(src, dense_s, dst_ref, npack):
    half = npack * LANES
    for j in range(npack):
        dense_s[:, j * LANES:(j + 1) * LANES] = src(j)
    for j in range(npack):
        hi, lo = _unpack_pair(dense_s[:, j * LANES:(j + 1) * LANES])
        dst_ref[:, j * LANES:(j + 1) * LANES] = hi.astype(dst_ref.dtype)
        dst_ref[:, half + j * LANES:half + (j + 1) * LANES] = lo.astype(dst_ref.dtype)


def _norm_route_kernel(h_ref, g_ref, w2_ref, wh_ref, b_ref, slab_ref, oh_ref, *, npack):
    x = h_ref[...]
    y = x * lax.rsqrt(jnp.mean(x * x, axis=-1, keepdims=True) + EPS) * g_ref[...]
    yh = y.astype(BF16)
    yl = (y - yh.astype(F32)).astype(BF16)
    r1 = _dot(yh, w2_ref[...])
    logits = r1[:, :LANES] + r1[:, LANES:] + _dot(yl, wh_ref[...]) + b_ref[...]
    comb, gidx = _route(logits)
    _store_packed_rows(slab_ref, y, npack)
    slab_ref[:, npack, :] = lax.bitcast_convert_type(comb, jnp.uint32)
    for j in range(npack + 1, slab_ref.shape[1]):
        slab_ref[:, j, :] = jnp.zeros(comb.shape, jnp.uint32)
    lane = lax.broadcasted_iota(jnp.int32, comb.shape, 1)
    oh_ref[...] = jnp.where(lane == gidx, 1.0, 0.0).astype(oh_ref.dtype)


def _norm_route(h, g, wg, bg, we, be):
    n, d = h.shape
    npack = d // (2 * LANES)
    srows = (npack + 1 + SUBLANES - 1) // SUBLANES * SUBLANES
    tr = _tile(n, 192, BF16_ROWS)
    zpad = LANES - N_GROUPS - N_EXPERTS
    wr = jnp.concatenate([wg, we, jnp.zeros((d, zpad), F32)], axis=1)
    wh = wr.astype(BF16)
    wl = (wr - wh.astype(F32)).astype(BF16)
    bias = jnp.concatenate([bg.astype(F32), be.astype(F32), jnp.zeros((zpad,), F32)]).reshape(1, LANES)
    return pl.pallas_call(
        functools.partial(_norm_route_kernel, npack=npack), grid=(n // tr,),
        in_specs=[pl.BlockSpec((tr, d), lambda i: (i, 0)), pl.BlockSpec((1, d), lambda i: (0, 0)),
                  pl.BlockSpec((d, 2 * LANES), lambda i: (0, 0)), pl.BlockSpec((d, LANES), lambda i: (0, 0)),
                  pl.BlockSpec((1, LANES), lambda i: (0, 0))],
        out_specs=[pl.BlockSpec((tr, srows, LANES), lambda i: (i, 0, 0)), pl.BlockSpec((tr, LANES), lambda i: (i, 0))],
        out_shape=[jax.ShapeDtypeStruct((n, srows, LANES), jnp.uint32), jax.ShapeDtypeStruct((n, LANES), BF16)],
        compiler_params=_params("parallel"), name="norm_route")(
            h, g.reshape(1, d).astype(F32), jnp.concatenate([wh, wl], axis=1), wh, bias)


def _rank_kernel(oh_ref, g_ref, rank_ref, cnt_ref, carry_s):
    @pl.when(pl.program_id(0) == 0)
    def _():
        carry_s[...] = jnp.zeros_like(carry_s)

    tr = oh_ref.shape[0]
    sel = jnp.where(lax.broadcasted_iota(jnp.int32, (SUBLANES, LANES), 0) ==
                    lax.broadcasted_iota(jnp.int32, (SUBLANES, LANES), 1), 1.0, 0.0).astype(BF16)
    oh_t = _dot_t(sel, oh_ref[...])
    before = (lax.broadcasted_iota(jnp.int32, (tr, tr), 0) < lax.broadcasted_iota(jnp.int32, (tr, tr), 1))
    cum = _dot(oh_t.astype(BF16), jnp.where(before, 1.0, 0.0).astype(BF16)) + carry_s[:, 0:1]
    gid = lax.broadcasted_iota(jnp.int32, (SUBLANES, tr), 0).astype(F32)
    rank_ref[0] = jnp.sum(oh_t * cum, axis=0, keepdims=True).astype(jnp.int32)
    g_ref[0] = jnp.sum(oh_t * gid, axis=0, keepdims=True).astype(jnp.int32)
    carry_s[...] = carry_s[...] + jnp.sum(oh_t, axis=1, keepdims=True)
    cnt_ref[...] = carry_s[...]


def _rank(onehot):
    n = onehot.shape[0]
    tr = _tile(n, 384, LANES)
    row = pl.BlockSpec((1, 1, tr), lambda i: (i, 0, 0))
    g, rank, cnt = pl.pallas_call(
        _rank_kernel, grid=(n // tr,), in_specs=[pl.BlockSpec((tr, LANES), lambda i: (i, 0))],
        out_specs=[row, row, pl.BlockSpec((SUBLANES, LANES), lambda i: (0, 0))],
        out_shape=[jax.ShapeDtypeStruct((n // tr, 1, tr), jnp.int32), jax.ShapeDtypeStruct((n // tr, 1, tr), jnp.int32),
                   jax.ShapeDtypeStruct((SUBLANES, LANES), F32)],
        scratch_shapes=[pltpu.VMEM((SUBLANES, LANES), F32)],
        compiler_params=_params("arbitrary"), name="rank")(onehot)
    return g.reshape(n), rank.reshape(n), cnt[:N_GROUPS, 0].astype(jnp.int32)


def _invert_kernel(g_ref, rank_ref, cnt_ref, pos_ref, idx_ref, tg_ref, *, n, tm, ntiles):
    bases = [jnp.int32(0)]
    for g in range(N_GROUPS - 1):
        bases.append(bases[-1] + (cnt_ref[g] + tm - 1) // tm * tm)

    def zero(i, c):
        idx_ref[i] = 0
        return c
    lax.fori_loop(0, ntiles * tm, zero, 0, unroll=8)

    def place(t, c):
        g = g_ref[t]
        base = bases[0]
        for k in range(1, N_GROUPS):
            base = jnp.where(g == k, bases[k], base)
        p = base + rank_ref[t]
        pos_ref[t] = p
        idx_ref[p] = t
        return c
    lax.fori_loop(0, n, place, 0, unroll=8)

    def tile_group(i, c):
        r = i * tm
        tg = jnp.int32(0)
        for k in range(1, N_GROUPS):
            tg = tg + (r >= bases[k]).astype(jnp.int32)
        tg_ref[i] = tg
        return c
    lax.fori_loop(0, ntiles, tile_group, 0)


def _invert(g, rank, cnt, tm, ntiles):
    n = g.shape[0]
    smem = pl.BlockSpec(memory_space=pltpu.SMEM)
    return pl.pallas_call(
        functools.partial(_invert_kernel, n=n, tm=tm, ntiles=ntiles),
        in_specs=[smem, smem, smem], out_specs=[smem, smem, smem],
        out_shape=[jax.ShapeDtypeStruct((n,), jnp.int32), jax.ShapeDtypeStruct((ntiles * tm,), jnp.int32),
                   jax.ShapeDtypeStruct((ntiles,), jnp.int32)],
        name="invert")(g, rank, cnt)


def _row_copy(src_hbm, buf, sem, src_row, slot, dst_row):
    return pltpu.make_async_copy(src_hbm.at[pl.ds(src_row, 1)], buf.at[slot, pl.ds(dst_row, 1)], sem.at[slot])


def _gather_rows(index_ref, src_hbm, buf, sem, rows):
    i = pl.program_id(0)
    steps = pl.num_programs(0)

    def issue(step, slot):
        def body(r, c):
            _row_copy(src_hbm, buf, sem, index_ref[step * rows + r], slot, r).start()
            return c
        lax.fori_loop(0, rows, body, 0, unroll=8)

    @pl.when(i == 0)
    def _():
        issue(0, 0)

    @pl.when(i + 1 < steps)
    def _():
        issue(i + 1, (i + 1) % 2)

    slot = i % 2

    def wait(r, c):
        _row_copy(src_hbm, buf, sem, 0, slot, r).wait()
        return c
    lax.fori_loop(0, rows, wait, 0, unroll=8)
    return slot


def _dispatch_kernel(idx_ref, slab_hbm, xs_ref, cs_ref, buf, dense_s, sem, *, npack):
    slot = _gather_rows(idx_ref, slab_hbm, buf, sem, xs_ref.shape[0])
    _load_packed_rows(lambda j: buf[slot, :, j, :], dense_s, xs_ref, npack)
    cs_ref[...] = lax.bitcast_convert_type(buf[slot, :, npack, :], F32)


def _dispatch(idx, slab, tm, ntiles, d):
    srows = slab.shape[1]
    return pl.pallas_call(
        functools.partial(_dispatch_kernel, npack=d // (2 * LANES)),
        grid_spec=pltpu.PrefetchScalarGridSpec(
            num_scalar_prefetch=1, grid=(ntiles,),
            in_specs=[pl.BlockSpec(memory_space=pl.ANY)],
            out_specs=[pl.BlockSpec((tm, d), lambda i, idx: (i, 0)), pl.BlockSpec((tm, LANES), lambda i, idx: (i, 0))],
            scratch_shapes=[pltpu.VMEM((2, tm, srows, LANES), jnp.uint32), pltpu.VMEM((tm, d // 2), jnp.uint32),
                            pltpu.SemaphoreType.DMA((2,))]),
        out_shape=[jax.ShapeDtypeStruct((ntiles * tm, d), BF16), jax.ShapeDtypeStruct((ntiles * tm, LANES), F32)],
        compiler_params=_params("arbitrary"), name="dispatch")(idx, slab)


def _expert_kernel(tg_ref, x_ref, c_ref, w13_ref, w2_ref, o_ref, acc_s, *, npack):
    i = pl.program_id(0)
    e = pl.program_id(1)

    @pl.when(e == 0)
    def _():
        acc_s[...] = jnp.zeros_like(acc_s)

    f = w2_ref.shape[1]
    h13 = _dot(x_ref[...], w13_ref[0])
    h1, h3 = h13[:, :f], h13[:, f:]
    comb = c_ref[...]
    lane = lax.broadcasted_iota(jnp.int32, comb.shape, 1)
    c = jnp.sum(jnp.where(lane == N_GROUPS + tg_ref[i] * EXPERTS_PER_GROUP + e, comb, 0.0), axis=1, keepdims=True)
    hid = (h1 * jax.nn.sigmoid(h1) * h3 * c).astype(BF16)
    acc_s[...] += _dot(hid, w2_ref[0])

    @pl.when(e == EXPERTS_PER_GROUP - 1)
    def _():
        _store_packed_rows(o_ref, acc_s, npack)


def _experts(tg, xs, cs, w13, w2, tm, layer):
    rows, d = xs.shape
    f = w2.shape[1]
    npack = d // (2 * LANES)
    wmap = lambda i, e, tg: (layer * N_EXPERTS + tg[i] * EXPERTS_PER_GROUP + e, 0, 0)
    return pl.pallas_call(
        functools.partial(_expert_kernel, npack=npack),
        grid_spec=pltpu.PrefetchScalarGridSpec(
            num_scalar_prefetch=1, grid=(rows // tm, EXPERTS_PER_GROUP),
            in_specs=[pl.BlockSpec((tm, d), lambda i, e, tg: (i, 0)), pl.BlockSpec((tm, LANES), lambda i, e, tg: (i, 0)),
                      pl.BlockSpec((1, d, 2 * f), wmap), pl.BlockSpec((1, f, d), wmap)],
            out_specs=pl.BlockSpec((tm, npack, LANES), lambda i, e, tg: (i, 0, 0)),
            scratch_shapes=[pltpu.VMEM((tm, d), F32)]),
        out_shape=jax.ShapeDtypeStruct((rows, npack, LANES), jnp.uint32),
        compiler_params=_params("parallel", "arbitrary"), name="experts")(tg, xs, cs, w13, w2)


def _collect_kernel(pos_ref, ys_hbm, o_ref, buf, dense_s, sem, *, npack):
    slot = _gather_rows(pos_ref, ys_hbm, buf, sem, o_ref.shape[0])
    _load_packed_rows(lambda j: buf[slot, :, j, :], dense_s, o_ref, npack)


def _collect_norm_kernel(pos_ref, ys_hbm, h_ref, g_ref, hnew_ref, hn_ref, buf, dense_s, delta_s, sem, *, npack):
    slot = _gather_rows(pos_ref, ys_hbm, buf, sem, h_ref.shape[0])
    _load_packed_rows(lambda j: buf[slot, :, j, :], dense_s, delta_s, npack)
    x = h_ref[...] + delta_s[...].astype(F32)
    hnew_ref[...] = x
    y = x * lax.rsqrt(jnp.mean(x * x, axis=-1, keepdims=True) + EPS) * g_ref[...]
    hn_ref[...] = y.astype(hn_ref.dtype)


def _collect(pos, ys, h=None, g=None):
    n = pos.shape[0]
    _, npack, _ = ys.shape
    d = 2 * npack * LANES
    tr = _tile(n, 384, BF16_ROWS)
    row = pl.BlockSpec((tr, d), lambda i, pos: (i, 0))
    scratch = [pltpu.VMEM((2, tr, npack, LANES), jnp.uint32), pltpu.VMEM((tr, d // 2), jnp.uint32)]
    sem = pltpu.SemaphoreType.DMA((2,))
    if h is None:
        return pl.pallas_call(
            functools.partial(_collect_kernel, npack=npack),
            grid_spec=pltpu.PrefetchScalarGridSpec(
                num_scalar_prefetch=1, grid=(n // tr,), in_specs=[pl.BlockSpec(memory_space=pl.ANY)],
                out_specs=row, scratch_shapes=scratch + [sem]),
            out_shape=jax.ShapeDtypeStruct((n, d), BF16),
            compiler_params=_params("arbitrary"), name="collect")(pos, ys)
    return pl.pallas_call(
        functools.partial(_collect_norm_kernel, npack=npack),
        grid_spec=pltpu.PrefetchScalarGridSpec(
            num_scalar_prefetch=1, grid=(n // tr,),
            in_specs=[pl.BlockSpec(memory_space=pl.ANY), row, pl.BlockSpec((1, d), lambda i, pos: (0, 0))],
            out_specs=[row, row], scratch_shapes=scratch + [pltpu.VMEM((tr, d), BF16), sem]),
        out_shape=[jax.ShapeDtypeStruct((n, d), F32), jax.ShapeDtypeStruct((n, d), BF16)],
        compiler_params=_params("arbitrary"), name="collect_norm")(pos, ys, h, g.reshape(1, d).astype(F32))


def _moe_layer(h, ffn_g, wg, bg, we, be, w13, w2, layer, next_norm=None):
    n, d = h.shape
    tm = 512 if n >= 4096 else 128
    ntiles = (n + N_GROUPS * (tm - 1)) // tm
    slab, onehot = _norm_route(h, ffn_g, wg, bg, we, be)
    g, rank, cnt = _rank(onehot)
    pos, idx, tg = _invert(g, rank, cnt, tm, ntiles)
    xs, cs = _dispatch(idx, slab, tm, ntiles, d)
    ys = _experts(tg, xs, cs, w13, w2, tm, layer)
    if next_norm is None:
        return _collect(pos, ys)
    return _collect(pos, ys, h, next_norm)


def _even_layer(h, hn, w_in, gate_bias, qk_conv, lr_up, lr_bias, norm_a, norm_b, w_out, batch):
    n, d = h.shape
    dk, dv = d // 16, d // 8
    qk_w, v_w = HEADS * dk, HEADS * dv
    a_end = 2 * qk_w + 2 * v_w
    b_start = a_end + GATE_COLS
    b_end = b_start + 2 * qk_w + 2 * v_w
    w_main, w_small = _split_cast(w_in, a_end, b_start, b_end)
    proj = _matmul(hn, w_main, BF16, tn_target=1024)
    small = _matmul(hn, w_small, F32, tn_target=LANES)
    gates_t = small[:, :GATE_COLS].reshape(n // CHUNK, CHUNK, GATE_COLS).transpose(0, 2, 1)
    qk = _qk_conv(proj, qk_conv, batch, dk)
    va_blk, oa_blk = 2 * qk_w // v_w, (2 * qk_w + v_w) // v_w
    b0 = a_end
    qb_blk, kb_blk = b0 // qk_w, (b0 + qk_w) // qk_w
    vb_blk, gb_blk = (b0 + 2 * qk_w) // v_w, (b0 + 2 * qk_w + v_w) // v_w
    ha, hb = [], []
    for rev in (False, True):
        ha.append(_mlstm(qk, proj, small, gates_t, gate_bias, batch, dk, dv, va_blk, rev))
        hb.append(_gla(proj, small, lr_up[int(rev)], lr_bias[int(rev)], batch, dk, dv, (qb_blk, kb_blk, vb_blk), rev))
    y = _even_combine(ha[0], ha[1], hb[0], hb[1], proj, norm_a, norm_b, batch, dv, oa_blk, gb_blk)
    return _matmul(y, _cast_bf16(w_out), F32, res=h)


def _odd_layer(h, hn, w_in, conv_w, conv_b, w_r, b_r, w_i, b_i, lam, w_out, batch):
    d_rnn = w_out.shape[0]
    proj = _matmul(hn, _cast_bf16(w_in), BF16, tn_target=1024)
    hs = [_rglru(proj, conv_w, conv_b, w_r[i], b_r[i], w_i[i], b_i[i], lam[i], batch, d_rnn, bool(i)) for i in (0, 1)]
    y = _odd_combine(proj, hs[0], hs[1], batch)
    return _matmul(y, _cast_bf16(w_out), F32, res=h)


def kernel(x, meta_tokens, mix_norm, ffn_norm, final_norm, ev_w_in, ev_gate_bias, ev_qk_conv, ev_lr_up, ev_lr_bias, ev_norm_a, ev_norm_b, ev_w_out, od_w_in, od_conv, od_conv_bias, od_w_r, od_b_r, od_w_i, od_b_i, od_lambda, od_w_out, moe_wg, moe_bg, moe_we, moe_be, moe_w1, moe_w3, moe_w2):
    batch, seq, d = x.shape
    depth = mix_norm.shape[0]
    assert seq % CHUNK == 0 and d % 16 == 0
    f = moe_w1.shape[-1]
    w13 = _cast_pair_bf16(moe_w1, moe_w3).reshape(depth * N_EXPERTS, d, 2 * f)
    w2 = _cast_bf16(moe_w2).reshape(depth * N_EXPERTS, f, d)
    h, hn = _frame_norm(x, meta_tokens, mix_norm[0])
    for layer in range(depth):
        if layer % 2 == 0:
            e = layer // 2
            h = _even_layer(h, hn, ev_w_in[e], ev_gate_bias[e], ev_qk_conv[e], ev_lr_up[e],
                            ev_lr_bias[e], ev_norm_a[e], ev_norm_b[e], ev_w_out[e], batch)
        else:
            o = layer // 2
            h = _odd_layer(h, hn, od_w_in[o], od_conv[o], od_conv_bias[o], od_w_r[o], od_b_r[o],
                           od_w_i[o], od_b_i[o], od_lambda[o], od_w_out[o], batch)
        moe_args = (h, ffn_norm[layer], moe_wg[layer], moe_bg[layer], moe_we[layer], moe_be[layer], w13, w2, layer)
        if layer + 1 < depth:
            h, hn = _moe_layer(*moe_args, next_norm=mix_norm[layer + 1])
        else:
            delta = _moe_layer(*moe_args)
    out = _final_norm(h, delta, final_norm, batch, seq)
    return out.reshape(batch, seq, d)
```

```python
import functools

import jax
import jax.numpy as jnp
from jax import lax
from jax.experimental import pallas as pl
from jax.experimental.pallas import tpu as pltpu

F32 = jnp.float32
BF16 = jnp.bfloat16
HIGHEST = lax.Precision.HIGHEST

N_META = 16
CHUNK = 128
PAD = CHUNK - N_META
SUB = 16
EPS = 1e-6
NEG = -1e30
HEADS = 4
GATE_COLS = 4 * HEADS
B_RANK = 16
B_TAU = 16.0
GLA_MAX_CHUNK_DECAY = 80.0
RNN_BLOCKS = 16
RNN_C = 8.0
N_GROUPS = 4
EXPERTS_PER_GROUP = 8
N_EXPERTS = N_GROUPS * EXPERTS_PER_GROUP
LANES = 128
SUBLANES = 8
BF16_ROWS = 16
VMEM_LIMIT = 56 * 1024 * 1024


def _params(*sem):
    return pltpu.CompilerParams(dimension_semantics=sem, vmem_limit_bytes=VMEM_LIMIT)


def _tile(n, target, mult):
    best = None
    for t in range(mult, min(n, target) + 1, mult):
        if n % t == 0:
            best = t
    assert best is not None, (n, target, mult)
    return best


def _log_sigmoid(x):
    return jnp.minimum(x, 0.0) - jnp.log1p(jnp.exp(-jnp.abs(x)))


def _split3(x):
    hi = x.astype(BF16)
    r1 = x - hi.astype(F32)
    mid = r1.astype(BF16)
    lo = (r1 - mid.astype(F32)).astype(BF16)
    return hi, mid, lo


def _dot(a, b):
    return jnp.dot(a, b, preferred_element_type=F32)


def _dot_t(a, b):
    return lax.dot_general(a, b, (((1,), (1,)), ((), ())), preferred_element_type=F32)


def _tdot(a, b, precision=None):
    return lax.dot_general(a, b, (((0,), (0,)), ((), ())), preferred_element_type=F32,
                           precision=precision)


def _frame_norm_kernel(x_ref, meta_ref, g_ref, h_ref, hn_ref):
    tr, d = h_ref.shape
    xb = x_ref[...]
    first = jnp.concatenate([jnp.zeros((PAD, d), F32), meta_ref[...]] + ([xb[:tr - CHUNK]] if tr > CHUNK else []), axis=0)
    x = jnp.where(pl.program_id(1) == 0, first, xb)
    h_ref[...] = x
    y = x * lax.rsqrt(jnp.mean(x * x, axis=-1, keepdims=True) + EPS) * g_ref[...]
    hn_ref[...] = y.astype(hn_ref.dtype)


def _frame_norm(x, meta, g):
    batch, seq, d = x.shape
    tp = PAD + N_META + seq
    nc = tp // CHUNK
    tr = CHUNK * _tile(nc, min(3, seq // CHUNK), 1)
    nt = tp // tr
    out = pl.BlockSpec((tr, d), lambda b, i: (b * nt + i, 0))
    src = pl.BlockSpec((pl.Element(tr), pl.Element(d)),
                       lambda b, i: (pl.multiple_of(b * seq + jnp.maximum(i * tr - CHUNK, 0), CHUNK), 0))
    return pl.pallas_call(
        _frame_norm_kernel, grid=(batch, nt),
        in_specs=[src, pl.BlockSpec((N_META, d), lambda b, i: (0, 0)), pl.BlockSpec((1, d), lambda b, i: (0, 0))],
        out_specs=[out, out],
        out_shape=[jax.ShapeDtypeStruct((batch * tp, d), F32), jax.ShapeDtypeStruct((batch * tp, d), BF16)],
        compiler_params=_params("parallel", "parallel"), name="frame_norm")(
            x.reshape(batch * seq, d), meta.astype(F32), g.reshape(1, d).astype(F32))


def _final_norm_kernel(h_ref, d_ref, g_ref, o_ref):
    x = h_ref[...] + d_ref[...].astype(F32)
    o_ref[...] = x * lax.rsqrt(jnp.mean(x * x, axis=-1, keepdims=True) + EPS) * g_ref[...]


def _final_norm(h, delta, g, batch, seq):
    n, d = h.shape
    tp = n // batch
    tr = _tile(seq, 512, BF16_ROWS)
    nt = seq // tr
    src = pl.BlockSpec((pl.Element(tr), pl.Element(d)),
                       lambda b, i: (pl.multiple_of(b * tp + CHUNK + i * tr, CHUNK), 0))
    return pl.pallas_call(
        _final_norm_kernel, grid=(batch, nt),
        in_specs=[src, src, pl.BlockSpec((1, d), lambda b, i: (0, 0))],
        out_specs=pl.BlockSpec((tr, d), lambda b, i: (b * nt + i, 0)),
        out_shape=jax.ShapeDtypeStruct((batch * seq, d), F32),
        compiler_params=_params("parallel", "parallel"), name="final_norm")(h, delta, g.reshape(1, d).astype(F32))


def _cast_kernel(w_ref, o_ref):
    o_ref[...] = w_ref[...].astype(o_ref.dtype)


def _cast_bf16(w, block_bytes=8 * 1024 * 1024):
    shape = w.shape
    w2 = w.reshape(-1, shape[-1])
    rows, cols = w2.shape
    tr = _tile(rows, max(BF16_ROWS, block_bytes // (4 * cols)), BF16_ROWS)
    spec = pl.BlockSpec((tr, cols), lambda i: (i, 0))
    out = pl.pallas_call(
        _cast_kernel, grid=(rows // tr,), in_specs=[spec], out_specs=spec,
        out_shape=jax.ShapeDtypeStruct((rows, cols), BF16),
        compiler_params=_params("parallel"), name="cast_bf16")(w2)
    return out.reshape(shape)


def _cast_pair_kernel(a_ref, b_ref, o_ref):
    f = a_ref.shape[1]
    o_ref[:, :f] = a_ref[...].astype(o_ref.dtype)
    o_ref[:, f:] = b_ref[...].astype(o_ref.dtype)


def _cast_pair_bf16(a, b, block_bytes=4 * 1024 * 1024):
    shape = a.shape
    f = shape[-1]
    a2, b2 = a.reshape(-1, f), b.reshape(-1, f)
    rows = a2.shape[0]
    tr = _tile(rows, max(BF16_ROWS, block_bytes // (4 * f)), BF16_ROWS)
    spec = pl.BlockSpec((tr, f), lambda i: (i, 0))
    out = pl.pallas_call(
        _cast_pair_kernel, grid=(rows // tr,), in_specs=[spec, spec],
        out_specs=pl.BlockSpec((tr, 2 * f), lambda i: (i, 0)),
        out_shape=jax.ShapeDtypeStruct((rows, 2 * f), BF16),
        compiler_params=_params("parallel"), name="cast_pair_bf16")(a2, b2)
    return out.reshape(shape[:-1] + (2 * f,))


def _split_cast_kernel(w_ref, main_ref, small_ref, *, a_end, b_start, b_end):
    w = w_ref[0]
    main_ref[:, :a_end] = w[:, :a_end].astype(main_ref.dtype)
    main_ref[:, a_end:] = w[:, b_start:b_end].astype(main_ref.dtype)
    rows = w.shape[0]
    narrow = jnp.concatenate([w[:, a_end:b_start], w[:, b_end:]], axis=1)
    pad = jnp.zeros((rows, small_ref.shape[1] - narrow.shape[1]), F32)
    small_ref[...] = jnp.concatenate([narrow, pad], axis=1).astype(small_ref.dtype)


def _split_cast(w_all, e, a_end, b_start, b_end):
    _, d, cols = w_all.shape
    tr = _tile(d, 256, BF16_ROWS)
    wide = a_end + b_end - b_start
    return pl.pallas_call(
        functools.partial(_split_cast_kernel, a_end=a_end, b_start=b_start, b_end=b_end), grid=(d // tr,),
        in_specs=[pl.BlockSpec((1, tr, cols), lambda i: (e, i, 0))],
        out_specs=[pl.BlockSpec((tr, wide), lambda i: (i, 0)), pl.BlockSpec((tr, LANES), lambda i: (i, 0))],
        out_shape=[jax.ShapeDtypeStruct((d, wide), BF16), jax.ShapeDtypeStruct((d, LANES), BF16)],
        compiler_params=_params("parallel"), name="split_cast")(w_all)


def _mm_kernel(*refs, has_res):
    if has_res:
        a_ref, w_ref, r_ref, o_ref = refs
    else:
        a_ref, w_ref, o_ref = refs
    acc = _dot(a_ref[...], w_ref[...])
    if has_res:
        acc = acc + r_ref[...]
    o_ref[...] = acc.astype(o_ref.dtype)


def _matmul(a, w, out_dtype, res=None, tm_target=1376, tn_target=512):
    n, k = a.shape
    m = w.shape[1]
    tm = _tile(n, tm_target, BF16_ROWS)
    tn = _tile(m, tn_target, LANES)
    in_specs = [pl.BlockSpec((tm, k), lambda i, j: (i, 0)), pl.BlockSpec((k, tn), lambda i, j: (0, j))]
    args = [a, w]
    if res is not None:
        in_specs.append(pl.BlockSpec((tm, tn), lambda i, j: (i, j)))
        args.append(res)
    return pl.pallas_call(
        functools.partial(_mm_kernel, has_res=res is not None),
        grid=(n // tm, m // tn), in_specs=in_specs,
        out_specs=pl.BlockSpec((tm, tn), lambda i, j: (i, j)),
        out_shape=jax.ShapeDtypeStruct((n, m), out_dtype),
        compiler_params=_params("parallel", "arbitrary"), name="matmul")(*args)


def _conv_taps(ext_s, cw_ref, tt):
    out = cw_ref[0:1, :] * ext_s[pl.ds(SUBLANES - 2, tt), :]
    for j in range(1, 4):
        out = out + cw_ref[j:j + 1, :] * ext_s[pl.ds(SUBLANES - 2 + j, tt), :]
    return out


def _fill_ext(ext_s, cur_ref, prev_ref, next_ref, row0, tt, has_next):
    rows = row0 + lax.broadcasted_iota(jnp.int32, (tt, 1), 0)
    ext_s[pl.ds(SUBLANES, tt), :] = jnp.where(rows >= PAD, cur_ref[...].astype(F32), 0.0)
    prow = row0 - SUBLANES + lax.broadcasted_iota(jnp.int32, (SUBLANES, 1), 0)
    ext_s[pl.ds(0, SUBLANES), :] = jnp.where(prow >= PAD, prev_ref[...].astype(F32)[SUBLANES:, :], 0.0)
    ext_s[pl.ds(SUBLANES + tt, SUBLANES), :] = jnp.where(has_next, next_ref[...].astype(F32)[:SUBLANES, :], 0.0)


def _qkconv_kernel(cur_ref, prev_ref, next_ref, cw_ref, o_ref, ext_s, *, tt, nt, kscale, half):
    t = pl.program_id(1)
    row0 = t * tt
    _fill_ext(ext_s, cur_ref, prev_ref, next_ref, row0, tt, t < nt - 1)
    y = _conv_taps(ext_s, cw_ref, tt)
    y = y * jax.nn.sigmoid(y)
    col = lax.broadcasted_iota(jnp.int32, (1, 2 * half), 1)
    y = y * jnp.where(col >= half, kscale, 1.0)
    rows = row0 + lax.broadcasted_iota(jnp.int32, (tt, 1), 0)
    o_ref[...] = jnp.where(rows >= PAD, y, 0.0).astype(o_ref.dtype)


def _halo_specs(width, col_block, tt, tp, n):
    per_b, per_t = tp // BF16_ROWS, tt // BF16_ROWS
    last = n // BF16_ROWS - 1

    def make(tmap):
        cur = pl.BlockSpec((tt, width), lambda b, t, *_: (b * (tp // tt) + tmap(t), col_block(*_)))
        prev = pl.BlockSpec((BF16_ROWS, width),
                            lambda b, t, *_: (jnp.maximum(b * per_b + tmap(t) * per_t - 1, 0), col_block(*_)))
        nxt = pl.BlockSpec((BF16_ROWS, width),
                           lambda b, t, *_: (jnp.minimum(b * per_b + (tmap(t) + 1) * per_t, last), col_block(*_)))
        return cur, prev, nxt
    return make


def _qk_conv(proj, conv_w, batch, dk):
    n = proj.shape[0]
    tp = n // batch
    width = 2 * HEADS * dk
    tt = _tile(tp, 688, BF16_ROWS)
    nt = tp // tt
    cur, prev, nxt = _halo_specs(width, lambda: 0, tt, tp, n)(lambda t: t)
    return pl.pallas_call(
        functools.partial(_qkconv_kernel, tt=tt, nt=nt, kscale=dk ** -0.5, half=HEADS * dk),
        grid=(batch, nt),
        in_specs=[cur, prev, nxt, pl.BlockSpec((4, width), lambda b, t: (0, 0))],
        out_specs=pl.BlockSpec((tt, width), lambda b, t: (b * nt + t, 0)),
        out_shape=jax.ShapeDtypeStruct((n, width), BF16),
        scratch_shapes=[pltpu.VMEM((tt + 2 * SUBLANES, width), F32)],
        compiler_params=_params("parallel", "parallel"), name="qk_conv")(proj, proj, proj, conv_w.astype(F32))


def _mlstm_kernel(q_ref, k_ref, v_ref, g_ref, gt_ref, gb_ref, gbt_ref, o_ref, c_s, m_s, *, reverse, dk, dv, nc, batch):
    step = pl.program_id(0)

    @pl.when(step == 0)
    def _():
        c_s[...] = jnp.zeros_like(c_s)
        m_s[...] = jnp.zeros_like(m_s)

    chunk = nc - 1 - step if reverse else step
    L = CHUNK
    real = chunk > 0
    valid_c = jnp.logical_or(real, lax.broadcasted_iota(jnp.int32, (L, 1), 0) >= PAD)
    valid_r = jnp.logical_or(real, lax.broadcasted_iota(jnp.int32, (1, L), 1) >= PAD)
    off = 2 * HEADS if reverse else 0
    ri = lax.broadcasted_iota(jnp.int32, (L, L), 0)
    ci = lax.broadcasted_iota(jnp.int32, (L, L), 1)
    mask = (ci >= ri) if reverse else (ci <= ri)
    inc = (ri >= ci) if reverse else (ri <= ci)
    last = 0 if reverse else L - 1
    ones_col = jnp.where(lax.broadcasted_iota(jnp.int32, (L, LANES), 1) == 0, 1.0, 0.0).astype(BF16)
    lane_pad = [jnp.zeros((dk, -L % LANES), BF16)] if L % LANES else []
    score_w = L + -L % LANES

    lf_c, lf_r, li_r = [], [], []
    for b in range(batch):
        g = g_ref[b, :, :GATE_COLS] + gb_ref[...]
        gt = gt_ref[b, 0] + gbt_ref[...]
        lf_c.append(jnp.where(valid_c, _log_sigmoid(g), 0.0))
        lf_r.append(jnp.where(valid_r, _log_sigmoid(gt), 0.0))
        li_r.append(jnp.where(valid_r, gt[off:off + HEADS, :], NEG))
    terms_c = [t for x in lf_c for t in _split3(x)]
    sums_c = _dot(mask.astype(BF16), jnp.concatenate(terms_c, axis=1))
    terms_r = [t for x in lf_r for t in _split3(x)]
    sums_r = _dot(jnp.concatenate(terms_r, axis=0), inc.astype(BF16))
    w = GATE_COLS

    for b in range(batch):
        cum_c = sum(sums_c[:, (3 * b + j) * w:(3 * b + j + 1) * w] for j in range(3))
        cum_r = sum(sums_r[(3 * b + j) * w:(3 * b + j + 1) * w, :] for j in range(3))
        for h in range(HEADS):
            sh = b * HEADS + h
            col = off + HEADS + h
            cc = cum_c[:, col:col + 1]
            cr = cum_r[col:col + 1, :]
            lir = li_r[b][h:h + 1, :]
            tot = cc[last:last + 1, :]
            m = m_s[sh, 0:1, 0:1]
            qh = q_ref[b, :, h * dk:(h + 1) * dk]
            k_t = k_ref[b, :, h * dk:(h + 1) * dk].astype(F32).T
            vh = jnp.where(valid_c, v_ref[b, :, h * dv:(h + 1) * dv], 0.0).astype(BF16)
            vaug = jnp.concatenate([vh, ones_col], axis=1)

            rhs = jnp.concatenate([k_t.astype(BF16)] + lane_pad + [c_s[sh].astype(BF16)], axis=1)
            qkc = _dot(qh, rhs)
            d_mat = jnp.where(mask, cc - cr + lir, NEG)
            inter = cc + m
            m_t = jnp.maximum(inter, jnp.max(d_mat, axis=1, keepdims=True))
            w_inter = jnp.exp(inter - m_t)
            s = qkc[:, :L] * jnp.exp(d_mat - m_t)

            gs = tot - cr + lir
            m_new = jnp.maximum(tot + m, jnp.max(gs, axis=1, keepdims=True))
            decay = jnp.exp(tot + m - m_new)
            ks_t = (k_t * jnp.exp(gs - m_new)).astype(BF16)
            sv = _dot(jnp.concatenate([s.astype(BF16), ks_t], axis=0), vaug)
            haug = w_inter * qkc[:, score_w:] + sv[:L]
            den = haug[:, dv:dv + 1]
            o_ref[b, :, h * dv:(h + 1) * dv] = (
                haug[:, :dv] / jnp.maximum(jnp.abs(den), jnp.exp(-m_t))).astype(o_ref.dtype)
            c_s[sh] = decay * c_s[sh] + sv[L:]
            m_s[sh] = jnp.broadcast_to(m_new, m_s.shape[1:])


def _chunk_spec(batch, width, col_block, nc, reverse):
    cidx = (lambda i: nc - 1 - i) if reverse else (lambda i: i)
    return pl.BlockSpec((batch, CHUNK, width), lambda i: (0, cidx(i), col_block))


def _mlstm(qk, proj, gates, gates_t, gate_bias, batch, dk, dv, v_block, reverse):
    n = qk.shape[0]
    tp = n // batch
    nc = tp // CHUNK
    cidx = (lambda i: nc - 1 - i) if reverse else (lambda i: i)
    gb = gate_bias.reshape(1, GATE_COLS).astype(F32)
    view = lambda a: a.reshape(batch, tp, a.shape[-1])
    out = pl.pallas_call(
        functools.partial(_mlstm_kernel, reverse=reverse, dk=dk, dv=dv, nc=nc, batch=batch),
        grid=(nc,),
        in_specs=[_chunk_spec(batch, HEADS * dk, 0, nc, reverse), _chunk_spec(batch, HEADS * dk, 1, nc, reverse),
                  _chunk_spec(batch, HEADS * dv, v_block, nc, reverse), _chunk_spec(batch, LANES, 0, nc, reverse),
                  pl.BlockSpec((batch, 1, GATE_COLS, CHUNK), lambda i: (0, cidx(i), 0, 0)),
                  pl.BlockSpec((1, GATE_COLS), lambda i: (0, 0)),
                  pl.BlockSpec((GATE_COLS, 1), lambda i: (0, 0))],
        out_specs=_chunk_spec(batch, HEADS * dv, 0, nc, reverse),
        out_shape=jax.ShapeDtypeStruct((batch, tp, HEADS * dv), BF16),
        scratch_shapes=[pltpu.VMEM((batch * HEADS, dk, dv + LANES), F32),
                        pltpu.VMEM((batch * HEADS, SUBLANES, LANES), F32)],
        compiler_params=_params("arbitrary"),
        name="mlstm_bwd" if reverse else "mlstm_fwd")(
            view(qk), view(qk), view(proj), view(gates), gates_t.reshape(batch, nc, GATE_COLS, CHUNK), gb,
            gb.reshape(GATE_COLS, 1))
    return out.reshape(n, HEADS * dv)


def _gla_head_exact(q, k, v, cumh, state, o_ref, b, h, *, reverse, dk, dv):
    L = CHUNK
    nsub = L // SUB
    sub_lane = lax.broadcasted_iota(jnp.int32, (SUB, L), 1)
    sub_row = lax.broadcasted_iota(jnp.int32, (SUB, 1), 0)
    o_inter = _dot_t((q * jnp.exp(cumh)).astype(BF16), state.astype(BF16))
    for blk in range(nsub):
        r0 = blk * SUB
        if reverse:
            cs = cumh[r0 + SUB:r0 + SUB + 1, :] if blk < nsub - 1 else jnp.zeros((1, dk), F32)
            earlier = sub_lane >= r0 + SUB
        else:
            cs = cumh[r0 - 1:r0, :] if blk > 0 else jnp.zeros((1, dk), F32)
            earlier = sub_lane < r0
        q_b = q[r0:r0 + SUB, :]
        cum_b = cumh[r0:r0 + SUB, :]
        qd = (q_b * jnp.exp(cum_b - cs)).astype(BF16)
        kd = (k * jnp.exp(jnp.minimum(cs - cumh, 0.0))).astype(BF16)
        att = jnp.where(earlier, _dot_t(qd, kd), 0.0)
        for j in range(SUB):
            s_idx = r0 + j
            tmask = (sub_row <= j) if reverse else (sub_row >= j)
            e = jnp.where(tmask, cum_b - cumh[s_idx:s_idx + 1, :], NEG)
            col = jnp.sum(q_b * k[s_idx:s_idx + 1, :] * jnp.exp(e), axis=1, keepdims=True)
            att = jnp.where(sub_lane == s_idx, col, att)
        o_b = o_inter[r0:r0 + SUB, :] + _dot(att.astype(BF16), v)
        o_ref[b, r0:r0 + SUB, h * dv:(h + 1) * dv] = o_b.astype(o_ref.dtype)


def _gla_head_factored(q, k, v, cumh, state, mask, o_ref, b, h, *, dv):
    qe = (q * jnp.exp(cumh)).astype(BF16)
    ke = (k * jnp.exp(-cumh)).astype(BF16)
    att = jnp.where(mask, _dot_t(qe, ke), 0.0)
    o = _dot_t(qe, state.astype(BF16)) + _dot(att.astype(BF16), v)
    o_ref[b, :, h * dv:(h + 1) * dv] = o.astype(o_ref.dtype)


def _gla_kernel(q_ref, k_ref, v_ref, lr_ref, up_ref, ub_ref, o_ref, s_s, *, reverse, dk, dv, nc, batch):
    step = pl.program_id(0)

    @pl.when(step == 0)
    def _():
        s_s[...] = jnp.zeros_like(s_s)

    chunk = nc - 1 - step if reverse else step
    L = CHUNK
    valid_c = jnp.logical_or(chunk > 0, lax.broadcasted_iota(jnp.int32, (L, 1), 0) >= PAD)
    off = GATE_COLS + (B_RANK if reverse else 0)
    ri = lax.broadcasted_iota(jnp.int32, (L, L), 0)
    ci = lax.broadcasted_iota(jnp.int32, (L, L), 1)
    mask = (ci >= ri) if reverse else (ci <= ri)
    last = 0 if reverse else L - 1
    cums = []
    for b in range(batch):
        z = jnp.dot(lr_ref[b, :, off:off + B_RANK], up_ref[...], precision=HIGHEST, preferred_element_type=F32)
        la = jnp.where(valid_c, _log_sigmoid(z + ub_ref[...]) / B_TAU, 0.0)
        cums.append(sum(_dot(mask.astype(BF16), t) for t in _split3(la)))
    lowest = jnp.min(jnp.concatenate([c[last:last + 1, :] for c in cums], axis=0))
    factorable = lowest >= -GLA_MAX_CHUNK_DECAY

    def run(factored):
        for b in range(batch):
            for h in range(HEADS):
                sl = slice(h * dk, (h + 1) * dk)
                q = jnp.where(valid_c, q_ref[b, :, sl], 0.0).astype(F32) * dk ** -0.5
                k = jnp.where(valid_c, k_ref[b, :, sl], 0.0).astype(F32)
                v = jnp.where(valid_c, v_ref[b, :, h * dv:(h + 1) * dv], 0.0).astype(BF16)
                cumh = cums[b][:, sl]
                tot = cumh[last:last + 1, :]
                state = s_s[b * HEADS + h]
                if factored:
                    _gla_head_factored(q, k, v, cumh, state, mask, o_ref, b, h, dv=dv)
                else:
                    _gla_head_exact(q, k, v, cumh, state, o_ref, b, h, reverse=reverse, dk=dk, dv=dv)
                kdec = (k * jnp.exp(tot - cumh)).astype(BF16)
                s_s[b * HEADS + h] = jnp.exp(tot) * state + _tdot(v, kdec)

    @pl.when(factorable)
    def _():
        run(True)

    @pl.when(jnp.logical_not(factorable))
    def _():
        run(False)


def _gla(proj, small, lr_up, lr_bias, batch, dk, dv, qkv_blocks, reverse):
    n = proj.shape[0]
    tp = n // batch
    nc = tp // CHUNK
    qb, kb, vb = qkv_blocks
    view = lambda a: a.reshape(batch, tp, a.shape[-1])
    out = pl.pallas_call(
        functools.partial(_gla_kernel, reverse=reverse, dk=dk, dv=dv, nc=nc, batch=batch),
        grid=(nc,),
        in_specs=[_chunk_spec(batch, HEADS * dk, qb, nc, reverse), _chunk_spec(batch, HEADS * dk, kb, nc, reverse),
                  _chunk_spec(batch, HEADS * dv, vb, nc, reverse), _chunk_spec(batch, LANES, 0, nc, reverse),
                  pl.BlockSpec((B_RANK, HEADS * dk), lambda i: (0, 0)),
                  pl.BlockSpec((1, HEADS * dk), lambda i: (0, 0))],
        out_specs=_chunk_spec(batch, HEADS * dv, 0, nc, reverse),
        out_shape=jax.ShapeDtypeStruct((batch, tp, HEADS * dv), BF16),
        scratch_shapes=[pltpu.VMEM((batch * HEADS, dv, dk), F32)],
        compiler_params=_params("arbitrary"),
        name="gla_bwd" if reverse else "gla_fwd")(
            view(proj), view(proj), view(proj), view(small), lr_up.astype(F32), lr_bias.reshape(1, -1).astype(F32))
    return out.reshape(n, HEADS * dv)


def _head_norm(x, g, dv):
    parts = []
    for h in range(HEADS):
        xh = x[:, h * dv:(h + 1) * dv]
        parts.append(xh * lax.rsqrt(jnp.mean(xh * xh, axis=-1, keepdims=True) + EPS))
    return jnp.concatenate(parts, axis=1) * g


def _even_combine_kernel(af_ref, ab_ref, bf_ref, bb_ref, oa_ref, gb_ref, na_ref, nb_ref, o_ref, *, tr, tp, dv):
    rows = (pl.program_id(0) * tr) % tp + lax.broadcasted_iota(jnp.int32, (tr, 1), 0)
    valid = rows >= PAD
    w = HEADS * dv
    ha = af_ref[...].astype(F32) + ab_ref[...].astype(F32)
    ya = jax.nn.sigmoid(oa_ref[...].astype(F32)) * _head_norm(ha, na_ref[...], dv)
    o_ref[:, :w] = jnp.where(valid, ya, 0.0).astype(o_ref.dtype)
    hb = bf_ref[...].astype(F32) + bb_ref[...].astype(F32)
    gb = gb_ref[...].astype(F32)
    yb = gb * jax.nn.sigmoid(gb) * _head_norm(hb, nb_ref[...], dv)
    o_ref[:, w:] = jnp.where(valid, yb, 0.0).astype(o_ref.dtype)


def _even_combine(ha_f, ha_b, hb_f, hb_b, proj, norm_a, norm_b, batch, dv, oa_block, gb_block):
    n, w = ha_f.shape
    tp = n // batch
    tr = _tile(tp, 384, BF16_ROWS)
    row = pl.BlockSpec((tr, w), lambda i: (i, 0))
    vec = pl.BlockSpec((1, w), lambda i: (0, 0))
    return pl.pallas_call(
        functools.partial(_even_combine_kernel, tr=tr, tp=tp, dv=dv),
        grid=(n // tr,),
        in_specs=[row, row, row, row, pl.BlockSpec((tr, w), lambda i: (i, oa_block)),
                  pl.BlockSpec((tr, w), lambda i: (i, gb_block)), vec, vec],
        out_specs=pl.BlockSpec((tr, 2 * w), lambda i: (i, 0)),
        out_shape=jax.ShapeDtypeStruct((n, 2 * w), BF16),
        compiler_params=_params("parallel"), name="even_combine")(
            ha_f, ha_b, hb_f, hb_b, proj, proj, norm_a.reshape(1, w).astype(F32), norm_b.reshape(1, w).astype(F32))


def _block_scan(a, b, reverse):
    sub = lax.broadcasted_iota(jnp.int32, a.shape, 1)
    for k in (1, 2, 4):
        if reverse:
            a_sh, b_sh, m = pltpu.roll(a, SUBLANES - k, 1), pltpu.roll(b, SUBLANES - k, 1), sub < SUBLANES - k
        else:
            a_sh, b_sh, m = pltpu.roll(a, k, 1), pltpu.roll(b, k, 1), sub >= k
        b = jnp.where(m, a * b_sh + b, b)
        a = jnp.where(m, a * a_sh, a)
    return a, b


def _rglru_kernel(cur_ref, prev_ref, next_ref, cw_ref, cb_ref, wr_ref, br_ref, wi_ref, bi_ref, lam_ref,
                  o_ref, ext_s, a_s, b_s, h_s, carry_s, *, reverse, tt, nt):
    step = pl.program_id(2)

    @pl.when(step == 0)
    def _():
        carry_s[...] = jnp.zeros_like(carry_s)

    t = nt - 1 - step if reverse else step
    row0 = t * tt
    _fill_ext(ext_s, cur_ref, prev_ref, next_ref, row0, tt, t < nt - 1)
    u = _conv_taps(ext_s, cw_ref, tt) + cb_ref[...]
    ub = u.astype(BF16)
    tr_ = jnp.tanh(_dot(ub, wr_ref[0]) + 0.5 * br_ref[...])
    ti_ = jnp.tanh(_dot(ub, wi_ref[0]) + 0.5 * bi_ref[...])
    lam = lam_ref[...]
    softplus = jnp.maximum(-lam, 0.0) + jnp.log1p(jnp.exp(-jnp.abs(lam)))
    half_c = (-0.5 * RNN_C) * softplus
    log_a = half_c * tr_ + half_c
    a = jnp.exp(log_a)
    rows = row0 + lax.broadcasted_iota(jnp.int32, (tt, 1), 0)
    inp = jnp.where(rows >= PAD, (0.5 * jnp.sqrt(1.0 - a * a) * u) * (ti_ + 1.0), 0.0)
    c = a.shape[1]
    ng = tt // SUBLANES
    a_g, b_g = _block_scan(a.reshape(ng, SUBLANES, c), inp.reshape(ng, SUBLANES, c), reverse)
    a_s[...] = a_g.reshape(tt, c)
    b_s[...] = b_g.reshape(tt, c)
    out_row = 0 if reverse else SUBLANES - 1

    def body(i, carry):
        g = ng - 1 - i if reverse else i
        r0 = pl.multiple_of(g * SUBLANES, SUBLANES)
        hh = b_s[pl.ds(r0, SUBLANES), :] + a_s[pl.ds(r0, SUBLANES), :] * carry
        h_s[pl.ds(r0, SUBLANES), :] = hh
        return hh[out_row:out_row + 1, :]

    carry_s[...] = lax.fori_loop(0, ng, body, carry_s[...])
    o_ref[...] = h_s[...].astype(o_ref.dtype)


def _pair_blocks(w):
    nb, r, _ = w.shape
    z = jnp.zeros((nb // 2, r, r), w.dtype)
    top = jnp.concatenate([w[0::2], z], axis=2)
    bot = jnp.concatenate([z, w[1::2]], axis=2)
    return (0.5 * jnp.concatenate([top, bot], axis=1)).astype(BF16)


def _rglru(proj, conv_w, conv_b, w_r, b_r, w_i, b_i, lam, batch, d_rnn, reverse):
    n = proj.shape[0]
    tp = n // batch
    cw = 2 * d_rnn // RNN_BLOCKS
    ncb = d_rnn // cw
    tt = _tile(tp, 688, BF16_ROWS)
    nt = tp // tt
    tmap = (lambda t: nt - 1 - t) if reverse else (lambda t: t)
    cur, prev, nxt = _halo_specs(cw, lambda j: ncb + j, tt, tp, n)(tmap)
    def swap(spec):
        f = spec.index_map
        return pl.BlockSpec(spec.block_shape, lambda b, j, t: f(b, t, j))
    vec = pl.BlockSpec((1, cw), lambda b, j, t: (0, j))
    wspec = pl.BlockSpec((1, cw, cw), lambda b, j, t: (j, 0, 0))
    row = lambda x: x.reshape(1, d_rnn).astype(F32)
    return pl.pallas_call(
        functools.partial(_rglru_kernel, reverse=reverse, tt=tt, nt=nt),
        grid=(batch, ncb, nt),
        in_specs=[swap(cur), swap(prev), swap(nxt), pl.BlockSpec((4, cw), lambda b, j, t: (0, j)), vec,
                  wspec, vec, wspec, vec, vec],
        out_specs=pl.BlockSpec((tt, cw), lambda b, j, t: (b * nt + tmap(t), j)),
        out_shape=jax.ShapeDtypeStruct((n, d_rnn), BF16),
        scratch_shapes=[pltpu.VMEM((tt + 2 * SUBLANES, cw), F32), pltpu.VMEM((tt, cw), F32),
                        pltpu.VMEM((tt, cw), F32), pltpu.VMEM((tt, cw), F32), pltpu.VMEM((1, cw), F32)],
        compiler_params=_params("parallel", "parallel", "arbitrary"),
        name="rglru_bwd" if reverse else "rglru_fwd")(
            proj, proj, proj, conv_w.astype(F32), row(conv_b), _pair_blocks(w_r), row(b_r),
            _pair_blocks(w_i), row(b_i), row(lam))


def _odd_combine_kernel(g_ref, hf_ref, hb_ref, o_ref, *, tr, tp):
    rows = (pl.program_id(0) * tr) % tp + lax.broadcasted_iota(jnp.int32, (tr, 1), 0)
    y = jax.nn.gelu(g_ref[...].astype(F32)) * (hf_ref[...].astype(F32) + hb_ref[...].astype(F32))
    o_ref[...] = jnp.where(rows >= PAD, y, 0.0).astype(o_ref.dtype)


def _odd_combine(proj, hf, hb, batch):
    n, w = hf.shape
    tp = n // batch
    tr = _tile(tp, 384, BF16_ROWS)
    row = pl.BlockSpec((tr, w), lambda i: (i, 0))
    return pl.pallas_call(
        functools.partial(_odd_combine_kernel, tr=tr, tp=tp), grid=(n // tr,),
        in_specs=[row, row, row], out_specs=row, out_shape=jax.ShapeDtypeStruct((n, w), BF16),
        compiler_params=_params("parallel"), name="odd_combine")(proj, hf, hb)


def _route(x):
    lane = lax.broadcasted_iota(jnp.int32, x.shape, 1)
    big = jnp.int32(2 * LANES)
    gmask = lane < N_GROUPS
    gmax = jnp.max(jnp.where(gmask, x, -jnp.inf), axis=1, keepdims=True)
    ge = jnp.where(gmask, jnp.exp(x - gmax), 0.0)
    gp = ge / jnp.sum(ge, axis=1, keepdims=True)
    gval = jnp.max(gp, axis=1, keepdims=True)
    gidx = jnp.min(jnp.where(jnp.logical_and(gmask, gp == gval), lane, big), axis=1, keepdims=True)
    lo = N_GROUPS + gidx * EXPERTS_PER_GROUP
    emask = jnp.logical_and(lane >= lo, lane < lo + EXPERTS_PER_GROUP)
    emax = jnp.max(jnp.where(emask, x, -jnp.inf), axis=1, keepdims=True)
    ee = jnp.where(emask, jnp.exp(x - emax), 0.0)
    ep = ee / jnp.sum(ee, axis=1, keepdims=True)
    v1 = jnp.max(jnp.where(emask, ep, -1.0), axis=1, keepdims=True)
    i1 = jnp.min(jnp.where(jnp.logical_and(emask, ep == v1), lane, big), axis=1, keepdims=True)
    rest = jnp.logical_and(emask, lane != i1)
    v2 = jnp.max(jnp.where(rest, ep, -1.0), axis=1, keepdims=True)
    i2 = jnp.min(jnp.where(jnp.logical_and(rest, ep == v2), lane, big), axis=1, keepdims=True)
    tot = v1 + v2
    comb = jnp.where(lane == i1, v1 / tot * gval, jnp.where(lane == i2, v2 / tot * gval, 0.0))
    return comb, gidx


def _pack_pair(hi, lo):
    bits = lambda v: lax.bitcast_convert_type(v.astype(BF16).astype(F32), jnp.uint32)
    return bits(hi) | (bits(lo) >> 16)


def _unpack_pair(w):
    hi = lax.bitcast_convert_type(w & jnp.uint32(0xFFFF0000), F32)
    lo = lax.bitcast_convert_type(w << 16, F32)
    return hi.astype(BF16), lo.astype(BF16)


def _store_packed_rows(ref, x, npack):
    half = npack * LANES
    for j in range(npack):
        ref[:, j, :] = _pack_pair(x[:, j * LANES:(j + 1) * LANES], x[:, half + j * LANES:half + (j + 1) * LANES])


def _load_packed_rows(src, dense_s, dst_ref, npack):
    half = npack * LANES
    for j in range(npack):
        dense_s[:, j * LANES:(j + 1) * LANES] = src(j)
    for j in range(npack):
        hi, lo = _unpack_pair(dense_s[:, j * LANES:(j + 1) * LANES])
        dst_ref[:, j * LANES:(j + 1) * LANES] = hi.astype(dst_ref.dtype)
        dst_ref[:, half + j * LANES:half + (j + 1) * LANES] = lo.astype(dst_ref.dtype)


def _norm_route_kernel(h_ref, g_ref, w2_ref, wh_ref, b_ref, slab_ref, oh_ref, *, npack):
    x = h_ref[...]
    y = x * lax.rsqrt(jnp.mean(x * x, axis=-1, keepdims=True) + EPS) * g_ref[...]
    yh = y.astype(BF16)
    yl = (y - yh.astype(F32)).astype(BF16)
    r1 = _dot(yh, w2_ref[...])
    logits = r1[:, :LANES] + r1[:, LANES:] + _dot(yl, wh_ref[...]) + b_ref[...]
    comb, gidx = _route(logits)
    _store_packed_rows(slab_ref, y, npack)
    slab_ref[:, npack, :] = lax.bitcast_convert_type(comb, jnp.uint32)
    for j in range(npack + 1, slab_ref.shape[1]):
        slab_ref[:, j, :] = jnp.zeros(comb.shape, jnp.uint32)
    lane = lax.broadcasted_iota(jnp.int32, comb.shape, 1)
    oh_ref[...] = jnp.where(lane == gidx, 1.0, 0.0).astype(oh_ref.dtype)


def _norm_route(h, g, wg, bg, we, be):
    n, d = h.shape
    npack = d // (2 * LANES)
    srows = (npack + 1 + SUBLANES - 1) // SUBLANES * SUBLANES
    tr = _tile(n, 192, BF16_ROWS)
    zpad = LANES - N_GROUPS - N_EXPERTS
    wr = jnp.concatenate([wg, we, jnp.zeros((d, zpad), F32)], axis=1)
    wh = wr.astype(BF16)
    wl = (wr - wh.astype(F32)).astype(BF16)
    bias = jnp.concatenate([bg.astype(F32), be.astype(F32), jnp.zeros((zpad,), F32)]).reshape(1, LANES)
    return pl.pallas_call(
        functools.partial(_norm_route_kernel, npack=npack), grid=(n // tr,),
        in_specs=[pl.BlockSpec((tr, d), lambda i: (i, 0)), pl.BlockSpec((1, d), lambda i: (0, 0)),
                  pl.BlockSpec((d, 2 * LANES), lambda i: (0, 0)), pl.BlockSpec((d, LANES), lambda i: (0, 0)),
                  pl.BlockSpec((1, LANES), lambda i: (0, 0))],
        out_specs=[pl.BlockSpec((tr, srows, LANES), lambda i: (i, 0, 0)), pl.BlockSpec((tr, LANES), lambda i: (i, 0))],
        out_shape=[jax.ShapeDtypeStruct((n, srows, LANES), jnp.uint32), jax.ShapeDtypeStruct((n, LANES), BF16)],
        compiler_params=_params("parallel"), name="norm_route")(
            h, g.reshape(1, d).astype(F32), jnp.concatenate([wh, wl], axis=1), wh, bias)


def _rank_kernel(oh_ref, g_ref, rank_ref, cnt_ref, carry_s):
    @pl.when(pl.program_id(0) == 0)
    def _():
        carry_s[...] = jnp.zeros_like(carry_s)

    tr = oh_ref.shape[0]
    sel = jnp.where(lax.broadcasted_iota(jnp.int32, (SUBLANES, LANES), 0) ==
                    lax.broadcasted_iota(jnp.int32, (SUBLANES, LANES), 1), 1.0, 0.0).astype(BF16)
    oh_t = _dot_t(sel, oh_ref[...])
    before = (lax.broadcasted_iota(jnp.int32, (tr, tr), 0) < lax.broadcasted_iota(jnp.int32, (tr, tr), 1))
    cum = _dot(oh_t.astype(BF16), jnp.where(before, 1.0, 0.0).astype(BF16)) + carry_s[:, 0:1]
    gid = lax.broadcasted_iota(jnp.int32, (SUBLANES, tr), 0).astype(F32)
    rank_ref[0] = jnp.sum(oh_t * cum, axis=0, keepdims=True).astype(jnp.int32)
    g_ref[0] = jnp.sum(oh_t * gid, axis=0, keepdims=True).astype(jnp.int32)
    carry_s[...] = carry_s[...] + jnp.sum(oh_t, axis=1, keepdims=True)
    cnt_ref[...] = carry_s[...]


def _rank(onehot):
    n = onehot.shape[0]
    tr = _tile(n, 384, LANES)
    row = pl.BlockSpec((1, 1, tr), lambda i: (i, 0, 0))
    g, rank, cnt = pl.pallas_call(
        _rank_kernel, grid=(n // tr,), in_specs=[pl.BlockSpec((tr, LANES), lambda i: (i, 0))],
        out_specs=[row, row, pl.BlockSpec((SUBLANES, LANES), lambda i: (0, 0))],
        out_shape=[jax.ShapeDtypeStruct((n // tr, 1, tr), jnp.int32), jax.ShapeDtypeStruct((n // tr, 1, tr), jnp.int32),
                   jax.ShapeDtypeStruct((SUBLANES, LANES), F32)],
        scratch_shapes=[pltpu.VMEM((SUBLANES, LANES), F32)],
        compiler_params=_params("arbitrary"), name="rank")(onehot)
    return g.reshape(n), rank.reshape(n), cnt[:N_GROUPS, 0].astype(jnp.int32)


def _invert_kernel(g_ref, rank_ref, cnt_ref, pos_ref, idx_ref, tg_ref, *, n, tm, ntiles):
    bases = [jnp.int32(0)]
    for g in range(N_GROUPS - 1):
        bases.append(bases[-1] + (cnt_ref[g] + tm - 1) // tm * tm)

    def zero(i, c):
        idx_ref[i] = 0
        return c
    lax.fori_loop(0, ntiles * tm, zero, 0, unroll=8)

    def place(t, c):
        g = g_ref[t]
        base = bases[0]
        for k in range(1, N_GROUPS):
            base = jnp.where(g == k, bases[k], base)
        p = base + rank_ref[t]
        pos_ref[t] = p
        idx_ref[p] = t
        return c
    lax.fori_loop(0, n, place, 0, unroll=8)

    def tile_group(i, c):
        r = i * tm
        tg = jnp.int32(0)
        for k in range(1, N_GROUPS):
            tg = tg + (r >= bases[k]).astype(jnp.int32)
        tg_ref[i] = tg
        return c
    lax.fori_loop(0, ntiles, tile_group, 0)


def _invert(g, rank, cnt, tm, ntiles):
    n = g.shape[0]
    smem = pl.BlockSpec(memory_space=pltpu.SMEM)
    return pl.pallas_call(
        functools.partial(_invert_kernel, n=n, tm=tm, ntiles=ntiles),
        in_specs=[smem, smem, smem], out_specs=[smem, smem, smem],
        out_shape=[jax.ShapeDtypeStruct((n,), jnp.int32), jax.ShapeDtypeStruct((ntiles * tm,), jnp.int32),
                   jax.ShapeDtypeStruct((ntiles,), jnp.int32)],
        name="invert")(g, rank, cnt)


def _row_copy(src_hbm, buf, sem, src_row, slot, dst_row):
    return pltpu.make_async_copy(src_hbm.at[pl.ds(src_row, 1)], buf.at[slot, pl.ds(dst_row, 1)], sem.at[slot])


def _gather_rows(index_ref, src_hbm, buf, sem, rows):
    i = pl.program_id(0)
    steps = pl.num_programs(0)

    def issue(step, slot):
        def body(r, c):
            _row_copy(src_hbm, buf, sem, index_ref[step * rows + r], slot, r).start()
            return c
        lax.fori_loop(0, rows, body, 0, unroll=8)

    @pl.when(i == 0)
    def _():
        issue(0, 0)

    @pl.when(i + 1 < steps)
    def _():
        issue(i + 1, (i + 1) % 2)

    slot = i % 2

    def wait(r, c):
        _row_copy(src_hbm, buf, sem, 0, slot, r).wait()
        return c
    lax.fori_loop(0, rows, wait, 0, unroll=8)
    return slot


def _dispatch_kernel(idx_ref, slab_hbm, xs_ref, cs_ref, buf, dense_s, sem, *, npack):
    slot = _gather_rows(idx_ref, slab_hbm, buf, sem, xs_ref.shape[0])
    _load_packed_rows(lambda j: buf[slot, :, j, :], dense_s, xs_ref, npack)
    cs_ref[...] = lax.bitcast_convert_type(buf[slot, :, npack, :], F32)


def _dispatch(idx, slab, tm, ntiles, d):
    srows = slab.shape[1]
    return pl.pallas_call(
        functools.partial(_dispatch_kernel, npack=d // (2 * LANES)),
        grid_spec=pltpu.PrefetchScalarGridSpec(
            num_scalar_prefetch=1, grid=(ntiles,),
            in_specs=[pl.BlockSpec(memory_space=pl.ANY)],
            out_specs=[pl.BlockSpec((tm, d), lambda i, idx: (i, 0)), pl.BlockSpec((tm, LANES), lambda i, idx: (i, 0))],
            scratch_shapes=[pltpu.VMEM((2, tm, srows, LANES), jnp.uint32), pltpu.VMEM((tm, d // 2), jnp.uint32),
                            pltpu.SemaphoreType.DMA((2,))]),
        out_shape=[jax.ShapeDtypeStruct((ntiles * tm, d), BF16), jax.ShapeDtypeStruct((ntiles * tm, LANES), F32)],
        compiler_params=_params("arbitrary"), name="dispatch")(idx, slab)


def _expert_kernel(tg_ref, x_ref, c_ref, w13_ref, w2_ref, o_ref, acc_s, *, npack):
    i = pl.program_id(0)
    e = pl.program_id(1)

    @pl.when(e == 0)
    def _():
        acc_s[...] = jnp.zeros_like(acc_s)

    f = w2_ref.shape[1]
    h13 = _dot(x_ref[...], w13_ref[0])
    h1, h3 = h13[:, :f], h13[:, f:]
    comb = c_ref[...]
    lane = lax.broadcasted_iota(jnp.int32, comb.shape, 1)
    c = jnp.sum(jnp.where(lane == N_GROUPS + tg_ref[i] * EXPERTS_PER_GROUP + e, comb, 0.0), axis=1, keepdims=True)
    hid = (h1 * jax.nn.sigmoid(h1) * h3 * c).astype(BF16)
    acc_s[...] += _dot(hid, w2_ref[0])

    @pl.when(e == EXPERTS_PER_GROUP - 1)
    def _():
        _store_packed_rows(o_ref, acc_s, npack)


def _experts(tg, xs, cs, w13, w2, tm, layer):
    rows, d = xs.shape
    f = w2.shape[1]
    npack = d // (2 * LANES)
    wmap = lambda i, e, tg: (layer * N_EXPERTS + tg[i] * EXPERTS_PER_GROUP + e, 0, 0)
    return pl.pallas_call(
        functools.partial(_expert_kernel, npack=npack),
        grid_spec=pltpu.PrefetchScalarGridSpec(
            num_scalar_prefetch=1, grid=(rows // tm, EXPERTS_PER_GROUP),
            in_specs=[pl.BlockSpec((tm, d), lambda i, e, tg: (i, 0)), pl.BlockSpec((tm, LANES), lambda i, e, tg: (i, 0)),
                      pl.BlockSpec((1, d, 2 * f), wmap), pl.BlockSpec((1, f, d), wmap)],
            out_specs=pl.BlockSpec((tm, npack, LANES), lambda i, e, tg: (i, 0, 0)),
            scratch_shapes=[pltpu.VMEM((tm, d), F32)]),
        out_shape=jax.ShapeDtypeStruct((rows, npack, LANES), jnp.uint32),
        compiler_params=_params("parallel", "arbitrary"), name="experts")(tg, xs, cs, w13, w2)


def _collect_kernel(pos_ref, ys_hbm, o_ref, buf, dense_s, sem, *, npack):
    slot = _gather_rows(pos_ref, ys_hbm, buf, sem, o_ref.shape[0])
    _load_packed_rows(lambda j: buf[slot, :, j, :], dense_s, o_ref, npack)


def _collect_norm_kernel(pos_ref, ys_hbm, h_ref, g_ref, hnew_ref, hn_ref, buf, dense_s, delta_s, sem, *, npack):
    slot = _gather_rows(pos_ref, ys_hbm, buf, sem, h_ref.shape[0])
    _load_packed_rows(lambda j: buf[slot, :, j, :], dense_s, delta_s, npack)
    x = h_ref[...] + delta_s[...].astype(F32)
    hnew_ref[...] = x
    y = x * lax.rsqrt(jnp.mean(x * x, axis=-1, keepdims=True) + EPS) * g_ref[...]
    hn_ref[...] = y.astype(hn_ref.dtype)


def _collect(pos, ys, h=None, g=None):
    n = pos.shape[0]
    _, npack, _ = ys.shape
    d = 2 * npack * LANES
    tr = _tile(n, 384, BF16_ROWS)
    row = pl.BlockSpec((tr, d), lambda i, pos: (i, 0))
    scratch = [pltpu.VMEM((2, tr, npack, LANES), jnp.uint32), pltpu.VMEM((tr, d // 2), jnp.uint32)]
    sem = pltpu.SemaphoreType.DMA((2,))
    if h is None:
        return pl.pallas_call(
            functools.partial(_collect_kernel, npack=npack),
            grid_spec=pltpu.PrefetchScalarGridSpec(
                num_scalar_prefetch=1, grid=(n // tr,), in_specs=[pl.BlockSpec(memory_space=pl.ANY)],
                out_specs=row, scratch_shapes=scratch + [sem]),
            out_shape=jax.ShapeDtypeStruct((n, d), BF16),
            compiler_params=_params("arbitrary"), name="collect")(pos, ys)
    return pl.pallas_call(
        functools.partial(_collect_norm_kernel, npack=npack),
        grid_spec=pltpu.PrefetchScalarGridSpec(
            num_scalar_prefetch=1, grid=(n // tr,),
            in_specs=[pl.BlockSpec(memory_space=pl.ANY), row, pl.BlockSpec((1, d), lambda i, pos: (0, 0))],
            out_specs=[row, row], scratch_shapes=scratch + [pltpu.VMEM((tr, d), BF16), sem]),
        out_shape=[jax.ShapeDtypeStruct((n, d), F32), jax.ShapeDtypeStruct((n, d), BF16)],
        compiler_params=_params("arbitrary"), name="collect_norm")(pos, ys, h, g.reshape(1, d).astype(F32))


def _moe_layer(h, ffn_g, wg, bg, we, be, w13, w2, layer, next_norm=None):
    n, d = h.shape
    tm = 512 if n >= 4096 else 128
    ntiles = (n + N_GROUPS * (tm - 1)) // tm
    slab, onehot = _norm_route(h, ffn_g, wg, bg, we, be)
    g, rank, cnt = _rank(onehot)
    pos, idx, tg = _invert(g, rank, cnt, tm, ntiles)
    xs, cs = _dispatch(idx, slab, tm, ntiles, d)
    ys = _experts(tg, xs, cs, w13, w2, tm, layer)
    if next_norm is None:
        return _collect(pos, ys)
    return _collect(pos, ys, h, next_norm)


def _even_layer(h, hn, w_in_all, e, gate_bias, qk_conv, lr_up, lr_bias, norm_a, norm_b, w_out, batch):
    n, d = h.shape
    dk, dv = d // 16, d // 8
    qk_w, v_w = HEADS * dk, HEADS * dv
    a_end = 2 * qk_w + 2 * v_w
    b_start = a_end + GATE_COLS
    b_end = b_start + 2 * qk_w + 2 * v_w
    w_main, w_small = _split_cast(w_in_all, e, a_end, b_start, b_end)
    proj = _matmul(hn, w_main, BF16, tn_target=1024)
    small = _matmul(hn, w_small, F32, tn_target=LANES)
    gates_t = small[:, :GATE_COLS].reshape(n // CHUNK, CHUNK, GATE_COLS).transpose(0, 2, 1)
    qk = _qk_conv(proj, qk_conv, batch, dk)
    va_blk, oa_blk = 2 * qk_w // v_w, (2 * qk_w + v_w) // v_w
    b0 = a_end
    qb_blk, kb_blk = b0 // qk_w, (b0 + qk_w) // qk_w
    vb_blk, gb_blk = (b0 + 2 * qk_w) // v_w, (b0 + 2 * qk_w + v_w) // v_w
    ha, hb = [], []
    for rev in (False, True):
        ha.append(_mlstm(qk, proj, small, gates_t, gate_bias, batch, dk, dv, va_blk, rev))
        hb.append(_gla(proj, small, lr_up[int(rev)], lr_bias[int(rev)], batch, dk, dv, (qb_blk, kb_blk, vb_blk), rev))
    y = _even_combine(ha[0], ha[1], hb[0], hb[1], proj, norm_a, norm_b, batch, dv, oa_blk, gb_blk)
    return _matmul(y, _cast_bf16(w_out), F32, res=h)


def _odd_layer(h, hn, w_in, conv_w, conv_b, w_r, b_r, w_i, b_i, lam, w_out, batch):
    d_rnn = w_out.shape[0]
    proj = _matmul(hn, _cast_bf16(w_in), BF16, tn_target=1024)
    hs = [_rglru(proj, conv_w, conv_b, w_r[i], b_r[i], w_i[i], b_i[i], lam[i], batch, d_rnn, bool(i)) for i in (0, 1)]
    y = _odd_combine(proj, hs[0], hs[1], batch)
    return _matmul(y, _cast_bf16(w_out), F32, res=h)


def kernel(x, meta_tokens, mix_norm, ffn_norm, final_norm, ev_w_in, ev_gate_bias, ev_qk_conv, ev_lr_up, ev_lr_bias, ev_norm_a, ev_norm_b, ev_w_out, od_w_in, od_conv, od_conv_bias, od_w_r, od_b_r, od_w_i, od_b_i, od_lambda, od_w_out, moe_wg, moe_bg, moe_we, moe_be, moe_w1, moe_w3, moe_w2):
    batch, seq, d = x.shape
    depth = mix_norm.shape[0]
    assert seq % CHUNK == 0 and d % 16 == 0
    f = moe_w1.shape[-1]
    w13 = _cast_pair_bf16(moe_w1, moe_w3).reshape(depth * N_EXPERTS, d, 2 * f)
    w2 = _cast_bf16(moe_w2).reshape(depth * N_EXPERTS, f, d)
    h, hn = _frame_norm(x, meta_tokens, mix_norm[0])
    for layer in range(depth):
        if layer % 2 == 0:
            e = layer // 2
            h = _even_layer(h, hn, ev_w_in, e, ev_gate_bias[e], ev_qk_conv[e], ev_lr_up[e],
                            ev_lr_bias[e], ev_norm_a[e], ev_norm_b[e], ev_w_out[e], batch)
        else:
            o = layer // 2
            h = _odd_layer(h, hn, od_w_in[o], od_conv[o], od_conv_bias[o], od_w_r[o], od_b_r[o],
                           od_w_i[o], od_b_i[o], od_lambda[o], od_w_out[o], batch)
        moe_args = (h, ffn_norm[layer], moe_wg[layer], moe_bg[layer], moe_we[layer], moe_be[layer], w13, w2, layer)
        if layer + 1 < depth:
            h, hn = _moe_layer(*moe_args, next_norm=mix_norm[layer + 1])
        else:
            delta = _moe_layer(*moe_args)
    out = _final_norm(h, delta, final_norm, batch, seq)
    return out.reshape(batch, seq, d)
```

```python
import functools

import jax
import jax.numpy as jnp
from jax import lax
from jax.experimental import pallas as pl
from jax.experimental.pallas import tpu as pltpu

F32 = jnp.float32
BF16 = jnp.bfloat16
HIGHEST = lax.Precision.HIGHEST

N_META = 16
CHUNK = 128
PAD = CHUNK - N_META
SUB = 16
EPS = 1e-6
NEG = -1e30
HEADS = 4
GATE_COLS = 4 * HEADS
B_RANK = 16
B_TAU = 16.0
GLA_MAX_CHUNK_DECAY = 80.0
RNN_BLOCKS = 16
RNN_C = 8.0
N_GROUPS = 4
EXPERTS_PER_GROUP = 8
N_EXPERTS = N_GROUPS * EXPERTS_PER_GROUP
LANES = 128
SUBLANES = 8
BF16_ROWS = 16
VMEM_LIMIT = 56 * 1024 * 1024


def _params(*sem):
    return pltpu.CompilerParams(dimension_semantics=sem, vmem_limit_bytes=VMEM_LIMIT)


def _tile(n, target, mult):
    best = None
    for t in range(mult, min(n, target) + 1, mult):
        if n % t == 0:
            best = t
    assert best is not None, (n, target, mult)
    return best


def _log_sigmoid(x):
    return jnp.minimum(x, 0.0) - jnp.log1p(jnp.exp(-jnp.abs(x)))


def _split3(x):
    hi = x.astype(BF16)
    r1 = x - hi.astype(F32)
    mid = r1.astype(BF16)
    lo = (r1 - mid.astype(F32)).astype(BF16)
    return hi, mid, lo


def _dot(a, b):
    return jnp.dot(a, b, preferred_element_type=F32)


def _dot_t(a, b):
    return lax.dot_general(a, b, (((1,), (1,)), ((), ())), preferred_element_type=F32)


def _tdot(a, b, precision=None):
    return lax.dot_general(a, b, (((0,), (0,)), ((), ())), preferred_element_type=F32,
                           precision=precision)


def _frame_norm_kernel(x_ref, meta_ref, g_ref, h_ref, hn_ref):
    tr, d = h_ref.shape
    xb = x_ref[...]
    first = jnp.concatenate([jnp.zeros((PAD, d), F32), meta_ref[...]] + ([xb[:tr - CHUNK]] if tr > CHUNK else []), axis=0)
    x = jnp.where(pl.program_id(1) == 0, first, xb)
    h_ref[...] = x
    y = x * lax.rsqrt(jnp.mean(x * x, axis=-1, keepdims=True) + EPS) * g_ref[...]
    hn_ref[...] = y.astype(hn_ref.dtype)


def _frame_norm(x, meta, g):
    batch, seq, d = x.shape
    tp = PAD + N_META + seq
    nc = tp // CHUNK
    tr = CHUNK * _tile(nc, min(3, seq // CHUNK), 1)
    nt = tp // tr
    out = pl.BlockSpec((tr, d), lambda b, i: (b * nt + i, 0))
    src = pl.BlockSpec((pl.Element(tr), pl.Element(d)),
                       lambda b, i: (pl.multiple_of(b * seq + jnp.maximum(i * tr - CHUNK, 0), CHUNK), 0))
    return pl.pallas_call(
        _frame_norm_kernel, grid=(batch, nt),
        in_specs=[src, pl.BlockSpec((N_META, d), lambda b, i: (0, 0)), pl.BlockSpec((1, d), lambda b, i: (0, 0))],
        out_specs=[out, out],
        out_shape=[jax.ShapeDtypeStruct((batch * tp, d), F32), jax.ShapeDtypeStruct((batch * tp, d), BF16)],
        compiler_params=_params("parallel", "parallel"), name="frame_norm")(
            x.reshape(batch * seq, d), meta.astype(F32), g.reshape(1, d).astype(F32))


def _final_norm_kernel(h_ref, d_ref, g_ref, o_ref):
    x = h_ref[...] + d_ref[...].astype(F32)
    o_ref[...] = x * lax.rsqrt(jnp.mean(x * x, axis=-1, keepdims=True) + EPS) * g_ref[...]


def _final_norm(h, delta, g, batch, seq):
    n, d = h.shape
    tp = n // batch
    tr = _tile(seq, 512, BF16_ROWS)
    nt = seq // tr
    src = pl.BlockSpec((pl.Element(tr), pl.Element(d)),
                       lambda b, i: (pl.multiple_of(b * tp + CHUNK + i * tr, CHUNK), 0))
    return pl.pallas_call(
        _final_norm_kernel, grid=(batch, nt),
        in_specs=[src, src, pl.BlockSpec((1, d), lambda b, i: (0, 0))],
        out_specs=pl.BlockSpec((tr, d), lambda b, i: (b * nt + i, 0)),
        out_shape=jax.ShapeDtypeStruct((batch * seq, d), F32),
        compiler_params=_params("parallel", "parallel"), name="final_norm")(h, delta, g.reshape(1, d).astype(F32))


def _cast_kernel(w_ref, o_ref):
    o_ref[...] = w_ref[...].astype(o_ref.dtype)


def _cast_bf16(w, block_bytes=8 * 1024 * 1024):
    shape = w.shape
    w2 = w.reshape(-1, shape[-1])
    rows, cols = w2.shape
    tr = _tile(rows, max(BF16_ROWS, block_bytes // (4 * cols)), BF16_ROWS)
    spec = pl.BlockSpec((tr, cols), lambda i: (i, 0))
    out = pl.pallas_call(
        _cast_kernel, grid=(rows // tr,), in_specs=[spec], out_specs=spec,
        out_shape=jax.ShapeDtypeStruct((rows, cols), BF16),
        compiler_params=_params("parallel"), name="cast_bf16")(w2)
    return out.reshape(shape)


def _cast_pair_kernel(a_ref, b_ref, o_ref):
    f = a_ref.shape[1]
    o_ref[:, :f] = a_ref[...].astype(o_ref.dtype)
    o_ref[:, f:] = b_ref[...].astype(o_ref.dtype)


def _cast_pair_bf16(a, b, block_bytes=4 * 1024 * 1024):
    shape = a.shape
    f = shape[-1]
    a2, b2 = a.reshape(-1, f), b.reshape(-1, f)
    rows = a2.shape[0]
    tr = _tile(rows, max(BF16_ROWS, block_bytes // (4 * f)), BF16_ROWS)
    spec = pl.BlockSpec((tr, f), lambda i: (i, 0))
    out = pl.pallas_call(
        _cast_pair_kernel, grid=(rows // tr,), in_specs=[spec, spec],
        out_specs=pl.BlockSpec((tr, 2 * f), lambda i: (i, 0)),
        out_shape=jax.ShapeDtypeStruct((rows, 2 * f), BF16),
        compiler_params=_params("parallel"), name="cast_pair_bf16")(a2, b2)
    return out.reshape(shape[:-1] + (2 * f,))


def _split_cast_kernel(w_ref, main_ref, small_ref, *, a_end, b_start, b_end):
    w = w_ref[0]
    main_ref[:, :a_end] = w[:, :a_end].astype(main_ref.dtype)
    main_ref[:, a_end:] = w[:, b_start:b_end].astype(main_ref.dtype)
    rows = w.shape[0]
    narrow = jnp.concatenate([w[:, a_end:b_start], w[:, b_end:]], axis=1)
    pad = jnp.zeros((rows, small_ref.shape[1] - narrow.shape[1]), F32)
    small_ref[...] = jnp.concatenate([narrow, pad], axis=1).astype(small_ref.dtype)


def _split_cast(w_all, e, a_end, b_start, b_end):
    _, d, cols = w_all.shape
    tr = _tile(d, 256, BF16_ROWS)
    wide = a_end + b_end - b_start
    return pl.pallas_call(
        functools.partial(_split_cast_kernel, a_end=a_end, b_start=b_start, b_end=b_end), grid=(d // tr,),
        in_specs=[pl.BlockSpec((1, tr, cols), lambda i: (e, i, 0))],
        out_specs=[pl.BlockSpec((tr, wide), lambda i: (i, 0)), pl.BlockSpec((tr, LANES), lambda i: (i, 0))],
        out_shape=[jax.ShapeDtypeStruct((d, wide), BF16), jax.ShapeDtypeStruct((d, LANES), BF16)],
        compiler_params=_params("parallel"), name="split_cast")(w_all)


def _mm_kernel(*refs, has_res):
    if has_res:
        a_ref, w_ref, r_ref, o_ref = refs
    else:
        a_ref, w_ref, o_ref = refs
    acc = _dot(a_ref[...], w_ref[...])
    if has_res:
        acc = acc + r_ref[...]
    o_ref[...] = acc.astype(o_ref.dtype)


def _matmul(a, w, out_dtype, res=None, tm_target=1376, tn_target=512):
    n, k = a.shape
    m = w.shape[1]
    tm = _tile(n, tm_target, BF16_ROWS)
    tn = _tile(m, tn_target, LANES)
    in_specs = [pl.BlockSpec((tm, k), lambda i, j: (i, 0)), pl.BlockSpec((k, tn), lambda i, j: (0, j))]
    args = [a, w]
    if res is not None:
        in_specs.append(pl.BlockSpec((tm, tn), lambda i, j: (i, j)))
        args.append(res)
    return pl.pallas_call(
        functools.partial(_mm_kernel, has_res=res is not None),
        grid=(n // tm, m // tn), in_specs=in_specs,
        out_specs=pl.BlockSpec((tm, tn), lambda i, j: (i, j)),
        out_shape=jax.ShapeDtypeStruct((n, m), out_dtype),
        compiler_params=_params("parallel", "arbitrary"), name="matmul")(*args)


def _conv_taps(ext_s, cw_ref, tt):
    out = cw_ref[0:1, :] * ext_s[pl.ds(SUBLANES - 2, tt), :]
    for j in range(1, 4):
        out = out + cw_ref[j:j + 1, :] * ext_s[pl.ds(SUBLANES - 2 + j, tt), :]
    return out


def _fill_ext(ext_s, cur_ref, prev_ref, next_ref, row0, tt, has_next):
    rows = row0 + lax.broadcasted_iota(jnp.int32, (tt, 1), 0)
    ext_s[pl.ds(SUBLANES, tt), :] = jnp.where(rows >= PAD, cur_ref[...].astype(F32), 0.0)
    prow = row0 - SUBLANES + lax.broadcasted_iota(jnp.int32, (SUBLANES, 1), 0)
    ext_s[pl.ds(0, SUBLANES), :] = jnp.where(prow >= PAD, prev_ref[...].astype(F32)[SUBLANES:, :], 0.0)
    ext_s[pl.ds(SUBLANES + tt, SUBLANES), :] = jnp.where(has_next, next_ref[...].astype(F32)[:SUBLANES, :], 0.0)


def _qkconv_kernel(cur_ref, prev_ref, next_ref, cw_ref, o_ref, ext_s, *, tt, nt, kscale, half):
    t = pl.program_id(1)
    row0 = t * tt
    _fill_ext(ext_s, cur_ref, prev_ref, next_ref, row0, tt, t < nt - 1)
    y = _conv_taps(ext_s, cw_ref, tt)
    y = y * jax.nn.sigmoid(y)
    col = lax.broadcasted_iota(jnp.int32, (1, 2 * half), 1)
    y = y * jnp.where(col >= half, kscale, 1.0)
    rows = row0 + lax.broadcasted_iota(jnp.int32, (tt, 1), 0)
    o_ref[...] = jnp.where(rows >= PAD, y, 0.0).astype(o_ref.dtype)


def _halo_specs(width, col_block, tt, tp, n):
    per_b, per_t = tp // BF16_ROWS, tt // BF16_ROWS
    last = n // BF16_ROWS - 1

    def make(tmap):
        cur = pl.BlockSpec((tt, width), lambda b, t, *_: (b * (tp // tt) + tmap(t), col_block(*_)))
        prev = pl.BlockSpec((BF16_ROWS, width),
                            lambda b, t, *_: (jnp.maximum(b * per_b + tmap(t) * per_t - 1, 0), col_block(*_)))
        nxt = pl.BlockSpec((BF16_ROWS, width),
                           lambda b, t, *_: (jnp.minimum(b * per_b + (tmap(t) + 1) * per_t, last), col_block(*_)))
        return cur, prev, nxt
    return make


def _qk_conv(proj, conv_w, batch, dk):
    n = proj.shape[0]
    tp = n // batch
    width = 2 * HEADS * dk
    tt = _tile(tp, 688, BF16_ROWS)
    nt = tp // tt
    cur, prev, nxt = _halo_specs(width, lambda: 0, tt, tp, n)(lambda t: t)
    return pl.pallas_call(
        functools.partial(_qkconv_kernel, tt=tt, nt=nt, kscale=dk ** -0.5, half=HEADS * dk),
        grid=(batch, nt),
        in_specs=[cur, prev, nxt, pl.BlockSpec((4, width), lambda b, t: (0, 0))],
        out_specs=pl.BlockSpec((tt, width), lambda b, t: (b * nt + t, 0)),
        out_shape=jax.ShapeDtypeStruct((n, width), BF16),
        scratch_shapes=[pltpu.VMEM((tt + 2 * SUBLANES, width), F32)],
        compiler_params=_params("parallel", "parallel"), name="qk_conv")(proj, proj, proj, conv_w.astype(F32))


def _mlstm_kernel(q_ref, k_ref, v_ref, g_ref, gt_ref, gb_ref, gbt_ref, o_ref, c_s, m_s, *, reverse, dk, dv, nc, batch):
    step = pl.program_id(0)

    @pl.when(step == 0)
    def _():
        c_s[...] = jnp.zeros_like(c_s)
        m_s[...] = jnp.zeros_like(m_s)

    chunk = nc - 1 - step if reverse else step
    L = CHUNK
    real = chunk > 0
    valid_c = jnp.logical_or(real, lax.broadcasted_iota(jnp.int32, (L, 1), 0) >= PAD)
    valid_r = jnp.logical_or(real, lax.broadcasted_iota(jnp.int32, (1, L), 1) >= PAD)
    off = 2 * HEADS if reverse else 0
    ri = lax.broadcasted_iota(jnp.int32, (L, L), 0)
    ci = lax.broadcasted_iota(jnp.int32, (L, L), 1)
    mask = (ci >= ri) if reverse else (ci <= ri)
    inc = (ri >= ci) if reverse else (ri <= ci)
    last = 0 if reverse else L - 1
    ones_col = jnp.where(lax.broadcasted_iota(jnp.int32, (L, LANES), 1) == 0, 1.0, 0.0).astype(BF16)
    lane_pad = [jnp.zeros((dk, -L % LANES), BF16)] if L % LANES else []
    score_w = L + -L % LANES

    lf_c, lf_r, li_r = [], [], []
    for b in range(batch):
        g = g_ref[b, :, :GATE_COLS] + gb_ref[...]
        gt = gt_ref[b, 0] + gbt_ref[...]
        lf_c.append(jnp.where(valid_c, _log_sigmoid(g), 0.0))
        lf_r.append(jnp.where(valid_r, _log_sigmoid(gt), 0.0))
        li_r.append(jnp.where(valid_r, gt[off:off + HEADS, :], NEG))
    terms_c = [t for x in lf_c for t in _split3(x)]
    sums_c = _dot(mask.astype(BF16), jnp.concatenate(terms_c, axis=1))
    terms_r = [t for x in lf_r for t in _split3(x)]
    sums_r = _dot(jnp.concatenate(terms_r, axis=0), inc.astype(BF16))
    w = GATE_COLS

    for b in range(batch):
        cum_c = sum(sums_c[:, (3 * b + j) * w:(3 * b + j + 1) * w] for j in range(3))
        cum_r = sum(sums_r[(3 * b + j) * w:(3 * b + j + 1) * w, :] for j in range(3))
        for h in range(HEADS):
            sh = b * HEADS + h
            col = off + HEADS + h
            cc = cum_c[:, col:col + 1]
            cr = cum_r[col:col + 1, :]
            lir = li_r[b][h:h + 1, :]
            tot = cc[last:last + 1, :]
            m = m_s[sh, 0:1, 0:1]
            qh = q_ref[b, :, h * dk:(h + 1) * dk]
            k_t = k_ref[b, :, h * dk:(h + 1) * dk].astype(F32).T
            vh = jnp.where(valid_c, v_ref[b, :, h * dv:(h + 1) * dv], 0.0).astype(BF16)
            vaug = jnp.concatenate([vh, ones_col], axis=1)

            rhs = jnp.concatenate([k_t.astype(BF16)] + lane_pad + [c_s[sh].astype(BF16)], axis=1)
            qkc = _dot(qh, rhs)
            d_mat = jnp.where(mask, cc - cr + lir, NEG)
            inter = cc + m
            m_t = jnp.maximum(inter, jnp.max(d_mat, axis=1, keepdims=True))
            w_inter = jnp.exp(inter - m_t)
            s = qkc[:, :L] * jnp.exp(d_mat - m_t)

            gs = tot - cr + lir
            m_new = jnp.maximum(tot + m, jnp.max(gs, axis=1, keepdims=True))
            decay = jnp.exp(tot + m - m_new)
            ks_t = (k_t * jnp.exp(gs - m_new)).astype(BF16)
            sv = _dot(jnp.concatenate([s.astype(BF16), ks_t], axis=0), vaug)
            haug = w_inter * qkc[:, score_w:] + sv[:L]
            den = haug[:, dv:dv + 1]
            o_ref[b, :, h * dv:(h + 1) * dv] = (
                haug[:, :dv] / jnp.maximum(jnp.abs(den), jnp.exp(-m_t))).astype(o_ref.dtype)
            c_s[sh] = decay * c_s[sh] + sv[L:]
            m_s[sh] = jnp.broadcast_to(m_new, m_s.shape[1:])


def _chunk_spec(batch, width, col_block, nc, reverse):
    cidx = (lambda i: nc - 1 - i) if reverse else (lambda i: i)
    return pl.BlockSpec((batch, CHUNK, width), lambda i: (0, cidx(i), col_block))


def _mlstm(qk, proj, gates, gates_t, gate_bias, batch, dk, dv, v_block, reverse):
    n = qk.shape[0]
    tp = n // batch
    nc = tp // CHUNK
    cidx = (lambda i: nc - 1 - i) if reverse else (lambda i: i)
    gb = gate_bias.reshape(1, GATE_COLS).astype(F32)
    view = lambda a: a.reshape(batch, tp, a.shape[-1])
    out = pl.pallas_call(
        functools.partial(_mlstm_kernel, reverse=reverse, dk=dk, dv=dv, nc=nc, batch=batch),
        grid=(nc,),
        in_specs=[_chunk_spec(batch, HEADS * dk, 0, nc, reverse), _chunk_spec(batch, HEADS * dk, 1, nc, reverse),
                  _chunk_spec(batch, HEADS * dv, v_block, nc, reverse), _chunk_spec(batch, LANES, 0, nc, reverse),
                  pl.BlockSpec((batch, 1, GATE_COLS, CHUNK), lambda i: (0, cidx(i), 0, 0)),
                  pl.BlockSpec((1, GATE_COLS), lambda i: (0, 0)),
                  pl.BlockSpec((GATE_COLS, 1), lambda i: (0, 0))],
        out_specs=_chunk_spec(batch, HEADS * dv, 0, nc, reverse),
        out_shape=jax.ShapeDtypeStruct((batch, tp, HEADS * dv), BF16),
        scratch_shapes=[pltpu.VMEM((batch * HEADS, dk, dv + LANES), F32),
                        pltpu.VMEM((batch * HEADS, SUBLANES, LANES), F32)],
        compiler_params=_params("arbitrary"),
        name="mlstm_bwd" if reverse else "mlstm_fwd")(
            view(qk), view(qk), view(proj), view(gates), gates_t.reshape(batch, nc, GATE_COLS, CHUNK), gb,
            gb.reshape(GATE_COLS, 1))
    return out.reshape(n, HEADS * dv)


def _gla_head_exact(q, k, v, cumh, state, o_ref, b, h, *, reverse, dk, dv):
    L = CHUNK
    nsub = L // SUB
    sub_lane = lax.broadcasted_iota(jnp.int32, (SUB, L), 1)
    sub_row = lax.broadcasted_iota(jnp.int32, (SUB, 1), 0)
    o_inter = _dot_t((q * jnp.exp(cumh)).astype(BF16), state.astype(BF16))
    for blk in range(nsub):
        r0 = blk * SUB
        if reverse:
            cs = cumh[r0 + SUB:r0 + SUB + 1, :] if blk < nsub - 1 else jnp.zeros((1, dk), F32)
            earlier = sub_lane >= r0 + SUB
        else:
            cs = cumh[r0 - 1:r0, :] if blk > 0 else jnp.zeros((1, dk), F32)
            earlier = sub_lane < r0
        q_b = q[r0:r0 + SUB, :]
        cum_b = cumh[r0:r0 + SUB, :]
        qd = (q_b * jnp.exp(cum_b - cs)).astype(BF16)
        kd = (k * jnp.exp(jnp.minimum(cs - cumh, 0.0))).astype(BF16)
        att = jnp.where(earlier, _dot_t(qd, kd), 0.0)
        for j in range(SUB):
            s_idx = r0 + j
            tmask = (sub_row <= j) if reverse else (sub_row >= j)
            e = jnp.where(tmask, cum_b - cumh[s_idx:s_idx + 1, :], NEG)
            col = jnp.sum(q_b * k[s_idx:s_idx + 1, :] * jnp.exp(e), axis=1, keepdims=True)
            att = jnp.where(sub_lane == s_idx, col, att)
        o_b = o_inter[r0:r0 + SUB, :] + _dot(att.astype(BF16), v)
        o_ref[b, r0:r0 + SUB, h * dv:(h + 1) * dv] = o_b.astype(o_ref.dtype)


def _gla_head_factored(q, k, v, cumh, state, mask, o_ref, b, h, *, dv):
    qe = (q * jnp.exp(cumh)).astype(BF16)
    ke = (k * jnp.exp(-cumh)).astype(BF16)
    att = jnp.where(mask, _dot_t(qe, ke), 0.0)
    o = _dot_t(qe, state.astype(BF16)) + _dot(att.astype(BF16), v)
    o_ref[b, :, h * dv:(h + 1) * dv] = o.astype(o_ref.dtype)


def _gla_kernel(q_ref, k_ref, v_ref, lr_ref, up_ref, ub_ref, o_ref, s_s, *, reverse, dk, dv, nc, batch):
    step = pl.program_id(0)

    @pl.when(step == 0)
    def _():
        s_s[...] = jnp.zeros_like(s_s)

    chunk = nc - 1 - step if reverse else step
    L = CHUNK
    valid_c = jnp.logical_or(chunk > 0, lax.broadcasted_iota(jnp.int32, (L, 1), 0) >= PAD)
    off = GATE_COLS + (B_RANK if reverse else 0)
    ri = lax.broadcasted_iota(jnp.int32, (L, L), 0)
    ci = lax.broadcasted_iota(jnp.int32, (L, L), 1)
    mask = (ci >= ri) if reverse else (ci <= ri)
    last = 0 if reverse else L - 1
    cums = []
    for b in range(batch):
        z = jnp.dot(lr_ref[b, :, off:off + B_RANK], up_ref[...], precision=HIGHEST, preferred_element_type=F32)
        la = jnp.where(valid_c, _log_sigmoid(z + ub_ref[...]) / B_TAU, 0.0)
        cums.append(sum(_dot(mask.astype(BF16), t) for t in _split3(la)))
    lowest = jnp.min(jnp.concatenate([c[last:last + 1, :] for c in cums], axis=0))
    factorable = lowest >= -GLA_MAX_CHUNK_DECAY

    def run(factored):
        for b in range(batch):
            for h in range(HEADS):
                sl = slice(h * dk, (h + 1) * dk)
                q = jnp.where(valid_c, q_ref[b, :, sl], 0.0).astype(F32) * dk ** -0.5
                k = jnp.where(valid_c, k_ref[b, :, sl], 0.0).astype(F32)
                v = jnp.where(valid_c, v_ref[b, :, h * dv:(h + 1) * dv], 0.0).astype(BF16)
                cumh = cums[b][:, sl]
                tot = cumh[last:last + 1, :]
                state = s_s[b * HEADS + h]
                if factored:
                    _gla_head_factored(q, k, v, cumh, state, mask, o_ref, b, h, dv=dv)
                else:
                    _gla_head_exact(q, k, v, cumh, state, o_ref, b, h, reverse=reverse, dk=dk, dv=dv)
                kdec = (k * jnp.exp(tot - cumh)).astype(BF16)
                s_s[b * HEADS + h] = jnp.exp(tot) * state + _tdot(v, kdec)

    @pl.when(factorable)
    def _():
        run(True)

    @pl.when(jnp.logical_not(factorable))
    def _():
        run(False)


def _gla(proj, small, lr_up, lr_bias, batch, dk, dv, qkv_blocks, reverse):
    n = proj.shape[0]
    tp = n // batch
    nc = tp // CHUNK
    qb, kb, vb = qkv_blocks
    view = lambda a: a.reshape(batch, tp, a.shape[-1])
    out = pl.pallas_call(
        functools.partial(_gla_kernel, reverse=reverse, dk=dk, dv=dv, nc=nc, batch=batch),
        grid=(nc,),
        in_specs=[_chunk_spec(batch, HEADS * dk, qb, nc, reverse), _chunk_spec(batch, HEADS * dk, kb, nc, reverse),
                  _chunk_spec(batch, HEADS * dv, vb, nc, reverse), _chunk_spec(batch, LANES, 0, nc, reverse),
                  pl.BlockSpec((B_RANK, HEADS * dk), lambda i: (0, 0)),
                  pl.BlockSpec((1, HEADS * dk), lambda i: (0, 0))],
        out_specs=_chunk_spec(batch, HEADS * dv, 0, nc, reverse),
        out_shape=jax.ShapeDtypeStruct((batch, tp, HEADS * dv), BF16),
        scratch_shapes=[pltpu.VMEM((batch * HEADS, dv, dk), F32)],
        compiler_params=_params("arbitrary"),
        name="gla_bwd" if reverse else "gla_fwd")(
            view(proj), view(proj), view(proj), view(small), lr_up.astype(F32), lr_bias.reshape(1, -1).astype(F32))
    return out.reshape(n, HEADS * dv)


def _head_norm(x, g, dv):
    parts = []
    for h in range(HEADS):
        xh = x[:, h * dv:(h + 1) * dv]
        parts.append(xh * lax.rsqrt(jnp.mean(xh * xh, axis=-1, keepdims=True) + EPS))
    return jnp.concatenate(parts, axis=1) * g


def _even_combine_kernel(af_ref, ab_ref, bf_ref, bb_ref, oa_ref, gb_ref, na_ref, nb_ref, o_ref, *, tr, tp, dv):
    rows = (pl.program_id(0) * tr) % tp + lax.broadcasted_iota(jnp.int32, (tr, 1), 0)
    valid = rows >= PAD
    w = HEADS * dv
    ha = af_ref[...].astype(F32) + ab_ref[...].astype(F32)
    ya = jax.nn.sigmoid(oa_ref[...].astype(F32)) * _head_norm(ha, na_ref[...], dv)
    o_ref[:, :w] = jnp.where(valid, ya, 0.0).astype(o_ref.dtype)
    hb = bf_ref[...].astype(F32) + bb_ref[...].astype(F32)
    gb = gb_ref[...].astype(F32)
    yb = gb * jax.nn.sigmoid(gb) * _head_norm(hb, nb_ref[...], dv)
    o_ref[:, w:] = jnp.where(valid, yb, 0.0).astype(o_ref.dtype)


def _even_combine(ha_f, ha_b, hb_f, hb_b, proj, norm_a, norm_b, batch, dv, oa_block, gb_block):
    n, w = ha_f.shape
    tp = n // batch
    tr = _tile(tp, 384, BF16_ROWS)
    row = pl.BlockSpec((tr, w), lambda i: (i, 0))
    vec = pl.BlockSpec((1, w), lambda i: (0, 0))
    return pl.pallas_call(
        functools.partial(_even_combine_kernel, tr=tr, tp=tp, dv=dv),
        grid=(n // tr,),
        in_specs=[row, row, row, row, pl.BlockSpec((tr, w), lambda i: (i, oa_block)),
                  pl.BlockSpec((tr, w), lambda i: (i, gb_block)), vec, vec],
        out_specs=pl.BlockSpec((tr, 2 * w), lambda i: (i, 0)),
        out_shape=jax.ShapeDtypeStruct((n, 2 * w), BF16),
        compiler_params=_params("parallel"), name="even_combine")(
            ha_f, ha_b, hb_f, hb_b, proj, proj, norm_a.reshape(1, w).astype(F32), norm_b.reshape(1, w).astype(F32))


def _block_scan(a, b, reverse):
    sub = lax.broadcasted_iota(jnp.int32, a.shape, 1)
    for k in (1, 2, 4):
        if reverse:
            a_sh, b_sh, m = pltpu.roll(a, SUBLANES - k, 1), pltpu.roll(b, SUBLANES - k, 1), sub < SUBLANES - k
        else:
            a_sh, b_sh, m = pltpu.roll(a, k, 1), pltpu.roll(b, k, 1), sub >= k
        b = jnp.where(m, a * b_sh + b, b)
        a = jnp.where(m, a * a_sh, a)
    return a, b


def _rglru_kernel(cur_ref, prev_ref, next_ref, cw_ref, cb_ref, wr_ref, br_ref, wi_ref, bi_ref, lam_ref,
                  o_ref, ext_s, a_s, b_s, h_s, carry_s, *, reverse, tt, nt):
    step = pl.program_id(2)

    @pl.when(step == 0)
    def _():
        carry_s[...] = jnp.zeros_like(carry_s)

    t = nt - 1 - step if reverse else step
    row0 = t * tt
    _fill_ext(ext_s, cur_ref, prev_ref, next_ref, row0, tt, t < nt - 1)
    u = _conv_taps(ext_s, cw_ref, tt) + cb_ref[...]
    ub = u.astype(BF16)
    tr_ = jnp.tanh(_dot(ub, wr_ref[0]) + 0.5 * br_ref[...])
    ti_ = jnp.tanh(_dot(ub, wi_ref[0]) + 0.5 * bi_ref[...])
    lam = lam_ref[...]
    softplus = jnp.maximum(-lam, 0.0) + jnp.log1p(jnp.exp(-jnp.abs(lam)))
    half_c = (-0.5 * RNN_C) * softplus
    log_a = half_c * tr_ + half_c
    a = jnp.exp(log_a)
    rows = row0 + lax.broadcasted_iota(jnp.int32, (tt, 1), 0)
    inp = jnp.where(rows >= PAD, (0.5 * jnp.sqrt(1.0 - a * a) * u) * (ti_ + 1.0), 0.0)
    c = a.shape[1]
    ng = tt // SUBLANES
    a_g, b_g = _block_scan(a.reshape(ng, SUBLANES, c), inp.reshape(ng, SUBLANES, c), reverse)
    a_s[...] = a_g.reshape(tt, c)
    b_s[...] = b_g.reshape(tt, c)
    out_row = 0 if reverse else SUBLANES - 1

    def body(i, carry):
        g = ng - 1 - i if reverse else i
        r0 = pl.multiple_of(g * SUBLANES, SUBLANES)
        hh = b_s[pl.ds(r0, SUBLANES), :] + a_s[pl.ds(r0, SUBLANES), :] * carry
        h_s[pl.ds(r0, SUBLANES), :] = hh
        return hh[out_row:out_row + 1, :]

    carry_s[...] = lax.fori_loop(0, ng, body, carry_s[...])
    o_ref[...] = h_s[...].astype(o_ref.dtype)


def _pair_blocks(w):
    nb, r, _ = w.shape
    z = jnp.zeros((nb // 2, r, r), w.dtype)
    top = jnp.concatenate([w[0::2], z], axis=2)
    bot = jnp.concatenate([z, w[1::2]], axis=2)
    return (0.5 * jnp.concatenate([top, bot], axis=1)).astype(BF16)


def _rglru(proj, conv_w, conv_b, w_r, b_r, w_i, b_i, lam, batch, d_rnn, reverse):
    n = proj.shape[0]
    tp = n // batch
    cw = 2 * d_rnn // RNN_BLOCKS
    ncb = d_rnn // cw
    tt = _tile(tp, 688, BF16_ROWS)
    nt = tp // tt
    tmap = (lambda t: nt - 1 - t) if reverse else (lambda t: t)
    cur, prev, nxt = _halo_specs(cw, lambda j: ncb + j, tt, tp, n)(tmap)
    def swap(spec):
        f = spec.index_map
        return pl.BlockSpec(spec.block_shape, lambda b, j, t: f(b, t, j))
    vec = pl.BlockSpec((1, cw), lambda b, j, t: (0, j))
    wspec = pl.BlockSpec((1, cw, cw), lambda b, j, t: (j, 0, 0))
    row = lambda x: x.reshape(1, d_rnn).astype(F32)
    return pl.pallas_call(
        functools.partial(_rglru_kernel, reverse=reverse, tt=tt, nt=nt),
        grid=(batch, ncb, nt),
        in_specs=[swap(cur), swap(prev), swap(nxt), pl.BlockSpec((4, cw), lambda b, j, t: (0, j)), vec,
                  wspec, vec, wspec, vec, vec],
        out_specs=pl.BlockSpec((tt, cw), lambda b, j, t: (b * nt + tmap(t), j)),
        out_shape=jax.ShapeDtypeStruct((n, d_rnn), BF16),
        scratch_shapes=[pltpu.VMEM((tt + 2 * SUBLANES, cw), F32), pltpu.VMEM((tt, cw), F32),
                        pltpu.VMEM((tt, cw), F32), pltpu.VMEM((tt, cw), F32), pltpu.VMEM((1, cw), F32)],
        compiler_params=_params("parallel", "parallel", "arbitrary"),
        name="rglru_bwd" if reverse else "rglru_fwd")(
            proj, proj, proj, conv_w.astype(F32), row(conv_b), _pair_blocks(w_r), row(b_r),
            _pair_blocks(w_i), row(b_i), row(lam))


def _odd_combine_kernel(g_ref, hf_ref, hb_ref, o_ref, *, tr, tp):
    rows = (pl.program_id(0) * tr) % tp + lax.broadcasted_iota(jnp.int32, (tr, 1), 0)
    y = jax.nn.gelu(g_ref[...].astype(F32)) * (hf_ref[...].astype(F32) + hb_ref[...].astype(F32))
    o_ref[...] = jnp.where(rows >= PAD, y, 0.0).astype(o_ref.dtype)


def _odd_combine(proj, hf, hb, batch):
    n, w = hf.shape
    tp = n // batch
    tr = _tile(tp, 384, BF16_ROWS)
    row = pl.BlockSpec((tr, w), lambda i: (i, 0))
    return pl.pallas_call(
        functools.partial(_odd_combine_kernel, tr=tr, tp=tp), grid=(n // tr,),
        in_specs=[row, row, row], out_specs=row, out_shape=jax.ShapeDtypeStruct((n, w), BF16),
        compiler_params=_params("parallel"), name="odd_combine")(proj, hf, hb)


def _route(x):
    lane = lax.broadcasted_iota(jnp.int32, x.shape, 1)
    big = jnp.int32(2 * LANES)
    gmask = lane < N_GROUPS
    gmax = jnp.max(jnp.where(gmask, x, -jnp.inf), axis=1, keepdims=True)
    ge = jnp.where(gmask, jnp.exp(x - gmax), 0.0)
    gp = ge / jnp.sum(ge, axis=1, keepdims=True)
    gval = jnp.max(gp, axis=1, keepdims=True)
    gidx = jnp.min(jnp.where(jnp.logical_and(gmask, gp == gval), lane, big), axis=1, keepdims=True)
    lo = N_GROUPS + gidx * EXPERTS_PER_GROUP
    emask = jnp.logical_and(lane >= lo, lane < lo + EXPERTS_PER_GROUP)
    emax = jnp.max(jnp.where(emask, x, -jnp.inf), axis=1, keepdims=True)
    ee = jnp.where(emask, jnp.exp(x - emax), 0.0)
    ep = ee / jnp.sum(ee, axis=1, keepdims=True)
    v1 = jnp.max(jnp.where(emask, ep, -1.0), axis=1, keepdims=True)
    i1 = jnp.min(jnp.where(jnp.logical_and(emask, ep == v1), lane, big), axis=1, keepdims=True)
    rest = jnp.logical_and(emask, lane != i1)
    v2 = jnp.max(jnp.where(rest, ep, -1.0), axis=1, keepdims=True)
    i2 = jnp.min(jnp.where(jnp.logical_and(rest, ep == v2), lane, big), axis=1, keepdims=True)
    tot = v1 + v2
    comb = jnp.where(lane == i1, v1 / tot * gval, jnp.where(lane == i2, v2 / tot * gval, 0.0))
    return comb, gidx


def _pack_pair(hi, lo):
    bits = lambda v: lax.bitcast_convert_type(v.astype(BF16).astype(F32), jnp.uint32)
    return bits(hi) | (bits(lo) >> 16)


def _unpack_pair(w):
    hi = lax.bitcast_convert_type(w & jnp.uint32(0xFFFF0000), F32)
    lo = lax.bitcast_convert_type(w << 16, F32)
    return hi.astype(BF16), lo.astype(BF16)


def _store_packed_rows(ref, x, npack):
    half = npack * LANES
    for j in range(npack):
        ref[:, j, :] = _pack_pair(x[:, j * LANES:(j + 1) * LANES], x[:, half + j * LANES:half + (j + 1) * LANES])


def _load_packed_rows(src, dense_s, dst_ref, npack):
    half = npack * LANES
    for j in range(npack):
        dense_s[:, j * LANES:(j + 1) * LANES] = src(j)
    for j in range(npack):
        hi, lo = _unpack_pair(dense_s[:, j * LANES:(j + 1) * LANES])
        dst_ref[:, j * LANES:(j + 1) * LANES] = hi.astype(dst_ref.dtype)
        dst_ref[:, half + j * LANES:half + (j + 1) * LANES] = lo.astype(dst_ref.dtype)


def _norm_route_kernel(h_ref, g_ref, w2_ref, wh_ref, b_ref, slab_ref, oh_ref, *, npack):
    x = h_ref[...]
    y = x * lax.rsqrt(jnp.mean(x * x, axis=-1, keepdims=True) + EPS) * g_ref[...]
    yh = y.astype(BF16)
    yl = (y - yh.astype(F32)).astype(BF16)
    r1 = _dot(yh, w2_ref[...])
    logits = r1[:, :LANES] + r1[:, LANES:] + _dot(yl, wh_ref[...]) + b_ref[...]
    comb, gidx = _route(logits)
    _store_packed_rows(slab_ref, y, npack)
    slab_ref[:, npack, :] = lax.bitcast_convert_type(comb, jnp.uint32)
    for j in range(npack + 1, slab_ref.shape[1]):
        slab_ref[:, j, :] = jnp.zeros(comb.shape, jnp.uint32)
    lane = lax.broadcasted_iota(jnp.int32, comb.shape, 1)
    oh_ref[...] = jnp.where(lane == gidx, 1.0, 0.0).astype(oh_ref.dtype)


def _norm_route(h, g, wg, bg, we, be):
    n, d = h.shape
    npack = d // (2 * LANES)
    srows = (npack + 1 + SUBLANES - 1) // SUBLANES * SUBLANES
    tr = _tile(n, 192, BF16_ROWS)
    zpad = LANES - N_GROUPS - N_EXPERTS
    wr = jnp.concatenate([wg, we, jnp.zeros((d, zpad), F32)], axis=1)
    wh = wr.astype(BF16)
    wl = (wr - wh.astype(F32)).astype(BF16)
    bias = jnp.concatenate([bg.astype(F32), be.astype(F32), jnp.zeros((zpad,), F32)]).reshape(1, LANES)
    return pl.pallas_call(
        functools.partial(_norm_route_kernel, npack=npack), grid=(n // tr,),
        in_specs=[pl.BlockSpec((tr, d), lambda i: (i, 0)), pl.BlockSpec((1, d), lambda i: (0, 0)),
                  pl.BlockSpec((d, 2 * LANES), lambda i: (0, 0)), pl.BlockSpec((d, LANES), lambda i: (0, 0)),
                  pl.BlockSpec((1, LANES), lambda i: (0, 0))],
        out_specs=[pl.BlockSpec((tr, srows, LANES), lambda i: (i, 0, 0)), pl.BlockSpec((tr, LANES), lambda i: (i, 0))],
        out_shape=[jax.ShapeDtypeStruct((n, srows, LANES), jnp.uint32), jax.ShapeDtypeStruct((n, LANES), BF16)],
        compiler_params=_params("parallel"), name="norm_route")(
            h, g.reshape(1, d).astype(F32), jnp.concatenate([wh, wl], axis=1), wh, bias)


def _rank_kernel(oh_ref, g_ref, rank_ref, cnt_ref, carry_s):
    @pl.when(pl.program_id(0) == 0)
    def _():
        carry_s[...] = jnp.zeros_like(carry_s)

    tr = oh_ref.shape[0]
    sel = jnp.where(lax.broadcasted_iota(jnp.int32, (SUBLANES, LANES), 0) ==
                    lax.broadcasted_iota(jnp.int32, (SUBLANES, LANES), 1), 1.0, 0.0).astype(BF16)
    oh_t = _dot_t(sel, oh_ref[...])
    before = (lax.broadcasted_iota(jnp.int32, (tr, tr), 0) < lax.broadcasted_iota(jnp.int32, (tr, tr), 1))
    cum = _dot(oh_t.astype(BF16), jnp.where(before, 1.0, 0.0).astype(BF16)) + carry_s[:, 0:1]
    gid = lax.broadcasted_iota(jnp.int32, (SUBLANES, tr), 0).astype(F32)
    rank_ref[0] = jnp.sum(oh_t * cum, axis=0, keepdims=True).astype(jnp.int32)
    g_ref[0] = jnp.sum(oh_t * gid, axis=0, keepdims=True).astype(jnp.int32)
    carry_s[...] = carry_s[...] + jnp.sum(oh_t, axis=1, keepdims=True)
    cnt_ref[...] = carry_s[...]


def _rank(onehot):
    n = onehot.shape[0]
    tr = _tile(n, 384, LANES)
    row = pl.BlockSpec((1, 1, tr), lambda i: (i, 0, 0))
    g, rank, cnt = pl.pallas_call(
        _rank_kernel, grid=(n // tr,), in_specs=[pl.BlockSpec((tr, LANES), lambda i: (i, 0))],
        out_specs=[row, row, pl.BlockSpec((SUBLANES, LANES), lambda i: (0, 0))],
        out_shape=[jax.ShapeDtypeStruct((n // tr, 1, tr), jnp.int32), jax.ShapeDtypeStruct((n // tr, 1, tr), jnp.int32),
                   jax.ShapeDtypeStruct((SUBLANES, LANES), F32)],
        scratch_shapes=[pltpu.VMEM((SUBLANES, LANES), F32)],
        compiler_params=_params("arbitrary"), name="rank")(onehot)
    return g.reshape(n), rank.reshape(n), cnt[:N_GROUPS, 0].astype(jnp.int32)


def _invert_kernel(g_ref, rank_ref, cnt_ref, pos_ref, idx_ref, tg_ref, *, n, tm, ntiles):
    bases = [jnp.int32(0)]
    for g in range(N_GROUPS - 1):
        bases.append(bases[-1] + (cnt_ref[g] + tm - 1) // tm * tm)

    def zero(i, c):
        idx_ref[i] = 0
        return c
    lax.fori_loop(0, ntiles * tm, zero, 0, unroll=8)

    def place(t, c):
        g = g_ref[t]
        base = bases[0]
        for k in range(1, N_GROUPS):
            base = jnp.where(g == k, bases[k], base)
        p = base + rank_ref[t]
        pos_ref[t] = p
        idx_ref[p] = t
        return c
    lax.fori_loop(0, n, place, 0, unroll=8)

    def tile_group(i, c):
        r = i * tm
        tg = jnp.int32(0)
        for k in range(1, N_GROUPS):
            tg = tg + (r >= bases[k]).astype(jnp.int32)
        tg_ref[i] = tg
        return c
    lax.fori_loop(0, ntiles, tile_group, 0)


def _invert(g, rank, cnt, tm, ntiles):
    n = g.shape[0]
    smem = pl.BlockSpec(memory_space=pltpu.SMEM)
    return pl.pallas_call(
        functools.partial(_invert_kernel, n=n, tm=tm, ntiles=ntiles),
        in_specs=[smem, smem, smem], out_specs=[smem, smem, smem],
        out_shape=[jax.ShapeDtypeStruct((n,), jnp.int32), jax.ShapeDtypeStruct((ntiles * tm,), jnp.int32),
                   jax.ShapeDtypeStruct((ntiles,), jnp.int32)],
        name="invert")(g, rank, cnt)


def _row_copy(src_hbm, buf, sem, src_row, slot, dst_row):
    return pltpu.make_async_copy(src_hbm.at[pl.ds(src_row, 1)], buf.at[slot, pl.ds(dst_row, 1)], sem.at[slot])


def _gather_rows(index_ref, src_hbm, buf, sem, rows):
    i = pl.program_id(0)
    steps = pl.num_programs(0)

    def issue(step, slot):
        def body(p, c):
            for k in range(2):
                r = 2 * p + k
                _row_copy(src_hbm, buf, sem, index_ref[step * rows + r], slot, r).start(priority=k)
            return c
        lax.fori_loop(0, rows // 2, body, 0, unroll=4)

    @pl.when(i == 0)
    def _():
        issue(0, 0)

    @pl.when(i + 1 < steps)
    def _():
        issue(i + 1, (i + 1) % 2)

    slot = i % 2

    def wait(r, c):
        _row_copy(src_hbm, buf, sem, 0, slot, r).wait()
        return c
    lax.fori_loop(0, rows, wait, 0, unroll=8)
    return slot


def _dispatch_kernel(idx_ref, slab_hbm, xs_ref, cs_ref, buf, dense_s, sem, *, npack):
    slot = _gather_rows(idx_ref, slab_hbm, buf, sem, xs_ref.shape[0])
    _load_packed_rows(lambda j: buf[slot, :, j, :], dense_s, xs_ref, npack)
    cs_ref[...] = lax.bitcast_convert_type(buf[slot, :, npack, :], F32)


def _dispatch(idx, slab, tm, ntiles, d):
    srows = slab.shape[1]
    return pl.pallas_call(
        functools.partial(_dispatch_kernel, npack=d // (2 * LANES)),
        grid_spec=pltpu.PrefetchScalarGridSpec(
            num_scalar_prefetch=1, grid=(ntiles,),
            in_specs=[pl.BlockSpec(memory_space=pl.ANY)],
            out_specs=[pl.BlockSpec((tm, d), lambda i, idx: (i, 0)), pl.BlockSpec((tm, LANES), lambda i, idx: (i, 0))],
            scratch_shapes=[pltpu.VMEM((2, tm, srows, LANES), jnp.uint32), pltpu.VMEM((tm, d // 2), jnp.uint32),
                            pltpu.SemaphoreType.DMA((2,))]),
        out_shape=[jax.ShapeDtypeStruct((ntiles * tm, d), BF16), jax.ShapeDtypeStruct((ntiles * tm, LANES), F32)],
        compiler_params=_params("arbitrary"), name="dispatch")(idx, slab)


def _expert_kernel(tg_ref, x_ref, c_ref, w13_ref, w2_ref, o_ref, acc_s, *, npack):
    i = pl.program_id(0)
    e = pl.program_id(1)

    @pl.when(e == 0)
    def _():
        acc_s[...] = jnp.zeros_like(acc_s)

    f = w2_ref.shape[1]
    h13 = _dot(x_ref[...], w13_ref[0])
    h1, h3 = h13[:, :f], h13[:, f:]
    comb = c_ref[...]
    lane = lax.broadcasted_iota(jnp.int32, comb.shape, 1)
    c = jnp.sum(jnp.where(lane == N_GROUPS + tg_ref[i] * EXPERTS_PER_GROUP + e, comb, 0.0), axis=1, keepdims=True)
    hid = (h1 * jax.nn.sigmoid(h1) * h3 * c).astype(BF16)
    acc_s[...] += _dot(hid, w2_ref[0])

    @pl.when(e == EXPERTS_PER_GROUP - 1)
    def _():
        _store_packed_rows(o_ref, acc_s, npack)


def _experts(tg, xs, cs, w13, w2, tm, layer):
    rows, d = xs.shape
    f = w2.shape[1]
    npack = d // (2 * LANES)
    wmap = lambda i, e, tg: (layer * N_EXPERTS + tg[i] * EXPERTS_PER_GROUP + e, 0, 0)
    return pl.pallas_call(
        functools.partial(_expert_kernel, npack=npack),
        grid_spec=pltpu.PrefetchScalarGridSpec(
            num_scalar_prefetch=1, grid=(rows // tm, EXPERTS_PER_GROUP),
            in_specs=[pl.BlockSpec((tm, d), lambda i, e, tg: (i, 0)), pl.BlockSpec((tm, LANES), lambda i, e, tg: (i, 0)),
                      pl.BlockSpec((1, d, 2 * f), wmap), pl.BlockSpec((1, f, d), wmap)],
            out_specs=pl.BlockSpec((tm, npack, LANES), lambda i, e, tg: (i, 0, 0)),
            scratch_shapes=[pltpu.VMEM((tm, d), F32)]),
        out_shape=jax.ShapeDtypeStruct((rows, npack, LANES), jnp.uint32),
        compiler_params=_params("parallel", "arbitrary"), name="experts")(tg, xs, cs, w13, w2)


def _collect_kernel(pos_ref, ys_hbm, o_ref, buf, dense_s, sem, *, npack):
    slot = _gather_rows(pos_ref, ys_hbm, buf, sem, o_ref.shape[0])
    _load_packed_rows(lambda j: buf[slot, :, j, :], dense_s, o_ref, npack)


def _collect_norm_kernel(pos_ref, ys_hbm, h_ref, g_ref, hnew_ref, hn_ref, buf, dense_s, delta_s, sem, *, npack):
    slot = _gather_rows(pos_ref, ys_hbm, buf, sem, h_ref.shape[0])
    _load_packed_rows(lambda j: buf[slot, :, j, :], dense_s, delta_s, npack)
    x = h_ref[...] + delta_s[...].astype(F32)
    hnew_ref[...] = x
    y = x * lax.rsqrt(jnp.mean(x * x, axis=-1, keepdims=True) + EPS) * g_ref[...]
    hn_ref[...] = y.astype(hn_ref.dtype)


def _collect(pos, ys, h=None, g=None):
    n = pos.shape[0]
    _, npack, _ = ys.shape
    d = 2 * npack * LANES
    tr = _tile(n, 384, BF16_ROWS)
    row = pl.BlockSpec((tr, d), lambda i, pos: (i, 0))
    scratch = [pltpu.VMEM((2, tr, npack, LANES), jnp.uint32), pltpu.VMEM((tr, d // 2), jnp.uint32)]
    sem = pltpu.SemaphoreType.DMA((2,))
    if h is None:
        return pl.pallas_call(
            functools.partial(_collect_kernel, npack=npack),
            grid_spec=pltpu.PrefetchScalarGridSpec(
                num_scalar_prefetch=1, grid=(n // tr,), in_specs=[pl.BlockSpec(memory_space=pl.ANY)],
                out_specs=row, scratch_shapes=scratch + [sem]),
            out_shape=jax.ShapeDtypeStruct((n, d), BF16),
            compiler_params=_params("arbitrary"), name="collect")(pos, ys)
    return pl.pallas_call(
        functools.partial(_collect_norm_kernel, npack=npack),
        grid_spec=pltpu.PrefetchScalarGridSpec(
            num_scalar_prefetch=1, grid=(n // tr,),
            in_specs=[pl.BlockSpec(memory_space=pl.ANY), row, pl.BlockSpec((1, d), lambda i, pos: (0, 0))],
            out_specs=[row, row], scratch_shapes=scratch + [pltpu.VMEM((tr, d), BF16), sem]),
        out_shape=[jax.ShapeDtypeStruct((n, d), F32), jax.ShapeDtypeStruct((n, d), BF16)],
        compiler_params=_params("arbitrary"), name="collect_norm")(pos, ys, h, g.reshape(1, d).astype(F32))


def _moe_layer(h, ffn_g, wg, bg, we, be, w13, w2, layer, next_norm=None):
    n, d = h.shape
    tm = 512 if n >= 4096 else 128
    ntiles = (n + N_GROUPS * (tm - 1)) // tm
    slab, onehot = _norm_route(h, ffn_g, wg, bg, we, be)
    g, rank, cnt = _rank(onehot)
    pos, idx, tg = _invert(g, rank, cnt, tm, ntiles)
    xs, cs = _dispatch(idx, slab, tm, ntiles, d)
    ys = _experts(tg, xs, cs, w13, w2, tm, layer)
    if next_norm is None:
        return _collect(pos, ys)
    return _collect(pos, ys, h, next_norm)


def _even_layer(h, hn, w_in_all, e, gate_bias, qk_conv, lr_up, lr_bias, norm_a, norm_b, w_out, batch):
    n, d = h.shape
    dk, dv = d // 16, d // 8
    qk_w, v_w = HEADS * dk, HEADS * dv
    a_end = 2 * qk_w + 2 * v_w
    b_start = a_end + GATE_COLS
    b_end = b_start + 2 * qk_w + 2 * v_w
    w_main, w_small = _split_cast(w_in_all, e, a_end, b_start, b_end)
    proj = _matmul(hn, w_main, BF16, tn_target=1024)
    small = _matmul(hn, w_small, F32, tn_target=LANES)
    gates_t = small[:, :GATE_COLS].reshape(n // CHUNK, CHUNK, GATE_COLS).transpose(0, 2, 1)
    qk = _qk_conv(proj, qk_conv, batch, dk)
    va_blk, oa_blk = 2 * qk_w // v_w, (2 * qk_w + v_w) // v_w
    b0 = a_end
    qb_blk, kb_blk = b0 // qk_w, (b0 + qk_w) // qk_w
    vb_blk, gb_blk = (b0 + 2 * qk_w) // v_w, (b0 + 2 * qk_w + v_w) // v_w
    ha, hb = [], []
    for rev in (False, True):
        ha.append(_mlstm(qk, proj, small, gates_t, gate_bias, batch, dk, dv, va_blk, rev))
        hb.append(_gla(proj, small, lr_up[int(rev)], lr_bias[int(rev)], batch, dk, dv, (qb_blk, kb_blk, vb_blk), rev))
    y = _even_combine(ha[0], ha[1], hb[0], hb[1], proj, norm_a, norm_b, batch, dv, oa_blk, gb_blk)
    return _matmul(y, _cast_bf16(w_out), F32, res=h)


def _odd_layer(h, hn, w_in, conv_w, conv_b, w_r, b_r, w_i, b_i, lam, w_out, batch):
    d_rnn = w_out.shape[0]
    proj = _matmul(hn, _cast_bf16(w_in), BF16, tn_target=1024)
    hs = [_rglru(proj, conv_w, conv_b, w_r[i], b_r[i], w_i[i], b_i[i], lam[i], batch, d_rnn, bool(i)) for i in (0, 1)]
    y = _odd_combine(proj, hs[0], hs[1], batch)
    return _matmul(y, _cast_bf16(w_out), F32, res=h)


def kernel(x, meta_tokens, mix_norm, ffn_norm, final_norm, ev_w_in, ev_gate_bias, ev_qk_conv, ev_lr_up, ev_lr_bias, ev_norm_a, ev_norm_b, ev_w_out, od_w_in, od_conv, od_conv_bias, od_w_r, od_b_r, od_w_i, od_b_i, od_lambda, od_w_out, moe_wg, moe_bg, moe_we, moe_be, moe_w1, moe_w3, moe_w2):
    batch, seq, d = x.shape
    depth = mix_norm.shape[0]
    assert seq % CHUNK == 0 and d % 16 == 0
    f = moe_w1.shape[-1]
    w13 = _cast_pair_bf16(moe_w1, moe_w3).reshape(depth * N_EXPERTS, d, 2 * f)
    w2 = _cast_bf16(moe_w2).reshape(depth * N_EXPERTS, f, d)
    h, hn = _frame_norm(x, meta_tokens, mix_norm[0])
    for layer in range(depth):
        if layer % 2 == 0:
            e = layer // 2
            h = _even_layer(h, hn, ev_w_in, e, ev_gate_bias[e], ev_qk_conv[e], ev_lr_up[e],
                            ev_lr_bias[e], ev_norm_a[e], ev_norm_b[e], ev_w_out[e], batch)
        else:
            o = layer // 2
            h = _odd_layer(h, hn, od_w_in[o], od_conv[o], od_conv_bias[o], od_w_r[o], od_b_r[o],
                           od_w_i[o], od_b_i[o], od_lambda[o], od_w_out[o], batch)
        moe_args = (h, ffn_norm[layer], moe_wg[layer], moe_bg[layer], moe_we[layer], moe_be[layer], w13, w2, layer)
        if layer + 1 < depth:
            h, hn = _moe_layer(*moe_args, next_norm=mix_norm[layer + 1])
        else:
            delta = _moe_layer(*moe_args)
    out = _final_norm(h, delta, final_norm, batch, seq)
    return out.reshape(batch, seq, d)
```
